```python
import jax
import jax.numpy as jnp
from jax import lax
import numpy as np

D_MODEL = 1024
BATCH = 4
SEQ = 4096
DEPTH = 4

MEM_LEN = 256
N_EVEN = (DEPTH + 1) // 2
N_ODD = DEPTH // 2
D_FF = 2816
NORM_EPS = 1e-6
ATTN_BLOCK = 128

RET_HEADS = 4
RET_DK = 128
RET_DV = 128
RET_CHUNK = 128
RET_ROPE_BASE = 10000.0

SSM_HEADS = 8
SSM_HEAD_DIM = 64
SSM_D_INNER = SSM_HEADS * SSM_HEAD_DIM
SSM_D_STATE = 64
SSM_GROUPS = 2
SSM_CONV = 4
SSM_CHUNK = 128
SSM_CONV_DIM = SSM_D_INNER + 2 * SSM_GROUPS * SSM_D_STATE

EVEN_SPLITS = (RET_HEADS * RET_DK, RET_HEADS * RET_DK, RET_HEADS * RET_DV, RET_HEADS * RET_DV, SSM_D_INNER, SSM_CONV_DIM, SSM_HEADS)
EVEN_IN = sum(EVEN_SPLITS)
EVEN_MIX = RET_HEADS * RET_DV + SSM_D_INNER

NSA_HEADS = 8
NSA_KV_HEADS = 2
NSA_HEAD_DIM = 64
NSA_CMP_BLOCK = 32
NSA_CMP_STRIDE = 16
NSA_CMP_HIDDEN = 256
NSA_SEL_BLOCK = 64
NSA_TOPN = 16
NSA_WINDOW = 512
NSA_SEL_QBLOCK = 64
NSA_FORCE_SCORE = 1e6

SWA_HEADS = 8
SWA_KV_HEADS = 2
SWA_HEAD_DIM = 64
SWA_WINDOW = 128

ODD_SPLITS = (NSA_HEADS * NSA_HEAD_DIM,) + (NSA_KV_HEADS * NSA_HEAD_DIM,) * 6 + (3 * NSA_HEADS, SWA_HEADS * SWA_HEAD_DIM, SWA_KV_HEADS * SWA_HEAD_DIM, SWA_KV_HEADS * SWA_HEAD_DIM)
ODD_IN = sum(ODD_SPLITS)
ODD_MIX = NSA_HEADS * NSA_HEAD_DIM + SWA_HEADS * SWA_HEAD_DIM

XATTN_HEADS = 4
XATTN_HEAD_DIM = D_MODEL // XATTN_HEADS

kernel_name = 'hybrid_retnet_mamba2_nsa_swa_macaron'


def split_last(t, sizes):
    cuts = [int(c) for c in np.cumsum(sizes)[:-1]]
    return jnp.split(t, cuts, axis=-1)


def rmsnorm(x, w):
    xf = x.astype(jnp.float32)
    y = xf * lax.rsqrt(jnp.mean(xf * xf, axis=-1, keepdims=True) + NORM_EPS)
    return (y * w.astype(jnp.float32)).astype(x.dtype)


def swiglu(h, w_gate, w_up, w_down):
    return (jax.nn.silu(h @ w_gate) * (h @ w_up)) @ w_down


def masked_softmax(s, mask):
    s = jnp.where(mask, s, -jnp.inf)
    m = jnp.max(s, axis=-1, keepdims=True)
    m = jnp.where(jnp.isfinite(m), m, 0.0)
    p = jnp.exp(s - m)
    return p / jnp.maximum(jnp.sum(p, axis=-1, keepdims=True), 1e-30)


def sink_softmax(s, mask, sink):
    s = jnp.where(mask, s, -jnp.inf)
    m = jnp.maximum(jnp.max(s, axis=-1, keepdims=True), sink)
    p = jnp.exp(s - m)
    return p / (jnp.sum(p, axis=-1, keepdims=True) + jnp.exp(sink - m))


def rotate_pairs(t, cos, sin):
    t1 = t[..., 0::2]
    t2 = t[..., 1::2]
    return jnp.stack([t1 * cos - t2 * sin, t1 * sin + t2 * cos], axis=-1).reshape(t.shape)


def retention(q, k, v):
    bsz, seq, nh, dk = q.shape
    dv = v.shape[-1]
    L = RET_CHUNK
    nc = seq // L
    cdt = q.dtype
    pos = jnp.arange(seq, dtype=jnp.float32)
    inv_freq = 1.0 / (RET_ROPE_BASE ** jnp.linspace(0.0, 1.0, dk // 2, dtype=jnp.float32))
    ang = pos[:, None] * inv_freq[None, :]
    cos = jnp.cos(ang)[:, None, :].astype(cdt)
    sin = jnp.sin(ang)[:, None, :].astype(cdt)
    q = rotate_pairs(q, cos, sin)
    k = rotate_pairs(k, cos, sin) * (dk ** -0.5)
    log_g = jnp.log1p(-jnp.exp2(-5.0 - jnp.arange(nh, dtype=jnp.float32)))
    idx = jnp.arange(L, dtype=jnp.float32)
    diff = idx[:, None] - idx[None, :]
    dmat = jnp.where(diff >= 0, jnp.exp(jnp.maximum(diff, 0.0)[None] * log_g[:, None, None]), 0.0).astype(cdt)
    zeta = jnp.exp((L - 1 - idx)[None, :] * log_g[:, None]).astype(cdt)
    xi = jnp.exp((idx + 1)[None, :] * log_g[:, None]).astype(cdt)
    chunk_decay = jnp.exp(L * log_g).astype(cdt)
    qc = q.reshape(bsz, nc, L, nh, dk)
    kc = k.reshape(bsz, nc, L, nh, dk)
    vc = v.reshape(bsz, nc, L, nh, dv)
    scores = jnp.einsum('bcihd,bcjhd->bchij', qc, kc) * dmat
    inner = jnp.einsum('bchij,bcjhe->bcihe', scores, vc)
    chunk_kv = jnp.einsum('bcjhd,hj,bcjhe->cbhde', kc, zeta, vc)

    def step(state, kv):
        return chunk_decay[None, :, None, None] * state + kv, state

    _, s_in = lax.scan(step, jnp.zeros((bsz, nh, dk, dv), chunk_kv.dtype), chunk_kv)
    cross = jnp.einsum('bcihd,hi,cbhde->bcihe', qc, xi, s_in)
    o = (inner + cross).reshape(bsz, seq, nh, dv).astype(jnp.float32)
    o = o * lax.rsqrt(jnp.mean(o * o, axis=-1, keepdims=True) + NORM_EPS)
    return o.astype(cdt)


def ssd_scan(xh, dt, a, bm, cm):
    bsz, seq, nh, hp = xh.shape
    ng, ns = bm.shape[2], bm.shape[3]
    ne = nh // ng
    L = SSM_CHUNK
    nc = seq // L
    cdt = xh.dtype
    cums = jnp.cumsum((dt * a).reshape(bsz, nc, L, ng, ne), axis=2)
    xdt = (xh * dt[..., None].astype(cdt)).reshape(bsz, nc, L, ng, ne, hp)
    bc = bm.reshape(bsz, nc, L, ng, ns)
    cc = cm.reshape(bsz, nc, L, ng, ns)
    idx = jnp.arange(L)
    causal = (idx[:, None] >= idx[None, :])[:, :, None, None]
    seg = cums[:, :, :, None] - cums[:, :, None, :]
    lmat = jnp.exp(jnp.where(causal, seg, -jnp.inf)).astype(cdt)
    cb = jnp.einsum('bclgn,bcsgn->bclsg', cc, bc)
    y_diag = jnp.einsum('bclsg,bclsge,bcsgep->bclgep', cb, lmat, xdt)
    decay_states = jnp.exp(cums[:, :, -1:] - cums).astype(cdt)
    states = jnp.einsum('bclgn,bclge,bclgep->cbgepn', bc, decay_states, xdt)
    chunk_decay = jnp.exp(cums[:, :, -1]).astype(cdt).transpose(1, 0, 2, 3)

    def step(state, inp):
        dec, st = inp
        return dec[..., None, None] * state + st, state

    _, s_in = lax.scan(step, jnp.zeros((bsz, ng, ne, hp, ns), states.dtype), (chunk_decay, states))
    y_off = jnp.einsum('bclgn,cbgepn,bclge->bclgep', cc, s_in, jnp.exp(cums).astype(cdt))
    return (y_diag + y_off).reshape(bsz, seq, nh, hp)


def even_mixer(h, w_in, w_out, conv_w, conv_b, dt_bias, a_log, d_skip, norm_w):
    bsz, seq, _ = h.shape
    q, k, v, g, z, xbc, dt_raw = split_last(h @ w_in, EVEN_SPLITS)
    o_ret = retention(q.reshape(bsz, seq, RET_HEADS, RET_DK), k.reshape(bsz, seq, RET_HEADS, RET_DK), v.reshape(bsz, seq, RET_HEADS, RET_DV))
    o_ret = jax.nn.silu(g) * o_ret.reshape(bsz, seq, RET_HEADS * RET_DV)
    xbc = lax.conv_general_dilated(xbc, conv_w[:, None, :].astype(xbc.dtype), window_strides=(1,), padding=[(SSM_CONV - 1, 0)], dimension_numbers=('NWC', 'WIO', 'NWC'), feature_group_count=SSM_CONV_DIM)
    xbc = jax.nn.silu(xbc + conv_b)
    xs, bm, cm = split_last(xbc, (SSM_D_INNER, SSM_GROUPS * SSM_D_STATE, SSM_GROUPS * SSM_D_STATE))
    dt = jax.nn.softplus(dt_raw.astype(jnp.float32) + dt_bias.astype(jnp.float32))
    a = -jnp.exp(a_log.astype(jnp.float32))
    xh = xs.reshape(bsz, seq, SSM_HEADS, SSM_HEAD_DIM)
    y = ssd_scan(xh, dt, a, bm.reshape(bsz, seq, SSM_GROUPS, SSM_D_STATE), cm.reshape(bsz, seq, SSM_GROUPS, SSM_D_STATE))
    y = (y + d_skip[:, None] * xh).reshape(bsz, seq, SSM_D_INNER)
    y = rmsnorm(y * jax.nn.silu(z), norm_w)
    return jnp.concatenate([o_ret, y], axis=-1) @ w_out


def banded_attention(q, k, v, window, sinks=None):
    bsz, ng, ne, seq, hd = q.shape
    qb_size = ATTN_BLOCK
    nb = seq // qb_size
    nprev = -(-(window - 1) // qb_size)
    nk = (nprev + 1) * qb_size

    def band(t):
        tb = jnp.pad(t.reshape(bsz, ng, nb, qb_size, hd), ((0, 0), (0, 0), (nprev, 0), (0, 0), (0, 0)))
        return jnp.concatenate([tb[:, :, j:j + nb] for j in range(nprev + 1)], axis=3)

    kb, vb = band(k), band(v)
    blk = jnp.arange(nb)
    qpos = blk[:, None] * qb_size + jnp.arange(qb_size)[None, :]
    kpos = (blk[:, None] - nprev) * qb_size + jnp.arange(nk)[None, :]
    rel = qpos[:, :, None] - kpos[:, None, :]
    mask = (rel >= 0) & (rel < window) & (kpos[:, None, :] >= 0)
    s = jnp.einsum('bgeiqd,bgikd->bgeiqk', q.reshape(bsz, ng, ne, nb, qb_size, hd), kb).astype(jnp.float32) * (hd ** -0.5)
    if sinks is None:
        p = masked_softmax(s, mask)
    else:
        p = sink_softmax(s, mask, sinks.astype(jnp.float32)[None, :, :, None, None, None])
    o = jnp.einsum('bgeiqk,bgikd->bgeiqd', p.astype(v.dtype), vb)
    return o.reshape(bsz, ng, ne, seq, hd)


def selected_block_attention(q, k, v, sel_idx):
    bsz, ng, ne, seq, hd = q.shape
    n = sel_idx.shape[-1]
    qb_size = NSA_SEL_QBLOCK
    nq = seq // qb_size
    kb = k.reshape(bsz, ng, seq // NSA_SEL_BLOCK, NSA_SEL_BLOCK, hd)
    vb = v.reshape(bsz, ng, seq // NSA_SEL_BLOCK, NSA_SEL_BLOCK, hd)
    q_ch = q.reshape(bsz, ng, ne, nq, qb_size, hd).transpose(3, 0, 1, 2, 4, 5)
    idx_ch = sel_idx.reshape(bsz, ng, nq, qb_size, n).transpose(2, 0, 1, 3, 4)
    t_ch = jnp.arange(seq).reshape(nq, qb_size)
    bi = jnp.arange(bsz)[:, None, None, None]
    gi = jnp.arange(ng)[None, :, None, None]
    offs = jnp.arange(NSA_SEL_BLOCK)
    scale = hd ** -0.5

    def body(args):
        qc, ic, tc = args
        kg = kb[bi, gi, ic].reshape(bsz, ng, qb_size, n * NSA_SEL_BLOCK, hd)
        vg = vb[bi, gi, ic].reshape(bsz, ng, qb_size, n * NSA_SEL_BLOCK, hd)
        s = jnp.einsum('bgeqd,bgqmd->bgeqm', qc, kg).astype(jnp.float32) * scale
        kpos = (ic[..., None] * NSA_SEL_BLOCK + offs).reshape(bsz, ng, qb_size, n * NSA_SEL_BLOCK)
        mask = (kpos <= tc[None, None, :, None])[:, :, None]
        p = masked_softmax(s, mask)
        return jnp.einsum('bgeqm,bgqmd->bgeqd', p.astype(vg.dtype), vg)

    o = lax.map(body, (q_ch, idx_ch, t_ch))
    return o.transpose(1, 2, 3, 0, 4, 5).reshape(bsz, ng, ne, seq, hd)


def native_sparse_attention(q, kc, vc, ks, vs, kw, vw, g, cmp_pe, cmp_w1, cmp_w2):
    bsz, ng, ne, seq, hd = q.shape
    scale = hd ** -0.5
    tpos = jnp.arange(seq)
    n_cmp = (seq - NSA_CMP_BLOCK) // NSA_CMP_STRIDE + 1
    starts = jnp.arange(n_cmp) * NSA_CMP_STRIDE
    tok = starts[:, None] + jnp.arange(NSA_CMP_BLOCK)[None, :]

    def compress(t, pe, w1, w2):
        blk = t[:, :, tok] + pe
        return jax.nn.silu(blk.reshape(bsz, ng, n_cmp, NSA_CMP_BLOCK * hd) @ w1) @ w2

    k_cmp = compress(kc, cmp_pe[0], cmp_w1[0], cmp_w2[0])
    v_cmp = compress(vc, cmp_pe[1], cmp_w1[1], cmp_w2[1])
    s_cmp = jnp.einsum('bgeqd,bgnd->bgeqn', q, k_cmp).astype(jnp.float32) * scale
    p_cmp = masked_softmax(s_cmp, (starts + NSA_CMP_BLOCK - 1)[None, :] <= tpos[:, None])
    o_cmp = jnp.einsum('bgeqn,bgnd->bgeqd', p_cmp.astype(vc.dtype), v_cmp)
    n_sel = seq // NSA_SEL_BLOCK
    sel_start = jnp.arange(n_sel) * NSA_SEL_BLOCK
    overlap = ((starts[:, None] < sel_start[None, :] + NSA_SEL_BLOCK) & (starts[:, None] + NSA_CMP_BLOCK > sel_start[None, :])).astype(jnp.float32)
    imp = jnp.einsum('bgqn,nj->bgqj', jnp.sum(p_cmp, axis=2), overlap)
    cur = (tpos // NSA_SEL_BLOCK)[:, None]
    jb = jnp.arange(n_sel)[None, :]
    valid = jb <= cur
    forced = valid & ((jb == 0) | (jb == cur) | (jb == cur - 1))
    score = jnp.where(forced, NSA_FORCE_SCORE, jnp.where(valid, imp, -jnp.inf))
    _, sel_idx = lax.top_k(score, min(NSA_TOPN, n_sel))
    o_sel = selected_block_attention(q, ks, vs, sel_idx)
    o_win = banded_attention(q, kw, vw, NSA_WINDOW)
    return g[..., 0:1] * o_cmp + g[..., 1:2] * o_sel + g[..., 2:3] * o_win


def odd_mixer(h, w_in, w_out, cmp_pe, cmp_w1, cmp_w2, sinks):
    bsz, seq, _ = h.shape
    q_n, kc, vc, ks, vs, kw, vw, gates, q_s, k_s, v_s = split_last(h @ w_in, ODD_SPLITS)

    def q_heads(t, nh, nkv, hd):
        return t.reshape(bsz, seq, nkv, nh // nkv, hd).transpose(0, 2, 3, 1, 4)

    def kv_heads(t, nkv, hd):
        return t.reshape(bsz, seq, nkv, hd).transpose(0, 2, 1, 3)

    def merge(o):
        return o.transpose(0, 3, 1, 2, 4).reshape(bsz, seq, -1)

    g = jax.nn.sigmoid(gates.astype(jnp.float32)).astype(h.dtype)
    g = g.reshape(bsz, seq, NSA_KV_HEADS, NSA_HEADS // NSA_KV_HEADS, 3).transpose(0, 2, 3, 1, 4)
    o_nsa = native_sparse_attention(
        q_heads(q_n, NSA_HEADS, NSA_KV_HEADS, NSA_HEAD_DIM),
        kv_heads(kc, NSA_KV_HEADS, NSA_HEAD_DIM), kv_heads(vc, NSA_KV_HEADS, NSA_HEAD_DIM),
        kv_heads(ks, NSA_KV_HEADS, NSA_HEAD_DIM), kv_heads(vs, NSA_KV_HEADS, NSA_HEAD_DIM),
        kv_heads(kw, NSA_KV_HEADS, NSA_HEAD_DIM), kv_heads(vw, NSA_KV_HEADS, NSA_HEAD_DIM),
        g, cmp_pe, cmp_w1, cmp_w2)
    o_swa = banded_attention(
        q_heads(q_s, SWA_HEADS, SWA_KV_HEADS, SWA_HEAD_DIM),
        kv_heads(k_s, SWA_KV_HEADS, SWA_HEAD_DIM), kv_heads(v_s, SWA_KV_HEADS, SWA_HEAD_DIM),
        SWA_WINDOW, sinks.reshape(SWA_KV_HEADS, SWA_HEADS // SWA_KV_HEADS))
    return jnp.concatenate([merge(o_nsa), merge(o_swa)], axis=-1) @ w_out


def cross_attention(h, memn, wq, wkv, wo):
    bsz, seq, _ = h.shape
    m = memn.shape[1]
    q = (h @ wq).reshape(bsz, seq, XATTN_HEADS, XATTN_HEAD_DIM)
    kv = (memn @ wkv).reshape(bsz, m, 2, XATTN_HEADS, XATTN_HEAD_DIM)
    s = jnp.einsum('bshd,bmhd->bhsm', q, kv[:, :, 0]).astype(jnp.float32) * (XATTN_HEAD_DIM ** -0.5)
    p = jax.nn.softmax(s, axis=-1).astype(h.dtype)
    o = jnp.einsum('bhsm,bmhd->bshd', p, kv[:, :, 1]).reshape(bsz, seq, XATTN_HEADS * XATTN_HEAD_DIM)
    return o @ wo


def setup_inputs(seed: int = 0) -> dict:
    key = jax.random.key(seed)
    keys = jax.random.split(key, 32)
    f32 = jnp.float32

    def nrm(k, shape, scale):
        return jax.random.normal(k, shape, f32) * scale

    dt0 = jnp.exp(jax.random.uniform(keys[16], (N_EVEN, SSM_HEADS), f32, float(np.log(1e-3)), float(np.log(1e-1))))
    return {
        'x': nrm(keys[0], (BATCH, SEQ, D_MODEL), 1.0),
        'mem': nrm(keys[1], (BATCH, MEM_LEN, D_MODEL), 1.0),
        'norm_w': 1.0 + nrm(keys[2], (DEPTH, 4, D_MODEL), 0.02),
        'final_norm_w': 1.0 + nrm(keys[3], (D_MODEL,), 0.02),
        'mem_norm_w': 1.0 + nrm(keys[4], (D_MODEL,), 0.02),
        'ffn_w_gate': nrm(keys[5], (DEPTH, 2, D_MODEL, D_FF), D_MODEL ** -0.5),
        'ffn_w_up': nrm(keys[6], (DEPTH, 2, D_MODEL, D_FF), D_MODEL ** -0.5),
        'ffn_w_down': nrm(keys[7], (DEPTH, 2, D_FF, D_MODEL), D_FF ** -0.5),
        'xattn_wq': nrm(keys[8], (DEPTH, D_MODEL, XATTN_HEADS * XATTN_HEAD_DIM), D_MODEL ** -0.5),
        'xattn_wkv': nrm(keys[9], (DEPTH, D_MODEL, 2 * XATTN_HEADS * XATTN_HEAD_DIM), D_MODEL ** -0.5),
        'xattn_wo': nrm(keys[10], (DEPTH, XATTN_HEADS * XATTN_HEAD_DIM, D_MODEL), (XATTN_HEADS * XATTN_HEAD_DIM) ** -0.5),
        'even_w_in': nrm(keys[11], (N_EVEN, D_MODEL, EVEN_IN), D_MODEL ** -0.5),
        'even_w_out': nrm(keys[12], (N_EVEN, EVEN_MIX, D_MODEL), EVEN_MIX ** -0.5),
        'ssm_conv_w': nrm(keys[13], (N_EVEN, SSM_CONV, SSM_CONV_DIM), SSM_CONV ** -0.5),
        'ssm_conv_b': nrm(keys[14], (N_EVEN, SSM_CONV_DIM), 0.01),
        'ssm_dt_bias': dt0 + jnp.log(-jnp.expm1(-dt0)),
        'ssm_a_log': jnp.log(jax.random.uniform(keys[17], (N_EVEN, SSM_HEADS), f32, 1.0, 16.0)),
        'ssm_d': 1.0 + nrm(keys[18], (N_EVEN, SSM_HEADS), 0.02),
        'ssm_norm_w': 1.0 + nrm(keys[19], (N_EVEN, SSM_D_INNER), 0.02),
        'odd_w_in': nrm(keys[20], (N_ODD, D_MODEL, ODD_IN), D_MODEL ** -0.5),
        'odd_w_out': nrm(keys[21], (N_ODD, ODD_MIX, D_MODEL), ODD_MIX ** -0.5),
        'nsa_cmp_pe': nrm(keys[22], (N_ODD, 2, NSA_CMP_BLOCK, NSA_HEAD_DIM), 0.02),
        'nsa_cmp_w1': nrm(keys[23], (N_ODD, 2, NSA_CMP_BLOCK * NSA_HEAD_DIM, NSA_CMP_HIDDEN), (NSA_CMP_BLOCK * NSA_HEAD_DIM) ** -0.5),
        'nsa_cmp_w2': nrm(keys[24], (N_ODD, 2, NSA_CMP_HIDDEN, NSA_HEAD_DIM), NSA_CMP_HIDDEN ** -0.5),
        'swa_sinks': nrm(keys[25], (N_ODD, SWA_HEADS), 0.5),
    }


def reference(x, mem, norm_w, final_norm_w, mem_norm_w, ffn_w_gate, ffn_w_up, ffn_w_down, xattn_wq, xattn_wkv, xattn_wo, even_w_in, even_w_out, ssm_conv_w, ssm_conv_b, ssm_dt_bias, ssm_a_log, ssm_d, ssm_norm_w, odd_w_in, odd_w_out, nsa_cmp_pe, nsa_cmp_w1, nsa_cmp_w2, swa_sinks):
    memn = rmsnorm(mem, mem_norm_w)
    for layer in range(DEPTH):
        i = layer // 2
        h = rmsnorm(x, norm_w[layer, 0])
        x = x + 0.5 * swiglu(h, ffn_w_gate[layer, 0], ffn_w_up[layer, 0], ffn_w_down[layer, 0])
        h = rmsnorm(x, norm_w[layer, 1])
        if layer % 2 == 0:
            x = x + even_mixer(h, even_w_in[i], even_w_out[i], ssm_conv_w[i], ssm_conv_b[i], ssm_dt_bias[i], ssm_a_log[i], ssm_d[i], ssm_norm_w[i])
        else:
            x = x + odd_mixer(h, odd_w_in[i], odd_w_out[i], nsa_cmp_pe[i], nsa_cmp_w1[i], nsa_cmp_w2[i], swa_sinks[i])
        h = rmsnorm(x, norm_w[layer, 2])
        x = x + cross_attention(h, memn, xattn_wq[layer], xattn_wkv[layer], xattn_wo[layer])
        h = rmsnorm(x, norm_w[layer, 3])
        x = x + 0.5 * swiglu(h, ffn_w_gate[layer, 1], ffn_w_up[layer, 1], ffn_w_down[layer, 1])
    return rmsnorm(x, final_norm_w)
```

```python
import functools

import numpy as np
import jax
import jax.numpy as jnp
from jax import lax
from jax.experimental import pallas as pl
from jax.experimental.pallas import tpu as pltpu

F32 = jnp.float32
BF16 = jnp.bfloat16

D_MODEL = 1024
DEPTH = 4
D_FF = 2816
NORM_EPS = 1e-6
ATTN_BLOCK = 128

RET_HEADS = 4
RET_DK = 128
RET_DV = 128
RET_CHUNK = 128
RET_ROPE_BASE = 10000.0

SSM_HEADS = 8
SSM_HEAD_DIM = 64
SSM_D_INNER = SSM_HEADS * SSM_HEAD_DIM
SSM_D_STATE = 64
SSM_GROUPS = 2
SSM_CONV = 4
SSM_CHUNK = 128
SSM_CONV_DIM = SSM_D_INNER + 2 * SSM_GROUPS * SSM_D_STATE

EVEN_SPLITS = (RET_HEADS * RET_DK, RET_HEADS * RET_DK, RET_HEADS * RET_DV, RET_HEADS * RET_DV, SSM_D_INNER, SSM_CONV_DIM, SSM_HEADS)

NSA_HEADS = 8
NSA_KV_HEADS = 2
NSA_HEAD_DIM = 64
NSA_CMP_BLOCK = 32
NSA_CMP_STRIDE = 16
NSA_SEL_BLOCK = 64
NSA_TOPN = 16
NSA_WINDOW = 512
NSA_SEL_QBLOCK = 64
NSA_FORCE_SCORE = 1e6

SWA_HEADS = 8
SWA_KV_HEADS = 2
SWA_HEAD_DIM = 64
SWA_WINDOW = 128

ODD_SPLITS = (NSA_HEADS * NSA_HEAD_DIM,) + (NSA_KV_HEADS * NSA_HEAD_DIM,) * 6 + (3 * NSA_HEADS, SWA_HEADS * SWA_HEAD_DIM, SWA_KV_HEADS * SWA_HEAD_DIM, SWA_KV_HEADS * SWA_HEAD_DIM)

XATTN_HEADS = 4
XATTN_HEAD_DIM = D_MODEL // XATTN_HEADS

VMEM_LIMIT_BYTES = 52 * 1024 * 1024
TOKEN_TILE = 512
FFN_TILE = 1408


def _compiler_params(semantics):
    return pltpu.CompilerParams(dimension_semantics=semantics, vmem_limit_bytes=VMEM_LIMIT_BYTES)


def _rms(x, w):
    return x * lax.rsqrt(jnp.mean(x * x, axis=-1, keepdims=True) + NORM_EPS) * w


def _ffn_kernel(x_ref, nw_ref, wg_ref, wu_ref, wd_ref, o_ref, h_ref, acc_ref):
    j = pl.program_id(1)

    @pl.when(j == 0)
    def _():
        h_ref[...] = _rms(x_ref[...], nw_ref[...]).astype(BF16)
        acc_ref[...] = jnp.zeros_like(acc_ref)

    h = h_ref[...]
    g = jnp.dot(h, wg_ref[...], preferred_element_type=F32)
    u = jnp.dot(h, wu_ref[...], preferred_element_type=F32)
    a = (g * jax.nn.sigmoid(g) * u).astype(BF16)
    acc_ref[...] += jnp.dot(a, wd_ref[...], preferred_element_type=F32)

    @pl.when(j == pl.num_programs(1) - 1)
    def _():
        o_ref[...] = x_ref[...] + 0.5 * acc_ref[...]


def _ffn(x, nw, wg, wu, wd, layer, k):
    t, d = x.shape
    tm, tf = TOKEN_TILE, FFN_TILE
    return pl.pallas_call(
        _ffn_kernel,
        grid=(t // tm, D_FF // tf),
        in_specs=[
            pl.BlockSpec((tm, d), lambda i, j: (i, 0)),
            pl.BlockSpec((1, d), lambda i, j: (0, 0)),
            pl.BlockSpec((None, None, d, tf), lambda i, j: (layer, k, 0, j)),
            pl.BlockSpec((None, None, d, tf), lambda i, j: (layer, k, 0, j)),
            pl.BlockSpec((None, None, tf, d), lambda i, j: (layer, k, j, 0)),
        ],
        out_specs=pl.BlockSpec((tm, d), lambda i, j: (i, 0)),
        out_shape=jax.ShapeDtypeStruct((t, d), F32),
        scratch_shapes=[pltpu.VMEM((tm, d), BF16), pltpu.VMEM((tm, d), F32)],
        compiler_params=_compiler_params(("parallel", "arbitrary")),
        name="ffn",
    )(x, nw, wg, wu, wd)


def _norm_proj_kernel(n_out, x_ref, nw_ref, *refs):
    w_refs, o_refs = refs[:n_out], refs[n_out:]
    h = _rms(x_ref[...], nw_ref[...]).astype(BF16)
    for w_ref, o_ref in zip(w_refs, o_refs):
        o_ref[...] = jnp.dot(h, w_ref[...], preferred_element_type=F32)


def _norm_proj(x, nw, weights, tm=TOKEN_TILE):
    t, d = x.shape
    n_out = len(weights)
    return pl.pallas_call(
        functools.partial(_norm_proj_kernel, n_out),
        grid=(t // tm,),
        in_specs=[pl.BlockSpec((tm, d), lambda i: (i, 0)), pl.BlockSpec((1, d), lambda i: (0, 0))]
        + [pl.BlockSpec(w.shape, lambda i: (0, 0)) for w in weights],
        out_specs=[pl.BlockSpec((tm, w.shape[1]), lambda i: (i, 0)) for w in weights],
        out_shape=[jax.ShapeDtypeStruct((t, w.shape[1]), F32) for w in weights],
        compiler_params=_compiler_params(("parallel",)),
        name="norm_proj",
    )(x, nw, *weights)


def _proj_residual_kernel(x_ref, a_ref, w_ref, o_ref):
    o_ref[...] = x_ref[...] + jnp.dot(a_ref[...].astype(BF16), w_ref[...], preferred_element_type=F32)


def _proj_residual(x, a, w, tm=TOKEN_TILE):
    t, d = x.shape
    kdim = a.shape[1]
    return pl.pallas_call(
        _proj_residual_kernel,
        grid=(t // tm,),
        in_specs=[pl.BlockSpec((tm, d), lambda i: (i, 0)), pl.BlockSpec((tm, kdim), lambda i: (i, 0)),
                  pl.BlockSpec((kdim, d), lambda i: (0, 0))],
        out_specs=pl.BlockSpec((tm, d), lambda i: (i, 0)),
        out_shape=jax.ShapeDtypeStruct((t, d), F32),
        compiler_params=_compiler_params(("parallel",)),
        name="proj_residual",
    )(x, a, w)


def _xattn_kernel(x_ref, nw_ref, wq_ref, kv_ref, wo_ref, o_ref):
    x = x_ref[...]
    h = _rms(x, nw_ref[...]).astype(BF16)
    q = jnp.dot(h, wq_ref[...], preferred_element_type=F32).astype(BF16)
    hd = XATTN_HEAD_DIM
    outs = []
    for hh in range(XATTN_HEADS):
        k = kv_ref[:, hh * hd:(hh + 1) * hd].astype(BF16)
        v = kv_ref[:, D_MODEL + hh * hd:D_MODEL + (hh + 1) * hd].astype(BF16)
        s = lax.dot_general(q[:, hh * hd:(hh + 1) * hd], k, (((1,), (1,)), ((), ())), preferred_element_type=F32)
        s = s * (hd ** -0.5)
        p = jnp.exp(s - jnp.max(s, axis=-1, keepdims=True))
        p = p / jnp.sum(p, axis=-1, keepdims=True)
        outs.append(jnp.dot(p.astype(BF16), v, preferred_element_type=F32).astype(BF16))
    o = jnp.concatenate(outs, axis=-1)
    o_ref[...] = x + jnp.dot(o, wo_ref[...], preferred_element_type=F32)


def _xattn(x, nw, wq, kv, wo, bsz, tm=TOKEN_TILE):
    t, d = x.shape
    seq = t // bsz
    nt = seq // tm
    m = kv.shape[1]
    return pl.pallas_call(
        _xattn_kernel,
        grid=(bsz, nt),
        in_specs=[pl.BlockSpec((tm, d), lambda b, i: (b * nt + i, 0)), pl.BlockSpec((1, d), lambda b, i: (0, 0)),
                  pl.BlockSpec((d, d), lambda b, i: (0, 0)), pl.BlockSpec((None, m, 2 * d), lambda b, i: (b, 0, 0)),
                  pl.BlockSpec((d, d), lambda b, i: (0, 0))],
        out_specs=pl.BlockSpec((tm, d), lambda b, i: (b * nt + i, 0)),
        out_shape=jax.ShapeDtypeStruct((t, d), F32),
        compiler_params=_compiler_params(("parallel", "parallel")),
        name="xattn",
    )(x, nw, wq, kv, wo)


def _final_norm_kernel(x_ref, nw_ref, o_ref):
    o_ref[...] = _rms(x_ref[...], nw_ref[...])


def _final_norm(x, nw, tm=TOKEN_TILE):
    t, d = x.shape
    return pl.pallas_call(
        _final_norm_kernel,
        grid=(t // tm,),
        in_specs=[pl.BlockSpec((tm, d), lambda i: (i, 0)), pl.BlockSpec((1, d), lambda i: (0, 0))],
        out_specs=pl.BlockSpec((tm, d), lambda i: (i, 0)),
        out_shape=jax.ShapeDtypeStruct((t, d), F32),
        compiler_params=_compiler_params(("parallel",)),
        name="final_norm",
    )(x, nw)


def _split_last(t, sizes):
    cuts = [int(c) for c in np.cumsum(sizes)[:-1]]
    return jnp.split(t, cuts, axis=-1)


def _masked_softmax(s, mask):
    s = jnp.where(mask, s, -jnp.inf)
    m = jnp.max(s, axis=-1, keepdims=True)
    m = jnp.where(jnp.isfinite(m), m, 0.0)
    p = jnp.exp(s - m)
    return p / jnp.maximum(jnp.sum(p, axis=-1, keepdims=True), 1e-30)


def _sink_softmax(s, mask, sink):
    s = jnp.where(mask, s, -jnp.inf)
    m = jnp.maximum(jnp.max(s, axis=-1, keepdims=True), sink)
    p = jnp.exp(s - m)
    return p / (jnp.sum(p, axis=-1, keepdims=True) + jnp.exp(sink - m))


def _rotate_pairs(t, cos, sin):
    t1 = t[..., 0::2]
    t2 = t[..., 1::2]
    return jnp.stack([t1 * cos - t2 * sin, t1 * sin + t2 * cos], axis=-1).reshape(t.shape)


def _retention(q, k, v):
    bsz, seq, nh, dk = q.shape
    dv = v.shape[-1]
    L = RET_CHUNK
    nc = seq // L
    cdt = q.dtype
    pos = jnp.arange(seq, dtype=jnp.float32)
    inv_freq = 1.0 / (RET_ROPE_BASE ** jnp.linspace(0.0, 1.0, dk // 2, dtype=jnp.float32))
    ang = pos[:, None] * inv_freq[None, :]
    cos = jnp.cos(ang)[:, None, :].astype(cdt)
    sin = jnp.sin(ang)[:, None, :].astype(cdt)
    q = _rotate_pairs(q, cos, sin)
    k = _rotate_pairs(k, cos, sin) * (dk ** -0.5)
    log_g = jnp.log1p(-jnp.exp2(-5.0 - jnp.arange(nh, dtype=jnp.float32)))
    idx = jnp.arange(L, dtype=jnp.float32)
    diff = idx[:, None] - idx[None, :]
    dmat = jnp.where(diff >= 0, jnp.exp(jnp.maximum(diff, 0.0)[None] * log_g[:, None, None]), 0.0).astype(cdt)
    zeta = jnp.exp((L - 1 - idx)[None, :] * log_g[:, None]).astype(cdt)
    xi = jnp.exp((idx + 1)[None, :] * log_g[:, None]).astype(cdt)
    chunk_decay = jnp.exp(L * log_g).astype(cdt)
    qc = q.reshape(bsz, nc, L, nh, dk)
    kc = k.reshape(bsz, nc, L, nh, dk)
    vc = v.reshape(bsz, nc, L, nh, dv)
    scores = jnp.einsum('bcihd,bcjhd->bchij', qc, kc) * dmat
    inner = jnp.einsum('bchij,bcjhe->bcihe', scores, vc)
    chunk_kv = jnp.einsum('bcjhd,hj,bcjhe->cbhde', kc, zeta, vc)

    def step(state, kv):
        return chunk_decay[None, :, None, None] * state + kv, state

    _, s_in = lax.scan(step, jnp.zeros((bsz, nh, dk, dv), chunk_kv.dtype), chunk_kv)
    cross = jnp.einsum('bcihd,hi,cbhde->bcihe', qc, xi, s_in)
    o = (inner + cross).reshape(bsz, seq, nh, dv).astype(jnp.float32)
    o = o * lax.rsqrt(jnp.mean(o * o, axis=-1, keepdims=True) + NORM_EPS)
    return o.astype(cdt)


def _ssd_scan(xh, dt, a, bm, cm):
    bsz, seq, nh, hp = xh.shape
    ng, ns = bm.shape[2], bm.shape[3]
    ne = nh // ng
    L = SSM_CHUNK
    nc = seq // L
    cdt = xh.dtype
    cums = jnp.cumsum((dt * a).reshape(bsz, nc, L, ng, ne), axis=2)
    xdt = (xh * dt[..., None].astype(cdt)).reshape(bsz, nc, L, ng, ne, hp)
    bc = bm.reshape(bsz, nc, L, ng, ns)
    cc = cm.reshape(bsz, nc, L, ng, ns)
    idx = jnp.arange(L)
    causal = (idx[:, None] >= idx[None, :])[:, :, None, None]
    seg = cums[:, :, :, None] - cums[:, :, None, :]
    lmat = jnp.exp(jnp.where(causal, seg, -jnp.inf)).astype(cdt)
    cb = jnp.einsum('bclgn,bcsgn->bclsg', cc, bc)
    y_diag = jnp.einsum('bclsg,bclsge,bcsgep->bclgep', cb, lmat, xdt)
    decay_states = jnp.exp(cums[:, :, -1:] - cums).astype(cdt)
    states = jnp.einsum('bclgn,bclge,bclgep->cbgepn', bc, decay_states, xdt)
    chunk_decay = jnp.exp(cums[:, :, -1]).astype(cdt).transpose(1, 0, 2, 3)

    def step(state, inp):
        dec, st = inp
        return dec[..., None, None] * state + st, state

    _, s_in = lax.scan(step, jnp.zeros((bsz, ng, ne, hp, ns), states.dtype), (chunk_decay, states))
    y_off = jnp.einsum('bclgn,cbgepn,bclge->bclgep', cc, s_in, jnp.exp(cums).astype(cdt))
    return (y_diag + y_off).reshape(bsz, seq, nh, hp)


def _even_core(q, k, v, g, z, xbc, dt_raw, conv_w, conv_b, dt_bias, a_log, d_skip, norm_w):
    bsz, seq, _ = q.shape
    o_ret = _retention(q.reshape(bsz, seq, RET_HEADS, RET_DK), k.reshape(bsz, seq, RET_HEADS, RET_DK), v.reshape(bsz, seq, RET_HEADS, RET_DV))
    o_ret = jax.nn.silu(g) * o_ret.reshape(bsz, seq, RET_HEADS * RET_DV)
    xbc = lax.conv_general_dilated(xbc, conv_w[:, None, :].astype(xbc.dtype), window_strides=(1,), padding=[(SSM_CONV - 1, 0)], dimension_numbers=('NWC', 'WIO', 'NWC'), feature_group_count=SSM_CONV_DIM)
    xbc = jax.nn.silu(xbc + conv_b)
    xs, bm, cm = _split_last(xbc, (SSM_D_INNER, SSM_GROUPS * SSM_D_STATE, SSM_GROUPS * SSM_D_STATE))
    dt = jax.nn.softplus(dt_raw.astype(jnp.float32) + dt_bias.astype(jnp.float32))
    a = -jnp.exp(a_log.astype(jnp.float32))
    xh = xs.reshape(bsz, seq, SSM_HEADS, SSM_HEAD_DIM)
    y = _ssd_scan(xh, dt, a, bm.reshape(bsz, seq, SSM_GROUPS, SSM_D_STATE), cm.reshape(bsz, seq, SSM_GROUPS, SSM_D_STATE))
    y = (y + d_skip[:, None] * xh).reshape(bsz, seq, SSM_D_INNER)
    y = y * jax.nn.silu(z)
    y = y * lax.rsqrt(jnp.mean(y * y, axis=-1, keepdims=True) + NORM_EPS) * norm_w
    return jnp.concatenate([o_ret, y], axis=-1)


def _banded_attention(q, k, v, window, sinks=None):
    bsz, ng, ne, seq, hd = q.shape
    qb_size = ATTN_BLOCK
    nb = seq // qb_size
    nprev = -(-(window - 1) // qb_size)
    nk = (nprev + 1) * qb_size

    def band(t):
        tb = jnp.pad(t.reshape(bsz, ng, nb, qb_size, hd), ((0, 0), (0, 0), (nprev, 0), (0, 0), (0, 0)))
        return jnp.concatenate([tb[:, :, j:j + nb] for j in range(nprev + 1)], axis=3)

    kb, vb = band(k), band(v)
    blk = jnp.arange(nb)
    qpos = blk[:, None] * qb_size + jnp.arange(qb_size)[None, :]
    kpos = (blk[:, None] - nprev) * qb_size + jnp.arange(nk)[None, :]
    rel = qpos[:, :, None] - kpos[:, None, :]
    mask = (rel >= 0) & (rel < window) & (kpos[:, None, :] >= 0)
    s = jnp.einsum('bgeiqd,bgikd->bgeiqk', q.reshape(bsz, ng, ne, nb, qb_size, hd), kb).astype(jnp.float32) * (hd ** -0.5)
    if sinks is None:
        p = _masked_softmax(s, mask)
    else:
        p = _sink_softmax(s, mask, sinks.astype(jnp.float32)[None, :, :, None, None, None])
    o = jnp.einsum('bgeiqk,bgikd->bgeiqd', p.astype(v.dtype), vb)
    return o.reshape(bsz, ng, ne, seq, hd)


def _selected_block_attention(q, k, v, sel_idx):
    bsz, ng, ne, seq, hd = q.shape
    n = sel_idx.shape[-1]
    qb_size = NSA_SEL_QBLOCK
    nq = seq // qb_size
    kb = k.reshape(bsz, ng, seq // NSA_SEL_BLOCK, NSA_SEL_BLOCK, hd)
    vb = v.reshape(bsz, ng, seq // NSA_SEL_BLOCK, NSA_SEL_BLOCK, hd)
    q_ch = q.reshape(bsz, ng, ne, nq, qb_size, hd).transpose(3, 0, 1, 2, 4, 5)
    idx_ch = sel_idx.reshape(bsz, ng, nq, qb_size, n).transpose(2, 0, 1, 3, 4)
    t_ch = jnp.arange(seq).reshape(nq, qb_size)
    bi = jnp.arange(bsz)[:, None, None, None]
    gi = jnp.arange(ng)[None, :, None, None]
    offs = jnp.arange(NSA_SEL_BLOCK)
    scale = hd ** -0.5

    def body(args):
        qc, ic, tc = args
        kg = kb[bi, gi, ic].reshape(bsz, ng, qb_size, n * NSA_SEL_BLOCK, hd)
        vg = vb[bi, gi, ic].reshape(bsz, ng, qb_size, n * NSA_SEL_BLOCK, hd)
        s = jnp.einsum('bgeqd,bgqmd->bgeqm', qc, kg).astype(jnp.float32) * scale
        kpos = (ic[..., None] * NSA_SEL_BLOCK + offs).reshape(bsz, ng, qb_size, n * NSA_SEL_BLOCK)
        mask = (kpos <= tc[None, None, :, None])[:, :, None]
        p = _masked_softmax(s, mask)
        return jnp.einsum('bgeqm,bgqmd->bgeqd', p.astype(vg.dtype), vg)

    o = lax.map(body, (q_ch, idx_ch, t_ch))
    return o.transpose(1, 2, 3, 0, 4, 5).reshape(bsz, ng, ne, seq, hd)


def _native_sparse_attention(q, kc, vc, ks, vs, kw, vw, g, cmp_pe, cmp_w1, cmp_w2):
    bsz, ng, ne, seq, hd = q.shape
    scale = hd ** -0.5
    tpos = jnp.arange(seq)
    n_cmp = (seq - NSA_CMP_BLOCK) // NSA_CMP_STRIDE + 1
    starts = jnp.arange(n_cmp) * NSA_CMP_STRIDE
    tok = starts[:, None] + jnp.arange(NSA_CMP_BLOCK)[None, :]

    def compress(t, pe, w1, w2):
        blk = t[:, :, tok] + pe
        return jax.nn.silu(blk.reshape(bsz, ng, n_cmp, NSA_CMP_BLOCK * hd) @ w1) @ w2

    k_cmp = compress(kc, cmp_pe[0], cmp_w1[0], cmp_w2[0])
    v_cmp = compress(vc, cmp_pe[1], cmp_w1[1], cmp_w2[1])
    s_cmp = jnp.einsum('bgeqd,bgnd->bgeqn', q, k_cmp).astype(jnp.float32) * scale
    p_cmp = _masked_softmax(s_cmp, (starts + NSA_CMP_BLOCK - 1)[None, :] <= tpos[:, None])
    o_cmp = jnp.einsum('bgeqn,bgnd->bgeqd', p_cmp.astype(vc.dtype), v_cmp)
    n_sel = seq // NSA_SEL_BLOCK
    sel_start = jnp.arange(n_sel) * NSA_SEL_BLOCK
    overlap = ((starts[:, None] < sel_start[None, :] + NSA_SEL_BLOCK) & (starts[:, None] + NSA_CMP_BLOCK > sel_start[None, :])).astype(jnp.float32)
    imp = jnp.einsum('bgqn,nj->bgqj', jnp.sum(p_cmp, axis=2), overlap)
    cur = (tpos // NSA_SEL_BLOCK)[:, None]
    jb = jnp.arange(n_sel)[None, :]
    valid = jb <= cur
    forced = valid & ((jb == 0) | (jb == cur) | (jb == cur - 1))
    score = jnp.where(forced, NSA_FORCE_SCORE, jnp.where(valid, imp, -jnp.inf))
    _, sel_idx = lax.top_k(score, min(NSA_TOPN, n_sel))
    o_sel = _selected_block_attention(q, ks, vs, sel_idx)
    o_win = _banded_attention(q, kw, vw, NSA_WINDOW)
    return g[..., 0:1] * o_cmp + g[..., 1:2] * o_sel + g[..., 2:3] * o_win


def _odd_core(q_n, kc, vc, ks, vs, kw, vw, gates, q_s, k_s, v_s, cmp_pe, cmp_w1, cmp_w2, sinks):
    bsz, seq, _ = q_n.shape

    def q_heads(t, nh, nkv, hd):
        return t.reshape(bsz, seq, nkv, nh // nkv, hd).transpose(0, 2, 3, 1, 4)

    def kv_heads(t, nkv, hd):
        return t.reshape(bsz, seq, nkv, hd).transpose(0, 2, 1, 3)

    def merge(o):
        return o.transpose(0, 3, 1, 2, 4).reshape(bsz, seq, -1)

    g = jax.nn.sigmoid(gates.astype(jnp.float32))
    g = g.reshape(bsz, seq, NSA_KV_HEADS, NSA_HEADS // NSA_KV_HEADS, 3).transpose(0, 2, 3, 1, 4)
    o_nsa = _native_sparse_attention(
        q_heads(q_n, NSA_HEADS, NSA_KV_HEADS, NSA_HEAD_DIM),
        kv_heads(kc, NSA_KV_HEADS, NSA_HEAD_DIM), kv_heads(vc, NSA_KV_HEADS, NSA_HEAD_DIM),
        kv_heads(ks, NSA_KV_HEADS, NSA_HEAD_DIM), kv_heads(vs, NSA_KV_HEADS, NSA_HEAD_DIM),
        kv_heads(kw, NSA_KV_HEADS, NSA_HEAD_DIM), kv_heads(vw, NSA_KV_HEADS, NSA_HEAD_DIM),
        g, cmp_pe, cmp_w1, cmp_w2)
    o_swa = _banded_attention(
        q_heads(q_s, SWA_HEADS, SWA_KV_HEADS, SWA_HEAD_DIM),
        kv_heads(k_s, SWA_KV_HEADS, SWA_HEAD_DIM), kv_heads(v_s, SWA_KV_HEADS, SWA_HEAD_DIM),
        SWA_WINDOW, sinks.reshape(SWA_KV_HEADS, SWA_HEADS // SWA_KV_HEADS))
    return jnp.concatenate([merge(o_nsa), merge(o_swa)], axis=-1)


def kernel(x, mem, norm_w, final_norm_w, mem_norm_w, ffn_w_gate, ffn_w_up, ffn_w_down, xattn_wq, xattn_wkv, xattn_wo, even_w_in, even_w_out, ssm_conv_w, ssm_conv_b, ssm_dt_bias, ssm_a_log, ssm_d, ssm_norm_w, odd_w_in, odd_w_out, nsa_cmp_pe, nsa_cmp_w1, nsa_cmp_w2, swa_sinks):
    bsz, seq, d = x.shape
    t = bsz * seq
    m = mem.shape[1]
    wg = ffn_w_gate.astype(BF16)
    wu = ffn_w_up.astype(BF16)
    wd = ffn_w_down.astype(BF16)
    wq = xattn_wq.astype(BF16)
    wkv = xattn_wkv.astype(BF16)
    wo = xattn_wo.astype(BF16)
    even_in = even_w_in.astype(BF16)
    even_out = even_w_out.astype(BF16)
    odd_in = odd_w_in.astype(BF16)
    odd_out = odd_w_out.astype(BF16)
    even_cuts = [int(c) for c in np.cumsum((0,) + EVEN_SPLITS)]
    odd_cuts = [int(c) for c in np.cumsum((0,) + ODD_SPLITS)]

    xt = x.reshape(t, d)
    mem2 = mem.reshape(bsz * m, d)
    mem_nw = mem_norm_w.reshape(1, d)
    for layer in range(DEPTH):
        i = layer // 2
        xt = _ffn(xt, norm_w[layer, 0].reshape(1, d), wg, wu, wd, layer, 0)
        nw1 = norm_w[layer, 1].reshape(1, d)
        if layer % 2 == 0:
            ws = [even_in[i][:, a:b] for a, b in zip(even_cuts[:-1], even_cuts[1:])]
            parts = _norm_proj(xt, nw1, ws)
            parts = [p.reshape(bsz, seq, -1) for p in parts]
            mix = _even_core(*parts, ssm_conv_w[i], ssm_conv_b[i], ssm_dt_bias[i], ssm_a_log[i], ssm_d[i], ssm_norm_w[i])
            xt = _proj_residual(xt, mix.reshape(t, -1), even_out[i])
        else:
            ws = [odd_in[i][:, a:b] for a, b in zip(odd_cuts[:-1], odd_cuts[1:])]
            parts = _norm_proj(xt, nw1, ws)
            parts = [p.reshape(bsz, seq, -1) for p in parts]
            mix = _odd_core(*parts, nsa_cmp_pe[i], nsa_cmp_w1[i], nsa_cmp_w2[i], swa_sinks[i])
            xt = _proj_residual(xt, mix.reshape(t, -1), odd_out[i])
        (kv,) = _norm_proj(mem2, mem_nw, [wkv[layer]])
        xt = _xattn(xt, norm_w[layer, 2].reshape(1, d), wq[layer], kv.reshape(bsz, m, 2 * d), wo[layer], bsz)
        xt = _ffn(xt, norm_w[layer, 3].reshape(1, d), wg, wu, wd, layer, 1)
    out = _final_norm(xt, final_norm_w.reshape(1, d))
    return out.reshape(bsz, seq, d)
```

```python
import functools

import numpy as np
import jax
import jax.numpy as jnp
from jax import lax
from jax.experimental import pallas as pl
from jax.experimental.pallas import tpu as pltpu

F32 = jnp.float32
BF16 = jnp.bfloat16

D_MODEL = 1024
DEPTH = 4
D_FF = 2816
NORM_EPS = 1e-6
ATTN_BLOCK = 128

RET_HEADS = 4
RET_DK = 128
RET_DV = 128
RET_CHUNK = 128
RET_ROPE_BASE = 10000.0

SSM_HEADS = 8
SSM_HEAD_DIM = 64
SSM_D_INNER = SSM_HEADS * SSM_HEAD_DIM
SSM_D_STATE = 64
SSM_GROUPS = 2
SSM_CONV = 4
SSM_CHUNK = 128
SSM_CONV_DIM = SSM_D_INNER + 2 * SSM_GROUPS * SSM_D_STATE

EVEN_SPLITS = (RET_HEADS * RET_DK, RET_HEADS * RET_DK, RET_HEADS * RET_DV, RET_HEADS * RET_DV, SSM_D_INNER, SSM_CONV_DIM, SSM_HEADS)

NSA_HEADS = 8
NSA_KV_HEADS = 2
NSA_HEAD_DIM = 64
NSA_CMP_BLOCK = 32
NSA_CMP_STRIDE = 16
NSA_CMP_HIDDEN = 256
NSA_SEL_BLOCK = 64
NSA_TOPN = 16
NSA_WINDOW = 512
NSA_FORCE_SCORE = 1e6

SWA_HEADS = 8
SWA_KV_HEADS = 2
SWA_HEAD_DIM = 64
SWA_WINDOW = 128

ODD_SPLITS = (NSA_HEADS * NSA_HEAD_DIM,) + (NSA_KV_HEADS * NSA_HEAD_DIM,) * 6 + (3 * NSA_HEADS, SWA_HEADS * SWA_HEAD_DIM, SWA_KV_HEADS * SWA_HEAD_DIM, SWA_KV_HEADS * SWA_HEAD_DIM)

XATTN_HEADS = 4
XATTN_HEAD_DIM = D_MODEL // XATTN_HEADS

VMEM_LIMIT_BYTES = 52 * 1024 * 1024
TOKEN_TILE = 512
FFN_TILE = 1408
LANES = 128
MASKED = -1e30

_NT = (((1,), (1,)), ((), ()))


def _compiler_params(semantics):
    return pltpu.CompilerParams(dimension_semantics=semantics, vmem_limit_bytes=VMEM_LIMIT_BYTES)


def _rms(x, w):
    return x * lax.rsqrt(jnp.mean(x * x, axis=-1, keepdims=True) + NORM_EPS) * w


def _ffn_kernel(x_ref, nw_ref, wg_ref, wu_ref, wd_ref, o_ref, h_ref, acc_ref):
    j = pl.program_id(1)

    @pl.when(j == 0)
    def _():
        h_ref[...] = _rms(x_ref[...], nw_ref[...]).astype(BF16)
        acc_ref[...] = jnp.zeros_like(acc_ref)

    h = h_ref[...]
    g = jnp.dot(h, wg_ref[...], preferred_element_type=F32)
    u = jnp.dot(h, wu_ref[...], preferred_element_type=F32)
    a = (g * jax.nn.sigmoid(g) * u).astype(BF16)
    acc_ref[...] += jnp.dot(a, wd_ref[...], preferred_element_type=F32)

    @pl.when(j == pl.num_programs(1) - 1)
    def _():
        o_ref[...] = x_ref[...] + 0.5 * acc_ref[...]


def _ffn(x, nw, wg, wu, wd, layer, k):
    t, d = x.shape
    tm, tf = TOKEN_TILE, FFN_TILE
    return pl.pallas_call(
        _ffn_kernel,
        grid=(t // tm, D_FF // tf),
        in_specs=[
            pl.BlockSpec((tm, d), lambda i, j: (i, 0)),
            pl.BlockSpec((1, d), lambda i, j: (0, 0)),
            pl.BlockSpec((None, None, d, tf), lambda i, j: (layer, k, 0, j)),
            pl.BlockSpec((None, None, d, tf), lambda i, j: (layer, k, 0, j)),
            pl.BlockSpec((None, None, tf, d), lambda i, j: (layer, k, j, 0)),
        ],
        out_specs=pl.BlockSpec((tm, d), lambda i, j: (i, 0)),
        out_shape=jax.ShapeDtypeStruct((t, d), F32),
        scratch_shapes=[pltpu.VMEM((tm, d), BF16), pltpu.VMEM((tm, d), F32)],
        compiler_params=_compiler_params(("parallel", "arbitrary")),
        name="ffn",
    )(x, nw, wg, wu, wd)


def _norm_proj_kernel(head_dims, x_ref, nw_ref, *refs):
    n_out = len(head_dims)
    w_refs, o_refs = refs[:n_out], refs[n_out:]
    h = _rms(x_ref[...], nw_ref[...]).astype(BF16)
    for hd, w_ref, o_ref in zip(head_dims, w_refs, o_refs):
        y = jnp.dot(h, w_ref[...], preferred_element_type=F32)
        if hd is None:
            o_ref[...] = y.astype(o_ref.dtype)
        else:
            for j in range(o_ref.shape[0]):
                o_ref[j] = y[:, j * hd:(j + 1) * hd].astype(o_ref.dtype)


def _norm_proj(x, nw, weights, head_dims=None, dtypes=None, tm=TOKEN_TILE):
    t, d = x.shape
    n_out = len(weights)
    head_dims = tuple(head_dims) if head_dims is not None else (None,) * n_out
    dtypes = tuple(dtypes) if dtypes is not None else (F32,) * n_out
    out_specs, out_shape = [], []
    for w, hd, dt in zip(weights, head_dims, dtypes):
        n = w.shape[1]
        if hd is None:
            out_specs.append(pl.BlockSpec((tm, n), lambda i: (i, 0)))
            out_shape.append(jax.ShapeDtypeStruct((t, n), dt))
        else:
            out_specs.append(pl.BlockSpec((n // hd, tm, hd), lambda i: (0, i, 0)))
            out_shape.append(jax.ShapeDtypeStruct((n // hd, t, hd), dt))
    return pl.pallas_call(
        functools.partial(_norm_proj_kernel, head_dims),
        grid=(t // tm,),
        in_specs=[pl.BlockSpec((tm, d), lambda i: (i, 0)), pl.BlockSpec((1, d), lambda i: (0, 0))]
        + [pl.BlockSpec(w.shape, lambda i: (0, 0)) for w in weights],
        out_specs=out_specs,
        out_shape=out_shape,
        compiler_params=_compiler_params(("parallel",)),
        name="norm_proj",
    )(x, nw, *weights)


def _proj_residual_kernel(group_sizes, x_ref, *refs):
    n_a = sum(group_sizes)
    a_refs, w_refs, o_ref = refs[:n_a], refs[n_a:-1], refs[-1]
    acc = x_ref[...]
    pos = 0
    for size, w_ref in zip(group_sizes, w_refs):
        a = a_refs[pos][...]
        for r in a_refs[pos + 1:pos + size]:
            a = a + r[...]
        pos += size
        acc = acc + jnp.dot(a.astype(BF16), w_ref[...], preferred_element_type=F32)
    o_ref[...] = acc


def _proj_residual(x, groups, weights, tm=TOKEN_TILE):
    t, d = x.shape
    arrays = [a for grp in groups for a in grp]
    return pl.pallas_call(
        functools.partial(_proj_residual_kernel, tuple(len(grp) for grp in groups)),
        grid=(t // tm,),
        in_specs=[pl.BlockSpec((tm, d), lambda i: (i, 0))]
        + [pl.BlockSpec((tm, a.shape[1]), lambda i: (i, 0)) for a in arrays]
        + [pl.BlockSpec(w.shape, lambda i: (0, 0)) for w in weights],
        out_specs=pl.BlockSpec((tm, d), lambda i: (i, 0)),
        out_shape=jax.ShapeDtypeStruct((t, d), F32),
        compiler_params=_compiler_params(("parallel",)),
        name="proj_residual",
    )(x, *arrays, *weights)


def _xattn_kernel(x_ref, nw_ref, wq_ref, kv_ref, wo_ref, o_ref):
    x = x_ref[...]
    h = _rms(x, nw_ref[...]).astype(BF16)
    q = jnp.dot(h, wq_ref[...], preferred_element_type=F32).astype(BF16)
    hd = XATTN_HEAD_DIM
    outs = []
    for hh in range(XATTN_HEADS):
        k = kv_ref[:, hh * hd:(hh + 1) * hd].astype(BF16)
        v = kv_ref[:, D_MODEL + hh * hd:D_MODEL + (hh + 1) * hd].astype(BF16)
        s = lax.dot_general(q[:, hh * hd:(hh + 1) * hd], k, _NT, preferred_element_type=F32)
        s = s * (hd ** -0.5)
        p = jnp.exp(s - jnp.max(s, axis=-1, keepdims=True))
        p = p / jnp.sum(p, axis=-1, keepdims=True)
        outs.append(jnp.dot(p.astype(BF16), v, preferred_element_type=F32).astype(BF16))
    o = jnp.concatenate(outs, axis=-1)
    o_ref[...] = x + jnp.dot(o, wo_ref[...], preferred_element_type=F32)


def _xattn(x, nw, wq, kv, wo, bsz, tm=TOKEN_TILE):
    t, d = x.shape
    seq = t // bsz
    nt = seq // tm
    m = kv.shape[1]
    return pl.pallas_call(
        _xattn_kernel,
        grid=(bsz, nt),
        in_specs=[pl.BlockSpec((tm, d), lambda b, i: (b * nt + i, 0)), pl.BlockSpec((1, d), lambda b, i: (0, 0)),
                  pl.BlockSpec((d, d), lambda b, i: (0, 0)), pl.BlockSpec((None, m, 2 * d), lambda b, i: (b, 0, 0)),
                  pl.BlockSpec((d, d), lambda b, i: (0, 0))],
        out_specs=pl.BlockSpec((tm, d), lambda b, i: (b * nt + i, 0)),
        out_shape=jax.ShapeDtypeStruct((t, d), F32),
        compiler_params=_compiler_params(("parallel", "parallel")),
        name="xattn",
    )(x, nw, wq, kv, wo)


def _final_norm_kernel(x_ref, nw_ref, o_ref):
    o_ref[...] = _rms(x_ref[...], nw_ref[...])


def _final_norm(x, nw, tm=TOKEN_TILE):
    t, d = x.shape
    return pl.pallas_call(
        _final_norm_kernel,
        grid=(t // tm,),
        in_specs=[pl.BlockSpec((tm, d), lambda i: (i, 0)), pl.BlockSpec((1, d), lambda i: (0, 0))],
        out_specs=pl.BlockSpec((tm, d), lambda i: (i, 0)),
        out_shape=jax.ShapeDtypeStruct((t, d), F32),
        compiler_params=_compiler_params(("parallel",)),
        name="final_norm",
    )(x, nw)


def _split_last(t, sizes):
    cuts = [int(c) for c in np.cumsum(sizes)[:-1]]
    return jnp.split(t, cuts, axis=-1)


def _rotate_pairs(t, cos, sin):
    t1 = t[..., 0::2]
    t2 = t[..., 1::2]
    return jnp.stack([t1 * cos - t2 * sin, t1 * sin + t2 * cos], axis=-1).reshape(t.shape)


def _retention(q, k, v):
    bsz, seq, nh, dk = q.shape
    dv = v.shape[-1]
    L = RET_CHUNK
    nc = seq // L
    cdt = q.dtype
    pos = jnp.arange(seq, dtype=jnp.float32)
    inv_freq = 1.0 / (RET_ROPE_BASE ** jnp.linspace(0.0, 1.0, dk // 2, dtype=jnp.float32))
    ang = pos[:, None] * inv_freq[None, :]
    cos = jnp.cos(ang)[:, None, :].astype(cdt)
    sin = jnp.sin(ang)[:, None, :].astype(cdt)
    q = _rotate_pairs(q, cos, sin)
    k = _rotate_pairs(k, cos, sin) * (dk ** -0.5)
    log_g = jnp.log1p(-jnp.exp2(-5.0 - jnp.arange(nh, dtype=jnp.float32)))
    idx = jnp.arange(L, dtype=jnp.float32)
    diff = idx[:, None] - idx[None, :]
    dmat = jnp.where(diff >= 0, jnp.exp(jnp.maximum(diff, 0.0)[None] * log_g[:, None, None]), 0.0).astype(cdt)
    zeta = jnp.exp((L - 1 - idx)[None, :] * log_g[:, None]).astype(cdt)
    xi = jnp.exp((idx + 1)[None, :] * log_g[:, None]).astype(cdt)
    chunk_decay = jnp.exp(L * log_g).astype(cdt)
    qc = q.reshape(bsz, nc, L, nh, dk)
    kc = k.reshape(bsz, nc, L, nh, dk)
    vc = v.reshape(bsz, nc, L, nh, dv)
    scores = jnp.einsum('bcihd,bcjhd->bchij', qc, kc) * dmat
    inner = jnp.einsum('bchij,bcjhe->bcihe', scores, vc)
    chunk_kv = jnp.einsum('bcjhd,hj,bcjhe->cbhde', kc, zeta, vc)

    def step(state, kv):
        return chunk_decay[None, :, None, None] * state + kv, state

    _, s_in = lax.scan(step, jnp.zeros((bsz, nh, dk, dv), chunk_kv.dtype), chunk_kv)
    cross = jnp.einsum('bcihd,hi,cbhde->bcihe', qc, xi, s_in)
    o = (inner + cross).reshape(bsz, seq, nh, dv).astype(jnp.float32)
    o = o * lax.rsqrt(jnp.mean(o * o, axis=-1, keepdims=True) + NORM_EPS)
    return o.astype(cdt)


def _ssd_scan(xh, dt, a, bm, cm):
    bsz, seq, nh, hp = xh.shape
    ng, ns = bm.shape[2], bm.shape[3]
    ne = nh // ng
    L = SSM_CHUNK
    nc = seq // L
    cdt = xh.dtype
    cums = jnp.cumsum((dt * a).reshape(bsz, nc, L, ng, ne), axis=2)
    xdt = (xh * dt[..., None].astype(cdt)).reshape(bsz, nc, L, ng, ne, hp)
    bc = bm.reshape(bsz, nc, L, ng, ns)
    cc = cm.reshape(bsz, nc, L, ng, ns)
    idx = jnp.arange(L)
    causal = (idx[:, None] >= idx[None, :])[:, :, None, None]
    seg = cums[:, :, :, None] - cums[:, :, None, :]
    lmat = jnp.exp(jnp.where(causal, seg, -jnp.inf)).astype(cdt)
    cb = jnp.einsum('bclgn,bcsgn->bclsg', cc, bc)
    y_diag = jnp.einsum('bclsg,bclsge,bcsgep->bclgep', cb, lmat, xdt)
    decay_states = jnp.exp(cums[:, :, -1:] - cums).astype(cdt)
    states = jnp.einsum('bclgn,bclge,bclgep->cbgepn', bc, decay_states, xdt)
    chunk_decay = jnp.exp(cums[:, :, -1]).astype(cdt).transpose(1, 0, 2, 3)

    def step(state, inp):
        dec, st = inp
        return dec[..., None, None] * state + st, state

    _, s_in = lax.scan(step, jnp.zeros((bsz, ng, ne, hp, ns), states.dtype), (chunk_decay, states))
    y_off = jnp.einsum('bclgn,cbgepn,bclge->bclgep', cc, s_in, jnp.exp(cums).astype(cdt))
    return (y_diag + y_off).reshape(bsz, seq, nh, hp)


def _even_core(q, k, v, g, z, xbc, dt_raw, conv_w, conv_b, dt_bias, a_log, d_skip, norm_w):
    bsz, seq, _ = q.shape
    o_ret = _retention(q.reshape(bsz, seq, RET_HEADS, RET_DK), k.reshape(bsz, seq, RET_HEADS, RET_DK), v.reshape(bsz, seq, RET_HEADS, RET_DV))
    o_ret = jax.nn.silu(g) * o_ret.reshape(bsz, seq, RET_HEADS * RET_DV)
    xbc = lax.conv_general_dilated(xbc, conv_w[:, None, :].astype(xbc.dtype), window_strides=(1,), padding=[(SSM_CONV - 1, 0)], dimension_numbers=('NWC', 'WIO', 'NWC'), feature_group_count=SSM_CONV_DIM)
    xbc = jax.nn.silu(xbc + conv_b)
    xs, bm, cm = _split_last(xbc, (SSM_D_INNER, SSM_GROUPS * SSM_D_STATE, SSM_GROUPS * SSM_D_STATE))
    dt = jax.nn.softplus(dt_raw.astype(jnp.float32) + dt_bias.astype(jnp.float32))
    a = -jnp.exp(a_log.astype(jnp.float32))
    xh = xs.reshape(bsz, seq, SSM_HEADS, SSM_HEAD_DIM)
    y = _ssd_scan(xh, dt, a, bm.reshape(bsz, seq, SSM_GROUPS, SSM_D_STATE), cm.reshape(bsz, seq, SSM_GROUPS, SSM_D_STATE))
    y = (y + d_skip[:, None] * xh).reshape(bsz, seq, SSM_D_INNER)
    y = y * jax.nn.silu(z)
    y = y * lax.rsqrt(jnp.mean(y * y, axis=-1, keepdims=True) + NORM_EPS) * norm_w
    return jnp.concatenate([o_ret, y], axis=-1)


Q_TILE = ATTN_BLOCK
GROUP_HEADS = NSA_HEADS // NSA_KV_HEADS
HEAD_DIM = NSA_HEAD_DIM
SEL_CHUNK = 512


def _query_positions(i, n_cols):
    row = lax.broadcasted_iota(jnp.int32, (GROUP_HEADS * Q_TILE, n_cols), 0)
    return i * Q_TILE + (row & (Q_TILE - 1))


def _store_heads(o_ref, o, gate, branch):
    outs = []
    for e in range(GROUP_HEADS):
        oe = o[e * Q_TILE:(e + 1) * Q_TILE]
        if gate is not None:
            oe = oe * gate[:, 3 * e + branch:3 * e + branch + 1]
        outs.append(oe)
    o_ref[...] = jnp.concatenate(outs, axis=-1)


def _cmp_kv_kernel(r_ref, pe_ref, w1_ref, w2_ref, o_ref):
    r = r_ref[...]
    half = NSA_CMP_STRIDE * HEAD_DIM
    top = jnp.dot((r + pe_ref[0:1, :]).astype(BF16), w1_ref[0:half, :], preferred_element_type=F32)
    bot = jnp.dot((r + pe_ref[1:2, :]).astype(BF16), w1_ref[half:2 * half, :], preferred_element_type=F32)
    hidden = top + pltpu.roll(bot, r.shape[0] - 1, 0)
    act = hidden * jax.nn.sigmoid(hidden)
    o_ref[...] = jnp.dot(act.astype(BF16), w2_ref[...], preferred_element_type=F32)


def _cmp_kv(r, pe, w1, w2):
    ng, bsz, n, width = r.shape
    return pl.pallas_call(
        _cmp_kv_kernel,
        grid=(ng, bsz),
        in_specs=[pl.BlockSpec((None, None, n, width), lambda g, b: (g, b, 0, 0)),
                  pl.BlockSpec(pe.shape, lambda g, b: (0, 0)),
                  pl.BlockSpec(w1.shape, lambda g, b: (0, 0)),
                  pl.BlockSpec(w2.shape, lambda g, b: (0, 0))],
        out_specs=pl.BlockSpec((None, None, n, HEAD_DIM), lambda g, b: (g, b, 0, 0)),
        out_shape=jax.ShapeDtypeStruct((ng, bsz, n, HEAD_DIM), F32),
        compiler_params=_compiler_params(("parallel", "parallel")),
        name="nsa_compress",
    )(r, pe, w1, w2)


def _nsa_cmp_kernel(q_ref, kc_ref, vc_ref, gate_ref, ov_ref, o_ref, sel_ref):
    i = pl.program_id(2)
    rows = GROUP_HEADS * Q_TILE
    q = q_ref[...].reshape(rows, HEAD_DIM)
    n_cmp = kc_ref.shape[0]
    s = lax.dot_general(q, kc_ref[...].astype(BF16), _NT, preferred_element_type=F32)
    t = _query_positions(i, n_cmp)
    n = lax.broadcasted_iota(jnp.int32, (rows, n_cmp), 1)
    valid = n * NSA_CMP_STRIDE + (NSA_CMP_BLOCK - 1) <= t
    s = jnp.where(valid, s, MASKED)
    m = jnp.max(s, axis=-1, keepdims=True)
    p = jnp.where(valid, jnp.exp(s - m), 0.0)
    p = p * (1.0 / jnp.maximum(jnp.sum(p, axis=-1, keepdims=True), 1e-30))
    o = jnp.dot(p.astype(BF16), vc_ref[...].astype(BF16), preferred_element_type=F32)
    _store_heads(o_ref, o, jax.nn.sigmoid(gate_ref[...]), 0)

    p_sum = p[0:Q_TILE]
    for e in range(1, GROUP_HEADS):
        p_sum = p_sum + p[e * Q_TILE:(e + 1) * Q_TILE]
    imp = lax.dot_general(ov_ref[...], p_sum, _NT, preferred_element_type=F32, precision=lax.Precision.HIGHEST)
    n_sel = imp.shape[0]
    jb = lax.broadcasted_iota(jnp.int32, (n_sel, Q_TILE), 0)
    lane = lax.broadcasted_iota(jnp.int32, (n_sel, Q_TILE), 1)
    cur = (i * Q_TILE + lane) // NSA_SEL_BLOCK
    valid_b = jb <= cur
    forced = jnp.where(valid_b, jnp.where(jb == 0, 1.0, 0.0) + jnp.where(jb == cur, 1.0, 0.0) + jnp.where(jb == cur - 1, 1.0, 0.0), 0.0)
    score = jnp.where(forced > 0.5, NSA_FORCE_SCORE, jnp.where(valid_b, imp, -jnp.inf))
    rank = jnp.zeros((n_sel, Q_TILE), F32)
    for ii in range(n_sel):
        row = score[ii:ii + 1, :]
        tie = jnp.where(jb > ii, 1.0, 0.0)
        rank = rank + jnp.where(row > score, 1.0, jnp.where(row == score, tie, 0.0))
    sel_t = jnp.where(valid_b, jnp.where(rank < NSA_TOPN, 1.0, 0.0), 0.0)
    sel_t = jnp.concatenate([sel_t, jnp.zeros((LANES - n_sel, Q_TILE), F32)], axis=0)
    sel_ref[...] = sel_t.T.astype(BF16)


def _nsa_cmp(q, k_cmp, v_cmp, gates, overlap_t):
    _, bsz, seq, hd = q.shape
    ng = k_cmp.shape[0]
    width = GROUP_HEADS * hd
    return pl.pallas_call(
        _nsa_cmp_kernel,
        grid=(bsz, ng, seq // Q_TILE),
        in_specs=[pl.BlockSpec((GROUP_HEADS, None, Q_TILE, hd), lambda b, g, i: (g, b, i, 0)),
                  pl.BlockSpec((None, None, k_cmp.shape[2], hd), lambda b, g, i: (g, b, 0, 0)),
                  pl.BlockSpec((None, None, k_cmp.shape[2], hd), lambda b, g, i: (g, b, 0, 0)),
                  pl.BlockSpec((None, None, Q_TILE, gates.shape[-1]), lambda b, g, i: (g, b, i, 0)),
                  pl.BlockSpec(overlap_t.shape, lambda b, g, i: (0, 0))],
        out_specs=[pl.BlockSpec((None, Q_TILE, width), lambda b, g, i: (b, i, g)),
                   pl.BlockSpec((None, None, Q_TILE, LANES), lambda b, g, i: (g, b, i, 0))],
        out_shape=[jax.ShapeDtypeStruct((bsz, seq, ng * width), F32),
                   jax.ShapeDtypeStruct((ng, bsz, seq, LANES), BF16)],
        compiler_params=_compiler_params(("parallel", "parallel", "parallel")),
        name="nsa_cmp",
    )(q, k_cmp, v_cmp, gates, overlap_t)


def _nsa_sel_kernel(q_ref, k_ref, v_ref, sel_ref, gate_ref, o_ref, m_ref, l_ref, acc_ref):
    i = pl.program_id(2)
    rows = GROUP_HEADS * Q_TILE
    q = q_ref[...].reshape(rows, HEAD_DIM)
    sel = sel_ref[...]
    m_ref[...] = jnp.full(m_ref.shape, MASKED, F32)
    l_ref[...] = jnp.zeros(l_ref.shape, F32)
    acc_ref[...] = jnp.zeros(acc_ref.shape, F32)
    t = i * Q_TILE + lax.broadcasted_iota(jnp.int32, (Q_TILE, SEL_CHUNK), 0)
    col = lax.broadcasted_iota(jnp.int32, (Q_TILE, SEL_CHUNK), 1)
    blk_row = lax.broadcasted_iota(jnp.int32, (LANES, SEL_CHUNK), 0)
    blk_col = lax.broadcasted_iota(jnp.int32, (LANES, SEL_CHUNK), 1)

    def body(c, carry):
        start = pl.multiple_of(c * SEL_CHUNK, SEL_CHUNK)
        k = k_ref[pl.ds(start, SEL_CHUNK), :]
        v = v_ref[pl.ds(start, SEL_CHUNK), :]
        s = lax.dot_general(q, k, _NT, preferred_element_type=F32)
        expand = jnp.where((start + blk_col) // NSA_SEL_BLOCK == blk_row, 1.0, 0.0).astype(BF16)
        picked = jnp.dot(sel, expand, preferred_element_type=F32)
        keep = jnp.where(start + col <= t, picked, 0.0)
        bias = (keep - 1.0) * (-MASKED)
        s = s + jnp.concatenate([bias] * GROUP_HEADS, axis=0)
        m_old = m_ref[...]
        m_new = jnp.maximum(m_old, jnp.max(s, axis=-1, keepdims=True))
        alpha = jnp.exp(m_old - m_new)
        p = jnp.exp(s - m_new)
        l_ref[...] = alpha * l_ref[...] + jnp.sum(p, axis=-1, keepdims=True)
        acc_ref[...] = alpha * acc_ref[...] + jnp.dot(p.astype(BF16), v, preferred_element_type=F32)
        m_ref[...] = m_new
        return carry

    n_chunks = (i * Q_TILE + Q_TILE + SEL_CHUNK - 1) // SEL_CHUNK
    lax.fori_loop(0, n_chunks, body, 0)
    o = acc_ref[...] * (1.0 / jnp.maximum(l_ref[...], 1e-30))
    _store_heads(o_ref, o, jax.nn.sigmoid(gate_ref[...]), 1)


def _nsa_sel(q, k, v, sel, gates):
    _, bsz, seq, hd = q.shape
    ng = k.shape[0]
    width = GROUP_HEADS * hd
    rows = GROUP_HEADS * Q_TILE
    return pl.pallas_call(
        _nsa_sel_kernel,
        grid=(bsz, ng, seq // Q_TILE),
        in_specs=[pl.BlockSpec((GROUP_HEADS, None, Q_TILE, hd), lambda b, g, i: (g, b, i, 0)),
                  pl.BlockSpec((None, None, seq, hd), lambda b, g, i: (g, b, 0, 0)),
                  pl.BlockSpec((None, None, seq, hd), lambda b, g, i: (g, b, 0, 0)),
                  pl.BlockSpec((None, None, Q_TILE, LANES), lambda b, g, i: (g, b, i, 0)),
                  pl.BlockSpec((None, None, Q_TILE, gates.shape[-1]), lambda b, g, i: (g, b, i, 0))],
        out_specs=pl.BlockSpec((None, Q_TILE, width), lambda b, g, i: (b, i, g)),
        out_shape=jax.ShapeDtypeStruct((bsz, seq, ng * width), F32),
        scratch_shapes=[pltpu.VMEM((rows, 1), F32), pltpu.VMEM((rows, 1), F32), pltpu.VMEM((rows, hd), F32)],
        compiler_params=_compiler_params(("parallel", "parallel", "parallel")),
        name="nsa_sel",
    )(q, k, v, sel, gates)


def _band_kernel(window, n_keys, gate_branch, has_sinks, q_ref, k_ref, v_ref, *refs):
    i = pl.program_id(2)
    rows = GROUP_HEADS * Q_TILE
    refs = list(refs)
    gate_ref = refs.pop(0) if gate_branch is not None else None
    sink_ref = refs.pop(0) if has_sinks else None
    (o_ref,) = refs
    q = q_ref[...].reshape(rows, HEAD_DIM)
    start = pl.multiple_of(jnp.maximum(i * Q_TILE + Q_TILE - n_keys, 0), Q_TILE)
    k = k_ref[pl.ds(start, n_keys), :]
    v = v_ref[pl.ds(start, n_keys), :]
    s = lax.dot_general(q, k, _NT, preferred_element_type=F32)
    t = i * Q_TILE + lax.broadcasted_iota(jnp.int32, (Q_TILE, n_keys), 0)
    rel = t - (start + lax.broadcasted_iota(jnp.int32, (Q_TILE, n_keys), 1))
    bias = jnp.where(rel >= 0, jnp.where(rel < window, 0.0, MASKED), MASKED)
    s = s + jnp.concatenate([bias] * GROUP_HEADS, axis=0)
    m = jnp.max(s, axis=-1, keepdims=True)
    if has_sinks:
        sink = jnp.concatenate([jnp.broadcast_to(sink_ref[:, e:e + 1], (Q_TILE, 1)) for e in range(GROUP_HEADS)], axis=0)
        m = jnp.maximum(m, sink)
    p = jnp.exp(s - m)
    denom = jnp.sum(p, axis=-1, keepdims=True)
    if has_sinks:
        denom = denom + jnp.exp(sink - m)
    else:
        denom = jnp.maximum(denom, 1e-30)
    o = jnp.dot(p.astype(BF16), v, preferred_element_type=F32) * (1.0 / denom)
    gate = jax.nn.sigmoid(gate_ref[...]) if gate_ref is not None else None
    _store_heads(o_ref, o, gate, gate_branch)


def _band_attention(q, k, v, window, gates=None, gate_branch=None, sinks=None):
    _, bsz, seq, hd = q.shape
    ng = k.shape[0]
    width = GROUP_HEADS * hd
    nprev = -(-(window - 1) // Q_TILE)
    n_keys = (nprev + 1) * Q_TILE
    in_specs = [pl.BlockSpec((GROUP_HEADS, None, Q_TILE, hd), lambda b, g, i: (g, b, i, 0)),
                pl.BlockSpec((None, None, seq, hd), lambda b, g, i: (g, b, 0, 0)),
                pl.BlockSpec((None, None, seq, hd), lambda b, g, i: (g, b, 0, 0))]
    args = [q, k, v]
    if gates is not None:
        in_specs.append(pl.BlockSpec((None, None, Q_TILE, gates.shape[-1]), lambda b, g, i: (g, b, i, 0)))
        args.append(gates)
    if sinks is not None:
        in_specs.append(pl.BlockSpec((None, 1, GROUP_HEADS), lambda b, g, i: (g, 0, 0)))
        args.append(sinks)
    return pl.pallas_call(
        functools.partial(_band_kernel, window, n_keys, gate_branch if gates is not None else None, sinks is not None),
        grid=(bsz, ng, seq // Q_TILE),
        in_specs=in_specs,
        out_specs=pl.BlockSpec((None, Q_TILE, width), lambda b, g, i: (b, i, g)),
        out_shape=jax.ShapeDtypeStruct((bsz, seq, ng * width), F32),
        compiler_params=_compiler_params(("parallel", "parallel", "parallel")),
        name="band_attention",
    )(*args)


def _overlap_t(seq):
    n_cmp = (seq - NSA_CMP_BLOCK) // NSA_CMP_STRIDE + 1
    n_pad = seq // NSA_CMP_STRIDE
    starts = np.arange(n_pad) * NSA_CMP_STRIDE
    sel_start = np.arange(seq // NSA_SEL_BLOCK) * NSA_SEL_BLOCK
    ov = (starts[None, :] < sel_start[:, None] + NSA_SEL_BLOCK) & (starts[None, :] + NSA_CMP_BLOCK > sel_start[:, None])
    ov = ov & (np.arange(n_pad)[None, :] < n_cmp)
    return jnp.asarray(ov.astype(np.float32))


def _odd_mixer(xt, nw, w_in, w_out, cmp_pe, cmp_w1, cmp_w2, sinks, bsz):
    t, d = xt.shape
    seq = t // bsz
    assert seq // NSA_SEL_BLOCK <= LANES and seq % SEL_CHUNK == 0
    cuts = [int(c) for c in np.cumsum((0,) + ODD_SPLITS)]
    ws = [w_in[:, a:b] for a, b in zip(cuts[:-1], cuts[1:])]
    scale = HEAD_DIM ** -0.5
    ws[0] = ws[0] * scale
    ws[8] = ws[8] * scale
    hd = HEAD_DIM
    head_dims = (hd, hd, hd, hd, hd, hd, hd, 3 * GROUP_HEADS, hd, hd, hd)
    dtypes = (BF16, F32, F32, BF16, BF16, BF16, BF16, F32, BF16, BF16, BF16)
    q_n, kc, vc, ks, vs, kw, vw, gates, q_s, k_s, v_s = _norm_proj(xt, nw, ws, head_dims, dtypes)

    def heads(a):
        return a.reshape(a.shape[0], bsz, seq, a.shape[-1])

    q_n, ks, vs, kw, vw, gates, q_s, k_s, v_s = map(heads, (q_n, ks, vs, kw, vw, gates, q_s, k_s, v_s))
    ng = NSA_KV_HEADS
    rows16 = NSA_CMP_STRIDE * hd
    pe = cmp_pe.reshape(2, 2, rows16)
    w1 = cmp_w1.astype(BF16)
    w2 = cmp_w2.astype(BF16)
    k_cmp = _cmp_kv(kc.reshape(ng, bsz, seq // NSA_CMP_STRIDE, rows16), pe[0], w1[0], w2[0])
    v_cmp = _cmp_kv(vc.reshape(ng, bsz, seq // NSA_CMP_STRIDE, rows16), pe[1], w1[1], w2[1])
    o_cmp, sel = _nsa_cmp(q_n, k_cmp, v_cmp, gates, _overlap_t(seq))
    o_sel = _nsa_sel(q_n, ks, vs, sel, gates)
    o_win = _band_attention(q_n, kw, vw, NSA_WINDOW, gates=gates, gate_branch=2)
    o_swa = _band_attention(q_s, k_s, v_s, SWA_WINDOW, sinks=sinks.reshape(SWA_KV_HEADS, 1, GROUP_HEADS))
    n_nsa = NSA_HEADS * hd
    return _proj_residual(xt, [[o_cmp.reshape(t, n_nsa), o_sel.reshape(t, n_nsa), o_win.reshape(t, n_nsa)],
                               [o_swa.reshape(t, SWA_HEADS * hd)]], [w_out[:n_nsa], w_out[n_nsa:]])


def kernel(x, mem, norm_w, final_norm_w, mem_norm_w, ffn_w_gate, ffn_w_up, ffn_w_down, xattn_wq, xattn_wkv, xattn_wo, even_w_in, even_w_out, ssm_conv_w, ssm_conv_b, ssm_dt_bias, ssm_a_log, ssm_d, ssm_norm_w, odd_w_in, odd_w_out, nsa_cmp_pe, nsa_cmp_w1, nsa_cmp_w2, swa_sinks):
    bsz, seq, d = x.shape
    t = bsz * seq
    m = mem.shape[1]
    wg = ffn_w_gate.astype(BF16)
    wu = ffn_w_up.astype(BF16)
    wd = ffn_w_down.astype(BF16)
    wq = xattn_wq.astype(BF16)
    wkv = xattn_wkv.astype(BF16)
    wo = xattn_wo.astype(BF16)
    even_in = even_w_in.astype(BF16)
    even_out = even_w_out.astype(BF16)
    odd_in = odd_w_in.astype(BF16)
    odd_out = odd_w_out.astype(BF16)
    even_cuts = [int(c) for c in np.cumsum((0,) + EVEN_SPLITS)]

    xt = x.reshape(t, d)
    mem2 = mem.reshape(bsz * m, d)
    mem_nw = mem_norm_w.reshape(1, d)
    for layer in range(DEPTH):
        i = layer // 2
        xt = _ffn(xt, norm_w[layer, 0].reshape(1, d), wg, wu, wd, layer, 0)
        nw1 = norm_w[layer, 1].reshape(1, d)
        if layer % 2 == 0:
            ws = [even_in[i][:, a:b] for a, b in zip(even_cuts[:-1], even_cuts[1:])]
            parts = _norm_proj(xt, nw1, ws)
            parts = [p.reshape(bsz, seq, -1) for p in parts]
            mix = _even_core(*parts, ssm_conv_w[i], ssm_conv_b[i], ssm_dt_bias[i], ssm_a_log[i], ssm_d[i], ssm_norm_w[i])
            xt = _proj_residual(xt, [[mix.reshape(t, -1)]], [even_out[i]])
        else:
            xt = _odd_mixer(xt, nw1, odd_in[i], odd_out[i], nsa_cmp_pe[i], nsa_cmp_w1[i], nsa_cmp_w2[i], swa_sinks[i], bsz)
        (kv,) = _norm_proj(mem2, mem_nw, [wkv[layer]])
        xt = _xattn(xt, norm_w[layer, 2].reshape(1, d), wq[layer], kv.reshape(bsz, m, 2 * d), wo[layer], bsz)
        xt = _ffn(xt, norm_w[layer, 3].reshape(1, d), wg, wu, wd, layer, 1)
    out = _final_norm(xt, final_norm_w.reshape(1, d))
    return out.reshape(bsz, seq, d)
```

```python
import functools

import numpy as np
import jax
import jax.numpy as jnp
from jax import lax
from jax.experimental import pallas as pl
from jax.experimental.pallas import tpu as pltpu

F32 = jnp.float32
BF16 = jnp.bfloat16

D_MODEL = 1024
DEPTH = 4
D_FF = 2816
NORM_EPS = 1e-6
ATTN_BLOCK = 128

RET_HEADS = 4
RET_DK = 128
RET_DV = 128
RET_CHUNK = 128
RET_ROPE_BASE = 10000.0

SSM_HEADS = 8
SSM_HEAD_DIM = 64
SSM_D_INNER = SSM_HEADS * SSM_HEAD_DIM
SSM_D_STATE = 64
SSM_GROUPS = 2
SSM_CONV = 4
SSM_CHUNK = 128
SSM_CONV_DIM = SSM_D_INNER + 2 * SSM_GROUPS * SSM_D_STATE

EVEN_SPLITS = (RET_HEADS * RET_DK, RET_HEADS * RET_DK, RET_HEADS * RET_DV, RET_HEADS * RET_DV, SSM_D_INNER, SSM_CONV_DIM, SSM_HEADS)

NSA_HEADS = 8
NSA_KV_HEADS = 2
NSA_HEAD_DIM = 64
NSA_CMP_BLOCK = 32
NSA_CMP_STRIDE = 16
NSA_CMP_HIDDEN = 256
NSA_SEL_BLOCK = 64
NSA_TOPN = 16
NSA_WINDOW = 512
NSA_FORCE_SCORE = 1e6

SWA_HEADS = 8
SWA_KV_HEADS = 2
SWA_HEAD_DIM = 64
SWA_WINDOW = 128

ODD_SPLITS = (NSA_HEADS * NSA_HEAD_DIM,) + (NSA_KV_HEADS * NSA_HEAD_DIM,) * 6 + (3 * NSA_HEADS, SWA_HEADS * SWA_HEAD_DIM, SWA_KV_HEADS * SWA_HEAD_DIM, SWA_KV_HEADS * SWA_HEAD_DIM)

XATTN_HEADS = 4
XATTN_HEAD_DIM = D_MODEL // XATTN_HEADS

VMEM_LIMIT_BYTES = 52 * 1024 * 1024
TOKEN_TILE = 512
FFN_TILE = 1408
LANES = 128
MASKED = -1e30

_NT = (((1,), (1,)), ((), ()))


def _compiler_params(semantics):
    return pltpu.CompilerParams(dimension_semantics=semantics, vmem_limit_bytes=VMEM_LIMIT_BYTES)


def _rms(x, w):
    return x * lax.rsqrt(jnp.mean(x * x, axis=-1, keepdims=True) + NORM_EPS) * w


def _ffn_kernel(x_ref, nw_ref, wg_ref, wu_ref, wd_ref, o_ref, h_ref, acc_ref):
    j = pl.program_id(1)

    @pl.when(j == 0)
    def _():
        h_ref[...] = _rms(x_ref[...], nw_ref[...]).astype(BF16)
        acc_ref[...] = jnp.zeros_like(acc_ref)

    h = h_ref[...]
    g = jnp.dot(h, wg_ref[...], preferred_element_type=F32)
    u = jnp.dot(h, wu_ref[...], preferred_element_type=F32)
    a = (g * jax.nn.sigmoid(g) * u).astype(BF16)
    acc_ref[...] += jnp.dot(a, wd_ref[...], preferred_element_type=F32)

    @pl.when(j == pl.num_programs(1) - 1)
    def _():
        o_ref[...] = x_ref[...] + 0.5 * acc_ref[...]


def _ffn(x, nw, wg, wu, wd, layer, k):
    t, d = x.shape
    tm, tf = TOKEN_TILE, FFN_TILE
    return pl.pallas_call(
        _ffn_kernel,
        grid=(t // tm, D_FF // tf),
        in_specs=[
            pl.BlockSpec((tm, d), lambda i, j: (i, 0)),
            pl.BlockSpec((1, d), lambda i, j: (0, 0)),
            pl.BlockSpec((None, None, d, tf), lambda i, j: (layer, k, 0, j)),
            pl.BlockSpec((None, None, d, tf), lambda i, j: (layer, k, 0, j)),
            pl.BlockSpec((None, None, tf, d), lambda i, j: (layer, k, j, 0)),
        ],
        out_specs=pl.BlockSpec((tm, d), lambda i, j: (i, 0)),
        out_shape=jax.ShapeDtypeStruct((t, d), F32),
        scratch_shapes=[pltpu.VMEM((tm, d), BF16), pltpu.VMEM((tm, d), F32)],
        compiler_params=_compiler_params(("parallel", "arbitrary")),
        name="ffn",
    )(x, nw, wg, wu, wd)


def _norm_proj_kernel(head_dims, x_ref, nw_ref, *refs):
    n_out = len(head_dims)
    w_refs, o_refs = refs[:n_out], refs[n_out:]
    h = _rms(x_ref[...], nw_ref[...]).astype(BF16)
    for hd, w_ref, o_ref in zip(head_dims, w_refs, o_refs):
        y = jnp.dot(h, w_ref[...], preferred_element_type=F32)
        if hd is None:
            o_ref[...] = y.astype(o_ref.dtype)
        else:
            for j in range(o_ref.shape[0]):
                o_ref[j] = y[:, j * hd:(j + 1) * hd].astype(o_ref.dtype)


def _norm_proj(x, nw, weights, head_dims=None, dtypes=None, tm=TOKEN_TILE):
    t, d = x.shape
    n_out = len(weights)
    head_dims = tuple(head_dims) if head_dims is not None else (None,) * n_out
    dtypes = tuple(dtypes) if dtypes is not None else (F32,) * n_out
    out_specs, out_shape = [], []
    for w, hd, dt in zip(weights, head_dims, dtypes):
        n = w.shape[1]
        if hd is None:
            out_specs.append(pl.BlockSpec((tm, n), lambda i: (i, 0)))
            out_shape.append(jax.ShapeDtypeStruct((t, n), dt))
        else:
            out_specs.append(pl.BlockSpec((n // hd, tm, hd), lambda i: (0, i, 0)))
            out_shape.append(jax.ShapeDtypeStruct((n // hd, t, hd), dt))
    return pl.pallas_call(
        functools.partial(_norm_proj_kernel, head_dims),
        grid=(t // tm,),
        in_specs=[pl.BlockSpec((tm, d), lambda i: (i, 0)), pl.BlockSpec((1, d), lambda i: (0, 0))]
        + [pl.BlockSpec(w.shape, lambda i: (0, 0)) for w in weights],
        out_specs=out_specs,
        out_shape=out_shape,
        compiler_params=_compiler_params(("parallel",)),
        name="norm_proj",
    )(x, nw, *weights)


def _proj_residual_kernel(group_sizes, x_ref, *refs):
    n_a = sum(group_sizes)
    a_refs, w_refs, o_ref = refs[:n_a], refs[n_a:-1], refs[-1]
    acc = x_ref[...]
    pos = 0
    for size, w_ref in zip(group_sizes, w_refs):
        a = a_refs[pos][...]
        for r in a_refs[pos + 1:pos + size]:
            a = a + r[...]
        pos += size
        acc = acc + jnp.dot(a.astype(BF16), w_ref[...], preferred_element_type=F32)
    o_ref[...] = acc


def _proj_residual(x, groups, weights, tm=TOKEN_TILE):
    t, d = x.shape
    arrays = [a for grp in groups for a in grp]
    return pl.pallas_call(
        functools.partial(_proj_residual_kernel, tuple(len(grp) for grp in groups)),
        grid=(t // tm,),
        in_specs=[pl.BlockSpec((tm, d), lambda i: (i, 0))]
        + [pl.BlockSpec((tm, a.shape[1]), lambda i: (i, 0)) for a in arrays]
        + [pl.BlockSpec(w.shape, lambda i: (0, 0)) for w in weights],
        out_specs=pl.BlockSpec((tm, d), lambda i: (i, 0)),
        out_shape=jax.ShapeDtypeStruct((t, d), F32),
        compiler_params=_compiler_params(("parallel",)),
        name="proj_residual",
    )(x, *arrays, *weights)


def _xattn_kernel(x_ref, nw_ref, wq_ref, kv_ref, wo_ref, o_ref):
    x = x_ref[...]
    h = _rms(x, nw_ref[...]).astype(BF16)
    q = jnp.dot(h, wq_ref[...], preferred_element_type=F32).astype(BF16)
    hd = XATTN_HEAD_DIM
    outs = []
    for hh in range(XATTN_HEADS):
        k = kv_ref[:, hh * hd:(hh + 1) * hd].astype(BF16)
        v = kv_ref[:, D_MODEL + hh * hd:D_MODEL + (hh + 1) * hd].astype(BF16)
        s = lax.dot_general(q[:, hh * hd:(hh + 1) * hd], k, _NT, preferred_element_type=F32)
        s = s * (hd ** -0.5)
        p = jnp.exp(s - jnp.max(s, axis=-1, keepdims=True))
        p = p / jnp.sum(p, axis=-1, keepdims=True)
        outs.append(jnp.dot(p.astype(BF16), v, preferred_element_type=F32).astype(BF16))
    o = jnp.concatenate(outs, axis=-1)
    o_ref[...] = x + jnp.dot(o, wo_ref[...], preferred_element_type=F32)


def _xattn(x, nw, wq, kv, wo, bsz, tm=TOKEN_TILE):
    t, d = x.shape
    seq = t // bsz
    nt = seq // tm
    m = kv.shape[1]
    return pl.pallas_call(
        _xattn_kernel,
        grid=(bsz, nt),
        in_specs=[pl.BlockSpec((tm, d), lambda b, i: (b * nt + i, 0)), pl.BlockSpec((1, d), lambda b, i: (0, 0)),
                  pl.BlockSpec((d, d), lambda b, i: (0, 0)), pl.BlockSpec((None, m, 2 * d), lambda b, i: (b, 0, 0)),
                  pl.BlockSpec((d, d), lambda b, i: (0, 0))],
        out_specs=pl.BlockSpec((tm, d), lambda b, i: (b * nt + i, 0)),
        out_shape=jax.ShapeDtypeStruct((t, d), F32),
        compiler_params=_compiler_params(("parallel", "parallel")),
        name="xattn",
    )(x, nw, wq, kv, wo)


def _final_norm_kernel(x_ref, nw_ref, o_ref):
    o_ref[...] = _rms(x_ref[...], nw_ref[...])


def _final_norm(x, nw, tm=TOKEN_TILE):
    t, d = x.shape
    return pl.pallas_call(
        _final_norm_kernel,
        grid=(t // tm,),
        in_specs=[pl.BlockSpec((tm, d), lambda i: (i, 0)), pl.BlockSpec((1, d), lambda i: (0, 0))],
        out_specs=pl.BlockSpec((tm, d), lambda i: (i, 0)),
        out_shape=jax.ShapeDtypeStruct((t, d), F32),
        compiler_params=_compiler_params(("parallel",)),
        name="final_norm",
    )(x, nw)


def _retention_kernel(q_ref, k_ref, v_ref, g_ref, cos_ref, sin_ref, dmat_ref, zeta_ref, xi_ref, decay_ref, o_ref, state_ref):
    @pl.when(pl.program_id(1) == 0)
    def _():
        state_ref[...] = jnp.zeros_like(state_ref)

    cos = cos_ref[...]
    sin = sin_ref[...]
    half = RET_DK // 2
    outs = []
    for h in range(RET_HEADS):
        qh = q_ref[:, h * RET_DK:(h + 1) * RET_DK]
        kh = k_ref[:, h * RET_DK:(h + 1) * RET_DK]
        qr = qh * cos + pltpu.roll(qh, half, 1) * sin
        kr = (kh * cos + pltpu.roll(kh, half, 1) * sin) * (RET_DK ** -0.5)
        vh = v_ref[:, h * RET_DV:(h + 1) * RET_DV].astype(BF16)
        scores = lax.dot_general(qr.astype(BF16), kr.astype(BF16), _NT, preferred_element_type=F32) * dmat_ref[h]
        inner = jnp.dot(scores.astype(BF16), vh, preferred_element_type=F32)
        state = state_ref[h]
        cross = jnp.dot((qr * xi_ref[h]).astype(BF16), state.astype(BF16), preferred_element_type=F32)
        kz = (kr * zeta_ref[h]).T.astype(BF16)
        state_ref[h] = decay_ref[h] * state + jnp.dot(kz, vh, preferred_element_type=F32)
        o = inner + cross
        o = o * lax.rsqrt(jnp.mean(o * o, axis=-1, keepdims=True) + NORM_EPS)
        gh = g_ref[:, h * RET_DV:(h + 1) * RET_DV]
        outs.append(gh * jax.nn.sigmoid(gh) * o)
    o_ref[...] = jnp.concatenate(outs, axis=-1)


def _retention_tables(seq):
    L = RET_CHUNK
    pos = jnp.arange(seq, dtype=F32)
    inv_freq = 1.0 / (RET_ROPE_BASE ** jnp.linspace(0.0, 1.0, RET_DK // 2, dtype=F32))
    ang = pos[:, None] * inv_freq[None, :]
    cos, sin = jnp.cos(ang), jnp.sin(ang)
    cos2 = jnp.concatenate([cos, cos], axis=-1)
    sin2 = jnp.concatenate([-sin, sin], axis=-1)
    log_g = jnp.log1p(-jnp.exp2(-5.0 - jnp.arange(RET_HEADS, dtype=F32)))
    idx = jnp.arange(L, dtype=F32)
    diff = idx[:, None] - idx[None, :]
    dmat = jnp.where(diff >= 0, jnp.exp(jnp.maximum(diff, 0.0)[None] * log_g[:, None, None]), 0.0)
    ones = jnp.ones((RET_HEADS, L, RET_DK), F32)
    zeta = jnp.exp((L - 1 - idx)[None, :] * log_g[:, None])[:, :, None] * ones
    xi = jnp.exp((idx + 1)[None, :] * log_g[:, None])[:, :, None] * ones
    decay = jnp.exp(L * log_g)[:, None, None] * ones
    return cos2, sin2, dmat, zeta, xi, decay


def _retention(q, k, v, g):
    bsz, seq, width = q.shape
    L = RET_CHUNK
    tables = _retention_tables(seq)
    tok = pl.BlockSpec((None, L, width), lambda b, c: (b, c, 0))
    rope = pl.BlockSpec((L, RET_DK), lambda b, c: (c, 0))
    const = pl.BlockSpec((RET_HEADS, L, RET_DK), lambda b, c: (0, 0, 0))
    return pl.pallas_call(
        _retention_kernel,
        grid=(bsz, seq // L),
        in_specs=[tok, tok, tok, tok, rope, rope, const, const, const, const],
        out_specs=tok,
        out_shape=jax.ShapeDtypeStruct((bsz, seq, width), F32),
        scratch_shapes=[pltpu.VMEM((RET_HEADS, RET_DK, RET_DV), F32)],
        compiler_params=_compiler_params(("parallel", "arbitrary")),
        name="retention",
    )(q, k, v, g, *tables)


CONV_PAD = 8


def _ssd_kernel(xbc_ref, dt_ref, z_ref, convw_ref, convb_ref, dtb_ref, alog_ref, dskip_ref, nw_ref, tri_ref, o_ref, ext_ref, state_ref):
    L = SSM_CHUNK
    hp = SSM_HEAD_DIM
    ns = SSM_D_STATE

    @pl.when(pl.program_id(1) == 0)
    def _():
        ext_ref[0:CONV_PAD, :] = jnp.zeros((CONV_PAD, SSM_CONV_DIM), F32)
        state_ref[...] = jnp.zeros_like(state_ref)

    ext_ref[CONV_PAD:CONV_PAD + L, :] = xbc_ref[...]
    conv = convb_ref[...]
    for w in range(SSM_CONV):
        off = CONV_PAD - (SSM_CONV - 1) + w
        conv = conv + ext_ref[off:off + L, :] * convw_ref[w:w + 1, :]
    ext_ref[0:CONV_PAD, :] = ext_ref[L:L + CONV_PAD, :]
    xa = conv * jax.nn.sigmoid(conv)
    xs = xa[:, :SSM_D_INNER]
    bmat = xa[:, SSM_D_INNER:SSM_D_INNER + SSM_GROUPS * ns]
    cmat = xa[:, SSM_D_INNER + SSM_GROUPS * ns:]

    x = dt_ref[...] + dtb_ref[...]
    dt = jnp.maximum(x, 0.0) + jnp.log1p(jnp.exp(-jnp.abs(x)))
    lane = lax.broadcasted_iota(jnp.int32, (L, LANES), 1)
    dta = jnp.where(lane < SSM_HEADS, dt * -jnp.exp(alog_ref[...]), 0.0)
    cums = jnp.dot(tri_ref[...], dta, preferred_element_type=F32, precision=lax.Precision.HIGHEST)
    cums_t = cums.T
    last = cums[L - 1:L, :]
    decay_to_end = jnp.exp(last - cums)
    decay_from_start = jnp.exp(cums)
    chunk_decay = jnp.exp(last)
    causal = lax.broadcasted_iota(jnp.int32, (L, L), 0) >= lax.broadcasted_iota(jnp.int32, (L, L), 1)
    heads_per_group = SSM_HEADS // SSM_GROUPS
    ys = []
    for g in range(SSM_GROUPS):
        bg = bmat[:, g * ns:(g + 1) * ns].astype(BF16)
        cg = cmat[:, g * ns:(g + 1) * ns].astype(BF16)
        cb = lax.dot_general(cg, bg, _NT, preferred_element_type=F32)
        for e in range(heads_per_group):
            h = g * heads_per_group + e
            xh = xs[:, h * hp:(h + 1) * hp]
            xdt = xh * dt[:, h:h + 1]
            seg = cums[:, h:h + 1] - cums_t[h:h + 1, :]
            lmat = jnp.where(causal, jnp.exp(seg), 0.0)
            y_diag = jnp.dot((cb * lmat).astype(BF16), xdt.astype(BF16), preferred_element_type=F32)
            state = state_ref[h]
            y_off = lax.dot_general(cg, state.astype(BF16), _NT, preferred_element_type=F32) * decay_from_start[:, h:h + 1]
            xw = (xdt * decay_to_end[:, h:h + 1]).astype(BF16)
            new = lax.dot_general(xw, bg, (((0,), (0,)), ((), ())), preferred_element_type=F32)
            state_ref[h] = chunk_decay[:, h:h + 1] * state + new
            ys.append(y_diag + y_off + dskip_ref[:, h * hp:(h + 1) * hp] * xh)
    y = jnp.concatenate(ys, axis=-1)
    z = z_ref[...]
    o_ref[...] = _rms(y * (z * jax.nn.sigmoid(z)), nw_ref[...])


def _ssd(xbc, dt_raw, z, conv_w, conv_b, dt_bias, a_log, d_skip, norm_w):
    bsz, seq, _ = xbc.shape
    L = SSM_CHUNK
    pad = LANES - SSM_HEADS
    params = [conv_w, conv_b.reshape(1, -1), jnp.pad(dt_bias, (0, pad)).reshape(1, LANES),
              jnp.pad(a_log, (0, pad)).reshape(1, LANES), jnp.repeat(d_skip, SSM_HEAD_DIM).reshape(1, -1),
              norm_w.reshape(1, -1), jnp.asarray(np.tril(np.ones((L, L), np.float32)))]

    def tok(width):
        return pl.BlockSpec((None, L, width), lambda b, c: (b, c, 0))

    return pl.pallas_call(
        _ssd_kernel,
        grid=(bsz, seq // L),
        in_specs=[tok(SSM_CONV_DIM), tok(LANES), tok(SSM_D_INNER)] + [pl.BlockSpec(p.shape, lambda b, c: (0, 0)) for p in params],
        out_specs=tok(SSM_D_INNER),
        out_shape=jax.ShapeDtypeStruct((bsz, seq, SSM_D_INNER), F32),
        scratch_shapes=[pltpu.VMEM((L + CONV_PAD, SSM_CONV_DIM), F32), pltpu.VMEM((SSM_HEADS, SSM_HEAD_DIM, SSM_D_STATE), F32)],
        compiler_params=_compiler_params(("parallel", "arbitrary")),
        name="ssd",
    )(xbc, dt_raw, z, *params)


def _even_mixer(xt, nw, w_in, w_out, conv_w, conv_b, dt_bias, a_log, d_skip, ssm_norm_w, bsz):
    t, d = xt.shape
    seq = t // bsz
    cuts = [int(c) for c in np.cumsum((0,) + EVEN_SPLITS)]
    ws = [w_in[:, a:b] for a, b in zip(cuts[:-1], cuts[1:])]
    perm = np.concatenate([h * RET_DK + np.concatenate([np.arange(0, RET_DK, 2), np.arange(1, RET_DK, 2)]) for h in range(RET_HEADS)])
    ws[0] = ws[0][:, perm]
    ws[1] = ws[1][:, perm]
    ws[6] = jnp.pad(ws[6], ((0, 0), (0, LANES - SSM_HEADS)))
    q, k, v, g, z, xbc, dt_raw = [a.reshape(bsz, seq, -1) for a in _norm_proj(xt, nw, ws)]
    o_ret = _retention(q, k, v, g)
    y = _ssd(xbc, dt_raw, z, conv_w, conv_b, dt_bias, a_log, d_skip, ssm_norm_w)
    n_ret = RET_HEADS * RET_DV
    return _proj_residual(xt, [[o_ret.reshape(t, n_ret)], [y.reshape(t, SSM_D_INNER)]], [w_out[:n_ret], w_out[n_ret:]])


Q_TILE = ATTN_BLOCK
GROUP_HEADS = NSA_HEADS // NSA_KV_HEADS
HEAD_DIM = NSA_HEAD_DIM
SEL_CHUNK = 512


def _query_positions(i, n_cols):
    row = lax.broadcasted_iota(jnp.int32, (GROUP_HEADS * Q_TILE, n_cols), 0)
    return i * Q_TILE + (row & (Q_TILE - 1))


def _store_heads(o_ref, o, gate, branch):
    outs = []
    for e in range(GROUP_HEADS):
        oe = o[e * Q_TILE:(e + 1) * Q_TILE]
        if gate is not None:
            oe = oe * gate[:, 3 * e + branch:3 * e + branch + 1]
        outs.append(oe)
    o_ref[...] = jnp.concatenate(outs, axis=-1)


def _cmp_kv_kernel(r_ref, pe_ref, w1_ref, w2_ref, o_ref):
    r = r_ref[...]
    half = NSA_CMP_STRIDE * HEAD_DIM
    top = jnp.dot((r + pe_ref[0:1, :]).astype(BF16), w1_ref[0:half, :], preferred_element_type=F32)
    bot = jnp.dot((r + pe_ref[1:2, :]).astype(BF16), w1_ref[half:2 * half, :], preferred_element_type=F32)
    hidden = top + pltpu.roll(bot, r.shape[0] - 1, 0)
    act = hidden * jax.nn.sigmoid(hidden)
    o_ref[...] = jnp.dot(act.astype(BF16), w2_ref[...], preferred_element_type=F32)


def _cmp_kv(r, pe, w1, w2):
    ng, bsz, n, width = r.shape
    return pl.pallas_call(
        _cmp_kv_kernel,
        grid=(ng, bsz),
        in_specs=[pl.BlockSpec((None, None, n, width), lambda g, b: (g, b, 0, 0)),
                  pl.BlockSpec(pe.shape, lambda g, b: (0, 0)),
                  pl.BlockSpec(w1.shape, lambda g, b: (0, 0)),
                  pl.BlockSpec(w2.shape, lambda g, b: (0, 0))],
        out_specs=pl.BlockSpec((None, None, n, HEAD_DIM), lambda g, b: (g, b, 0, 0)),
        out_shape=jax.ShapeDtypeStruct((ng, bsz, n, HEAD_DIM), F32),
        compiler_params=_compiler_params(("parallel", "parallel")),
        name="nsa_compress",
    )(r, pe, w1, w2)


def _nsa_cmp_kernel(q_ref, kc_ref, vc_ref, gate_ref, ov_ref, o_ref, sel_ref):
    i = pl.program_id(2)
    rows = GROUP_HEADS * Q_TILE
    q = q_ref[...].reshape(rows, HEAD_DIM)
    n_cmp = kc_ref.shape[0]
    s = lax.dot_general(q, kc_ref[...].astype(BF16), _NT, preferred_element_type=F32)
    t = _query_positions(i, n_cmp)
    n = lax.broadcasted_iota(jnp.int32, (rows, n_cmp), 1)
    valid = n * NSA_CMP_STRIDE + (NSA_CMP_BLOCK - 1) <= t
    s = jnp.where(valid, s, MASKED)
    m = jnp.max(s, axis=-1, keepdims=True)
    p = jnp.where(valid, jnp.exp(s - m), 0.0)
    p = p * (1.0 / jnp.maximum(jnp.sum(p, axis=-1, keepdims=True), 1e-30))
    o = jnp.dot(p.astype(BF16), vc_ref[...].astype(BF16), preferred_element_type=F32)
    _store_heads(o_ref, o, jax.nn.sigmoid(gate_ref[...]), 0)

    p_sum = p[0:Q_TILE]
    for e in range(1, GROUP_HEADS):
        p_sum = p_sum + p[e * Q_TILE:(e + 1) * Q_TILE]
    imp = lax.dot_general(ov_ref[...], p_sum, _NT, preferred_element_type=F32, precision=lax.Precision.HIGHEST)
    n_sel = imp.shape[0]
    jb = lax.broadcasted_iota(jnp.int32, (n_sel, Q_TILE), 0)
    lane = lax.broadcasted_iota(jnp.int32, (n_sel, Q_TILE), 1)
    cur = (i * Q_TILE + lane) // NSA_SEL_BLOCK
    valid_b = jb <= cur
    forced = jnp.where(valid_b, jnp.where(jb == 0, 1.0, 0.0) + jnp.where(jb == cur, 1.0, 0.0) + jnp.where(jb == cur - 1, 1.0, 0.0), 0.0)
    score = jnp.where(forced > 0.5, NSA_FORCE_SCORE, jnp.where(valid_b, imp, -jnp.inf))
    rank = jnp.zeros((n_sel, Q_TILE), F32)
    for ii in range(n_sel):
        row = score[ii:ii + 1, :]
        tie = jnp.where(jb > ii, 1.0, 0.0)
        rank = rank + jnp.where(row > score, 1.0, jnp.where(row == score, tie, 0.0))
    sel_t = jnp.where(valid_b, jnp.where(rank < NSA_TOPN, 1.0, 0.0), 0.0)
    sel_t = jnp.concatenate([sel_t, jnp.zeros((LANES - n_sel, Q_TILE), F32)], axis=0)
    sel_ref[...] = sel_t.T.astype(BF16)


def _nsa_cmp(q, k_cmp, v_cmp, gates, overlap_t):
    _, bsz, seq, hd = q.shape
    ng = k_cmp.shape[0]
    width = GROUP_HEADS * hd
    return pl.pallas_call(
        _nsa_cmp_kernel,
        grid=(bsz, ng, seq // Q_TILE),
        in_specs=[pl.BlockSpec((GROUP_HEADS, None, Q_TILE, hd), lambda b, g, i: (g, b, i, 0)),
                  pl.BlockSpec((None, None, k_cmp.shape[2], hd), lambda b, g, i: (g, b, 0, 0)),
                  pl.BlockSpec((None, None, k_cmp.shape[2], hd), lambda b, g, i: (g, b, 0, 0)),
                  pl.BlockSpec((None, None, Q_TILE, gates.shape[-1]), lambda b, g, i: (g, b, i, 0)),
                  pl.BlockSpec(overlap_t.shape, lambda b, g, i: (0, 0))],
        out_specs=[pl.BlockSpec((None, Q_TILE, width), lambda b, g, i: (b, i, g)),
                   pl.BlockSpec((None, None, Q_TILE, LANES), lambda b, g, i: (g, b, i, 0))],
        out_shape=[jax.ShapeDtypeStruct((bsz, seq, ng * width), F32),
                   jax.ShapeDtypeStruct((ng, bsz, seq, LANES), BF16)],
        compiler_params=_compiler_params(("parallel", "parallel", "parallel")),
        name="nsa_cmp",
    )(q, k_cmp, v_cmp, gates, overlap_t)


def _nsa_sel_kernel(q_ref, k_ref, v_ref, sel_ref, gate_ref, o_ref, m_ref, l_ref, acc_ref):
    i = pl.program_id(2)
    rows = GROUP_HEADS * Q_TILE
    q = q_ref[...].reshape(rows, HEAD_DIM)
    sel = sel_ref[...]
    m_ref[...] = jnp.full(m_ref.shape, MASKED, F32)
    l_ref[...] = jnp.zeros(l_ref.shape, F32)
    acc_ref[...] = jnp.zeros(acc_ref.shape, F32)
    t = i * Q_TILE + lax.broadcasted_iota(jnp.int32, (Q_TILE, SEL_CHUNK), 0)
    col = lax.broadcasted_iota(jnp.int32, (Q_TILE, SEL_CHUNK), 1)
    blk_row = lax.broadcasted_iota(jnp.int32, (LANES, SEL_CHUNK), 0)
    blk_col = lax.broadcasted_iota(jnp.int32, (LANES, SEL_CHUNK), 1)

    def body(c, carry):
        start = pl.multiple_of(c * SEL_CHUNK, SEL_CHUNK)
        k = k_ref[pl.ds(start, SEL_CHUNK), :]
        v = v_ref[pl.ds(start, SEL_CHUNK), :]
        s = lax.dot_general(q, k, _NT, preferred_element_type=F32)
        expand = jnp.where((start + blk_col) // NSA_SEL_BLOCK == blk_row, 1.0, 0.0).astype(BF16)
        picked = jnp.dot(sel, expand, preferred_element_type=F32)
        keep = jnp.where(start + col <= t, picked, 0.0)
        bias = (keep - 1.0) * (-MASKED)
        s = s + jnp.concatenate([bias] * GROUP_HEADS, axis=0)
        m_old = m_ref[...]
        m_new = jnp.maximum(m_old, jnp.max(s, axis=-1, keepdims=True))
        alpha = jnp.exp(m_old - m_new)
        p = jnp.exp(s - m_new)
        l_ref[...] = alpha * l_ref[...] + jnp.sum(p, axis=-1, keepdims=True)
        acc_ref[...] = alpha * acc_ref[...] + jnp.dot(p.astype(BF16), v, preferred_element_type=F32)
        m_ref[...] = m_new
        return carry

    n_chunks = (i * Q_TILE + Q_TILE + SEL_CHUNK - 1) // SEL_CHUNK
    lax.fori_loop(0, n_chunks, body, 0)
    o = acc_ref[...] * (1.0 / jnp.maximum(l_ref[...], 1e-30))
    _store_heads(o_ref, o, jax.nn.sigmoid(gate_ref[...]), 1)


def _nsa_sel(q, k, v, sel, gates):
    _, bsz, seq, hd = q.shape
    ng = k.shape[0]
    width = GROUP_HEADS * hd
    rows = GROUP_HEADS * Q_TILE
    return pl.pallas_call(
        _nsa_sel_kernel,
        grid=(bsz, ng, seq // Q_TILE),
        in_specs=[pl.BlockSpec((GROUP_HEADS, None, Q_TILE, hd), lambda b, g, i: (g, b, i, 0)),
                  pl.BlockSpec((None, None, seq, hd), lambda b, g, i: (g, b, 0, 0)),
                  pl.BlockSpec((None, None, seq, hd), lambda b, g, i: (g, b, 0, 0)),
                  pl.BlockSpec((None, None, Q_TILE, LANES), lambda b, g, i: (g, b, i, 0)),
                  pl.BlockSpec((None, None, Q_TILE, gates.shape[-1]), lambda b, g, i: (g, b, i, 0))],
        out_specs=pl.BlockSpec((None, Q_TILE, width), lambda b, g, i: (b, i, g)),
        out_shape=jax.ShapeDtypeStruct((bsz, seq, ng * width), F32),
        scratch_shapes=[pltpu.VMEM((rows, 1), F32), pltpu.VMEM((rows, 1), F32), pltpu.VMEM((rows, hd), F32)],
        compiler_params=_compiler_params(("parallel", "parallel", "parallel")),
        name="nsa_sel",
    )(q, k, v, sel, gates)


def _band_kernel(window, n_keys, gate_branch, has_sinks, q_ref, k_ref, v_ref, *refs):
    i = pl.program_id(2)
    rows = GROUP_HEADS * Q_TILE
    refs = list(refs)
    gate_ref = refs.pop(0) if gate_branch is not None else None
    sink_ref = refs.pop(0) if has_sinks else None
    (o_ref,) = refs
    q = q_ref[...].reshape(rows, HEAD_DIM)
    start = pl.multiple_of(jnp.maximum(i * Q_TILE + Q_TILE - n_keys, 0), Q_TILE)
    k = k_ref[pl.ds(start, n_keys), :]
    v = v_ref[pl.ds(start, n_keys), :]
    s = lax.dot_general(q, k, _NT, preferred_element_type=F32)
    t = i * Q_TILE + lax.broadcasted_iota(jnp.int32, (Q_TILE, n_keys), 0)
    rel = t - (start + lax.broadcasted_iota(jnp.int32, (Q_TILE, n_keys), 1))
    bias = jnp.where(rel >= 0, jnp.where(rel < window, 0.0, MASKED), MASKED)
    s = s + jnp.concatenate([bias] * GROUP_HEADS, axis=0)
    m = jnp.max(s, axis=-1, keepdims=True)
    if has_sinks:
        sink = jnp.concatenate([jnp.broadcast_to(sink_ref[:, e:e + 1], (Q_TILE, 1)) for e in range(GROUP_HEADS)], axis=0)
        m = jnp.maximum(m, sink)
    p = jnp.exp(s - m)
    denom = jnp.sum(p, axis=-1, keepdims=True)
    if has_sinks:
        denom = denom + jnp.exp(sink - m)
    else:
        denom = jnp.maximum(denom, 1e-30)
    o = jnp.dot(p.astype(BF16), v, preferred_element_type=F32) * (1.0 / denom)
    gate = jax.nn.sigmoid(gate_ref[...]) if gate_ref is not None else None
    _store_heads(o_ref, o, gate, gate_branch)


def _band_attention(q, k, v, window, gates=None, gate_branch=None, sinks=None):
    _, bsz, seq, hd = q.shape
    ng = k.shape[0]
    width = GROUP_HEADS * hd
    nprev = -(-(window - 1) // Q_TILE)
    n_keys = (nprev + 1) * Q_TILE
    in_specs = [pl.BlockSpec((GROUP_HEADS, None, Q_TILE, hd), lambda b, g, i: (g, b, i, 0)),
                pl.BlockSpec((None, None, seq, hd), lambda b, g, i: (g, b, 0, 0)),
                pl.BlockSpec((None, None, seq, hd), lambda b, g, i: (g, b, 0, 0))]
    args = [q, k, v]
    if gates is not None:
        in_specs.append(pl.BlockSpec((None, None, Q_TILE, gates.shape[-1]), lambda b, g, i: (g, b, i, 0)))
        args.append(gates)
    if sinks is not None:
        in_specs.append(pl.BlockSpec((None, 1, GROUP_HEADS), lambda b, g, i: (g, 0, 0)))
        args.append(sinks)
    return pl.pallas_call(
        functools.partial(_band_kernel, window, n_keys, gate_branch if gates is not None else None, sinks is not None),
        grid=(bsz, ng, seq // Q_TILE),
        in_specs=in_specs,
        out_specs=pl.BlockSpec((None, Q_TILE, width), lambda b, g, i: (b, i, g)),
        out_shape=jax.ShapeDtypeStruct((bsz, seq, ng * width), F32),
        compiler_params=_compiler_params(("parallel", "parallel", "parallel")),
        name="band_attention",
    )(*args)


def _overlap_t(seq):
    n_cmp = (seq - NSA_CMP_BLOCK) // NSA_CMP_STRIDE + 1
    n_pad = seq // NSA_CMP_STRIDE
    starts = np.arange(n_pad) * NSA_CMP_STRIDE
    sel_start = np.arange(seq // NSA_SEL_BLOCK) * NSA_SEL_BLOCK
    ov = (starts[None, :] < sel_start[:, None] + NSA_SEL_BLOCK) & (starts[None, :] + NSA_CMP_BLOCK > sel_start[:, None])
    ov = ov & (np.arange(n_pad)[None, :] < n_cmp)
    return jnp.asarray(ov.astype(np.float32))


def _odd_mixer(xt, nw, w_in, w_out, cmp_pe, cmp_w1, cmp_w2, sinks, bsz):
    t, d = xt.shape
    seq = t // bsz
    assert seq // NSA_SEL_BLOCK <= LANES and seq % SEL_CHUNK == 0
    cuts = [int(c) for c in np.cumsum((0,) + ODD_SPLITS)]
    ws = [w_in[:, a:b] for a, b in zip(cuts[:-1], cuts[1:])]
    scale = HEAD_DIM ** -0.5
    ws[0] = ws[0] * scale
    ws[8] = ws[8] * scale
    hd = HEAD_DIM
    head_dims = (hd, hd, hd, hd, hd, hd, hd, 3 * GROUP_HEADS, hd, hd, hd)
    dtypes = (BF16, F32, F32, BF16, BF16, BF16, BF16, F32, BF16, BF16, BF16)
    q_n, kc, vc, ks, vs, kw, vw, gates, q_s, k_s, v_s = _norm_proj(xt, nw, ws, head_dims, dtypes)

    def heads(a):
        return a.reshape(a.shape[0], bsz, seq, a.shape[-1])

    q_n, ks, vs, kw, vw, gates, q_s, k_s, v_s = map(heads, (q_n, ks, vs, kw, vw, gates, q_s, k_s, v_s))
    ng = NSA_KV_HEADS
    rows16 = NSA_CMP_STRIDE * hd
    pe = cmp_pe.reshape(2, 2, rows16)
    w1 = cmp_w1.astype(BF16)
    w2 = cmp_w2.astype(BF16)
    k_cmp = _cmp_kv(kc.reshape(ng, bsz, seq // NSA_CMP_STRIDE, rows16), pe[0], w1[0], w2[0])
    v_cmp = _cmp_kv(vc.reshape(ng, bsz, seq // NSA_CMP_STRIDE, rows16), pe[1], w1[1], w2[1])
    o_cmp, sel = _nsa_cmp(q_n, k_cmp, v_cmp, gates, _overlap_t(seq))
    o_sel = _nsa_sel(q_n, ks, vs, sel, gates)
    o_win = _band_attention(q_n, kw, vw, NSA_WINDOW, gates=gates, gate_branch=2)
    o_swa = _band_attention(q_s, k_s, v_s, SWA_WINDOW, sinks=sinks.reshape(SWA_KV_HEADS, 1, GROUP_HEADS))
    n_nsa = NSA_HEADS * hd
    return _proj_residual(xt, [[o_cmp.reshape(t, n_nsa), o_sel.reshape(t, n_nsa), o_win.reshape(t, n_nsa)],
                               [o_swa.reshape(t, SWA_HEADS * hd)]], [w_out[:n_nsa], w_out[n_nsa:]])


def kernel(x, mem, norm_w, final_norm_w, mem_norm_w, ffn_w_gate, ffn_w_up, ffn_w_down, xattn_wq, xattn_wkv, xattn_wo, even_w_in, even_w_out, ssm_conv_w, ssm_conv_b, ssm_dt_bias, ssm_a_log, ssm_d, ssm_norm_w, odd_w_in, odd_w_out, nsa_cmp_pe, nsa_cmp_w1, nsa_cmp_w2, swa_sinks):
    bsz, seq, d = x.shape
    t = bsz * seq
    m = mem.shape[1]
    wg = ffn_w_gate.astype(BF16)
    wu = ffn_w_up.astype(BF16)
    wd = ffn_w_down.astype(BF16)
    wq = xattn_wq.astype(BF16)
    wkv = xattn_wkv.astype(BF16)
    wo = xattn_wo.astype(BF16)
    even_in = even_w_in.astype(BF16)
    even_out = even_w_out.astype(BF16)
    odd_in = odd_w_in.astype(BF16)
    odd_out = odd_w_out.astype(BF16)

    xt = x.reshape(t, d)
    mem2 = mem.reshape(bsz * m, d)
    mem_nw = mem_norm_w.reshape(1, d)
    for layer in range(DEPTH):
        i = layer // 2
        xt = _ffn(xt, norm_w[layer, 0].reshape(1, d), wg, wu, wd, layer, 0)
        nw1 = norm_w[layer, 1].reshape(1, d)
        if layer % 2 == 0:
            xt = _even_mixer(xt, nw1, even_in[i], even_out[i], ssm_conv_w[i], ssm_conv_b[i], ssm_dt_bias[i], ssm_a_log[i], ssm_d[i], ssm_norm_w[i], bsz)
        else:
            xt = _odd_mixer(xt, nw1, odd_in[i], odd_out[i], nsa_cmp_pe[i], nsa_cmp_w1[i], nsa_cmp_w2[i], swa_sinks[i], bsz)
        (kv,) = _norm_proj(mem2, mem_nw, [wkv[layer]])
        xt = _xattn(xt, norm_w[layer, 2].reshape(1, d), wq[layer], kv.reshape(bsz, m, 2 * d), wo[layer], bsz)
        xt = _ffn(xt, norm_w[layer, 3].reshape(1, d), wg, wu, wd, layer, 1)
    out = _final_norm(xt, final_norm_w.reshape(1, d))
    return out.reshape(bsz, seq, d)
```

```python
import functools

import numpy as np
import jax
import jax.numpy as jnp
from jax import lax
from jax.experimental import pallas as pl
from jax.experimental.pallas import tpu as pltpu

F32 = jnp.float32
BF16 = jnp.bfloat16

D_MODEL = 1024
DEPTH = 4
D_FF = 2816
NORM_EPS = 1e-6
ATTN_BLOCK = 128

RET_HEADS = 4
RET_DK = 128
RET_DV = 128
RET_CHUNK = 128
RET_ROPE_BASE = 10000.0

SSM_HEADS = 8
SSM_HEAD_DIM = 64
SSM_D_INNER = SSM_HEADS * SSM_HEAD_DIM
SSM_D_STATE = 64
SSM_GROUPS = 2
SSM_CONV = 4
SSM_CHUNK = 128
SSM_CONV_DIM = SSM_D_INNER + 2 * SSM_GROUPS * SSM_D_STATE

EVEN_SPLITS = (RET_HEADS * RET_DK, RET_HEADS * RET_DK, RET_HEADS * RET_DV, RET_HEADS * RET_DV, SSM_D_INNER, SSM_CONV_DIM, SSM_HEADS)

NSA_HEADS = 8
NSA_KV_HEADS = 2
NSA_HEAD_DIM = 64
NSA_CMP_BLOCK = 32
NSA_CMP_STRIDE = 16
NSA_CMP_HIDDEN = 256
NSA_SEL_BLOCK = 64
NSA_TOPN = 16
NSA_WINDOW = 512
NSA_FORCE_SCORE = 1e6

SWA_HEADS = 8
SWA_KV_HEADS = 2
SWA_HEAD_DIM = 64
SWA_WINDOW = 128

ODD_SPLITS = (NSA_HEADS * NSA_HEAD_DIM,) + (NSA_KV_HEADS * NSA_HEAD_DIM,) * 6 + (3 * NSA_HEADS, SWA_HEADS * SWA_HEAD_DIM, SWA_KV_HEADS * SWA_HEAD_DIM, SWA_KV_HEADS * SWA_HEAD_DIM)

XATTN_HEADS = 4
XATTN_HEAD_DIM = D_MODEL // XATTN_HEADS

VMEM_LIMIT_BYTES = 52 * 1024 * 1024
TOKEN_TILE = 512
FFN_TILE = 1408
LANES = 128
MASKED = -1e30

_NT = (((1,), (1,)), ((), ()))


def _compiler_params(semantics):
    return pltpu.CompilerParams(dimension_semantics=semantics, vmem_limit_bytes=VMEM_LIMIT_BYTES)


def _rms(x, w):
    return x * lax.rsqrt(jnp.mean(x * x, axis=-1, keepdims=True) + NORM_EPS) * w


def _ffn_kernel(x_ref, nw_ref, wg_ref, wu_ref, wd_ref, o_ref, h_ref, acc_ref):
    j = pl.program_id(1)

    @pl.when(j == 0)
    def _():
        h_ref[...] = _rms(x_ref[...], nw_ref[...]).astype(BF16)
        acc_ref[...] = jnp.zeros_like(acc_ref)

    h = h_ref[...]
    g = jnp.dot(h, wg_ref[...], preferred_element_type=F32)
    u = jnp.dot(h, wu_ref[...], preferred_element_type=F32)
    a = (g * jax.nn.sigmoid(g) * u).astype(BF16)
    acc_ref[...] += jnp.dot(a, wd_ref[...], preferred_element_type=F32)

    @pl.when(j == pl.num_programs(1) - 1)
    def _():
        o_ref[...] = x_ref[...] + 0.5 * acc_ref[...]


def _ffn(x, nw, wg, wu, wd, layer, k):
    t, d = x.shape
    tm, tf = TOKEN_TILE, FFN_TILE
    return pl.pallas_call(
        _ffn_kernel,
        grid=(t // tm, D_FF // tf),
        in_specs=[
            pl.BlockSpec((tm, d), lambda i, j: (i, 0)),
            pl.BlockSpec((1, d), lambda i, j: (0, 0)),
            pl.BlockSpec((None, None, d, tf), lambda i, j: (layer, k, 0, j)),
            pl.BlockSpec((None, None, d, tf), lambda i, j: (layer, k, 0, j)),
            pl.BlockSpec((None, None, tf, d), lambda i, j: (layer, k, j, 0)),
        ],
        out_specs=pl.BlockSpec((tm, d), lambda i, j: (i, 0)),
        out_shape=jax.ShapeDtypeStruct((t, d), F32),
        scratch_shapes=[pltpu.VMEM((tm, d), BF16), pltpu.VMEM((tm, d), F32)],
        compiler_params=_compiler_params(("parallel", "arbitrary")),
        name="ffn",
    )(x, nw, wg, wu, wd)


def _norm_proj_kernel(head_dims, x_ref, nw_ref, *refs):
    n_out = len(head_dims)
    w_refs, o_refs = refs[:n_out], refs[n_out:]
    h = _rms(x_ref[...], nw_ref[...]).astype(BF16)
    for hd, w_ref, o_ref in zip(head_dims, w_refs, o_refs):
        y = jnp.dot(h, w_ref[...], preferred_element_type=F32)
        if hd is None:
            o_ref[...] = y.astype(o_ref.dtype)
        else:
            for j in range(o_ref.shape[0]):
                o_ref[j] = y[:, j * hd:(j + 1) * hd].astype(o_ref.dtype)


def _norm_proj(x, nw, weights, head_dims=None, dtypes=None, tm=TOKEN_TILE):
    t, d = x.shape
    n_out = len(weights)
    head_dims = tuple(head_dims) if head_dims is not None else (None,) * n_out
    dtypes = tuple(dtypes) if dtypes is not None else (F32,) * n_out
    out_specs, out_shape = [], []
    for w, hd, dt in zip(weights, head_dims, dtypes):
        n = w.shape[1]
        if hd is None:
            out_specs.append(pl.BlockSpec((tm, n), lambda i: (i, 0)))
            out_shape.append(jax.ShapeDtypeStruct((t, n), dt))
        else:
            out_specs.append(pl.BlockSpec((n // hd, tm, hd), lambda i: (0, i, 0)))
            out_shape.append(jax.ShapeDtypeStruct((n // hd, t, hd), dt))
    return pl.pallas_call(
        functools.partial(_norm_proj_kernel, head_dims),
        grid=(t // tm,),
        in_specs=[pl.BlockSpec((tm, d), lambda i: (i, 0)), pl.BlockSpec((1, d), lambda i: (0, 0))]
        + [pl.BlockSpec(w.shape, lambda i: (0, 0)) for w in weights],
        out_specs=out_specs,
        out_shape=out_shape,
        compiler_params=_compiler_params(("parallel",)),
        name="norm_proj",
    )(x, nw, *weights)


def _proj_residual_kernel(group_sizes, x_ref, *refs):
    n_a = sum(group_sizes)
    a_refs, w_refs, o_ref = refs[:n_a], refs[n_a:-1], refs[-1]
    acc = x_ref[...]
    pos = 0
    for size, w_ref in zip(group_sizes, w_refs):
        a = a_refs[pos][...]
        for r in a_refs[pos + 1:pos + size]:
            a = a + r[...]
        pos += size
        acc = acc + jnp.dot(a.astype(BF16), w_ref[...], preferred_element_type=F32)
    o_ref[...] = acc


def _proj_residual(x, groups, weights, tm=TOKEN_TILE):
    t, d = x.shape
    arrays = [a for grp in groups for a in grp]
    return pl.pallas_call(
        functools.partial(_proj_residual_kernel, tuple(len(grp) for grp in groups)),
        grid=(t // tm,),
        in_specs=[pl.BlockSpec((tm, d), lambda i: (i, 0))]
        + [pl.BlockSpec((tm, a.shape[1]), lambda i: (i, 0)) for a in arrays]
        + [pl.BlockSpec(w.shape, lambda i: (0, 0)) for w in weights],
        out_specs=pl.BlockSpec((tm, d), lambda i: (i, 0)),
        out_shape=jax.ShapeDtypeStruct((t, d), F32),
        compiler_params=_compiler_params(("parallel",)),
        name="proj_residual",
    )(x, *arrays, *weights)


def _xattn_kernel(x_ref, nw_ref, wq_ref, kv_ref, wo_ref, o_ref):
    x = x_ref[...]
    h = _rms(x, nw_ref[...]).astype(BF16)
    q = jnp.dot(h, wq_ref[...], preferred_element_type=F32).astype(BF16)
    hd = XATTN_HEAD_DIM
    outs = []
    for hh in range(XATTN_HEADS):
        k = kv_ref[:, hh * hd:(hh + 1) * hd].astype(BF16)
        v = kv_ref[:, D_MODEL + hh * hd:D_MODEL + (hh + 1) * hd].astype(BF16)
        s = lax.dot_general(q[:, hh * hd:(hh + 1) * hd], k, _NT, preferred_element_type=F32)
        s = s * (hd ** -0.5)
        p = jnp.exp(s - jnp.max(s, axis=-1, keepdims=True))
        p = p / jnp.sum(p, axis=-1, keepdims=True)
        outs.append(jnp.dot(p.astype(BF16), v, preferred_element_type=F32).astype(BF16))
    o = jnp.concatenate(outs, axis=-1)
    o_ref[...] = x + jnp.dot(o, wo_ref[...], preferred_element_type=F32)


def _xattn(x, nw, wq, kv, wo, bsz, tm=TOKEN_TILE):
    t, d = x.shape
    seq = t // bsz
    nt = seq // tm
    m = kv.shape[1]
    return pl.pallas_call(
        _xattn_kernel,
        grid=(bsz, nt),
        in_specs=[pl.BlockSpec((tm, d), lambda b, i: (b * nt + i, 0)), pl.BlockSpec((1, d), lambda b, i: (0, 0)),
                  pl.BlockSpec((d, d), lambda b, i: (0, 0)), pl.BlockSpec((None, m, 2 * d), lambda b, i: (b, 0, 0)),
                  pl.BlockSpec((d, d), lambda b, i: (0, 0))],
        out_specs=pl.BlockSpec((tm, d), lambda b, i: (b * nt + i, 0)),
        out_shape=jax.ShapeDtypeStruct((t, d), F32),
        compiler_params=_compiler_params(("parallel", "parallel")),
        name="xattn",
    )(x, nw, wq, kv, wo)


def _final_norm_kernel(x_ref, nw_ref, o_ref):
    o_ref[...] = _rms(x_ref[...], nw_ref[...])


def _final_norm(x, nw, tm=TOKEN_TILE):
    t, d = x.shape
    return pl.pallas_call(
        _final_norm_kernel,
        grid=(t // tm,),
        in_specs=[pl.BlockSpec((tm, d), lambda i: (i, 0)), pl.BlockSpec((1, d), lambda i: (0, 0))],
        out_specs=pl.BlockSpec((tm, d), lambda i: (i, 0)),
        out_shape=jax.ShapeDtypeStruct((t, d), F32),
        compiler_params=_compiler_params(("parallel",)),
        name="final_norm",
    )(x, nw)


def _retention_kernel(q_ref, k_ref, v_ref, g_ref, cos_ref, sin_ref, dmat_ref, zeta_ref, xi_ref, decay_ref, o_ref, state_ref):
    @pl.when(pl.program_id(1) == 0)
    def _():
        state_ref[...] = jnp.zeros_like(state_ref)

    cos = cos_ref[...]
    sin = sin_ref[...]
    half = RET_DK // 2
    outs = []
    for h in range(RET_HEADS):
        qh = q_ref[:, h * RET_DK:(h + 1) * RET_DK]
        kh = k_ref[:, h * RET_DK:(h + 1) * RET_DK]
        qr = qh * cos + pltpu.roll(qh, half, 1) * sin
        kr = (kh * cos + pltpu.roll(kh, half, 1) * sin) * (RET_DK ** -0.5)
        vh = v_ref[:, h * RET_DV:(h + 1) * RET_DV].astype(BF16)
        scores = lax.dot_general(qr.astype(BF16), kr.astype(BF16), _NT, preferred_element_type=F32) * dmat_ref[h]
        inner = jnp.dot(scores.astype(BF16), vh, preferred_element_type=F32)
        state = state_ref[h]
        cross = jnp.dot((qr * xi_ref[h]).astype(BF16), state.astype(BF16), preferred_element_type=F32)
        kz = (kr * zeta_ref[h]).T.astype(BF16)
        state_ref[h] = decay_ref[h] * state + jnp.dot(kz, vh, preferred_element_type=F32)
        o = inner + cross
        o = o * lax.rsqrt(jnp.mean(o * o, axis=-1, keepdims=True) + NORM_EPS)
        gh = g_ref[:, h * RET_DV:(h + 1) * RET_DV]
        outs.append(gh * jax.nn.sigmoid(gh) * o)
    o_ref[...] = jnp.concatenate(outs, axis=-1)


def _retention_tables(seq):
    L = RET_CHUNK
    pos = jnp.arange(seq, dtype=F32)
    inv_freq = 1.0 / (RET_ROPE_BASE ** jnp.linspace(0.0, 1.0, RET_DK // 2, dtype=F32))
    ang = pos[:, None] * inv_freq[None, :]
    cos, sin = jnp.cos(ang), jnp.sin(ang)
    cos2 = jnp.concatenate([cos, cos], axis=-1)
    sin2 = jnp.concatenate([-sin, sin], axis=-1)
    log_g = jnp.log1p(-jnp.exp2(-5.0 - jnp.arange(RET_HEADS, dtype=F32)))
    idx = jnp.arange(L, dtype=F32)
    diff = idx[:, None] - idx[None, :]
    dmat = jnp.where(diff >= 0, jnp.exp(jnp.maximum(diff, 0.0)[None] * log_g[:, None, None]), 0.0)
    ones = jnp.ones((RET_HEADS, L, RET_DK), F32)
    zeta = jnp.exp((L - 1 - idx)[None, :] * log_g[:, None])[:, :, None] * ones
    xi = jnp.exp((idx + 1)[None, :] * log_g[:, None])[:, :, None] * ones
    decay = jnp.exp(L * log_g)[:, None, None] * ones
    return cos2, sin2, dmat, zeta, xi, decay


def _retention(q, k, v, g):
    bsz, seq, width = q.shape
    L = RET_CHUNK
    tables = _retention_tables(seq)
    tok = pl.BlockSpec((None, L, width), lambda b, c: (b, c, 0))
    rope = pl.BlockSpec((L, RET_DK), lambda b, c: (c, 0))
    const = pl.BlockSpec((RET_HEADS, L, RET_DK), lambda b, c: (0, 0, 0))
    return pl.pallas_call(
        _retention_kernel,
        grid=(bsz, seq // L),
        in_specs=[tok, tok, tok, tok, rope, rope, const, const, const, const],
        out_specs=tok,
        out_shape=jax.ShapeDtypeStruct((bsz, seq, width), F32),
        scratch_shapes=[pltpu.VMEM((RET_HEADS, RET_DK, RET_DV), F32)],
        compiler_params=_compiler_params(("parallel", "arbitrary")),
        name="retention",
    )(q, k, v, g, *tables)


CONV_PAD = 8


def _ssd_kernel(xbc_ref, dt_ref, z_ref, convw_ref, convb_ref, dtb_ref, alog_ref, dskip_ref, nw_ref, tri_ref, o_ref, ext_ref, state_ref):
    L = SSM_CHUNK
    hp = SSM_HEAD_DIM
    ns = SSM_D_STATE

    @pl.when(pl.program_id(1) == 0)
    def _():
        ext_ref[0:CONV_PAD, :] = jnp.zeros((CONV_PAD, SSM_CONV_DIM), F32)
        state_ref[...] = jnp.zeros_like(state_ref)

    ext_ref[CONV_PAD:CONV_PAD + L, :] = xbc_ref[...]
    conv = convb_ref[...]
    for w in range(SSM_CONV):
        off = CONV_PAD - (SSM_CONV - 1) + w
        conv = conv + ext_ref[off:off + L, :] * convw_ref[w:w + 1, :]
    ext_ref[0:CONV_PAD, :] = ext_ref[L:L + CONV_PAD, :]
    xa = conv * jax.nn.sigmoid(conv)
    xs = xa[:, :SSM_D_INNER]
    bmat = xa[:, SSM_D_INNER:SSM_D_INNER + SSM_GROUPS * ns]
    cmat = xa[:, SSM_D_INNER + SSM_GROUPS * ns:]

    x = dt_ref[...] + dtb_ref[...]
    dt = jnp.maximum(x, 0.0) + jnp.log1p(jnp.exp(-jnp.abs(x)))
    lane = lax.broadcasted_iota(jnp.int32, (L, LANES), 1)
    dta = jnp.where(lane < SSM_HEADS, dt * -jnp.exp(alog_ref[...]), 0.0)
    cums = jnp.dot(tri_ref[...], dta, preferred_element_type=F32, precision=lax.Precision.HIGHEST)
    cums_t = cums.T
    last = cums[L - 1:L, :]
    decay_to_end = jnp.exp(last - cums)
    decay_from_start = jnp.exp(cums)
    chunk_decay = jnp.exp(last)
    causal = lax.broadcasted_iota(jnp.int32, (L, L), 0) >= lax.broadcasted_iota(jnp.int32, (L, L), 1)
    heads_per_group = SSM_HEADS // SSM_GROUPS
    ys = []
    for g in range(SSM_GROUPS):
        bg = bmat[:, g * ns:(g + 1) * ns].astype(BF16)
        cg = cmat[:, g * ns:(g + 1) * ns].astype(BF16)
        cb = lax.dot_general(cg, bg, _NT, preferred_element_type=F32)
        for e in range(heads_per_group):
            h = g * heads_per_group + e
            xh = xs[:, h * hp:(h + 1) * hp]
            xdt = xh * dt[:, h:h + 1]
            seg = cums[:, h:h + 1] - cums_t[h:h + 1, :]
            lmat = jnp.where(causal, jnp.exp(seg), 0.0)
            y_diag = jnp.dot((cb * lmat).astype(BF16), xdt.astype(BF16), preferred_element_type=F32)
            state = state_ref[h]
            y_off = lax.dot_general(cg, state.astype(BF16), _NT, preferred_element_type=F32) * decay_from_start[:, h:h + 1]
            xw = (xdt * decay_to_end[:, h:h + 1]).astype(BF16)
            new = lax.dot_general(xw, bg, (((0,), (0,)), ((), ())), preferred_element_type=F32)
            state_ref[h] = chunk_decay[:, h:h + 1] * state + new
            ys.append(y_diag + y_off + dskip_ref[:, h * hp:(h + 1) * hp] * xh)
    y = jnp.concatenate(ys, axis=-1)
    z = z_ref[...]
    o_ref[...] = _rms(y * (z * jax.nn.sigmoid(z)), nw_ref[...])


def _ssd(xbc, dt_raw, z, conv_w, conv_b, dt_bias, a_log, d_skip, norm_w):
    bsz, seq, _ = xbc.shape
    L = SSM_CHUNK
    pad = LANES - SSM_HEADS
    params = [conv_w, conv_b.reshape(1, -1), jnp.pad(dt_bias, (0, pad)).reshape(1, LANES),
              jnp.pad(a_log, (0, pad)).reshape(1, LANES), jnp.repeat(d_skip, SSM_HEAD_DIM).reshape(1, -1),
              norm_w.reshape(1, -1), jnp.asarray(np.tril(np.ones((L, L), np.float32)))]

    def tok(width):
        return pl.BlockSpec((None, L, width), lambda b, c: (b, c, 0))

    return pl.pallas_call(
        _ssd_kernel,
        grid=(bsz, seq // L),
        in_specs=[tok(SSM_CONV_DIM), tok(LANES), tok(SSM_D_INNER)] + [pl.BlockSpec(p.shape, lambda b, c: (0, 0)) for p in params],
        out_specs=tok(SSM_D_INNER),
        out_shape=jax.ShapeDtypeStruct((bsz, seq, SSM_D_INNER), F32),
        scratch_shapes=[pltpu.VMEM((L + CONV_PAD, SSM_CONV_DIM), F32), pltpu.VMEM((SSM_HEADS, SSM_HEAD_DIM, SSM_D_STATE), F32)],
        compiler_params=_compiler_params(("parallel", "arbitrary")),
        name="ssd",
    )(xbc, dt_raw, z, *params)


def _even_mixer(xt, nw, w_in, w_out, conv_w, conv_b, dt_bias, a_log, d_skip, ssm_norm_w, bsz):
    t, d = xt.shape
    seq = t // bsz
    cuts = [int(c) for c in np.cumsum((0,) + EVEN_SPLITS)]
    ws = [w_in[:, a:b] for a, b in zip(cuts[:-1], cuts[1:])]
    perm = np.concatenate([h * RET_DK + np.concatenate([np.arange(0, RET_DK, 2), np.arange(1, RET_DK, 2)]) for h in range(RET_HEADS)])
    ws[0] = ws[0][:, perm]
    ws[1] = ws[1][:, perm]
    ws[6] = jnp.pad(ws[6], ((0, 0), (0, LANES - SSM_HEADS)))
    q, k, v, g, z, xbc, dt_raw = [a.reshape(bsz, seq, -1) for a in _norm_proj(xt, nw, ws)]
    o_ret = _retention(q, k, v, g)
    y = _ssd(xbc, dt_raw, z, conv_w, conv_b, dt_bias, a_log, d_skip, ssm_norm_w)
    n_ret = RET_HEADS * RET_DV
    return _proj_residual(xt, [[o_ret.reshape(t, n_ret)], [y.reshape(t, SSM_D_INNER)]], [w_out[:n_ret], w_out[n_ret:]])


Q_TILE = ATTN_BLOCK
GROUP_HEADS = NSA_HEADS // NSA_KV_HEADS
HEAD_DIM = NSA_HEAD_DIM
SEL_CHUNK = 512


def _cmp_kv_kernel(r_ref, pe_ref, w1_ref, w2_ref, o_ref):
    r = r_ref[...]
    half = NSA_CMP_STRIDE * HEAD_DIM
    top = jnp.dot((r + pe_ref[0:1, :]).astype(BF16), w1_ref[0:half, :], preferred_element_type=F32)
    bot = jnp.dot((r + pe_ref[1:2, :]).astype(BF16), w1_ref[half:2 * half, :], preferred_element_type=F32)
    hidden = top + pltpu.roll(bot, r.shape[0] - 1, 0)
    act = hidden * jax.nn.sigmoid(hidden)
    o_ref[...] = jnp.dot(act.astype(BF16), w2_ref[...], preferred_element_type=F32)


def _cmp_kv(r, pe, w1, w2):
    ng, bsz, n, width = r.shape
    return pl.pallas_call(
        _cmp_kv_kernel,
        grid=(ng, bsz),
        in_specs=[pl.BlockSpec((None, None, n, width), lambda g, b: (g, b, 0, 0)),
                  pl.BlockSpec(pe.shape, lambda g, b: (0, 0)),
                  pl.BlockSpec(w1.shape, lambda g, b: (0, 0)),
                  pl.BlockSpec(w2.shape, lambda g, b: (0, 0))],
        out_specs=pl.BlockSpec((None, None, n, HEAD_DIM), lambda g, b: (g, b, 0, 0)),
        out_shape=jax.ShapeDtypeStruct((ng, bsz, n, HEAD_DIM), F32),
        compiler_params=_compiler_params(("parallel", "parallel")),
        name="nsa_compress",
    )(r, pe, w1, w2)


def _heads_from_t(o_t, gate, branch):
    outs = []
    for e in range(GROUP_HEADS):
        blk = o_t[:, e * Q_TILE:(e + 1) * Q_TILE]
        blk = jnp.concatenate([blk, jnp.zeros((LANES - HEAD_DIM, Q_TILE), F32)], axis=0)
        oe = blk.T[:, :HEAD_DIM]
        if gate is not None:
            oe = oe * gate[:, 3 * e + branch:3 * e + branch + 1]
        outs.append(oe)
    return jnp.concatenate(outs, axis=-1)


def _tile_heads(x):
    return jnp.concatenate([x] * GROUP_HEADS, axis=1)


def _pv_t(v, p):
    return lax.dot_general(v, p.astype(BF16), (((0,), (0,)), ((), ())), preferred_element_type=F32)


def _nsa_cmp_kernel(q_ref, kc_ref, vc_ref, gate_ref, ov_ref, o_ref, sel_ref):
    i = pl.program_id(1)
    rows = GROUP_HEADS * Q_TILE
    width = GROUP_HEADS * HEAD_DIM
    n_cmp = kc_ref.shape[1]
    n = lax.broadcasted_iota(jnp.int32, (n_cmp, Q_TILE), 0)
    t = i * Q_TILE + lax.broadcasted_iota(jnp.int32, (n_cmp, Q_TILE), 1)
    valid = _tile_heads(jnp.where(n * NSA_CMP_STRIDE + (NSA_CMP_BLOCK - 1) <= t, 1.0, 0.0))
    bias = (valid - 1.0) * (-MASKED)
    n_sel = ov_ref.shape[0]
    jb = lax.broadcasted_iota(jnp.int32, (n_sel, Q_TILE), 0)
    cur = (i * Q_TILE + lax.broadcasted_iota(jnp.int32, (n_sel, Q_TILE), 1)) // NSA_SEL_BLOCK
    valid_b = jb <= cur
    forced = jnp.where(valid_b, jnp.where(jb == 0, 1.0, 0.0) + jnp.where(jb == cur, 1.0, 0.0) + jnp.where(jb == cur - 1, 1.0, 0.0), 0.0)
    for g in range(NSA_KV_HEADS):
        q = q_ref[g * GROUP_HEADS:(g + 1) * GROUP_HEADS].reshape(rows, HEAD_DIM)
        s = lax.dot_general(kc_ref[g].astype(BF16), q, _NT, preferred_element_type=F32) + bias
        m = jnp.max(s, axis=0, keepdims=True)
        p = jnp.exp(s - m) * valid
        p = p * (1.0 / jnp.maximum(jnp.sum(p, axis=0, keepdims=True), 1e-30))
        o_t = _pv_t(vc_ref[g].astype(BF16), p)
        o_ref[:, g * width:(g + 1) * width] = _heads_from_t(o_t, jax.nn.sigmoid(gate_ref[g]), 0)

        p_sum = p[:, 0:Q_TILE]
        for e in range(1, GROUP_HEADS):
            p_sum = p_sum + p[:, e * Q_TILE:(e + 1) * Q_TILE]
        imp = jnp.dot(ov_ref[...], p_sum, preferred_element_type=F32, precision=lax.Precision.HIGHEST)
        score = jnp.where(forced > 0.5, NSA_FORCE_SCORE, jnp.where(valid_b, imp, -jnp.inf))
        rank = jnp.zeros((n_sel, Q_TILE), F32)
        for ii in range(n_sel):
            row = score[ii:ii + 1, :]
            tie = jnp.where(jb > ii, 1.0, 0.0)
            rank = rank + jnp.where(row > score, 1.0, jnp.where(row == score, tie, 0.0))
        sel_t = jnp.where(valid_b, jnp.where(rank < NSA_TOPN, 1.0, 0.0), 0.0)
        sel_t = jnp.concatenate([sel_t, jnp.zeros((LANES - n_sel, Q_TILE), F32)], axis=0)
        sel_ref[g] = sel_t.T.astype(BF16)


def _nsa_cmp(q, k_cmp, v_cmp, gates, overlap_t):
    nh, bsz, seq, hd = q.shape
    ng, _, n_cmp, _ = k_cmp.shape
    return pl.pallas_call(
        _nsa_cmp_kernel,
        grid=(bsz, seq // Q_TILE),
        in_specs=[pl.BlockSpec((nh, None, Q_TILE, hd), lambda b, i: (0, b, i, 0)),
                  pl.BlockSpec((ng, None, n_cmp, hd), lambda b, i: (0, b, 0, 0)),
                  pl.BlockSpec((ng, None, n_cmp, hd), lambda b, i: (0, b, 0, 0)),
                  pl.BlockSpec((ng, None, Q_TILE, gates.shape[-1]), lambda b, i: (0, b, i, 0)),
                  pl.BlockSpec(overlap_t.shape, lambda b, i: (0, 0))],
        out_specs=[pl.BlockSpec((None, Q_TILE, nh * hd), lambda b, i: (b, i, 0)),
                   pl.BlockSpec((ng, None, Q_TILE, LANES), lambda b, i: (0, b, i, 0))],
        out_shape=[jax.ShapeDtypeStruct((bsz, seq, nh * hd), F32),
                   jax.ShapeDtypeStruct((ng, bsz, seq, LANES), BF16)],
        compiler_params=_compiler_params(("parallel", "parallel")),
        name="nsa_cmp",
    )(q, k_cmp, v_cmp, gates, overlap_t)


def _nsa_sel_kernel(q_ref, k_ref, v_ref, sel_ref, gate_ref, o_ref, s_ref):
    i = pl.program_id(1)
    rows = GROUP_HEADS * Q_TILE
    width = GROUP_HEADS * HEAD_DIM
    n_blk = LANES - HEAD_DIM
    kc = SEL_CHUNK
    seq = k_ref.shape[1]
    groups = range(NSA_KV_HEADS)

    def scores(g, start, lhs):
        return lax.dot_general(k_ref[g, pl.ds(start, kc), :], lhs, _NT, preferred_element_type=F32)

    start_d = pl.multiple_of(jnp.maximum(i * Q_TILE + Q_TILE - kc, 0), Q_TILE)
    n_past = (start_d + kc - 1) // kc
    kpos = start_d + lax.broadcasted_iota(jnp.int32, (kc, rows), 0)
    qpos = i * Q_TILE + (lax.broadcasted_iota(jnp.int32, (kc, rows), 1) & (Q_TILE - 1))
    blk = lax.broadcasted_iota(jnp.int32, (Q_TILE, n_blk), 1)
    lhs_past, lhs_none = [], []
    for g in groups:
        q = q_ref[g * GROUP_HEADS:(g + 1) * GROUP_HEADS].reshape(rows, HEAD_DIM)
        sel = sel_ref[g][:, :n_blk].astype(F32)
        neg_diag = (sel - 1.0) * (-MASKED)
        neg_past = jnp.where(blk < start_d // NSA_SEL_BLOCK, neg_diag, MASKED)
        lhs_diag = jnp.concatenate([q, jnp.concatenate([neg_diag.astype(BF16)] * GROUP_HEADS, axis=0)], axis=-1)
        lhs_past.append(jnp.concatenate([q, jnp.concatenate([neg_past.astype(BF16)] * GROUP_HEADS, axis=0)], axis=-1))
        lhs_none.append(jnp.concatenate([q, jnp.full((rows, n_blk), MASKED, BF16)], axis=-1))
        s_ref[g, 0] = jnp.where(kpos <= qpos, scores(g, start_d, lhs_diag), MASKED)

    def step(c, carry, cur, nxt):
        next_start = pl.multiple_of(jnp.minimum(c * kc, seq - kc), kc)
        v_start = pl.multiple_of(jnp.where(c == 0, start_d, jnp.minimum((c - 1) * kc, seq - kc)), Q_TILE)
        out = []
        for g in groups:
            m, l, acc = carry[g]
            s_ref[g, nxt] = scores(g, next_start, jnp.where(c < n_past, lhs_past[g], lhs_none[g]))
            s = s_ref[g, cur]
            m_new = jnp.maximum(m, jnp.max(s, axis=0, keepdims=True))
            alpha = jnp.exp(m - m_new)
            p = jnp.exp(s - m_new)
            l = alpha * l + jnp.sum(p, axis=0, keepdims=True)
            out.append((m_new, l, alpha * acc + _pv_t(v_ref[g, pl.ds(v_start, kc), :], p)))
        return tuple(out)

    def body(j, carry):
        return step(2 * j + 1, step(2 * j, carry, 0, 1), 1, 0)

    init = (jnp.full((1, rows), MASKED, F32), jnp.zeros((1, rows), F32), jnp.zeros((HEAD_DIM, rows), F32))
    final = lax.fori_loop(0, (n_past + 2) // 2, body, tuple(init for _ in groups))
    for g in groups:
        _, l, acc = final[g]
        o_ref[:, g * width:(g + 1) * width] = _heads_from_t(acc * (1.0 / l), jax.nn.sigmoid(gate_ref[g]), 1)


def _nsa_sel(q, k, v, sel, gates):
    nh, bsz, seq, hd = q.shape
    ng = k.shape[0]
    onehot = (np.arange(seq)[:, None] // NSA_SEL_BLOCK == np.arange(LANES - hd)[None, :]).astype(np.float32)
    k_ext = jnp.concatenate([k, jnp.broadcast_to(jnp.asarray(onehot, BF16), (ng, bsz, seq, LANES - hd))], axis=-1)
    return pl.pallas_call(
        _nsa_sel_kernel,
        grid=(bsz, seq // Q_TILE),
        in_specs=[pl.BlockSpec((nh, None, Q_TILE, hd), lambda b, i: (0, b, i, 0)),
                  pl.BlockSpec((ng, None, seq, LANES), lambda b, i: (0, b, 0, 0)),
                  pl.BlockSpec((ng, None, seq, hd), lambda b, i: (0, b, 0, 0)),
                  pl.BlockSpec((ng, None, Q_TILE, LANES), lambda b, i: (0, b, i, 0)),
                  pl.BlockSpec((ng, None, Q_TILE, gates.shape[-1]), lambda b, i: (0, b, i, 0))],
        out_specs=pl.BlockSpec((None, Q_TILE, nh * hd), lambda b, i: (b, i, 0)),
        out_shape=jax.ShapeDtypeStruct((bsz, seq, nh * hd), F32),
        scratch_shapes=[pltpu.VMEM((ng, 2, SEL_CHUNK, GROUP_HEADS * Q_TILE), F32)],
        compiler_params=_compiler_params(("parallel", "parallel")),
        name="nsa_sel",
    )(q, k_ext, v, sel, gates)


def _band_kernel(window, n_keys, gate_branch, has_sinks, q_ref, k_ref, v_ref, *refs):
    i = pl.program_id(1)
    rows = GROUP_HEADS * Q_TILE
    width = GROUP_HEADS * HEAD_DIM
    refs = list(refs)
    gate_ref = refs.pop(0) if gate_branch is not None else None
    sink_ref = refs.pop(0) if has_sinks else None
    (o_ref,) = refs
    start = pl.multiple_of(jnp.maximum(i * Q_TILE + Q_TILE - n_keys, 0), Q_TILE)
    rel = (i * Q_TILE + lax.broadcasted_iota(jnp.int32, (n_keys, Q_TILE), 1)) - (start + lax.broadcasted_iota(jnp.int32, (n_keys, Q_TILE), 0))
    bias = _tile_heads(jnp.where(rel >= 0, jnp.where(rel < window, 0.0, MASKED), MASKED))
    for g in range(k_ref.shape[0]):
        q = q_ref[g * GROUP_HEADS:(g + 1) * GROUP_HEADS].reshape(rows, HEAD_DIM)
        s = lax.dot_general(k_ref[g, pl.ds(start, n_keys), :], q, _NT, preferred_element_type=F32) + bias
        m = jnp.max(s, axis=0, keepdims=True)
        if has_sinks:
            sink = jnp.concatenate([jnp.broadcast_to(sink_ref[g][:, e:e + 1], (1, Q_TILE)) for e in range(GROUP_HEADS)], axis=1)
            m = jnp.maximum(m, sink)
        p = jnp.exp(s - m)
        denom = jnp.sum(p, axis=0, keepdims=True)
        if has_sinks:
            denom = denom + jnp.exp(sink - m)
        else:
            denom = jnp.maximum(denom, 1e-30)
        o_t = _pv_t(v_ref[g, pl.ds(start, n_keys), :], p) * (1.0 / denom)
        gate = jax.nn.sigmoid(gate_ref[g]) if gate_ref is not None else None
        o_ref[:, g * width:(g + 1) * width] = _heads_from_t(o_t, gate, gate_branch)


def _band_attention(q, k, v, window, gates=None, gate_branch=None, sinks=None):
    nh, bsz, seq, hd = q.shape
    ng = k.shape[0]
    nprev = -(-(window - 1) // Q_TILE)
    n_keys = (nprev + 1) * Q_TILE
    in_specs = [pl.BlockSpec((nh, None, Q_TILE, hd), lambda b, i: (0, b, i, 0)),
                pl.BlockSpec((ng, None, seq, hd), lambda b, i: (0, b, 0, 0)),
                pl.BlockSpec((ng, None, seq, hd), lambda b, i: (0, b, 0, 0))]
    args = [q, k, v]
    if gates is not None:
        in_specs.append(pl.BlockSpec((ng, None, Q_TILE, gates.shape[-1]), lambda b, i: (0, b, i, 0)))
        args.append(gates)
    if sinks is not None:
        in_specs.append(pl.BlockSpec(sinks.shape, lambda b, i: (0, 0, 0)))
        args.append(sinks)
    return pl.pallas_call(
        functools.partial(_band_kernel, window, n_keys, gate_branch if gates is not None else None, sinks is not None),
        grid=(bsz, seq // Q_TILE),
        in_specs=in_specs,
        out_specs=pl.BlockSpec((None, Q_TILE, nh * hd), lambda b, i: (b, i, 0)),
        out_shape=jax.ShapeDtypeStruct((bsz, seq, nh * hd), F32),
        compiler_params=_compiler_params(("parallel", "parallel")),
        name="band_attention",
    )(*args)


def _overlap_t(seq):
    n_cmp = (seq - NSA_CMP_BLOCK) // NSA_CMP_STRIDE + 1
    n_pad = seq // NSA_CMP_STRIDE
    starts = np.arange(n_pad) * NSA_CMP_STRIDE
    sel_start = np.arange(seq // NSA_SEL_BLOCK) * NSA_SEL_BLOCK
    ov = (starts[None, :] < sel_start[:, None] + NSA_SEL_BLOCK) & (starts[None, :] + NSA_CMP_BLOCK > sel_start[:, None])
    ov = ov & (np.arange(n_pad)[None, :] < n_cmp)
    return jnp.asarray(ov.astype(np.float32))


def _odd_mixer(xt, nw, w_in, w_out, cmp_pe, cmp_w1, cmp_w2, sinks, bsz):
    t, d = xt.shape
    seq = t // bsz
    assert seq // NSA_SEL_BLOCK <= LANES and seq % SEL_CHUNK == 0
    cuts = [int(c) for c in np.cumsum((0,) + ODD_SPLITS)]
    ws = [w_in[:, a:b] for a, b in zip(cuts[:-1], cuts[1:])]
    scale = HEAD_DIM ** -0.5
    ws[0] = ws[0] * scale
    ws[8] = ws[8] * scale
    hd = HEAD_DIM
    head_dims = (hd, hd, hd, hd, hd, hd, hd, 3 * GROUP_HEADS, hd, hd, hd)
    dtypes = (BF16, F32, F32, BF16, BF16, BF16, BF16, F32, BF16, BF16, BF16)
    q_n, kc, vc, ks, vs, kw, vw, gates, q_s, k_s, v_s = _norm_proj(xt, nw, ws, head_dims, dtypes)

    def heads(a):
        return a.reshape(a.shape[0], bsz, seq, a.shape[-1])

    q_n, ks, vs, kw, vw, gates, q_s, k_s, v_s = map(heads, (q_n, ks, vs, kw, vw, gates, q_s, k_s, v_s))
    ng = NSA_KV_HEADS
    rows16 = NSA_CMP_STRIDE * hd
    pe = cmp_pe.reshape(2, 2, rows16)
    w1 = cmp_w1.astype(BF16)
    w2 = cmp_w2.astype(BF16)
    k_cmp = _cmp_kv(kc.reshape(ng, bsz, seq // NSA_CMP_STRIDE, rows16), pe[0], w1[0], w2[0])
    v_cmp = _cmp_kv(vc.reshape(ng, bsz, seq // NSA_CMP_STRIDE, rows16), pe[1], w1[1], w2[1])
    o_cmp, sel = _nsa_cmp(q_n, k_cmp, v_cmp, gates, _overlap_t(seq))
    o_sel = _nsa_sel(q_n, ks, vs, sel, gates)
    o_win = _band_attention(q_n, kw, vw, NSA_WINDOW, gates=gates, gate_branch=2)
    o_swa = _band_attention(q_s, k_s, v_s, SWA_WINDOW, sinks=sinks.reshape(SWA_KV_HEADS, 1, GROUP_HEADS))
    n_nsa = NSA_HEADS * hd
    return _proj_residual(xt, [[o_cmp.reshape(t, n_nsa), o_sel.reshape(t, n_nsa), o_win.reshape(t, n_nsa)],
                               [o_swa.reshape(t, SWA_HEADS * hd)]], [w_out[:n_nsa], w_out[n_nsa:]])


def kernel(x, mem, norm_w, final_norm_w, mem_norm_w, ffn_w_gate, ffn_w_up, ffn_w_down, xattn_wq, xattn_wkv, xattn_wo, even_w_in, even_w_out, ssm_conv_w, ssm_conv_b, ssm_dt_bias, ssm_a_log, ssm_d, ssm_norm_w, odd_w_in, odd_w_out, nsa_cmp_pe, nsa_cmp_w1, nsa_cmp_w2, swa_sinks):
    bsz, seq, d = x.shape
    t = bsz * seq
    m = mem.shape[1]
    wg = ffn_w_gate.astype(BF16)
    wu = ffn_w_up.astype(BF16)
    wd = ffn_w_down.astype(BF16)
    wq = xattn_wq.astype(BF16)
    wkv = xattn_wkv.astype(BF16)
    wo = xattn_wo.astype(BF16)
    even_in = even_w_in.astype(BF16)
    even_out = even_w_out.astype(BF16)
    odd_in = odd_w_in.astype(BF16)
    odd_out = odd_w_out.astype(BF16)

    xt = x.reshape(t, d)
    mem2 = mem.reshape(bsz * m, d)
    mem_nw = mem_norm_w.reshape(1, d)
    for layer in range(DEPTH):
        i = layer // 2
        xt = _ffn(xt, norm_w[layer, 0].reshape(1, d), wg, wu, wd, layer, 0)
        nw1 = norm_w[layer, 1].reshape(1, d)
        if layer % 2 == 0:
            xt = _even_mixer(xt, nw1, even_in[i], even_out[i], ssm_conv_w[i], ssm_conv_b[i], ssm_dt_bias[i], ssm_a_log[i], ssm_d[i], ssm_norm_w[i], bsz)
        else:
            xt = _odd_mixer(xt, nw1, odd_in[i], odd_out[i], nsa_cmp_pe[i], nsa_cmp_w1[i], nsa_cmp_w2[i], swa_sinks[i], bsz)
        (kv,) = _norm_proj(mem2, mem_nw, [wkv[layer]])
        xt = _xattn(xt, norm_w[layer, 2].reshape(1, d), wq[layer], kv.reshape(bsz, m, 2 * d), wo[layer], bsz)
        xt = _ffn(xt, norm_w[layer, 3].reshape(1, d), wg, wu, wd, layer, 1)
    out = _final_norm(xt, final_norm_w.reshape(1, d))
    return out.reshape(bsz, seq, d)
```

```python
import functools

import numpy as np
import jax
import jax.numpy as jnp
from jax import lax
from jax.experimental import pallas as pl
from jax.experimental.pallas import tpu as pltpu

F32 = jnp.float32
BF16 = jnp.bfloat16

D_MODEL = 1024
DEPTH = 4
D_FF = 2816
NORM_EPS = 1e-6
ATTN_BLOCK = 128

RET_HEADS = 4
RET_DK = 128
RET_DV = 128
RET_CHUNK = 128
RET_ROPE_BASE = 10000.0

SSM_HEADS = 8
SSM_HEAD_DIM = 64
SSM_D_INNER = SSM_HEADS * SSM_HEAD_DIM
SSM_D_STATE = 64
SSM_GROUPS = 2
SSM_CONV = 4
SSM_CHUNK = 128
SSM_CONV_DIM = SSM_D_INNER + 2 * SSM_GROUPS * SSM_D_STATE

EVEN_SPLITS = (RET_HEADS * RET_DK, RET_HEADS * RET_DK, RET_HEADS * RET_DV, RET_HEADS * RET_DV, SSM_D_INNER, SSM_CONV_DIM, SSM_HEADS)

NSA_HEADS = 8
NSA_KV_HEADS = 2
NSA_HEAD_DIM = 64
NSA_CMP_BLOCK = 32
NSA_CMP_STRIDE = 16
NSA_CMP_HIDDEN = 256
NSA_SEL_BLOCK = 64
NSA_TOPN = 16
NSA_WINDOW = 512
NSA_FORCE_SCORE = 1e6

SWA_HEADS = 8
SWA_KV_HEADS = 2
SWA_HEAD_DIM = 64
SWA_WINDOW = 128

ODD_SPLITS = (NSA_HEADS * NSA_HEAD_DIM,) + (NSA_KV_HEADS * NSA_HEAD_DIM,) * 6 + (3 * NSA_HEADS, SWA_HEADS * SWA_HEAD_DIM, SWA_KV_HEADS * SWA_HEAD_DIM, SWA_KV_HEADS * SWA_HEAD_DIM)

XATTN_HEADS = 4
XATTN_HEAD_DIM = D_MODEL // XATTN_HEADS

VMEM_LIMIT_BYTES = 52 * 1024 * 1024
TOKEN_TILE = 512
FFN_TILE = 256
LANES = 128
MASKED = -1e30

_NT = (((1,), (1,)), ((), ()))


def _compiler_params(semantics):
    return pltpu.CompilerParams(dimension_semantics=semantics, vmem_limit_bytes=VMEM_LIMIT_BYTES)


def _rms(x, w):
    return x * lax.rsqrt(jnp.mean(x * x, axis=-1, keepdims=True) + NORM_EPS) * w


def _ffn_kernel(has_out_norm, x_ref, nw_ref, wg_ref, wu_ref, wd_ref, *refs):
    out_nw_ref = refs[0] if has_out_norm else None
    o_ref, a_ref = refs[-2:]
    x = x_ref[...]
    h = _rms(x, nw_ref[...]).astype(BF16)
    for c in range(D_FF // FFN_TILE):
        cols = slice(c * FFN_TILE, (c + 1) * FFN_TILE)
        g = jnp.dot(h, wg_ref[:, cols], preferred_element_type=F32)
        u = jnp.dot(h, wu_ref[:, cols], preferred_element_type=F32)
        a_ref[:, cols] = (g * jax.nn.sigmoid(g) * u).astype(BF16)
    y = x + 0.5 * jnp.dot(a_ref[...], wd_ref[...], preferred_element_type=F32)
    o_ref[...] = _rms(y, out_nw_ref[...]) if has_out_norm else y


def _ffn(x, nw, wg, wu, wd, layer, k, out_nw=None):
    t, d = x.shape
    tm = TOKEN_TILE
    resident = pl.Buffered(1)
    vec = pl.BlockSpec((1, d), lambda i: (0, 0))
    return pl.pallas_call(
        functools.partial(_ffn_kernel, out_nw is not None),
        grid=(t // tm,),
        in_specs=[
            pl.BlockSpec((tm, d), lambda i: (i, 0)),
            vec,
            pl.BlockSpec((None, None, d, D_FF), lambda i: (layer, k, 0, 0), pipeline_mode=resident),
            pl.BlockSpec((None, None, d, D_FF), lambda i: (layer, k, 0, 0), pipeline_mode=resident),
            pl.BlockSpec((None, None, D_FF, d), lambda i: (layer, k, 0, 0), pipeline_mode=resident),
        ] + ([vec] if out_nw is not None else []),
        out_specs=pl.BlockSpec((tm, d), lambda i: (i, 0)),
        out_shape=jax.ShapeDtypeStruct((t, d), F32),
        scratch_shapes=[pltpu.VMEM((tm, D_FF), BF16)],
        compiler_params=_compiler_params(("parallel",)),
        name="ffn",
    )(x, nw, wg, wu, wd, *([out_nw] if out_nw is not None else []))


def _norm_proj_kernel(head_dims, x_ref, nw_ref, *refs):
    n_out = len(head_dims)
    w_refs, o_refs = refs[:n_out], refs[n_out:]
    h = _rms(x_ref[...], nw_ref[...]).astype(BF16)
    for hd, w_ref, o_ref in zip(head_dims, w_refs, o_refs):
        y = jnp.dot(h, w_ref[...], preferred_element_type=F32)
        if hd is None:
            o_ref[...] = y.astype(o_ref.dtype)
        else:
            for j in range(o_ref.shape[0]):
                o_ref[j] = y[:, j * hd:(j + 1) * hd].astype(o_ref.dtype)


def _norm_proj(x, nw, weights, head_dims=None, dtypes=None, tm=TOKEN_TILE):
    t, d = x.shape
    n_out = len(weights)
    head_dims = tuple(head_dims) if head_dims is not None else (None,) * n_out
    dtypes = tuple(dtypes) if dtypes is not None else (F32,) * n_out
    out_specs, out_shape = [], []
    for w, hd, dt in zip(weights, head_dims, dtypes):
        n = w.shape[1]
        if hd is None:
            out_specs.append(pl.BlockSpec((tm, n), lambda i: (i, 0)))
            out_shape.append(jax.ShapeDtypeStruct((t, n), dt))
        else:
            out_specs.append(pl.BlockSpec((n // hd, tm, hd), lambda i: (0, i, 0)))
            out_shape.append(jax.ShapeDtypeStruct((n // hd, t, hd), dt))
    return pl.pallas_call(
        functools.partial(_norm_proj_kernel, head_dims),
        grid=(t // tm,),
        in_specs=[pl.BlockSpec((tm, d), lambda i: (i, 0)), pl.BlockSpec((1, d), lambda i: (0, 0))]
        + [pl.BlockSpec(w.shape, lambda i: (0, 0)) for w in weights],
        out_specs=out_specs,
        out_shape=out_shape,
        compiler_params=_compiler_params(("parallel",)),
        name="norm_proj",
    )(x, nw, *weights)


def _proj_residual_kernel(group_sizes, x_ref, *refs):
    n_a = sum(group_sizes)
    a_refs, w_refs, o_ref = refs[:n_a], refs[n_a:-1], refs[-1]
    acc = x_ref[...]
    pos = 0
    for size, w_ref in zip(group_sizes, w_refs):
        a = a_refs[pos][...]
        for r in a_refs[pos + 1:pos + size]:
            a = a.astype(F32) + r[...].astype(F32)
        pos += size
        acc = acc + jnp.dot(a.astype(BF16), w_ref[...], preferred_element_type=F32)
    o_ref[...] = acc


def _proj_residual(x, groups, weights, tm=TOKEN_TILE):
    t, d = x.shape
    arrays = [a for grp in groups for a in grp]
    return pl.pallas_call(
        functools.partial(_proj_residual_kernel, tuple(len(grp) for grp in groups)),
        grid=(t // tm,),
        in_specs=[pl.BlockSpec((tm, d), lambda i: (i, 0))]
        + [pl.BlockSpec((tm, a.shape[1]), lambda i: (i, 0)) for a in arrays]
        + [pl.BlockSpec(w.shape, lambda i: (0, 0)) for w in weights],
        out_specs=pl.BlockSpec((tm, d), lambda i: (i, 0)),
        out_shape=jax.ShapeDtypeStruct((t, d), F32),
        compiler_params=_compiler_params(("parallel",)),
        name="proj_residual",
    )(x, *arrays, *weights)


def _xattn_kernel(x_ref, nw_ref, wq_ref, kv_ref, wo_ref, o_ref):
    x = x_ref[...]
    h = _rms(x, nw_ref[...]).astype(BF16)
    q = jnp.dot(h, wq_ref[...], preferred_element_type=F32).astype(BF16)
    hd = XATTN_HEAD_DIM
    outs = []
    for hh in range(XATTN_HEADS):
        k = kv_ref[:, hh * hd:(hh + 1) * hd].astype(BF16)
        v = kv_ref[:, D_MODEL + hh * hd:D_MODEL + (hh + 1) * hd].astype(BF16)
        s = lax.dot_general(q[:, hh * hd:(hh + 1) * hd], k, _NT, preferred_element_type=F32)
        s = s * (hd ** -0.5)
        p = jnp.exp(s - jnp.max(s, axis=-1, keepdims=True))
        p = p / jnp.sum(p, axis=-1, keepdims=True)
        outs.append(jnp.dot(p.astype(BF16), v, preferred_element_type=F32).astype(BF16))
    o = jnp.concatenate(outs, axis=-1)
    o_ref[...] = x + jnp.dot(o, wo_ref[...], preferred_element_type=F32)


def _xattn(x, nw, wq, kv, wo, bsz, tm=TOKEN_TILE):
    t, d = x.shape
    seq = t // bsz
    nt = seq // tm
    m = kv.shape[1]
    return pl.pallas_call(
        _xattn_kernel,
        grid=(bsz, nt),
        in_specs=[pl.BlockSpec((tm, d), lambda b, i: (b * nt + i, 0)), pl.BlockSpec((1, d), lambda b, i: (0, 0)),
                  pl.BlockSpec((d, d), lambda b, i: (0, 0)), pl.BlockSpec((None, m, 2 * d), lambda b, i: (b, 0, 0)),
                  pl.BlockSpec((d, d), lambda b, i: (0, 0))],
        out_specs=pl.BlockSpec((tm, d), lambda b, i: (b * nt + i, 0)),
        out_shape=jax.ShapeDtypeStruct((t, d), F32),
        compiler_params=_compiler_params(("parallel", "parallel")),
        name="xattn",
    )(x, nw, wq, kv, wo)


def _retention_kernel(q_ref, k_ref, v_ref, g_ref, cos_ref, sin_ref, dmat_ref, zeta_ref, xi_ref, decay_ref, o_ref, state_ref):
    @pl.when(pl.program_id(1) == 0)
    def _():
        state_ref[...] = jnp.zeros_like(state_ref)

    cos = cos_ref[...]
    sin = sin_ref[...]
    half = RET_DK // 2
    outs = []
    for h in range(RET_HEADS):
        qh = q_ref[:, h * RET_DK:(h + 1) * RET_DK]
        kh = k_ref[:, h * RET_DK:(h + 1) * RET_DK]
        qr = qh * cos + pltpu.roll(qh, half, 1) * sin
        kr = (kh * cos + pltpu.roll(kh, half, 1) * sin) * (RET_DK ** -0.5)
        vh = v_ref[:, h * RET_DV:(h + 1) * RET_DV].astype(BF16)
        scores = lax.dot_general(qr.astype(BF16), kr.astype(BF16), _NT, preferred_element_type=F32) * dmat_ref[h]
        inner = jnp.dot(scores.astype(BF16), vh, preferred_element_type=F32)
        state = state_ref[h]
        cross = jnp.dot((qr * xi_ref[h]).astype(BF16), state.astype(BF16), preferred_element_type=F32)
        kz = (kr * zeta_ref[h]).T.astype(BF16)
        state_ref[h] = decay_ref[h] * state + jnp.dot(kz, vh, preferred_element_type=F32)
        o = inner + cross
        o = o * lax.rsqrt(jnp.mean(o * o, axis=-1, keepdims=True) + NORM_EPS)
        gh = g_ref[:, h * RET_DV:(h + 1) * RET_DV]
        outs.append(gh * jax.nn.sigmoid(gh) * o)
    o_ref[...] = jnp.concatenate(outs, axis=-1).astype(o_ref.dtype)


def _retention_tables(seq):
    L = RET_CHUNK
    pos = jnp.arange(seq, dtype=F32)
    inv_freq = 1.0 / (RET_ROPE_BASE ** jnp.linspace(0.0, 1.0, RET_DK // 2, dtype=F32))
    ang = pos[:, None] * inv_freq[None, :]
    cos, sin = jnp.cos(ang), jnp.sin(ang)
    cos2 = jnp.concatenate([cos, cos], axis=-1)
    sin2 = jnp.concatenate([-sin, sin], axis=-1)
    log_g = jnp.log1p(-jnp.exp2(-5.0 - jnp.arange(RET_HEADS, dtype=F32)))
    idx = jnp.arange(L, dtype=F32)
    diff = idx[:, None] - idx[None, :]
    dmat = jnp.where(diff >= 0, jnp.exp(jnp.maximum(diff, 0.0)[None] * log_g[:, None, None]), 0.0)
    ones = jnp.ones((RET_HEADS, L, RET_DK), F32)
    zeta = jnp.exp((L - 1 - idx)[None, :] * log_g[:, None])[:, :, None] * ones
    xi = jnp.exp((idx + 1)[None, :] * log_g[:, None])[:, :, None] * ones
    decay = jnp.exp(L * log_g)[:, None, None] * ones
    return cos2, sin2, dmat, zeta, xi, decay


def _retention(q, k, v, g):
    bsz, seq, width = q.shape
    L = RET_CHUNK
    tables = _retention_tables(seq)
    tok = pl.BlockSpec((None, L, width), lambda b, c: (b, c, 0))
    rope = pl.BlockSpec((L, RET_DK), lambda b, c: (c, 0))
    const = pl.BlockSpec((RET_HEADS, L, RET_DK), lambda b, c: (0, 0, 0))
    return pl.pallas_call(
        _retention_kernel,
        grid=(bsz, seq // L),
        in_specs=[tok, tok, tok, tok, rope, rope, const, const, const, const],
        out_specs=tok,
        out_shape=jax.ShapeDtypeStruct((bsz, seq, width), BF16),
        scratch_shapes=[pltpu.VMEM((RET_HEADS, RET_DK, RET_DV), F32)],
        compiler_params=_compiler_params(("parallel", "arbitrary")),
        name="retention",
    )(q, k, v, g, *tables)


CONV_PAD = 8


def _ssd_kernel(xbc_ref, dt_ref, z_ref, convw_ref, convb_ref, dtb_ref, alog_ref, dskip_ref, nw_ref, tri_ref, o_ref, ext_ref, state_ref):
    L = SSM_CHUNK
    hp = SSM_HEAD_DIM
    ns = SSM_D_STATE

    @pl.when(pl.program_id(1) == 0)
    def _():
        ext_ref[0:CONV_PAD, :] = jnp.zeros((CONV_PAD, SSM_CONV_DIM), F32)
        state_ref[...] = jnp.zeros_like(state_ref)

    ext_ref[CONV_PAD:CONV_PAD + L, :] = xbc_ref[...]
    conv = convb_ref[...]
    for w in range(SSM_CONV):
        off = CONV_PAD - (SSM_CONV - 1) + w
        conv = conv + ext_ref[off:off + L, :] * convw_ref[w:w + 1, :]
    ext_ref[0:CONV_PAD, :] = ext_ref[L:L + CONV_PAD, :]
    xa = conv * jax.nn.sigmoid(conv)
    xs = xa[:, :SSM_D_INNER]
    bmat = xa[:, SSM_D_INNER:SSM_D_INNER + SSM_GROUPS * ns]
    cmat = xa[:, SSM_D_INNER + SSM_GROUPS * ns:]

    x = dt_ref[...] + dtb_ref[...]
    dt = jnp.maximum(x, 0.0) + jnp.log1p(jnp.exp(-jnp.abs(x)))
    lane = lax.broadcasted_iota(jnp.int32, (L, LANES), 1)
    dta = jnp.where(lane < SSM_HEADS, dt * -jnp.exp(alog_ref[...]), 0.0)
    cums = jnp.dot(tri_ref[...], dta, preferred_element_type=F32, precision=lax.Precision.HIGHEST)
    cums_t = cums.T
    last = cums[L - 1:L, :]
    decay_to_end = jnp.exp(last - cums)
    decay_from_start = jnp.exp(cums)
    chunk_decay = jnp.exp(last)
    causal = lax.broadcasted_iota(jnp.int32, (L, L), 0) >= lax.broadcasted_iota(jnp.int32, (L, L), 1)
    heads_per_group = SSM_HEADS // SSM_GROUPS
    ys = []
    for g in range(SSM_GROUPS):
        bg = bmat[:, g * ns:(g + 1) * ns].astype(BF16)
        cg = cmat[:, g * ns:(g + 1) * ns].astype(BF16)
        cb = lax.dot_general(cg, bg, _NT, preferred_element_type=F32)
        for e in range(heads_per_group):
            h = g * heads_per_group + e
            xh = xs[:, h * hp:(h + 1) * hp]
            xdt = xh * dt[:, h:h + 1]
            seg = cums[:, h:h + 1] - cums_t[h:h + 1, :]
            lmat = jnp.where(causal, jnp.exp(seg), 0.0)
            y_diag = jnp.dot((cb * lmat).astype(BF16), xdt.astype(BF16), preferred_element_type=F32)
            state = state_ref[h]
            y_off = lax.dot_general(cg, state.astype(BF16), _NT, preferred_element_type=F32) * decay_from_start[:, h:h + 1]
            xw = (xdt * decay_to_end[:, h:h + 1]).astype(BF16)
            new = lax.dot_general(xw, bg, (((0,), (0,)), ((), ())), preferred_element_type=F32)
            state_ref[h] = chunk_decay[:, h:h + 1] * state + new
            ys.append(y_diag + y_off + dskip_ref[:, h * hp:(h + 1) * hp] * xh)
    y = jnp.concatenate(ys, axis=-1)
    z = z_ref[...]
    o_ref[...] = _rms(y * (z * jax.nn.sigmoid(z)), nw_ref[...]).astype(o_ref.dtype)


def _ssd(xbc, dt_raw, z, conv_w, conv_b, dt_bias, a_log, d_skip, norm_w):
    bsz, seq, _ = xbc.shape
    L = SSM_CHUNK
    pad = LANES - SSM_HEADS
    params = [conv_w, conv_b.reshape(1, -1), jnp.pad(dt_bias, (0, pad)).reshape(1, LANES),
              jnp.pad(a_log, (0, pad)).reshape(1, LANES), jnp.repeat(d_skip, SSM_HEAD_DIM).reshape(1, -1),
              norm_w.reshape(1, -1), jnp.asarray(np.tril(np.ones((L, L), np.float32)))]

    def tok(width):
        return pl.BlockSpec((None, L, width), lambda b, c: (b, c, 0))

    return pl.pallas_call(
        _ssd_kernel,
        grid=(bsz, seq // L),
        in_specs=[tok(SSM_CONV_DIM), tok(LANES), tok(SSM_D_INNER)] + [pl.BlockSpec(p.shape, lambda b, c: (0, 0)) for p in params],
        out_specs=tok(SSM_D_INNER),
        out_shape=jax.ShapeDtypeStruct((bsz, seq, SSM_D_INNER), BF16),
        scratch_shapes=[pltpu.VMEM((L + CONV_PAD, SSM_CONV_DIM), F32), pltpu.VMEM((SSM_HEADS, SSM_HEAD_DIM, SSM_D_STATE), F32)],
        compiler_params=_compiler_params(("parallel", "arbitrary")),
        name="ssd",
    )(xbc, dt_raw, z, *params)


def _even_mixer(xt, nw, w_in, w_out, conv_w, conv_b, dt_bias, a_log, d_skip, ssm_norm_w, bsz):
    t, d = xt.shape
    seq = t // bsz
    cuts = [int(c) for c in np.cumsum((0,) + EVEN_SPLITS)]
    ws = [w_in[:, a:b] for a, b in zip(cuts[:-1], cuts[1:])]
    perm = np.concatenate([h * RET_DK + np.concatenate([np.arange(0, RET_DK, 2), np.arange(1, RET_DK, 2)]) for h in range(RET_HEADS)])
    ws[0] = ws[0][:, perm]
    ws[1] = ws[1][:, perm]
    ws[6] = jnp.pad(ws[6], ((0, 0), (0, LANES - SSM_HEADS)))
    dtypes = (F32, F32, BF16, F32, F32, F32, F32)
    q, k, v, g, z, xbc, dt_raw = [a.reshape(bsz, seq, -1) for a in _norm_proj(xt, nw, ws, dtypes=dtypes)]
    o_ret = _retention(q, k, v, g)
    y = _ssd(xbc, dt_raw, z, conv_w, conv_b, dt_bias, a_log, d_skip, ssm_norm_w)
    n_ret = RET_HEADS * RET_DV
    return _proj_residual(xt, [[o_ret.reshape(t, n_ret)], [y.reshape(t, SSM_D_INNER)]], [w_out[:n_ret], w_out[n_ret:]])


Q_TILE = ATTN_BLOCK
GROUP_HEADS = NSA_HEADS // NSA_KV_HEADS
HEAD_DIM = NSA_HEAD_DIM
SEL_CHUNK = 512


def _cmp_kv_kernel(r_ref, pe_ref, w1_ref, w2_ref, o_ref):
    r = r_ref[...]
    half = NSA_CMP_STRIDE * HEAD_DIM
    top = jnp.dot((r + pe_ref[0:1, :]).astype(BF16), w1_ref[0:half, :], preferred_element_type=F32)
    bot = jnp.dot((r + pe_ref[1:2, :]).astype(BF16), w1_ref[half:2 * half, :], preferred_element_type=F32)
    hidden = top + pltpu.roll(bot, r.shape[0] - 1, 0)
    act = hidden * jax.nn.sigmoid(hidden)
    o_ref[...] = jnp.dot(act.astype(BF16), w2_ref[...], preferred_element_type=F32)


def _cmp_kv(r, pe, w1, w2):
    ng, bsz, n, width = r.shape
    return pl.pallas_call(
        _cmp_kv_kernel,
        grid=(ng, bsz),
        in_specs=[pl.BlockSpec((None, None, n, width), lambda g, b: (g, b, 0, 0)),
                  pl.BlockSpec(pe.shape, lambda g, b: (0, 0)),
                  pl.BlockSpec(w1.shape, lambda g, b: (0, 0)),
                  pl.BlockSpec(w2.shape, lambda g, b: (0, 0))],
        out_specs=pl.BlockSpec((None, None, n, HEAD_DIM), lambda g, b: (g, b, 0, 0)),
        out_shape=jax.ShapeDtypeStruct((ng, bsz, n, HEAD_DIM), F32),
        compiler_params=_compiler_params(("parallel", "parallel")),
        name="nsa_compress",
    )(r, pe, w1, w2)


def _heads_from_t(o_t, gate, branch):
    outs = []
    for e in range(GROUP_HEADS):
        blk = o_t[:, e * Q_TILE:(e + 1) * Q_TILE]
        blk = jnp.concatenate([blk, jnp.zeros((LANES - HEAD_DIM, Q_TILE), F32)], axis=0)
        oe = blk.T[:, :HEAD_DIM]
        if gate is not None:
            oe = oe * gate[:, 3 * e + branch:3 * e + branch + 1]
        outs.append(oe)
    return jnp.concatenate(outs, axis=-1)


def _tile_heads(x):
    return jnp.concatenate([x] * GROUP_HEADS, axis=1)


def _pv_t(v, p):
    return lax.dot_general(v, p.astype(BF16), (((0,), (0,)), ((), ())), preferred_element_type=F32)


def _nsa_cmp_kernel(q_ref, kc_ref, vc_ref, gate_ref, ov_ref, o_ref, sel_ref):
    i = pl.program_id(1)
    rows = GROUP_HEADS * Q_TILE
    width = GROUP_HEADS * HEAD_DIM
    n_cmp = kc_ref.shape[1]
    n = lax.broadcasted_iota(jnp.int32, (n_cmp, Q_TILE), 0)
    t = i * Q_TILE + lax.broadcasted_iota(jnp.int32, (n_cmp, Q_TILE), 1)
    valid = _tile_heads(jnp.where(n * NSA_CMP_STRIDE + (NSA_CMP_BLOCK - 1) <= t, 1.0, 0.0))
    bias = (valid - 1.0) * (-MASKED)
    n_sel = ov_ref.shape[0]
    jb = lax.broadcasted_iota(jnp.int32, (n_sel, Q_TILE), 0)
    cur = (i * Q_TILE + lax.broadcasted_iota(jnp.int32, (n_sel, Q_TILE), 1)) // NSA_SEL_BLOCK
    valid_b = jb <= cur
    forced = jnp.where(valid_b, jnp.where(jb == 0, 1.0, 0.0) + jnp.where(jb == cur, 1.0, 0.0) + jnp.where(jb == cur - 1, 1.0, 0.0), 0.0)
    for g in range(NSA_KV_HEADS):
        q = q_ref[g * GROUP_HEADS:(g + 1) * GROUP_HEADS].reshape(rows, HEAD_DIM)
        s = lax.dot_general(kc_ref[g].astype(BF16), q, _NT, preferred_element_type=F32) + bias
        m = jnp.max(s, axis=0, keepdims=True)
        p = jnp.exp(s - m) * valid
        p = p * (1.0 / jnp.maximum(jnp.sum(p, axis=0, keepdims=True), 1e-30))
        o_t = _pv_t(vc_ref[g].astype(BF16), p)
        o_ref[:, g * width:(g + 1) * width] = _heads_from_t(o_t, jax.nn.sigmoid(gate_ref[g]), 0).astype(o_ref.dtype)

        p_sum = p[:, 0:Q_TILE]
        for e in range(1, GROUP_HEADS):
            p_sum = p_sum + p[:, e * Q_TILE:(e + 1) * Q_TILE]
        imp = jnp.dot(ov_ref[...], p_sum, preferred_element_type=F32, precision=lax.Precision.HIGHEST)
        score = jnp.where(forced > 0.5, NSA_FORCE_SCORE, jnp.where(valid_b, imp, -jnp.inf))
        rank = jnp.zeros((n_sel, Q_TILE), F32)
        for ii in range(n_sel):
            row = score[ii:ii + 1, :]
            tie = jnp.where(jb > ii, 1.0, 0.0)
            rank = rank + jnp.where(row > score, 1.0, jnp.where(row == score, tie, 0.0))
        sel_t = jnp.where(valid_b, jnp.where(rank < NSA_TOPN, 1.0, 0.0), 0.0)
        sel_t = jnp.concatenate([sel_t, jnp.zeros((LANES - n_sel, Q_TILE), F32)], axis=0)
        sel_ref[g] = sel_t.T.astype(BF16)


def _nsa_cmp(q, k_cmp, v_cmp, gates, overlap_t):
    nh, bsz, seq, hd = q.shape
    ng, _, n_cmp, _ = k_cmp.shape
    return pl.pallas_call(
        _nsa_cmp_kernel,
        grid=(bsz, seq // Q_TILE),
        in_specs=[pl.BlockSpec((nh, None, Q_TILE, hd), lambda b, i: (0, b, i, 0)),
                  pl.BlockSpec((ng, None, n_cmp, hd), lambda b, i: (0, b, 0, 0)),
                  pl.BlockSpec((ng, None, n_cmp, hd), lambda b, i: (0, b, 0, 0)),
                  pl.BlockSpec((ng, None, Q_TILE, gates.shape[-1]), lambda b, i: (0, b, i, 0)),
                  pl.BlockSpec(overlap_t.shape, lambda b, i: (0, 0))],
        out_specs=[pl.BlockSpec((None, Q_TILE, nh * hd), lambda b, i: (b, i, 0)),
                   pl.BlockSpec((ng, None, Q_TILE, LANES), lambda b, i: (0, b, i, 0))],
        out_shape=[jax.ShapeDtypeStruct((bsz, seq, nh * hd), BF16),
                   jax.ShapeDtypeStruct((ng, bsz, seq, LANES), BF16)],
        compiler_params=_compiler_params(("parallel", "parallel")),
        name="nsa_cmp",
    )(q, k_cmp, v_cmp, gates, overlap_t)


def _nsa_sel_kernel(q_ref, k_ref, v_ref, sel_ref, gate_ref, o_ref, s_ref):
    i = pl.program_id(1)
    rows = GROUP_HEADS * Q_TILE
    width = GROUP_HEADS * HEAD_DIM
    n_blk = LANES - HEAD_DIM
    kc = SEL_CHUNK
    seq = k_ref.shape[1]
    groups = range(NSA_KV_HEADS)

    def scores(g, start, lhs):
        return lax.dot_general(k_ref[g, pl.ds(start, kc), :], lhs, _NT, preferred_element_type=F32)

    start_d = pl.multiple_of(jnp.maximum(i * Q_TILE + Q_TILE - kc, 0), Q_TILE)
    n_past = (start_d + kc - 1) // kc
    kpos = start_d + lax.broadcasted_iota(jnp.int32, (kc, rows), 0)
    qpos = i * Q_TILE + (lax.broadcasted_iota(jnp.int32, (kc, rows), 1) & (Q_TILE - 1))
    blk = lax.broadcasted_iota(jnp.int32, (Q_TILE, n_blk), 1)
    lhs_past, lhs_none = [], []
    for g in groups:
        q = q_ref[g * GROUP_HEADS:(g + 1) * GROUP_HEADS].reshape(rows, HEAD_DIM)
        sel = sel_ref[g][:, :n_blk].astype(F32)
        neg_diag = (sel - 1.0) * (-MASKED)
        neg_past = jnp.where(blk < start_d // NSA_SEL_BLOCK, neg_diag, MASKED)
        lhs_diag = jnp.concatenate([q, jnp.concatenate([neg_diag.astype(BF16)] * GROUP_HEADS, axis=0)], axis=-1)
        lhs_past.append(jnp.concatenate([q, jnp.concatenate([neg_past.astype(BF16)] * GROUP_HEADS, axis=0)], axis=-1))
        lhs_none.append(jnp.concatenate([q, jnp.full((rows, n_blk), MASKED, BF16)], axis=-1))
        s_ref[g, 0] = jnp.where(kpos <= qpos, scores(g, start_d, lhs_diag), MASKED)

    def step(c, carry, cur, nxt):
        next_start = pl.multiple_of(jnp.minimum(c * kc, seq - kc), kc)
        v_start = pl.multiple_of(jnp.where(c == 0, start_d, jnp.minimum((c - 1) * kc, seq - kc)), Q_TILE)
        out = []
        for g in groups:
            m, l, acc = carry[g]
            s_ref[g, nxt] = scores(g, next_start, jnp.where(c < n_past, lhs_past[g], lhs_none[g]))
            s = s_ref[g, cur]
            m_new = jnp.maximum(m, jnp.max(s, axis=0, keepdims=True))
            alpha = jnp.exp(m - m_new)
            p = jnp.exp(s - m_new)
            l = alpha * l + jnp.sum(p, axis=0, keepdims=True)
            out.append((m_new, l, alpha * acc + _pv_t(v_ref[g, pl.ds(v_start, kc), :], p)))
        return tuple(out)

    def body(j, carry):
        return step(2 * j + 1, step(2 * j, carry, 0, 1), 1, 0)

    init = (jnp.full((1, rows), MASKED, F32), jnp.zeros((1, rows), F32), jnp.zeros((HEAD_DIM, rows), F32))
    final = lax.fori_loop(0, (n_past + 2) // 2, body, tuple(init for _ in groups))
    for g in groups:
        _, l, acc = final[g]
        o_ref[:, g * width:(g + 1) * width] = _heads_from_t(acc * (1.0 / l), jax.nn.sigmoid(gate_ref[g]), 1).astype(o_ref.dtype)


def _nsa_sel(q, k, v, sel, gates):
    nh, bsz, seq, hd = q.shape
    ng = k.shape[0]
    onehot = (np.arange(seq)[:, None] // NSA_SEL_BLOCK == np.arange(LANES - hd)[None, :]).astype(np.float32)
    k_ext = jnp.concatenate([k, jnp.broadcast_to(jnp.asarray(onehot, BF16), (ng, bsz, seq, LANES - hd))], axis=-1)
    return pl.pallas_call(
        _nsa_sel_kernel,
        grid=(bsz, seq // Q_TILE),
        in_specs=[pl.BlockSpec((nh, None, Q_TILE, hd), lambda b, i: (0, b, i, 0)),
                  pl.BlockSpec((ng, None, seq, LANES), lambda b, i: (0, b, 0, 0)),
                  pl.BlockSpec((ng, None, seq, hd), lambda b, i: (0, b, 0, 0)),
                  pl.BlockSpec((ng, None, Q_TILE, LANES), lambda b, i: (0, b, i, 0)),
                  pl.BlockSpec((ng, None, Q_TILE, gates.shape[-1]), lambda b, i: (0, b, i, 0))],
        out_specs=pl.BlockSpec((None, Q_TILE, nh * hd), lambda b, i: (b, i, 0)),
        out_shape=jax.ShapeDtypeStruct((bsz, seq, nh * hd), BF16),
        scratch_shapes=[pltpu.VMEM((ng, 2, SEL_CHUNK, GROUP_HEADS * Q_TILE), F32)],
        compiler_params=_compiler_params(("parallel", "parallel")),
        name="nsa_sel",
    )(q, k_ext, v, sel, gates)


def _band_kernel(window, n_keys, gate_branch, has_sinks, q_ref, k_ref, v_ref, *refs):
    i = pl.program_id(1)
    rows = GROUP_HEADS * Q_TILE
    width = GROUP_HEADS * HEAD_DIM
    refs = list(refs)
    gate_ref = refs.pop(0) if gate_branch is not None else None
    sink_ref = refs.pop(0) if has_sinks else None
    (o_ref,) = refs
    start = pl.multiple_of(jnp.maximum(i * Q_TILE + Q_TILE - n_keys, 0), Q_TILE)
    rel = (i * Q_TILE + lax.broadcasted_iota(jnp.int32, (n_keys, Q_TILE), 1)) - (start + lax.broadcasted_iota(jnp.int32, (n_keys, Q_TILE), 0))
    bias = _tile_heads(jnp.where(rel >= 0, jnp.where(rel < window, 0.0, MASKED), MASKED))
    for g in range(k_ref.shape[0]):
        q = q_ref[g * GROUP_HEADS:(g + 1) * GROUP_HEADS].reshape(rows, HEAD_DIM)
        s = lax.dot_general(k_ref[g, pl.ds(start, n_keys), :], q, _NT, preferred_element_type=F32) + bias
        m = jnp.max(s, axis=0, keepdims=True)
        if has_sinks:
            sink = jnp.concatenate([jnp.broadcast_to(sink_ref[g][:, e:e + 1], (1, Q_TILE)) for e in range(GROUP_HEADS)], axis=1)
            m = jnp.maximum(m, sink)
        p = jnp.exp(s - m)
        denom = jnp.sum(p, axis=0, keepdims=True)
        if has_sinks:
            denom = denom + jnp.exp(sink - m)
        else:
            denom = jnp.maximum(denom, 1e-30)
        o_t = _pv_t(v_ref[g, pl.ds(start, n_keys), :], p) * (1.0 / denom)
        gate = jax.nn.sigmoid(gate_ref[g]) if gate_ref is not None else None
        o_ref[:, g * width:(g + 1) * width] = _heads_from_t(o_t, gate, gate_branch).astype(o_ref.dtype)


def _band_attention(q, k, v, window, gates=None, gate_branch=None, sinks=None):
    nh, bsz, seq, hd = q.shape
    ng = k.shape[0]
    nprev = -(-(window - 1) // Q_TILE)
    n_keys = (nprev + 1) * Q_TILE
    in_specs = [pl.BlockSpec((nh, None, Q_TILE, hd), lambda b, i: (0, b, i, 0)),
                pl.BlockSpec((ng, None, seq, hd), lambda b, i: (0, b, 0, 0)),
                pl.BlockSpec((ng, None, seq, hd), lambda b, i: (0, b, 0, 0))]
    args = [q, k, v]
    if gates is not None:
        in_specs.append(pl.BlockSpec((ng, None, Q_TILE, gates.shape[-1]), lambda b, i: (0, b, i, 0)))
        args.append(gates)
    if sinks is not None:
        in_specs.append(pl.BlockSpec(sinks.shape, lambda b, i: (0, 0, 0)))
        args.append(sinks)
    return pl.pallas_call(
        functools.partial(_band_kernel, window, n_keys, gate_branch if gates is not None else None, sinks is not None),
        grid=(bsz, seq // Q_TILE),
        in_specs=in_specs,
        out_specs=pl.BlockSpec((None, Q_TILE, nh * hd), lambda b, i: (b, i, 0)),
        out_shape=jax.ShapeDtypeStruct((bsz, seq, nh * hd), BF16),
        compiler_params=_compiler_params(("parallel", "parallel")),
        name="band_attention",
    )(*args)


def _overlap_t(seq):
    n_cmp = (seq - NSA_CMP_BLOCK) // NSA_CMP_STRIDE + 1
    n_pad = seq // NSA_CMP_STRIDE
    starts = np.arange(n_pad) * NSA_CMP_STRIDE
    sel_start = np.arange(seq // NSA_SEL_BLOCK) * NSA_SEL_BLOCK
    ov = (starts[None, :] < sel_start[:, None] + NSA_SEL_BLOCK) & (starts[None, :] + NSA_CMP_BLOCK > sel_start[:, None])
    ov = ov & (np.arange(n_pad)[None, :] < n_cmp)
    return jnp.asarray(ov.astype(np.float32))


def _odd_mixer(xt, nw, w_in, w_out, cmp_pe, cmp_w1, cmp_w2, sinks, bsz):
    t, d = xt.shape
    seq = t // bsz
    assert seq // NSA_SEL_BLOCK <= LANES and seq % SEL_CHUNK == 0
    cuts = [int(c) for c in np.cumsum((0,) + ODD_SPLITS)]
    ws = [w_in[:, a:b] for a, b in zip(cuts[:-1], cuts[1:])]
    scale = HEAD_DIM ** -0.5
    ws[0] = ws[0] * scale
    ws[8] = ws[8] * scale
    hd = HEAD_DIM
    head_dims = (hd, hd, hd, hd, hd, hd, hd, 3 * GROUP_HEADS, hd, hd, hd)
    dtypes = (BF16, F32, F32, BF16, BF16, BF16, BF16, F32, BF16, BF16, BF16)
    q_n, kc, vc, ks, vs, kw, vw, gates, q_s, k_s, v_s = _norm_proj(xt, nw, ws, head_dims, dtypes)

    def heads(a):
        return a.reshape(a.shape[0], bsz, seq, a.shape[-1])

    q_n, ks, vs, kw, vw, gates, q_s, k_s, v_s = map(heads, (q_n, ks, vs, kw, vw, gates, q_s, k_s, v_s))
    ng = NSA_KV_HEADS
    rows16 = NSA_CMP_STRIDE * hd
    pe = cmp_pe.reshape(2, 2, rows16)
    w1 = cmp_w1.astype(BF16)
    w2 = cmp_w2.astype(BF16)
    k_cmp = _cmp_kv(kc.reshape(ng, bsz, seq // NSA_CMP_STRIDE, rows16), pe[0], w1[0], w2[0])
    v_cmp = _cmp_kv(vc.reshape(ng, bsz, seq // NSA_CMP_STRIDE, rows16), pe[1], w1[1], w2[1])
    o_cmp, sel = _nsa_cmp(q_n, k_cmp, v_cmp, gates, _overlap_t(seq))
    o_sel = _nsa_sel(q_n, ks, vs, sel, gates)
    o_win = _band_attention(q_n, kw, vw, NSA_WINDOW, gates=gates, gate_branch=2)
    o_swa = _band_attention(q_s, k_s, v_s, SWA_WINDOW, sinks=sinks.reshape(SWA_KV_HEADS, 1, GROUP_HEADS))
    n_nsa = NSA_HEADS * hd
    return _proj_residual(xt, [[o_cmp.reshape(t, n_nsa), o_sel.reshape(t, n_nsa), o_win.reshape(t, n_nsa)],
                               [o_swa.reshape(t, SWA_HEADS * hd)]], [w_out[:n_nsa], w_out[n_nsa:]])


def kernel(x, mem, norm_w, final_norm_w, mem_norm_w, ffn_w_gate, ffn_w_up, ffn_w_down, xattn_wq, xattn_wkv, xattn_wo, even_w_in, even_w_out, ssm_conv_w, ssm_conv_b, ssm_dt_bias, ssm_a_log, ssm_d, ssm_norm_w, odd_w_in, odd_w_out, nsa_cmp_pe, nsa_cmp_w1, nsa_cmp_w2, swa_sinks):
    bsz, seq, d = x.shape
    t = bsz * seq
    m = mem.shape[1]
    wg = ffn_w_gate.astype(BF16)
    wu = ffn_w_up.astype(BF16)
    wd = ffn_w_down.astype(BF16)
    wq = xattn_wq.astype(BF16)
    wkv = xattn_wkv.astype(BF16)
    wo = xattn_wo.astype(BF16)
    even_in = even_w_in.astype(BF16)
    even_out = even_w_out.astype(BF16)
    odd_in = odd_w_in.astype(BF16)
    odd_out = odd_w_out.astype(BF16)

    xt = x.reshape(t, d)
    mem2 = mem.reshape(bsz * m, d)
    mem_nw = mem_norm_w.reshape(1, d)
    for layer in range(DEPTH):
        i = layer // 2
        xt = _ffn(xt, norm_w[layer, 0].reshape(1, d), wg, wu, wd, layer, 0)
        nw1 = norm_w[layer, 1].reshape(1, d)
        if layer % 2 == 0:
            xt = _even_mixer(xt, nw1, even_in[i], even_out[i], ssm_conv_w[i], ssm_conv_b[i], ssm_dt_bias[i], ssm_a_log[i], ssm_d[i], ssm_norm_w[i], bsz)
        else:
            xt = _odd_mixer(xt, nw1, odd_in[i], odd_out[i], nsa_cmp_pe[i], nsa_cmp_w1[i], nsa_cmp_w2[i], swa_sinks[i], bsz)
        (kv,) = _norm_proj(mem2, mem_nw, [wkv[layer]])
        xt = _xattn(xt, norm_w[layer, 2].reshape(1, d), wq[layer], kv.reshape(bsz, m, 2 * d), wo[layer], bsz)
        out_nw = final_norm_w.reshape(1, d) if layer == DEPTH - 1 else None
        xt = _ffn(xt, norm_w[layer, 3].reshape(1, d), wg, wu, wd, layer, 1, out_nw)
    return xt.reshape(bsz, seq, d)
```

```python
import functools

import numpy as np
import jax
import jax.numpy as jnp
from jax import lax
from jax.experimental import pallas as pl
from jax.experimental.pallas import tpu as pltpu

F32 = jnp.float32
BF16 = jnp.bfloat16

D_MODEL = 1024
DEPTH = 4
D_FF = 2816
NORM_EPS = 1e-6
ATTN_BLOCK = 128

RET_HEADS = 4
RET_DK = 128
RET_DV = 128
RET_CHUNK = 128
RET_ROPE_BASE = 10000.0

SSM_HEADS = 8
SSM_HEAD_DIM = 64
SSM_D_INNER = SSM_HEADS * SSM_HEAD_DIM
SSM_D_STATE = 64
SSM_GROUPS = 2
SSM_CONV = 4
SSM_CHUNK = 128
SSM_CONV_DIM = SSM_D_INNER + 2 * SSM_GROUPS * SSM_D_STATE

EVEN_SPLITS = (RET_HEADS * RET_DK, RET_HEADS * RET_DK, RET_HEADS * RET_DV, RET_HEADS * RET_DV, SSM_D_INNER, SSM_CONV_DIM, SSM_HEADS)

NSA_HEADS = 8
NSA_KV_HEADS = 2
NSA_HEAD_DIM = 64
NSA_CMP_BLOCK = 32
NSA_CMP_STRIDE = 16
NSA_CMP_HIDDEN = 256
NSA_SEL_BLOCK = 64
NSA_TOPN = 16
NSA_WINDOW = 512
NSA_FORCE_SCORE = 1e6

SWA_HEADS = 8
SWA_KV_HEADS = 2
SWA_HEAD_DIM = 64
SWA_WINDOW = 128

ODD_SPLITS = (NSA_HEADS * NSA_HEAD_DIM,) + (NSA_KV_HEADS * NSA_HEAD_DIM,) * 6 + (3 * NSA_HEADS, SWA_HEADS * SWA_HEAD_DIM, SWA_KV_HEADS * SWA_HEAD_DIM, SWA_KV_HEADS * SWA_HEAD_DIM)

XATTN_HEADS = 4
XATTN_HEAD_DIM = D_MODEL // XATTN_HEADS

VMEM_LIMIT_BYTES = 52 * 1024 * 1024
TOKEN_TILE = 512
FFN_TILE = 256
LANES = 128
MASKED = -1e30

_NT = (((1,), (1,)), ((), ()))


def _compiler_params(semantics):
    return pltpu.CompilerParams(dimension_semantics=semantics, vmem_limit_bytes=VMEM_LIMIT_BYTES)


def _rms(x, w):
    return x * lax.rsqrt(jnp.mean(x * x, axis=-1, keepdims=True) + NORM_EPS) * w


def _ffn_kernel(has_out_norm, x_ref, nw_ref, wg_ref, wu_ref, wd_ref, *refs):
    out_nw_ref = refs[0] if has_out_norm else None
    o_ref, a_ref = refs[-2:]
    x = x_ref[...]
    h = _rms(x, nw_ref[...]).astype(BF16)
    for c in range(D_FF // FFN_TILE):
        cols = slice(c * FFN_TILE, (c + 1) * FFN_TILE)
        g = jnp.dot(h, wg_ref[:, cols], preferred_element_type=F32)
        u = jnp.dot(h, wu_ref[:, cols], preferred_element_type=F32)
        a_ref[:, cols] = (g * jax.nn.sigmoid(g) * u).astype(BF16)
    y = x + 0.5 * jnp.dot(a_ref[...], wd_ref[...], preferred_element_type=F32)
    o_ref[...] = _rms(y, out_nw_ref[...]) if has_out_norm else y


def _ffn(x, nw, wg, wu, wd, layer, k, out_nw=None):
    t, d = x.shape
    tm = TOKEN_TILE
    resident = pl.Buffered(1)
    vec = pl.BlockSpec((1, d), lambda i: (0, 0))
    return pl.pallas_call(
        functools.partial(_ffn_kernel, out_nw is not None),
        grid=(t // tm,),
        in_specs=[
            pl.BlockSpec((tm, d), lambda i: (i, 0)),
            vec,
            pl.BlockSpec((None, None, d, D_FF), lambda i: (layer, k, 0, 0), pipeline_mode=resident),
            pl.BlockSpec((None, None, d, D_FF), lambda i: (layer, k, 0, 0), pipeline_mode=resident),
            pl.BlockSpec((None, None, D_FF, d), lambda i: (layer, k, 0, 0), pipeline_mode=resident),
        ] + ([vec] if out_nw is not None else []),
        out_specs=pl.BlockSpec((tm, d), lambda i: (i, 0)),
        out_shape=jax.ShapeDtypeStruct((t, d), F32),
        scratch_shapes=[pltpu.VMEM((tm, D_FF), BF16)],
        compiler_params=_compiler_params(("parallel",)),
        name="ffn",
    )(x, nw, wg, wu, wd, *([out_nw] if out_nw is not None else []))


def _norm_proj_kernel(head_dims, x_ref, nw_ref, *refs):
    n_out = len(head_dims)
    w_refs, o_refs = refs[:n_out], refs[n_out:]
    h = _rms(x_ref[...], nw_ref[...]).astype(BF16)
    for hd, w_ref, o_ref in zip(head_dims, w_refs, o_refs):
        y = jnp.dot(h, w_ref[...], preferred_element_type=F32)
        if hd is None:
            o_ref[...] = y.astype(o_ref.dtype)
        else:
            for j in range(o_ref.shape[0]):
                o_ref[j] = y[:, j * hd:(j + 1) * hd].astype(o_ref.dtype)


def _norm_proj(x, nw, weights, head_dims=None, dtypes=None, tm=TOKEN_TILE):
    t, d = x.shape
    n_out = len(weights)
    head_dims = tuple(head_dims) if head_dims is not None else (None,) * n_out
    dtypes = tuple(dtypes) if dtypes is not None else (F32,) * n_out
    out_specs, out_shape = [], []
    for w, hd, dt in zip(weights, head_dims, dtypes):
        n = w.shape[1]
        if hd is None:
            out_specs.append(pl.BlockSpec((tm, n), lambda i: (i, 0)))
            out_shape.append(jax.ShapeDtypeStruct((t, n), dt))
        else:
            out_specs.append(pl.BlockSpec((n // hd, tm, hd), lambda i: (0, i, 0)))
            out_shape.append(jax.ShapeDtypeStruct((n // hd, t, hd), dt))
    return pl.pallas_call(
        functools.partial(_norm_proj_kernel, head_dims),
        grid=(t // tm,),
        in_specs=[pl.BlockSpec((tm, d), lambda i: (i, 0)), pl.BlockSpec((1, d), lambda i: (0, 0))]
        + [pl.BlockSpec(w.shape, lambda i: (0, 0)) for w in weights],
        out_specs=out_specs,
        out_shape=out_shape,
        compiler_params=_compiler_params(("parallel",)),
        name="norm_proj",
    )(x, nw, *weights)


def _mix_xattn_kernel(group_sizes, x_ref, *refs):
    n_a = sum(group_sizes)
    n_g = len(group_sizes)
    a_refs, w_refs = refs[:n_a], refs[n_a:n_a + n_g]
    nw_ref, wq_ref, kv_ref, wo_ref, o_ref = refs[n_a + n_g:]
    x = x_ref[...]
    pos = 0
    for size, w_ref in zip(group_sizes, w_refs):
        a = a_refs[pos][...]
        for r in a_refs[pos + 1:pos + size]:
            a = a.astype(F32) + r[...].astype(F32)
        pos += size
        x = x + jnp.dot(a.astype(BF16), w_ref[...], preferred_element_type=F32)
    h = _rms(x, nw_ref[...]).astype(BF16)
    q = jnp.dot(h, wq_ref[...], preferred_element_type=F32).astype(BF16)
    hd = XATTN_HEAD_DIM
    outs = []
    for hh in range(XATTN_HEADS):
        k = kv_ref[:, hh * hd:(hh + 1) * hd].astype(BF16)
        v = kv_ref[:, D_MODEL + hh * hd:D_MODEL + (hh + 1) * hd].astype(BF16)
        s = lax.dot_general(q[:, hh * hd:(hh + 1) * hd], k, _NT, preferred_element_type=F32)
        s = s * (hd ** -0.5)
        p = jnp.exp(s - jnp.max(s, axis=-1, keepdims=True))
        p = p / jnp.sum(p, axis=-1, keepdims=True)
        outs.append(jnp.dot(p.astype(BF16), v, preferred_element_type=F32).astype(BF16))
    o = jnp.concatenate(outs, axis=-1)
    o_ref[...] = x + jnp.dot(o, wo_ref[...], preferred_element_type=F32)


def _mix_xattn(x, groups, weights, nw, wq, kv, wo, bsz, tm=TOKEN_TILE):
    t, d = x.shape
    seq = t // bsz
    nt = seq // tm
    m = kv.shape[1]
    arrays = [a for grp in groups for a in grp]

    def tok(width):
        return pl.BlockSpec((tm, width), lambda b, i: (b * nt + i, 0))

    def const(shape):
        return pl.BlockSpec(shape, lambda b, i: (0, 0))

    return pl.pallas_call(
        functools.partial(_mix_xattn_kernel, tuple(len(grp) for grp in groups)),
        grid=(bsz, nt),
        in_specs=[tok(d)] + [tok(a.shape[1]) for a in arrays] + [const(w.shape) for w in weights]
        + [const((1, d)), const((d, d)), pl.BlockSpec((None, m, 2 * d), lambda b, i: (b, 0, 0)), const((d, d))],
        out_specs=tok(d),
        out_shape=jax.ShapeDtypeStruct((t, d), F32),
        compiler_params=_compiler_params(("parallel", "parallel")),
        name="mix_xattn",
    )(x, *arrays, *weights, nw, wq, kv, wo)


def _retention_kernel(q_ref, k_ref, v_ref, g_ref, cos_ref, sin_ref, dmat_ref, zeta_ref, xi_ref, decay_ref, o_ref, state_ref):
    @pl.when(pl.program_id(1) == 0)
    def _():
        state_ref[...] = jnp.zeros_like(state_ref)

    cos = cos_ref[...]
    sin = sin_ref[...]
    half = RET_DK // 2
    outs = []
    for h in range(RET_HEADS):
        qh = q_ref[:, h * RET_DK:(h + 1) * RET_DK]
        kh = k_ref[:, h * RET_DK:(h + 1) * RET_DK]
        qr = qh * cos + pltpu.roll(qh, half, 1) * sin
        kr = (kh * cos + pltpu.roll(kh, half, 1) * sin) * (RET_DK ** -0.5)
        vh = v_ref[:, h * RET_DV:(h + 1) * RET_DV].astype(BF16)
        scores = lax.dot_general(qr.astype(BF16), kr.astype(BF16), _NT, preferred_element_type=F32) * dmat_ref[h]
        inner = jnp.dot(scores.astype(BF16), vh, preferred_element_type=F32)
        state = state_ref[h]
        cross = jnp.dot((qr * xi_ref[h]).astype(BF16), state.astype(BF16), preferred_element_type=F32)
        kz = (kr * zeta_ref[h]).T.astype(BF16)
        state_ref[h] = decay_ref[h] * state + jnp.dot(kz, vh, preferred_element_type=F32)
        o = inner + cross
        o = o * lax.rsqrt(jnp.mean(o * o, axis=-1, keepdims=True) + NORM_EPS)
        gh = g_ref[:, h * RET_DV:(h + 1) * RET_DV]
        outs.append(gh * jax.nn.sigmoid(gh) * o)
    o_ref[...] = jnp.concatenate(outs, axis=-1).astype(o_ref.dtype)


def _retention_tables(seq):
    L = RET_CHUNK
    pos = jnp.arange(seq, dtype=F32)
    inv_freq = 1.0 / (RET_ROPE_BASE ** jnp.linspace(0.0, 1.0, RET_DK // 2, dtype=F32))
    ang = pos[:, None] * inv_freq[None, :]
    cos, sin = jnp.cos(ang), jnp.sin(ang)
    cos2 = jnp.concatenate([cos, cos], axis=-1)
    sin2 = jnp.concatenate([-sin, sin], axis=-1)
    log_g = jnp.log1p(-jnp.exp2(-5.0 - jnp.arange(RET_HEADS, dtype=F32)))
    idx = jnp.arange(L, dtype=F32)
    diff = idx[:, None] - idx[None, :]
    dmat = jnp.where(diff >= 0, jnp.exp(jnp.maximum(diff, 0.0)[None] * log_g[:, None, None]), 0.0)
    ones = jnp.ones((RET_HEADS, L, RET_DK), F32)
    zeta = jnp.exp((L - 1 - idx)[None, :] * log_g[:, None])[:, :, None] * ones
    xi = jnp.exp((idx + 1)[None, :] * log_g[:, None])[:, :, None] * ones
    decay = jnp.exp(L * log_g)[:, None, None] * ones
    return cos2, sin2, dmat, zeta, xi, decay


def _retention(q, k, v, g):
    bsz, seq, width = q.shape
    L = RET_CHUNK
    tables = _retention_tables(seq)
    tok = pl.BlockSpec((None, L, width), lambda b, c: (b, c, 0))
    rope = pl.BlockSpec((L, RET_DK), lambda b, c: (c, 0))
    const = pl.BlockSpec((RET_HEADS, L, RET_DK), lambda b, c: (0, 0, 0))
    return pl.pallas_call(
        _retention_kernel,
        grid=(bsz, seq // L),
        in_specs=[tok, tok, tok, tok, rope, rope, const, const, const, const],
        out_specs=tok,
        out_shape=jax.ShapeDtypeStruct((bsz, seq, width), BF16),
        scratch_shapes=[pltpu.VMEM((RET_HEADS, RET_DK, RET_DV), F32)],
        compiler_params=_compiler_params(("parallel", "arbitrary")),
        name="retention",
    )(q, k, v, g, *tables)


CONV_PAD = 8


def _ssd_kernel(xbc_ref, dt_ref, z_ref, convw_ref, convb_ref, dtb_ref, alog_ref, dskip_ref, nw_ref, tri_ref, o_ref, ext_ref, state_ref):
    L = SSM_CHUNK
    hp = SSM_HEAD_DIM
    ns = SSM_D_STATE

    @pl.when(pl.program_id(1) == 0)
    def _():
        ext_ref[0:CONV_PAD, :] = jnp.zeros((CONV_PAD, SSM_CONV_DIM), F32)
        state_ref[...] = jnp.zeros_like(state_ref)

    ext_ref[CONV_PAD:CONV_PAD + L, :] = xbc_ref[...]
    conv = convb_ref[...]
    for w in range(SSM_CONV):
        off = CONV_PAD - (SSM_CONV - 1) + w
        conv = conv + ext_ref[off:off + L, :] * convw_ref[w:w + 1, :]
    ext_ref[0:CONV_PAD, :] = ext_ref[L:L + CONV_PAD, :]
    xa = conv * jax.nn.sigmoid(conv)
    xs = xa[:, :SSM_D_INNER]
    bmat = xa[:, SSM_D_INNER:SSM_D_INNER + SSM_GROUPS * ns]
    cmat = xa[:, SSM_D_INNER + SSM_GROUPS * ns:]

    x = dt_ref[...] + dtb_ref[...]
    dt = jnp.maximum(x, 0.0) + jnp.log1p(jnp.exp(-jnp.abs(x)))
    lane = lax.broadcasted_iota(jnp.int32, (L, LANES), 1)
    dta = jnp.where(lane < SSM_HEADS, dt * -jnp.exp(alog_ref[...]), 0.0)
    cums = jnp.dot(tri_ref[...], dta, preferred_element_type=F32, precision=lax.Precision.HIGHEST)
    cums_t = cums.T
    last = cums[L - 1:L, :]
    decay_to_end = jnp.exp(last - cums)
    decay_from_start = jnp.exp(cums)
    chunk_decay = jnp.exp(last)
    causal = lax.broadcasted_iota(jnp.int32, (L, L), 0) >= lax.broadcasted_iota(jnp.int32, (L, L), 1)
    heads_per_group = SSM_HEADS // SSM_GROUPS
    ys = []
    for g in range(SSM_GROUPS):
        bg = bmat[:, g * ns:(g + 1) * ns].astype(BF16)
        cg = cmat[:, g * ns:(g + 1) * ns].astype(BF16)
        cb = lax.dot_general(cg, bg, _NT, preferred_element_type=F32)
        for e in range(heads_per_group):
            h = g * heads_per_group + e
            xh = xs[:, h * hp:(h + 1) * hp]
            xdt = xh * dt[:, h:h + 1]
            seg = cums[:, h:h + 1] - cums_t[h:h + 1, :]
            lmat = jnp.where(causal, jnp.exp(seg), 0.0)
            y_diag = jnp.dot((cb * lmat).astype(BF16), xdt.astype(BF16), preferred_element_type=F32)
            state = state_ref[h]
            y_off = lax.dot_general(cg, state.astype(BF16), _NT, preferred_element_type=F32) * decay_from_start[:, h:h + 1]
            xw = (xdt * decay_to_end[:, h:h + 1]).astype(BF16)
            new = lax.dot_general(xw, bg, (((0,), (0,)), ((), ())), preferred_element_type=F32)
            state_ref[h] = chunk_decay[:, h:h + 1] * state + new
            ys.append(y_diag + y_off + dskip_ref[:, h * hp:(h + 1) * hp] * xh)
    y = jnp.concatenate(ys, axis=-1)
    z = z_ref[...]
    o_ref[...] = _rms(y * (z * jax.nn.sigmoid(z)), nw_ref[...]).astype(o_ref.dtype)


def _ssd(xbc, dt_raw, z, conv_w, conv_b, dt_bias, a_log, d_skip, norm_w):
    bsz, seq, _ = xbc.shape
    L = SSM_CHUNK
    pad = LANES - SSM_HEADS
    params = [conv_w, conv_b.reshape(1, -1), jnp.pad(dt_bias, (0, pad)).reshape(1, LANES),
              jnp.pad(a_log, (0, pad)).reshape(1, LANES), jnp.repeat(d_skip, SSM_HEAD_DIM).reshape(1, -1),
              norm_w.reshape(1, -1), jnp.asarray(np.tril(np.ones((L, L), np.float32)))]

    def tok(width):
        return pl.BlockSpec((None, L, width), lambda b, c: (b, c, 0))

    return pl.pallas_call(
        _ssd_kernel,
        grid=(bsz, seq // L),
        in_specs=[tok(SSM_CONV_DIM), tok(LANES), tok(SSM_D_INNER)] + [pl.BlockSpec(p.shape, lambda b, c: (0, 0)) for p in params],
        out_specs=tok(SSM_D_INNER),
        out_shape=jax.ShapeDtypeStruct((bsz, seq, SSM_D_INNER), BF16),
        scratch_shapes=[pltpu.VMEM((L + CONV_PAD, SSM_CONV_DIM), F32), pltpu.VMEM((SSM_HEADS, SSM_HEAD_DIM, SSM_D_STATE), F32)],
        compiler_params=_compiler_params(("parallel", "arbitrary")),
        name="ssd",
    )(xbc, dt_raw, z, *params)


def _even_mixer(xt, nw, w_in, w_out, conv_w, conv_b, dt_bias, a_log, d_skip, ssm_norm_w, bsz):
    t, d = xt.shape
    seq = t // bsz
    cuts = [int(c) for c in np.cumsum((0,) + EVEN_SPLITS)]
    ws = [w_in[:, a:b] for a, b in zip(cuts[:-1], cuts[1:])]
    perm = np.concatenate([h * RET_DK + np.concatenate([np.arange(0, RET_DK, 2), np.arange(1, RET_DK, 2)]) for h in range(RET_HEADS)])
    ws[0] = ws[0][:, perm]
    ws[1] = ws[1][:, perm]
    ws[6] = jnp.pad(ws[6], ((0, 0), (0, LANES - SSM_HEADS)))
    dtypes = (F32, F32, BF16, F32, F32, F32, F32)
    q, k, v, g, z, xbc, dt_raw = [a.reshape(bsz, seq, -1) for a in _norm_proj(xt, nw, ws, dtypes=dtypes)]
    o_ret = _retention(q, k, v, g)
    y = _ssd(xbc, dt_raw, z, conv_w, conv_b, dt_bias, a_log, d_skip, ssm_norm_w)
    n_ret = RET_HEADS * RET_DV
    return [[o_ret.reshape(t, n_ret)], [y.reshape(t, SSM_D_INNER)]], [w_out[:n_ret], w_out[n_ret:]]


Q_TILE = ATTN_BLOCK
GROUP_HEADS = NSA_HEADS // NSA_KV_HEADS
HEAD_DIM = NSA_HEAD_DIM
SEL_CHUNK = 512


def _cmp_kv_kernel(r_ref, pe_ref, w1_ref, w2_ref, o_ref):
    r = r_ref[...]
    half = NSA_CMP_STRIDE * HEAD_DIM
    top = jnp.dot((r + pe_ref[0:1, :]).astype(BF16), w1_ref[0:half, :], preferred_element_type=F32)
    bot = jnp.dot((r + pe_ref[1:2, :]).astype(BF16), w1_ref[half:2 * half, :], preferred_element_type=F32)
    hidden = top + pltpu.roll(bot, r.shape[0] - 1, 0)
    act = hidden * jax.nn.sigmoid(hidden)
    o_ref[...] = jnp.dot(act.astype(BF16), w2_ref[...], preferred_element_type=F32)


def _cmp_kv(r, pe, w1, w2):
    ng, bsz, n, width = r.shape
    return pl.pallas_call(
        _cmp_kv_kernel,
        grid=(ng, bsz),
        in_specs=[pl.BlockSpec((None, None, n, width), lambda g, b: (g, b, 0, 0)),
                  pl.BlockSpec(pe.shape, lambda g, b: (0, 0)),
                  pl.BlockSpec(w1.shape, lambda g, b: (0, 0)),
                  pl.BlockSpec(w2.shape, lambda g, b: (0, 0))],
        out_specs=pl.BlockSpec((None, None, n, HEAD_DIM), lambda g, b: (g, b, 0, 0)),
        out_shape=jax.ShapeDtypeStruct((ng, bsz, n, HEAD_DIM), F32),
        compiler_params=_compiler_params(("parallel", "parallel")),
        name="nsa_compress",
    )(r, pe, w1, w2)


def _heads_from_t(o_t):
    pairs = []
    for e in range(0, GROUP_HEADS, 2):
        pair = jnp.concatenate([o_t[:, e * Q_TILE:(e + 1) * Q_TILE], o_t[:, (e + 1) * Q_TILE:(e + 2) * Q_TILE]], axis=0)
        pairs.append(pair.T)
    return jnp.concatenate(pairs, axis=-1)


def _gate_rows(gate_ref, g):
    gate = jax.nn.sigmoid(gate_ref[g])
    gate = jnp.concatenate([gate, jnp.zeros((Q_TILE, LANES - gate.shape[1]), F32)], axis=1)
    return gate.T


def _gate_cols(gate_t, branch):
    return jnp.concatenate([gate_t[3 * e + branch:3 * e + branch + 1, :] for e in range(GROUP_HEADS)], axis=1)


def _tile_heads(x):
    return jnp.concatenate([x] * GROUP_HEADS, axis=1)


def _pv_t(v, p):
    return lax.dot_general(v, p.astype(BF16), (((0,), (0,)), ((), ())), preferred_element_type=F32)


def _band_window(window):
    return (-(-(window - 1) // Q_TILE) + 1) * Q_TILE


def _band_branch(i, q, k_ref, v_ref, g, window, sink=None):
    n_keys = _band_window(window)
    start = pl.multiple_of(jnp.maximum(i * Q_TILE + Q_TILE - n_keys, 0), Q_TILE)
    rel = (i * Q_TILE + lax.broadcasted_iota(jnp.int32, (n_keys, Q_TILE), 1)) - (start + lax.broadcasted_iota(jnp.int32, (n_keys, Q_TILE), 0))
    bias = _tile_heads(jnp.where(rel >= 0, jnp.where(rel < window, 0.0, MASKED), MASKED))
    s = lax.dot_general(k_ref[g, pl.ds(start, n_keys), :], q, _NT, preferred_element_type=F32) + bias
    m = jnp.max(s, axis=0, keepdims=True)
    if sink is not None:
        m = jnp.maximum(m, sink)
    p = jnp.exp(s - m)
    denom = jnp.sum(p, axis=0, keepdims=True)
    denom = denom + jnp.exp(sink - m) if sink is not None else jnp.maximum(denom, 1e-30)
    return _pv_t(v_ref[g, pl.ds(start, n_keys), :], p) * (1.0 / denom)


def _nsa_dense_kernel(q_ref, kc_ref, vc_ref, kw_ref, vw_ref, gate_ref, ov_ref, qs_ref, ks_ref, vs_ref, sink_ref,
                      o_ref, sel_ref, swa_ref):
    i = pl.program_id(1)
    rows = GROUP_HEADS * Q_TILE
    width = GROUP_HEADS * HEAD_DIM
    n_cmp = kc_ref.shape[1]
    n = lax.broadcasted_iota(jnp.int32, (n_cmp, Q_TILE), 0)
    t = i * Q_TILE + lax.broadcasted_iota(jnp.int32, (n_cmp, Q_TILE), 1)
    valid = _tile_heads(jnp.where(n * NSA_CMP_STRIDE + (NSA_CMP_BLOCK - 1) <= t, 1.0, 0.0))
    bias = (valid - 1.0) * (-MASKED)
    n_sel = ov_ref.shape[0]
    jb = lax.broadcasted_iota(jnp.int32, (n_sel, Q_TILE), 0)
    cur = (i * Q_TILE + lax.broadcasted_iota(jnp.int32, (n_sel, Q_TILE), 1)) // NSA_SEL_BLOCK
    valid_b = jb <= cur
    forced = jnp.where(valid_b, jnp.where(jb == 0, 1.0, 0.0) + jnp.where(jb == cur, 1.0, 0.0) + jnp.where(jb == cur - 1, 1.0, 0.0), 0.0)
    for g in range(NSA_KV_HEADS):
        q = q_ref[g * GROUP_HEADS:(g + 1) * GROUP_HEADS].reshape(rows, HEAD_DIM)
        s = lax.dot_general(kc_ref[g].astype(BF16), q, _NT, preferred_element_type=F32) + bias
        m = jnp.max(s, axis=0, keepdims=True)
        p = jnp.exp(s - m) * valid
        p = p * (1.0 / jnp.maximum(jnp.sum(p, axis=0, keepdims=True), 1e-30))
        o_cmp = _pv_t(vc_ref[g].astype(BF16), p)
        o_win = _band_branch(i, q, kw_ref, vw_ref, g, NSA_WINDOW)
        gate_t = _gate_rows(gate_ref, g)
        o_ref[:, g * width:(g + 1) * width] = _heads_from_t(o_cmp * _gate_cols(gate_t, 0) + o_win * _gate_cols(gate_t, 2)).astype(o_ref.dtype)

        p_sum = p[:, 0:Q_TILE]
        for e in range(1, GROUP_HEADS):
            p_sum = p_sum + p[:, e * Q_TILE:(e + 1) * Q_TILE]
        imp = jnp.dot(ov_ref[...], p_sum, preferred_element_type=F32, precision=lax.Precision.HIGHEST)
        score = jnp.where(forced > 0.5, NSA_FORCE_SCORE, jnp.where(valid_b, imp, -jnp.inf))
        rank = jnp.zeros((n_sel, Q_TILE), F32)
        for ii in range(n_sel):
            row = score[ii:ii + 1, :]
            tie = jnp.where(jb > ii, 1.0, 0.0)
            rank = rank + jnp.where(row > score, 1.0, jnp.where(row == score, tie, 0.0))
        sel_t = jnp.where(valid_b, jnp.where(rank < NSA_TOPN, 1.0, 0.0), 0.0)
        sel_t = jnp.concatenate([sel_t, jnp.zeros((LANES - n_sel, Q_TILE), F32)], axis=0)
        sel_ref[g] = sel_t.T.astype(BF16)

    for g in range(SWA_KV_HEADS):
        q = qs_ref[g * GROUP_HEADS:(g + 1) * GROUP_HEADS].reshape(rows, HEAD_DIM)
        sink = jnp.concatenate([jnp.broadcast_to(sink_ref[g][:, e:e + 1], (1, Q_TILE)) for e in range(GROUP_HEADS)], axis=1)
        o_swa = _band_branch(i, q, ks_ref, vs_ref, g, SWA_WINDOW, sink)
        swa_ref[:, g * width:(g + 1) * width] = _heads_from_t(o_swa).astype(swa_ref.dtype)


def _nsa_dense(q, k_cmp, v_cmp, kw, vw, gates, overlap_t, q_s, k_s, v_s, sinks):
    nh, bsz, seq, hd = q.shape
    ng, _, n_cmp, _ = k_cmp.shape
    heads = pl.BlockSpec((nh, None, Q_TILE, hd), lambda b, i: (0, b, i, 0))
    cmp_kv = pl.BlockSpec((ng, None, n_cmp, hd), lambda b, i: (0, b, 0, 0))
    seq_kv = pl.BlockSpec((ng, None, seq, hd), lambda b, i: (0, b, 0, 0))
    out = pl.BlockSpec((None, Q_TILE, nh * hd), lambda b, i: (b, i, 0))
    return pl.pallas_call(
        _nsa_dense_kernel,
        grid=(bsz, seq // Q_TILE),
        in_specs=[heads, cmp_kv, cmp_kv, seq_kv, seq_kv,
                  pl.BlockSpec((ng, None, Q_TILE, gates.shape[-1]), lambda b, i: (0, b, i, 0)),
                  pl.BlockSpec(overlap_t.shape, lambda b, i: (0, 0)),
                  heads, seq_kv, seq_kv,
                  pl.BlockSpec(sinks.shape, lambda b, i: (0, 0, 0))],
        out_specs=[out, pl.BlockSpec((ng, None, Q_TILE, LANES), lambda b, i: (0, b, i, 0)), out],
        out_shape=[jax.ShapeDtypeStruct((bsz, seq, nh * hd), BF16),
                   jax.ShapeDtypeStruct((ng, bsz, seq, LANES), BF16),
                   jax.ShapeDtypeStruct((bsz, seq, nh * hd), BF16)],
        compiler_params=_compiler_params(("parallel", "parallel")),
        name="nsa_dense",
    )(q, k_cmp, v_cmp, kw, vw, gates, overlap_t, q_s, k_s, v_s, sinks)


def _nsa_sel_kernel(q_ref, k_ref, v_ref, sel_ref, gate_ref, o_ref, s_ref):
    i = pl.program_id(1)
    rows = GROUP_HEADS * Q_TILE
    width = GROUP_HEADS * HEAD_DIM
    n_blk = LANES - HEAD_DIM
    kc = SEL_CHUNK
    seq = k_ref.shape[1]
    groups = range(NSA_KV_HEADS)

    def scores(g, start, lhs):
        return lax.dot_general(k_ref[g, pl.ds(start, kc), :], lhs, _NT, preferred_element_type=F32)

    start_d = pl.multiple_of(jnp.maximum(i * Q_TILE + Q_TILE - kc, 0), Q_TILE)
    n_past = (start_d + kc - 1) // kc
    kpos = start_d + lax.broadcasted_iota(jnp.int32, (kc, rows), 0)
    qpos = i * Q_TILE + (lax.broadcasted_iota(jnp.int32, (kc, rows), 1) & (Q_TILE - 1))
    blk = lax.broadcasted_iota(jnp.int32, (Q_TILE, n_blk), 1)
    lhs_past, lhs_none = [], []
    for g in groups:
        q = q_ref[g * GROUP_HEADS:(g + 1) * GROUP_HEADS].reshape(rows, HEAD_DIM)
        sel = sel_ref[g][:, :n_blk].astype(F32)
        neg_diag = (sel - 1.0) * (-MASKED)
        neg_past = jnp.where(blk < start_d // NSA_SEL_BLOCK, neg_diag, MASKED)
        lhs_diag = jnp.concatenate([q, jnp.concatenate([neg_diag.astype(BF16)] * GROUP_HEADS, axis=0)], axis=-1)
        lhs_past.append(jnp.concatenate([q, jnp.concatenate([neg_past.astype(BF16)] * GROUP_HEADS, axis=0)], axis=-1))
        lhs_none.append(jnp.concatenate([q, jnp.full((rows, n_blk), MASKED, BF16)], axis=-1))
        s_ref[g, 0] = jnp.where(kpos <= qpos, scores(g, start_d, lhs_diag), MASKED)

    def step(c, carry, cur, nxt):
        next_start = pl.multiple_of(jnp.minimum(c * kc, seq - kc), kc)
        v_start = pl.multiple_of(jnp.where(c == 0, start_d, jnp.minimum((c - 1) * kc, seq - kc)), Q_TILE)
        out = []
        for g in groups:
            m, l, acc = carry[g]
            s_ref[g, nxt] = scores(g, next_start, jnp.where(c < n_past, lhs_past[g], lhs_none[g]))
            s = s_ref[g, cur]
            m_new = jnp.maximum(m, jnp.max(s, axis=0, keepdims=True))
            alpha = jnp.exp(m - m_new)
            p = jnp.exp(s - m_new)
            l = alpha * l + jnp.sum(p, axis=0, keepdims=True)
            out.append((m_new, l, alpha * acc + _pv_t(v_ref[g, pl.ds(v_start, kc), :], p)))
        return tuple(out)

    def body(j, carry):
        return step(2 * j + 1, step(2 * j, carry, 0, 1), 1, 0)

    init = (jnp.full((1, rows), MASKED, F32), jnp.zeros((1, rows), F32), jnp.zeros((HEAD_DIM, rows), F32))
    final = lax.fori_loop(0, (n_past + 2) // 2, body, tuple(init for _ in groups))
    for g in groups:
        _, l, acc = final[g]
        gate = _gate_cols(_gate_rows(gate_ref, g), 1)
        o_ref[:, g * width:(g + 1) * width] = _heads_from_t(acc * (gate / l)).astype(o_ref.dtype)


def _nsa_sel(q, k, v, sel, gates):
    nh, bsz, seq, hd = q.shape
    ng = k.shape[0]
    onehot = (np.arange(seq)[:, None] // NSA_SEL_BLOCK == np.arange(LANES - hd)[None, :]).astype(np.float32)
    k_ext = jnp.concatenate([k, jnp.broadcast_to(jnp.asarray(onehot, BF16), (ng, bsz, seq, LANES - hd))], axis=-1)
    return pl.pallas_call(
        _nsa_sel_kernel,
        grid=(bsz, seq // Q_TILE),
        in_specs=[pl.BlockSpec((nh, None, Q_TILE, hd), lambda b, i: (0, b, i, 0)),
                  pl.BlockSpec((ng, None, seq, LANES), lambda b, i: (0, b, 0, 0)),
                  pl.BlockSpec((ng, None, seq, hd), lambda b, i: (0, b, 0, 0)),
                  pl.BlockSpec((ng, None, Q_TILE, LANES), lambda b, i: (0, b, i, 0)),
                  pl.BlockSpec((ng, None, Q_TILE, gates.shape[-1]), lambda b, i: (0, b, i, 0))],
        out_specs=pl.BlockSpec((None, Q_TILE, nh * hd), lambda b, i: (b, i, 0)),
        out_shape=jax.ShapeDtypeStruct((bsz, seq, nh * hd), BF16),
        scratch_shapes=[pltpu.VMEM((ng, 2, SEL_CHUNK, GROUP_HEADS * Q_TILE), F32)],
        compiler_params=_compiler_params(("parallel", "parallel")),
        name="nsa_sel",
    )(q, k_ext, v, sel, gates)


def _overlap_t(seq):
    n_cmp = (seq - NSA_CMP_BLOCK) // NSA_CMP_STRIDE + 1
    n_pad = seq // NSA_CMP_STRIDE
    starts = np.arange(n_pad) * NSA_CMP_STRIDE
    sel_start = np.arange(seq // NSA_SEL_BLOCK) * NSA_SEL_BLOCK
    ov = (starts[None, :] < sel_start[:, None] + NSA_SEL_BLOCK) & (starts[None, :] + NSA_CMP_BLOCK > sel_start[:, None])
    ov = ov & (np.arange(n_pad)[None, :] < n_cmp)
    return jnp.asarray(ov.astype(np.float32))


def _odd_mixer(xt, nw, w_in, w_out, cmp_pe, cmp_w1, cmp_w2, sinks, bsz):
    t, d = xt.shape
    seq = t // bsz
    assert seq // NSA_SEL_BLOCK <= LANES and seq % SEL_CHUNK == 0
    cuts = [int(c) for c in np.cumsum((0,) + ODD_SPLITS)]
    ws = [w_in[:, a:b] for a, b in zip(cuts[:-1], cuts[1:])]
    scale = HEAD_DIM ** -0.5
    ws[0] = ws[0] * scale
    ws[8] = ws[8] * scale
    hd = HEAD_DIM
    head_dims = (hd, hd, hd, hd, hd, hd, hd, 3 * GROUP_HEADS, hd, hd, hd)
    dtypes = (BF16, F32, F32, BF16, BF16, BF16, BF16, F32, BF16, BF16, BF16)
    q_n, kc, vc, ks, vs, kw, vw, gates, q_s, k_s, v_s = _norm_proj(xt, nw, ws, head_dims, dtypes)

    def heads(a):
        return a.reshape(a.shape[0], bsz, seq, a.shape[-1])

    q_n, ks, vs, kw, vw, gates, q_s, k_s, v_s = map(heads, (q_n, ks, vs, kw, vw, gates, q_s, k_s, v_s))
    ng = NSA_KV_HEADS
    rows16 = NSA_CMP_STRIDE * hd
    pe = cmp_pe.reshape(2, 2, rows16)
    w1 = cmp_w1.astype(BF16)
    w2 = cmp_w2.astype(BF16)
    k_cmp = _cmp_kv(kc.reshape(ng, bsz, seq // NSA_CMP_STRIDE, rows16), pe[0], w1[0], w2[0])
    v_cmp = _cmp_kv(vc.reshape(ng, bsz, seq // NSA_CMP_STRIDE, rows16), pe[1], w1[1], w2[1])
    o_cw, sel, o_swa = _nsa_dense(q_n, k_cmp, v_cmp, kw, vw, gates, _overlap_t(seq), q_s, k_s, v_s,
                                  sinks.reshape(SWA_KV_HEADS, 1, GROUP_HEADS))
    o_sel = _nsa_sel(q_n, ks, vs, sel, gates)
    n_nsa = NSA_HEADS * hd
    return [[o_cw.reshape(t, n_nsa), o_sel.reshape(t, n_nsa)], [o_swa.reshape(t, SWA_HEADS * hd)]], [w_out[:n_nsa], w_out[n_nsa:]]


def kernel(x, mem, norm_w, final_norm_w, mem_norm_w, ffn_w_gate, ffn_w_up, ffn_w_down, xattn_wq, xattn_wkv, xattn_wo, even_w_in, even_w_out, ssm_conv_w, ssm_conv_b, ssm_dt_bias, ssm_a_log, ssm_d, ssm_norm_w, odd_w_in, odd_w_out, nsa_cmp_pe, nsa_cmp_w1, nsa_cmp_w2, swa_sinks):
    bsz, seq, d = x.shape
    t = bsz * seq
    m = mem.shape[1]
    wg = ffn_w_gate.astype(BF16)
    wu = ffn_w_up.astype(BF16)
    wd = ffn_w_down.astype(BF16)
    wq = xattn_wq.astype(BF16)
    wkv = xattn_wkv.astype(BF16)
    wo = xattn_wo.astype(BF16)
    even_in = even_w_in.astype(BF16)
    even_out = even_w_out.astype(BF16)
    odd_in = odd_w_in.astype(BF16)
    odd_out = odd_w_out.astype(BF16)

    xt = x.reshape(t, d)
    mem2 = mem.reshape(bsz * m, d)
    mem_nw = mem_norm_w.reshape(1, d)
    for layer in range(DEPTH):
        i = layer // 2
        xt = _ffn(xt, norm_w[layer, 0].reshape(1, d), wg, wu, wd, layer, 0)
        nw1 = norm_w[layer, 1].reshape(1, d)
        if layer % 2 == 0:
            mix, mix_w = _even_mixer(xt, nw1, even_in[i], even_out[i], ssm_conv_w[i], ssm_conv_b[i], ssm_dt_bias[i], ssm_a_log[i], ssm_d[i], ssm_norm_w[i], bsz)
        else:
            mix, mix_w = _odd_mixer(xt, nw1, odd_in[i], odd_out[i], nsa_cmp_pe[i], nsa_cmp_w1[i], nsa_cmp_w2[i], swa_sinks[i], bsz)
        (kv,) = _norm_proj(mem2, mem_nw, [wkv[layer]])
        xt = _mix_xattn(xt, mix, mix_w, norm_w[layer, 2].reshape(1, d), wq[layer], kv.reshape(bsz, m, 2 * d), wo[layer], bsz)
        out_nw = final_norm_w.reshape(1, d) if layer == DEPTH - 1 else None
        xt = _ffn(xt, norm_w[layer, 3].reshape(1, d), wg, wu, wd, layer, 1, out_nw)
    return xt.reshape(bsz, seq, d)
```

```python
import functools

import numpy as np
import jax
import jax.numpy as jnp
from jax import lax
from jax.experimental import pallas as pl
from jax.experimental.pallas import tpu as pltpu

F32 = jnp.float32
BF16 = jnp.bfloat16

D_MODEL = 1024
DEPTH = 4
D_FF = 2816
NORM_EPS = 1e-6
ATTN_BLOCK = 128

RET_HEADS = 4
RET_DK = 128
RET_DV = 128
RET_CHUNK = 128
RET_ROPE_BASE = 10000.0

SSM_HEADS = 8
SSM_HEAD_DIM = 64
SSM_D_INNER = SSM_HEADS * SSM_HEAD_DIM
SSM_D_STATE = 64
SSM_GROUPS = 2
SSM_CONV = 4
SSM_CHUNK = 128
SSM_CONV_DIM = SSM_D_INNER + 2 * SSM_GROUPS * SSM_D_STATE

EVEN_SPLITS = (RET_HEADS * RET_DK, RET_HEADS * RET_DK, RET_HEADS * RET_DV, RET_HEADS * RET_DV, SSM_D_INNER, SSM_CONV_DIM, SSM_HEADS)

NSA_HEADS = 8
NSA_KV_HEADS = 2
NSA_HEAD_DIM = 64
NSA_CMP_BLOCK = 32
NSA_CMP_STRIDE = 16
NSA_CMP_HIDDEN = 256
NSA_SEL_BLOCK = 64
NSA_TOPN = 16
NSA_WINDOW = 512
NSA_FORCE_SCORE = 1e6

SWA_HEADS = 8
SWA_KV_HEADS = 2
SWA_HEAD_DIM = 64
SWA_WINDOW = 128

ODD_SPLITS = (NSA_HEADS * NSA_HEAD_DIM,) + (NSA_KV_HEADS * NSA_HEAD_DIM,) * 6 + (3 * NSA_HEADS, SWA_HEADS * SWA_HEAD_DIM, SWA_KV_HEADS * SWA_HEAD_DIM, SWA_KV_HEADS * SWA_HEAD_DIM)

XATTN_HEADS = 4
XATTN_HEAD_DIM = D_MODEL // XATTN_HEADS

VMEM_LIMIT_BYTES = 52 * 1024 * 1024
TOKEN_TILE = 512
FFN_TILE = 256
LANES = 128
MASKED = -1e30

_NT = (((1,), (1,)), ((), ()))


def _compiler_params(semantics):
    return pltpu.CompilerParams(dimension_semantics=semantics, vmem_limit_bytes=VMEM_LIMIT_BYTES)


def _rms(x, w):
    return x * lax.rsqrt(jnp.mean(x * x, axis=-1, keepdims=True) + NORM_EPS) * w


def _ffn_kernel(has_out_norm, x_ref, nw_ref, wg_ref, wu_ref, wd_ref, *refs):
    out_nw_ref = refs[0] if has_out_norm else None
    o_ref, a_ref = refs[-2:]
    x = x_ref[...]
    h = _rms(x, nw_ref[...]).astype(BF16)
    for c in range(D_FF // FFN_TILE):
        cols = slice(c * FFN_TILE, (c + 1) * FFN_TILE)
        g = jnp.dot(h, wg_ref[:, cols], preferred_element_type=F32)
        u = jnp.dot(h, wu_ref[:, cols], preferred_element_type=F32)
        a_ref[:, cols] = (g * jax.nn.sigmoid(g) * u).astype(BF16)
    y = x + 0.5 * jnp.dot(a_ref[...], wd_ref[...], preferred_element_type=F32)
    o_ref[...] = _rms(y, out_nw_ref[...]) if has_out_norm else y


def _ffn(x, nw, wg, wu, wd, layer, k, out_nw=None):
    t, d = x.shape
    tm = TOKEN_TILE
    resident = pl.Buffered(1)
    vec = pl.BlockSpec((1, d), lambda i: (0, 0))
    return pl.pallas_call(
        functools.partial(_ffn_kernel, out_nw is not None),
        grid=(t // tm,),
        in_specs=[
            pl.BlockSpec((tm, d), lambda i: (i, 0)),
            vec,
            pl.BlockSpec((None, None, d, D_FF), lambda i: (layer, k, 0, 0), pipeline_mode=resident),
            pl.BlockSpec((None, None, d, D_FF), lambda i: (layer, k, 0, 0), pipeline_mode=resident),
            pl.BlockSpec((None, None, D_FF, d), lambda i: (layer, k, 0, 0), pipeline_mode=resident),
        ] + ([vec] if out_nw is not None else []),
        out_specs=pl.BlockSpec((tm, d), lambda i: (i, 0)),
        out_shape=jax.ShapeDtypeStruct((t, d), F32),
        scratch_shapes=[pltpu.VMEM((tm, D_FF), BF16)],
        compiler_params=_compiler_params(("parallel",)),
        name="ffn",
    )(x, nw, wg, wu, wd, *([out_nw] if out_nw is not None else []))


def _norm_proj_kernel(head_dims, x_ref, nw_ref, *refs):
    n_out = len(head_dims)
    w_refs, o_refs = refs[:n_out], refs[n_out:]
    h = _rms(x_ref[...], nw_ref[...]).astype(BF16)
    for hd, w_ref, o_ref in zip(head_dims, w_refs, o_refs):
        y = jnp.dot(h, w_ref[...], preferred_element_type=F32)
        if hd is None:
            o_ref[...] = y.astype(o_ref.dtype)
        else:
            for j in range(o_ref.shape[0]):
                o_ref[j] = y[:, j * hd:(j + 1) * hd].astype(o_ref.dtype)


def _norm_proj(x, nw, weights, head_dims=None, dtypes=None, tm=TOKEN_TILE):
    t, d = x.shape
    n_out = len(weights)
    head_dims = tuple(head_dims) if head_dims is not None else (None,) * n_out
    dtypes = tuple(dtypes) if dtypes is not None else (F32,) * n_out
    out_specs, out_shape = [], []
    for w, hd, dt in zip(weights, head_dims, dtypes):
        n = w.shape[1]
        if hd is None:
            out_specs.append(pl.BlockSpec((tm, n), lambda i: (i, 0)))
            out_shape.append(jax.ShapeDtypeStruct((t, n), dt))
        else:
            out_specs.append(pl.BlockSpec((n // hd, tm, hd), lambda i: (0, i, 0)))
            out_shape.append(jax.ShapeDtypeStruct((n // hd, t, hd), dt))
    return pl.pallas_call(
        functools.partial(_norm_proj_kernel, head_dims),
        grid=(t // tm,),
        in_specs=[pl.BlockSpec((tm, d), lambda i: (i, 0)), pl.BlockSpec((1, d), lambda i: (0, 0))]
        + [pl.BlockSpec(w.shape, lambda i: (0, 0)) for w in weights],
        out_specs=out_specs,
        out_shape=out_shape,
        compiler_params=_compiler_params(("parallel",)),
        name="norm_proj",
    )(x, nw, *weights)


def _mix_xattn_kernel(group_sizes, x_ref, *refs):
    n_a = sum(group_sizes)
    n_g = len(group_sizes)
    a_refs, w_refs = refs[:n_a], refs[n_a:n_a + n_g]
    nw_ref, wq_ref, kv_ref, wo_ref, o_ref = refs[n_a + n_g:]
    x = x_ref[...]
    pos = 0
    for size, w_ref in zip(group_sizes, w_refs):
        a = a_refs[pos][...]
        for r in a_refs[pos + 1:pos + size]:
            a = a.astype(F32) + r[...].astype(F32)
        pos += size
        x = x + jnp.dot(a.astype(BF16), w_ref[...], preferred_element_type=F32)
    h = _rms(x, nw_ref[...]).astype(BF16)
    q = jnp.dot(h, wq_ref[...], preferred_element_type=F32).astype(BF16)
    hd = XATTN_HEAD_DIM
    outs = []
    for hh in range(XATTN_HEADS):
        k = kv_ref[:, hh * hd:(hh + 1) * hd].astype(BF16)
        v = kv_ref[:, D_MODEL + hh * hd:D_MODEL + (hh + 1) * hd].astype(BF16)
        s = lax.dot_general(q[:, hh * hd:(hh + 1) * hd], k, _NT, preferred_element_type=F32)
        s = s * (hd ** -0.5)
        p = jnp.exp(s - jnp.max(s, axis=-1, keepdims=True))
        p = p / jnp.sum(p, axis=-1, keepdims=True)
        outs.append(jnp.dot(p.astype(BF16), v, preferred_element_type=F32).astype(BF16))
    o = jnp.concatenate(outs, axis=-1)
    o_ref[...] = x + jnp.dot(o, wo_ref[...], preferred_element_type=F32)


def _mix_xattn(x, groups, weights, nw, wq, kv, wo, bsz, tm=TOKEN_TILE):
    t, d = x.shape
    seq = t // bsz
    nt = seq // tm
    m = kv.shape[1]
    arrays = [a for grp in groups for a in grp]

    def tok(width):
        return pl.BlockSpec((tm, width), lambda b, i: (b * nt + i, 0))

    def const(shape):
        return pl.BlockSpec(shape, lambda b, i: (0, 0))

    return pl.pallas_call(
        functools.partial(_mix_xattn_kernel, tuple(len(grp) for grp in groups)),
        grid=(bsz, nt),
        in_specs=[tok(d)] + [tok(a.shape[1]) for a in arrays] + [const(w.shape) for w in weights]
        + [const((1, d)), const((d, d)), pl.BlockSpec((None, m, 2 * d), lambda b, i: (b, 0, 0)), const((d, d))],
        out_specs=tok(d),
        out_shape=jax.ShapeDtypeStruct((t, d), F32),
        compiler_params=_compiler_params(("parallel", "parallel")),
        name="mix_xattn",
    )(x, *arrays, *weights, nw, wq, kv, wo)


def _retention_kernel(q_ref, k_ref, v_ref, g_ref, cos_ref, sin_ref, dmat_ref, zeta_ref, xi_ref, decay_ref, o_ref, state_ref):
    @pl.when(pl.program_id(1) == 0)
    def _():
        state_ref[...] = jnp.zeros_like(state_ref)

    cos = cos_ref[...]
    sin = sin_ref[...]
    half = RET_DK // 2
    outs = []
    for h in range(RET_HEADS):
        qh = q_ref[:, h * RET_DK:(h + 1) * RET_DK]
        kh = k_ref[:, h * RET_DK:(h + 1) * RET_DK]
        qr = qh * cos + pltpu.roll(qh, half, 1) * sin
        kr = (kh * cos + pltpu.roll(kh, half, 1) * sin) * (RET_DK ** -0.5)
        vh = v_ref[:, h * RET_DV:(h + 1) * RET_DV].astype(BF16)
        scores = lax.dot_general(qr.astype(BF16), kr.astype(BF16), _NT, preferred_element_type=F32) * dmat_ref[h]
        inner = jnp.dot(scores.astype(BF16), vh, preferred_element_type=F32)
        state = state_ref[h]
        cross = jnp.dot((qr * xi_ref[h]).astype(BF16), state.astype(BF16), preferred_element_type=F32)
        kz = (kr * zeta_ref[h]).T.astype(BF16)
        state_ref[h] = decay_ref[h] * state + jnp.dot(kz, vh, preferred_element_type=F32)
        o = inner + cross
        o = o * lax.rsqrt(jnp.mean(o * o, axis=-1, keepdims=True) + NORM_EPS)
        gh = g_ref[:, h * RET_DV:(h + 1) * RET_DV]
        outs.append(gh * jax.nn.sigmoid(gh) * o)
    o_ref[...] = jnp.concatenate(outs, axis=-1).astype(o_ref.dtype)


def _retention_tables(seq):
    L = RET_CHUNK
    pos = jnp.arange(seq, dtype=F32)
    inv_freq = 1.0 / (RET_ROPE_BASE ** jnp.linspace(0.0, 1.0, RET_DK // 2, dtype=F32))
    ang = pos[:, None] * inv_freq[None, :]
    cos, sin = jnp.cos(ang), jnp.sin(ang)
    cos2 = jnp.concatenate([cos, cos], axis=-1)
    sin2 = jnp.concatenate([-sin, sin], axis=-1)
    log_g = jnp.log1p(-jnp.exp2(-5.0 - jnp.arange(RET_HEADS, dtype=F32)))
    idx = jnp.arange(L, dtype=F32)
    diff = idx[:, None] - idx[None, :]
    dmat = jnp.where(diff >= 0, jnp.exp(jnp.maximum(diff, 0.0)[None] * log_g[:, None, None]), 0.0)
    ones = jnp.ones((RET_HEADS, L, RET_DK), F32)
    zeta = jnp.exp((L - 1 - idx)[None, :] * log_g[:, None])[:, :, None] * ones
    xi = jnp.exp((idx + 1)[None, :] * log_g[:, None])[:, :, None] * ones
    decay = jnp.exp(L * log_g)[:, None, None] * ones
    return cos2, sin2, dmat, zeta, xi, decay


CONV_PAD = 8


def _ssd_kernel(xbc_ref, dt_ref, z_ref, convw_ref, convb_ref, dtb_ref, alog_ref, dskip_ref, nw_ref, tri_ref, o_ref, ext_ref, state_ref):
    L = SSM_CHUNK
    hp = SSM_HEAD_DIM
    ns = SSM_D_STATE

    @pl.when(pl.program_id(1) == 0)
    def _():
        ext_ref[0:CONV_PAD, :] = jnp.zeros((CONV_PAD, SSM_CONV_DIM), F32)
        state_ref[...] = jnp.zeros_like(state_ref)

    ext_ref[CONV_PAD:CONV_PAD + L, :] = xbc_ref[...]
    conv = convb_ref[...]
    for w in range(SSM_CONV):
        off = CONV_PAD - (SSM_CONV - 1) + w
        conv = conv + ext_ref[off:off + L, :] * convw_ref[w:w + 1, :]
    ext_ref[0:CONV_PAD, :] = ext_ref[L:L + CONV_PAD, :]
    xa = conv * jax.nn.sigmoid(conv)
    xs = xa[:, :SSM_D_INNER]
    bmat = xa[:, SSM_D_INNER:SSM_D_INNER + SSM_GROUPS * ns]
    cmat = xa[:, SSM_D_INNER + SSM_GROUPS * ns:]

    x = dt_ref[...] + dtb_ref[...]
    dt = jnp.maximum(x, 0.0) + jnp.log1p(jnp.exp(-jnp.abs(x)))
    lane = lax.broadcasted_iota(jnp.int32, (L, LANES), 1)
    dta = jnp.where(lane < SSM_HEADS, dt * -jnp.exp(alog_ref[...]), 0.0)
    cums = jnp.dot(tri_ref[...], dta, preferred_element_type=F32, precision=lax.Precision.HIGHEST)
    cums_t = cums.T
    last = cums[L - 1:L, :]
    decay_to_end = jnp.exp(last - cums)
    decay_from_start = jnp.exp(cums)
    chunk_decay = jnp.exp(last)
    causal = lax.broadcasted_iota(jnp.int32, (L, L), 0) >= lax.broadcasted_iota(jnp.int32, (L, L), 1)
    heads_per_group = SSM_HEADS // SSM_GROUPS
    ys = []
    for g in range(SSM_GROUPS):
        bg = bmat[:, g * ns:(g + 1) * ns].astype(BF16)
        cg = cmat[:, g * ns:(g + 1) * ns].astype(BF16)
        cb = lax.dot_general(cg, bg, _NT, preferred_element_type=F32)
        for e in range(heads_per_group):
            h = g * heads_per_group + e
            xh = xs[:, h * hp:(h + 1) * hp]
            xdt = xh * dt[:, h:h + 1]
            seg = cums[:, h:h + 1] - cums_t[h:h + 1, :]
            lmat = jnp.where(causal, jnp.exp(seg), 0.0)
            y_diag = jnp.dot((cb * lmat).astype(BF16), xdt.astype(BF16), preferred_element_type=F32)
            state = state_ref[h]
            y_off = lax.dot_general(cg, state.astype(BF16), _NT, preferred_element_type=F32) * decay_from_start[:, h:h + 1]
            xw = (xdt * decay_to_end[:, h:h + 1]).astype(BF16)
            new = lax.dot_general(xw, bg, (((0,), (0,)), ((), ())), preferred_element_type=F32)
            state_ref[h] = chunk_decay[:, h:h + 1] * state + new
            ys.append(y_diag + y_off + dskip_ref[:, h * hp:(h + 1) * hp] * xh)
    y = jnp.concatenate(ys, axis=-1)
    z = z_ref[...]
    o_ref[...] = _rms(y * (z * jax.nn.sigmoid(z)), nw_ref[...]).astype(o_ref.dtype)


N_RET_IN = 10
N_SSD_IN = 10


def _even_core_kernel(*refs):
    ret_in, ssd_in = refs[:N_RET_IN], refs[N_RET_IN:N_RET_IN + N_SSD_IN]
    o_ret_ref, o_ssd_ref, ret_state_ref, ext_ref, ssd_state_ref = refs[N_RET_IN + N_SSD_IN:]
    _retention_kernel(*ret_in, o_ret_ref, ret_state_ref)
    _ssd_kernel(*ssd_in, o_ssd_ref, ext_ref, ssd_state_ref)


def _even_core(q, k, v, g, xbc, dt_raw, z, conv_w, conv_b, dt_bias, a_log, d_skip, norm_w):
    bsz, seq, width = q.shape
    L = RET_CHUNK
    assert SSM_CHUNK == L
    tables = _retention_tables(seq)
    pad = LANES - SSM_HEADS
    params = [conv_w, conv_b.reshape(1, -1), jnp.pad(dt_bias, (0, pad)).reshape(1, LANES),
              jnp.pad(a_log, (0, pad)).reshape(1, LANES), jnp.repeat(d_skip, SSM_HEAD_DIM).reshape(1, -1),
              norm_w.reshape(1, -1), jnp.asarray(np.tril(np.ones((L, L), np.float32)))]

    def tok(w):
        return pl.BlockSpec((None, L, w), lambda b, c: (b, c, 0))

    rope = pl.BlockSpec((L, RET_DK), lambda b, c: (c, 0))
    const = pl.BlockSpec((RET_HEADS, L, RET_DK), lambda b, c: (0, 0, 0))
    return pl.pallas_call(
        _even_core_kernel,
        grid=(bsz, seq // L),
        in_specs=[tok(width)] * 4 + [rope, rope, const, const, const, const]
        + [tok(SSM_CONV_DIM), tok(LANES), tok(SSM_D_INNER)] + [pl.BlockSpec(p.shape, lambda b, c: (0, 0)) for p in params],
        out_specs=[tok(width), tok(SSM_D_INNER)],
        out_shape=[jax.ShapeDtypeStruct((bsz, seq, width), BF16), jax.ShapeDtypeStruct((bsz, seq, SSM_D_INNER), BF16)],
        scratch_shapes=[pltpu.VMEM((RET_HEADS, RET_DK, RET_DV), F32),
                        pltpu.VMEM((L + CONV_PAD, SSM_CONV_DIM), F32), pltpu.VMEM((SSM_HEADS, SSM_HEAD_DIM, SSM_D_STATE), F32)],
        compiler_params=_compiler_params(("parallel", "arbitrary")),
        name="even_core",
    )(q, k, v, g, *tables, xbc, dt_raw, z, *params)


def _even_mixer(xt, nw, w_in, w_out, conv_w, conv_b, dt_bias, a_log, d_skip, ssm_norm_w, bsz):
    t, d = xt.shape
    seq = t // bsz
    cuts = [int(c) for c in np.cumsum((0,) + EVEN_SPLITS)]
    ws = [w_in[:, a:b] for a, b in zip(cuts[:-1], cuts[1:])]
    perm = np.concatenate([h * RET_DK + np.concatenate([np.arange(0, RET_DK, 2), np.arange(1, RET_DK, 2)]) for h in range(RET_HEADS)])
    ws[0] = ws[0][:, perm]
    ws[1] = ws[1][:, perm]
    ws[6] = jnp.pad(ws[6], ((0, 0), (0, LANES - SSM_HEADS)))
    dtypes = (F32, F32, BF16, F32, F32, F32, F32)
    q, k, v, g, z, xbc, dt_raw = [a.reshape(bsz, seq, -1) for a in _norm_proj(xt, nw, ws, dtypes=dtypes)]
    o_ret, y = _even_core(q, k, v, g, xbc, dt_raw, z, conv_w, conv_b, dt_bias, a_log, d_skip, ssm_norm_w)
    n_ret = RET_HEADS * RET_DV
    return [[o_ret.reshape(t, n_ret)], [y.reshape(t, SSM_D_INNER)]], [w_out[:n_ret], w_out[n_ret:]]


Q_TILE = ATTN_BLOCK
GROUP_HEADS = NSA_HEADS // NSA_KV_HEADS
HEAD_DIM = NSA_HEAD_DIM
SEL_CHUNK = 512


def _cmp_kv_kernel(r_ref, pe_ref, w1_ref, w2_ref, o_ref):
    r = r_ref[...]
    half = NSA_CMP_STRIDE * HEAD_DIM
    top = jnp.dot((r + pe_ref[0:1, :]).astype(BF16), w1_ref[0:half, :], preferred_element_type=F32)
    bot = jnp.dot((r + pe_ref[1:2, :]).astype(BF16), w1_ref[half:2 * half, :], preferred_element_type=F32)
    hidden = top + pltpu.roll(bot, r.shape[0] - 1, 0)
    act = hidden * jax.nn.sigmoid(hidden)
    o_ref[...] = jnp.dot(act.astype(BF16), w2_ref[...], preferred_element_type=F32)


def _cmp_kv(r, pe, w1, w2):
    ng, bsz, n, width = r.shape
    return pl.pallas_call(
        _cmp_kv_kernel,
        grid=(ng, bsz),
        in_specs=[pl.BlockSpec((None, None, n, width), lambda g, b: (g, b, 0, 0)),
                  pl.BlockSpec(pe.shape, lambda g, b: (0, 0)),
                  pl.BlockSpec(w1.shape, lambda g, b: (0, 0)),
                  pl.BlockSpec(w2.shape, lambda g, b: (0, 0))],
        out_specs=pl.BlockSpec((None, None, n, HEAD_DIM), lambda g, b: (g, b, 0, 0)),
        out_shape=jax.ShapeDtypeStruct((ng, bsz, n, HEAD_DIM), F32),
        compiler_params=_compiler_params(("parallel", "parallel")),
        name="nsa_compress",
    )(r, pe, w1, w2)


def _heads_from_t(o_t):
    pairs = []
    for e in range(0, GROUP_HEADS, 2):
        pair = jnp.concatenate([o_t[:, e * Q_TILE:(e + 1) * Q_TILE], o_t[:, (e + 1) * Q_TILE:(e + 2) * Q_TILE]], axis=0)
        pairs.append(pair.T)
    return jnp.concatenate(pairs, axis=-1)


def _gate_rows(gate_ref, g):
    gate = jax.nn.sigmoid(gate_ref[g])
    gate = jnp.concatenate([gate, jnp.zeros((Q_TILE, LANES - gate.shape[1]), F32)], axis=1)
    return gate.T


def _gate_cols(gate_t, branch):
    return jnp.concatenate([gate_t[3 * e + branch:3 * e + branch + 1, :] for e in range(GROUP_HEADS)], axis=1)


def _tile_heads(x):
    return jnp.concatenate([x] * GROUP_HEADS, axis=1)


def _pv_t(v, p):
    return lax.dot_general(v, p.astype(BF16), (((0,), (0,)), ((), ())), preferred_element_type=F32)


def _band_window(window):
    return (-(-(window - 1) // Q_TILE) + 1) * Q_TILE


def _band_branch(i, q, k_ref, v_ref, g, window, sink=None):
    n_keys = _band_window(window)
    start = pl.multiple_of(jnp.maximum(i * Q_TILE + Q_TILE - n_keys, 0), Q_TILE)
    rel = (i * Q_TILE + lax.broadcasted_iota(jnp.int32, (n_keys, Q_TILE), 1)) - (start + lax.broadcasted_iota(jnp.int32, (n_keys, Q_TILE), 0))
    bias = _tile_heads(jnp.where(rel >= 0, jnp.where(rel < window, 0.0, MASKED), MASKED))
    s = lax.dot_general(k_ref[g, pl.ds(start, n_keys), :], q, _NT, preferred_element_type=F32) + bias
    m = jnp.max(s, axis=0, keepdims=True)
    if sink is not None:
        m = jnp.maximum(m, sink)
    p = jnp.exp(s - m)
    denom = jnp.sum(p, axis=0, keepdims=True)
    denom = denom + jnp.exp(sink - m) if sink is not None else jnp.maximum(denom, 1e-30)
    return _pv_t(v_ref[g, pl.ds(start, n_keys), :], p) * (1.0 / denom)


def _nsa_dense_kernel(q_ref, kc_ref, vc_ref, kw_ref, vw_ref, gate_ref, ov_ref, qs_ref, ks_ref, vs_ref, sink_ref,
                      o_ref, sel_ref, swa_ref):
    i = pl.program_id(1)
    rows = GROUP_HEADS * Q_TILE
    width = GROUP_HEADS * HEAD_DIM
    n_cmp = kc_ref.shape[1]
    n = lax.broadcasted_iota(jnp.int32, (n_cmp, Q_TILE), 0)
    t = i * Q_TILE + lax.broadcasted_iota(jnp.int32, (n_cmp, Q_TILE), 1)
    valid = _tile_heads(jnp.where(n * NSA_CMP_STRIDE + (NSA_CMP_BLOCK - 1) <= t, 1.0, 0.0))
    bias = (valid - 1.0) * (-MASKED)
    n_sel = ov_ref.shape[0]
    jb = lax.broadcasted_iota(jnp.int32, (n_sel, Q_TILE), 0)
    cur = (i * Q_TILE + lax.broadcasted_iota(jnp.int32, (n_sel, Q_TILE), 1)) // NSA_SEL_BLOCK
    valid_b = jb <= cur
    forced = jnp.where(valid_b, jnp.where(jb == 0, 1.0, 0.0) + jnp.where(jb == cur, 1.0, 0.0) + jnp.where(jb == cur - 1, 1.0, 0.0), 0.0)
    for g in range(NSA_KV_HEADS):
        q = q_ref[g * GROUP_HEADS:(g + 1) * GROUP_HEADS].reshape(rows, HEAD_DIM)
        s = lax.dot_general(kc_ref[g].astype(BF16), q, _NT, preferred_element_type=F32) + bias
        m = jnp.max(s, axis=0, keepdims=True)
        p = jnp.exp(s - m) * valid
        p = p * (1.0 / jnp.maximum(jnp.sum(p, axis=0, keepdims=True), 1e-30))
        o_cmp = _pv_t(vc_ref[g].astype(BF16), p)
        o_win = _band_branch(i, q, kw_ref, vw_ref, g, NSA_WINDOW)
        gate_t = _gate_rows(gate_ref, g)
        o_ref[:, g * width:(g + 1) * width] = _heads_from_t(o_cmp * _gate_cols(gate_t, 0) + o_win * _gate_cols(gate_t, 2)).astype(o_ref.dtype)

        p_sum = p[:, 0:Q_TILE]
        for e in range(1, GROUP_HEADS):
            p_sum = p_sum + p[:, e * Q_TILE:(e + 1) * Q_TILE]
        imp = jnp.dot(ov_ref[...], p_sum, preferred_element_type=F32, precision=lax.Precision.HIGHEST)
        score = jnp.where(forced > 0.5, NSA_FORCE_SCORE, jnp.where(valid_b, imp, -jnp.inf))
        sub = 8
        ranks = []
        for v in range(n_sel // sub):
            blk_scores = score[v * sub:(v + 1) * sub, :]
            jb_v = jb[v * sub:(v + 1) * sub, :]
            rank = jnp.zeros((sub, Q_TILE), F32)
            for ii in range(n_sel):
                row = score[ii:ii + 1, :]
                if ii < v * sub:
                    ahead = jnp.where(row >= blk_scores, 1.0, 0.0)
                elif ii >= (v + 1) * sub:
                    ahead = jnp.where(row > blk_scores, 1.0, 0.0)
                else:
                    tie = jnp.where(jb_v > ii, 1.0, 0.0)
                    ahead = jnp.where(row > blk_scores, 1.0, jnp.where(row == blk_scores, tie, 0.0))
                rank = rank + ahead
            ranks.append(rank)
        rank = jnp.concatenate(ranks, axis=0)
        sel_t = jnp.where(valid_b, jnp.where(rank < NSA_TOPN, 1.0, 0.0), 0.0)
        sel_t = jnp.concatenate([sel_t, jnp.zeros((LANES - n_sel, Q_TILE), F32)], axis=0)
        sel_ref[g] = sel_t.T.astype(BF16)

    for g in range(SWA_KV_HEADS):
        q = qs_ref[g * GROUP_HEADS:(g + 1) * GROUP_HEADS].reshape(rows, HEAD_DIM)
        sink = jnp.concatenate([jnp.broadcast_to(sink_ref[g][:, e:e + 1], (1, Q_TILE)) for e in range(GROUP_HEADS)], axis=1)
        o_swa = _band_branch(i, q, ks_ref, vs_ref, g, SWA_WINDOW, sink)
        swa_ref[:, g * width:(g + 1) * width] = _heads_from_t(o_swa).astype(swa_ref.dtype)


def _nsa_dense(q, k_cmp, v_cmp, kw, vw, gates, overlap_t, q_s, k_s, v_s, sinks):
    nh, bsz, seq, hd = q.shape
    ng, _, n_cmp, _ = k_cmp.shape
    heads = pl.BlockSpec((nh, None, Q_TILE, hd), lambda b, i: (0, b, i, 0))
    cmp_kv = pl.BlockSpec((ng, None, n_cmp, hd), lambda b, i: (0, b, 0, 0))
    seq_kv = pl.BlockSpec((ng, None, seq, hd), lambda b, i: (0, b, 0, 0))
    out = pl.BlockSpec((None, Q_TILE, nh * hd), lambda b, i: (b, i, 0))
    return pl.pallas_call(
        _nsa_dense_kernel,
        grid=(bsz, seq // Q_TILE),
        in_specs=[heads, cmp_kv, cmp_kv, seq_kv, seq_kv,
                  pl.BlockSpec((ng, None, Q_TILE, gates.shape[-1]), lambda b, i: (0, b, i, 0)),
                  pl.BlockSpec(overlap_t.shape, lambda b, i: (0, 0)),
                  heads, seq_kv, seq_kv,
                  pl.BlockSpec(sinks.shape, lambda b, i: (0, 0, 0))],
        out_specs=[out, pl.BlockSpec((ng, None, Q_TILE, LANES), lambda b, i: (0, b, i, 0)), out],
        out_shape=[jax.ShapeDtypeStruct((bsz, seq, nh * hd), BF16),
                   jax.ShapeDtypeStruct((ng, bsz, seq, LANES), BF16),
                   jax.ShapeDtypeStruct((bsz, seq, nh * hd), BF16)],
        compiler_params=_compiler_params(("parallel", "parallel")),
        name="nsa_dense",
    )(q, k_cmp, v_cmp, kw, vw, gates, overlap_t, q_s, k_s, v_s, sinks)


def _nsa_sel_kernel(q_ref, k_ref, v_ref, sel_ref, gate_ref, o_ref, s_ref, m_ref, l_ref, acc_ref):
    i = pl.program_id(1)
    rows = GROUP_HEADS * Q_TILE
    width = GROUP_HEADS * HEAD_DIM
    n_blk = LANES - HEAD_DIM
    kc = SEL_CHUNK
    seq = k_ref.shape[1]
    groups = range(NSA_KV_HEADS)

    def scores(g, start, lhs):
        return lax.dot_general(k_ref[g, pl.ds(start, kc), :], lhs, _NT, preferred_element_type=F32)

    start_d = pl.multiple_of(jnp.maximum(i * Q_TILE + Q_TILE - kc, 0), Q_TILE)
    n_past = (start_d + kc - 1) // kc
    kpos = start_d + lax.broadcasted_iota(jnp.int32, (kc, rows), 0)
    qpos = i * Q_TILE + (lax.broadcasted_iota(jnp.int32, (kc, rows), 1) & (Q_TILE - 1))
    blk = lax.broadcasted_iota(jnp.int32, (Q_TILE, n_blk), 1)
    lhs_past, lhs_none = [], []
    for g in groups:
        q = q_ref[g * GROUP_HEADS:(g + 1) * GROUP_HEADS].reshape(rows, HEAD_DIM)
        sel = sel_ref[g][:, :n_blk].astype(F32)
        neg_diag = (sel - 1.0) * (-MASKED)
        neg_past = jnp.where(blk < start_d // NSA_SEL_BLOCK, neg_diag, MASKED)
        lhs_diag = jnp.concatenate([q, jnp.concatenate([neg_diag.astype(BF16)] * GROUP_HEADS, axis=0)], axis=-1)
        lhs_past.append(jnp.concatenate([q, jnp.concatenate([neg_past.astype(BF16)] * GROUP_HEADS, axis=0)], axis=-1))
        lhs_none.append(jnp.concatenate([q, jnp.full((rows, n_blk), MASKED, BF16)], axis=-1))
        s_ref[g, 0] = jnp.where(kpos <= qpos, scores(g, start_d, lhs_diag), MASKED)

        m_ref[g] = jnp.full((1, rows), MASKED, F32)
        l_ref[g] = jnp.zeros((1, rows), F32)
        acc_ref[g] = jnp.zeros((HEAD_DIM, rows), F32)

    def step(c, cur, nxt, lookahead=True):
        next_start = pl.multiple_of(jnp.minimum(c * kc, seq - kc), kc)
        v_start = pl.multiple_of(jnp.where(c == 0, start_d, (c - 1) * kc), Q_TILE)
        for g in groups:
            if lookahead:
                s_ref[g, nxt] = scores(g, next_start, jnp.where(c < n_past, lhs_past[g], lhs_none[g]))
            s = s_ref[g, cur]
            m = m_ref[g]
            m_new = jnp.maximum(m, jnp.max(s, axis=0, keepdims=True))
            alpha = jnp.exp(m - m_new)
            p = jnp.exp(s - m_new)
            m_ref[g] = m_new
            l_ref[g] = alpha * l_ref[g] + jnp.sum(p, axis=0, keepdims=True)
            acc_ref[g] = alpha * acc_ref[g] + _pv_t(v_ref[g, pl.ds(v_start, kc), :], p)

    def body(j, carry):
        step(2 * j, 0, 1)
        step(2 * j + 1, 1, 0)
        return carry

    lax.fori_loop(0, (n_past + 1) // 2, body, 0)

    @pl.when(n_past % 2 == 0)
    def _():
        step(n_past, 0, 1, lookahead=False)

    for g in groups:
        gate = _gate_cols(_gate_rows(gate_ref, g), 1)
        o_ref[:, g * width:(g + 1) * width] = _heads_from_t(acc_ref[g] * (gate / l_ref[g])).astype(o_ref.dtype)


def _nsa_sel(q, k, v, sel, gates):
    nh, bsz, seq, hd = q.shape
    ng = k.shape[0]
    onehot = (np.arange(seq)[:, None] // NSA_SEL_BLOCK == np.arange(LANES - hd)[None, :]).astype(np.float32)
    k_ext = jnp.concatenate([k, jnp.broadcast_to(jnp.asarray(onehot, BF16), (ng, bsz, seq, LANES - hd))], axis=-1)
    return pl.pallas_call(
        _nsa_sel_kernel,
        grid=(bsz, seq // Q_TILE),
        in_specs=[pl.BlockSpec((nh, None, Q_TILE, hd), lambda b, i: (0, b, i, 0)),
                  pl.BlockSpec((ng, None, seq, LANES), lambda b, i: (0, b, 0, 0)),
                  pl.BlockSpec((ng, None, seq, hd), lambda b, i: (0, b, 0, 0)),
                  pl.BlockSpec((ng, None, Q_TILE, LANES), lambda b, i: (0, b, i, 0)),
                  pl.BlockSpec((ng, None, Q_TILE, gates.shape[-1]), lambda b, i: (0, b, i, 0))],
        out_specs=pl.BlockSpec((None, Q_TILE, nh * hd), lambda b, i: (b, i, 0)),
        out_shape=jax.ShapeDtypeStruct((bsz, seq, nh * hd), BF16),
        scratch_shapes=[pltpu.VMEM((ng, 2, SEL_CHUNK, GROUP_HEADS * Q_TILE), F32),
                        pltpu.VMEM((ng, 1, GROUP_HEADS * Q_TILE), F32), pltpu.VMEM((ng, 1, GROUP_HEADS * Q_TILE), F32),
                        pltpu.VMEM((ng, hd, GROUP_HEADS * Q_TILE), F32)],
        compiler_params=_compiler_params(("parallel", "parallel")),
        name="nsa_sel",
    )(q, k_ext, v, sel, gates)


def _overlap_t(seq):
    n_cmp = (seq - NSA_CMP_BLOCK) // NSA_CMP_STRIDE + 1
    n_pad = seq // NSA_CMP_STRIDE
    starts = np.arange(n_pad) * NSA_CMP_STRIDE
    sel_start = np.arange(seq // NSA_SEL_BLOCK) * NSA_SEL_BLOCK
    ov = (starts[None, :] < sel_start[:, None] + NSA_SEL_BLOCK) & (starts[None, :] + NSA_CMP_BLOCK > sel_start[:, None])
    ov = ov & (np.arange(n_pad)[None, :] < n_cmp)
    return jnp.asarray(ov.astype(np.float32))


def _odd_mixer(xt, nw, w_in, w_out, cmp_pe, cmp_w1, cmp_w2, sinks, bsz):
    t, d = xt.shape
    seq = t // bsz
    assert seq // NSA_SEL_BLOCK <= LANES and seq % SEL_CHUNK == 0
    cuts = [int(c) for c in np.cumsum((0,) + ODD_SPLITS)]
    ws = [w_in[:, a:b] for a, b in zip(cuts[:-1], cuts[1:])]
    scale = HEAD_DIM ** -0.5
    ws[0] = ws[0] * scale
    ws[8] = ws[8] * scale
    hd = HEAD_DIM
    head_dims = (hd, hd, hd, hd, hd, hd, hd, 3 * GROUP_HEADS, hd, hd, hd)
    dtypes = (BF16, F32, F32, BF16, BF16, BF16, BF16, F32, BF16, BF16, BF16)
    q_n, kc, vc, ks, vs, kw, vw, gates, q_s, k_s, v_s = _norm_proj(xt, nw, ws, head_dims, dtypes)

    def heads(a):
        return a.reshape(a.shape[0], bsz, seq, a.shape[-1])

    q_n, ks, vs, kw, vw, gates, q_s, k_s, v_s = map(heads, (q_n, ks, vs, kw, vw, gates, q_s, k_s, v_s))
    ng = NSA_KV_HEADS
    rows16 = NSA_CMP_STRIDE * hd
    pe = cmp_pe.reshape(2, 2, rows16)
    w1 = cmp_w1.astype(BF16)
    w2 = cmp_w2.astype(BF16)
    k_cmp = _cmp_kv(kc.reshape(ng, bsz, seq // NSA_CMP_STRIDE, rows16), pe[0], w1[0], w2[0])
    v_cmp = _cmp_kv(vc.reshape(ng, bsz, seq // NSA_CMP_STRIDE, rows16), pe[1], w1[1], w2[1])
    o_cw, sel, o_swa = _nsa_dense(q_n, k_cmp, v_cmp, kw, vw, gates, _overlap_t(seq), q_s, k_s, v_s,
                                  sinks.reshape(SWA_KV_HEADS, 1, GROUP_HEADS))
    o_sel = _nsa_sel(q_n, ks, vs, sel, gates)
    n_nsa = NSA_HEADS * hd
    return [[o_cw.reshape(t, n_nsa), o_sel.reshape(t, n_nsa)], [o_swa.reshape(t, SWA_HEADS * hd)]], [w_out[:n_nsa], w_out[n_nsa:]]


def kernel(x, mem, norm_w, final_norm_w, mem_norm_w, ffn_w_gate, ffn_w_up, ffn_w_down, xattn_wq, xattn_wkv, xattn_wo, even_w_in, even_w_out, ssm_conv_w, ssm_conv_b, ssm_dt_bias, ssm_a_log, ssm_d, ssm_norm_w, odd_w_in, odd_w_out, nsa_cmp_pe, nsa_cmp_w1, nsa_cmp_w2, swa_sinks):
    bsz, seq, d = x.shape
    t = bsz * seq
    m = mem.shape[1]
    wg = ffn_w_gate.astype(BF16)
    wu = ffn_w_up.astype(BF16)
    wd = ffn_w_down.astype(BF16)
    wq = xattn_wq.astype(BF16)
    wkv = xattn_wkv.astype(BF16)
    wo = xattn_wo.astype(BF16)
    even_in = even_w_in.astype(BF16)
    even_out = even_w_out.astype(BF16)
    odd_in = odd_w_in.astype(BF16)
    odd_out = odd_w_out.astype(BF16)

    xt = x.reshape(t, d)
    mem2 = mem.reshape(bsz * m, d)
    mem_nw = mem_norm_w.reshape(1, d)
    for layer in range(DEPTH):
        i = layer // 2
        xt = _ffn(xt, norm_w[layer, 0].reshape(1, d), wg, wu, wd, layer, 0)
        nw1 = norm_w[layer, 1].reshape(1, d)
        if layer % 2 == 0:
            mix, mix_w = _even_mixer(xt, nw1, even_in[i], even_out[i], ssm_conv_w[i], ssm_conv_b[i], ssm_dt_bias[i], ssm_a_log[i], ssm_d[i], ssm_norm_w[i], bsz)
        else:
            mix, mix_w = _odd_mixer(xt, nw1, odd_in[i], odd_out[i], nsa_cmp_pe[i], nsa_cmp_w1[i], nsa_cmp_w2[i], swa_sinks[i], bsz)
        (kv,) = _norm_proj(mem2, mem_nw, [wkv[layer]])
        xt = _mix_xattn(xt, mix, mix_w, norm_w[layer, 2].reshape(1, d), wq[layer], kv.reshape(bsz, m, 2 * d), wo[layer], bsz)
        out_nw = final_norm_w.reshape(1, d) if layer == DEPTH - 1 else None
        xt = _ffn(xt, norm_w[layer, 3].reshape(1, d), wg, wu, wd, layer, 1, out_nw)
    return xt.reshape(bsz, seq, d)
```

```python
import functools

import numpy as np
import jax
import jax.numpy as jnp
from jax import lax
from jax.experimental import pallas as pl
from jax.experimental.pallas import tpu as pltpu

F32 = jnp.float32
BF16 = jnp.bfloat16

D_MODEL = 1024
DEPTH = 4
D_FF = 2816
NORM_EPS = 1e-6
ATTN_BLOCK = 128

RET_HEADS = 4
RET_DK = 128
RET_DV = 128
RET_CHUNK = 128
RET_ROPE_BASE = 10000.0

SSM_HEADS = 8
SSM_HEAD_DIM = 64
SSM_D_INNER = SSM_HEADS * SSM_HEAD_DIM
SSM_D_STATE = 64
SSM_GROUPS = 2
SSM_CONV = 4
SSM_CHUNK = 128
SSM_CONV_DIM = SSM_D_INNER + 2 * SSM_GROUPS * SSM_D_STATE

EVEN_SPLITS = (RET_HEADS * RET_DK, RET_HEADS * RET_DK, RET_HEADS * RET_DV, RET_HEADS * RET_DV, SSM_D_INNER, SSM_CONV_DIM, SSM_HEADS)

NSA_HEADS = 8
NSA_KV_HEADS = 2
NSA_HEAD_DIM = 64
NSA_CMP_BLOCK = 32
NSA_CMP_STRIDE = 16
NSA_CMP_HIDDEN = 256
NSA_SEL_BLOCK = 64
NSA_TOPN = 16
NSA_WINDOW = 512
NSA_FORCE_SCORE = 1e6

SWA_HEADS = 8
SWA_KV_HEADS = 2
SWA_HEAD_DIM = 64
SWA_WINDOW = 128

ODD_SPLITS = (NSA_HEADS * NSA_HEAD_DIM,) + (NSA_KV_HEADS * NSA_HEAD_DIM,) * 6 + (3 * NSA_HEADS, SWA_HEADS * SWA_HEAD_DIM, SWA_KV_HEADS * SWA_HEAD_DIM, SWA_KV_HEADS * SWA_HEAD_DIM)

XATTN_HEADS = 4
XATTN_HEAD_DIM = D_MODEL // XATTN_HEADS

VMEM_LIMIT_BYTES = 52 * 1024 * 1024
TOKEN_TILE = 512
FFN_TILE = 256
LANES = 128
MASKED = -1e30
LOG2_E = float(np.log2(np.e))

_NT = (((1,), (1,)), ((), ()))


def _compiler_params(semantics):
    return pltpu.CompilerParams(dimension_semantics=semantics, vmem_limit_bytes=VMEM_LIMIT_BYTES)


def _rms(x, w):
    return x * lax.rsqrt(jnp.mean(x * x, axis=-1, keepdims=True) + NORM_EPS) * w


def _ffn_kernel(has_out_norm, x_ref, nw_ref, wg_ref, wu_ref, wd_ref, *refs):
    out_nw_ref = refs[0] if has_out_norm else None
    o_ref, a_ref = refs[-2:]
    x = x_ref[...]
    h = _rms(x, nw_ref[...]).astype(BF16)
    for c in range(D_FF // FFN_TILE):
        cols = slice(c * FFN_TILE, (c + 1) * FFN_TILE)
        g = jnp.dot(h, wg_ref[:, cols], preferred_element_type=F32)
        u = jnp.dot(h, wu_ref[:, cols], preferred_element_type=F32)
        a_ref[:, cols] = (g * jax.nn.sigmoid(g) * u).astype(BF16)
    y = x + 0.5 * jnp.dot(a_ref[...], wd_ref[...], preferred_element_type=F32)
    o_ref[...] = _rms(y, out_nw_ref[...]) if has_out_norm else y


def _ffn(x, nw, wg, wu, wd, layer, k, out_nw=None):
    t, d = x.shape
    tm = TOKEN_TILE
    resident = pl.Buffered(1)
    vec = pl.BlockSpec((1, d), lambda i: (0, 0))
    return pl.pallas_call(
        functools.partial(_ffn_kernel, out_nw is not None),
        grid=(t // tm,),
        in_specs=[
            pl.BlockSpec((tm, d), lambda i: (i, 0)),
            vec,
            pl.BlockSpec((None, None, d, D_FF), lambda i: (layer, k, 0, 0), pipeline_mode=resident),
            pl.BlockSpec((None, None, d, D_FF), lambda i: (layer, k, 0, 0), pipeline_mode=resident),
            pl.BlockSpec((None, None, D_FF, d), lambda i: (layer, k, 0, 0), pipeline_mode=resident),
        ] + ([vec] if out_nw is not None else []),
        out_specs=pl.BlockSpec((tm, d), lambda i: (i, 0)),
        out_shape=jax.ShapeDtypeStruct((t, d), F32),
        scratch_shapes=[pltpu.VMEM((tm, D_FF), BF16)],
        compiler_params=_compiler_params(("parallel",)),
        name="ffn",
    )(x, nw, wg, wu, wd, *([out_nw] if out_nw is not None else []))


def _norm_proj_kernel(head_dims, offsets, x_ref, nw_ref, w_ref, *o_refs):
    h = _rms(x_ref[...], nw_ref[...]).astype(BF16)
    y_all = jnp.dot(h, w_ref[...], preferred_element_type=F32)
    for hd, off, o_ref in zip(head_dims, offsets, o_refs):
        if hd is None:
            o_ref[...] = y_all[:, off:off + o_ref.shape[-1]].astype(o_ref.dtype)
        else:
            for j in range(o_ref.shape[0]):
                o_ref[j] = y_all[:, off + j * hd:off + (j + 1) * hd].astype(o_ref.dtype)


def _norm_proj(x, nw, weights, head_dims=None, dtypes=None, tm=TOKEN_TILE):
    t, d = x.shape
    n_out = len(weights)
    head_dims = tuple(head_dims) if head_dims is not None else (None,) * n_out
    dtypes = tuple(dtypes) if dtypes is not None else (F32,) * n_out
    out_specs, out_shape, offsets, padded = [], [], [], []
    total = 0
    for w, hd, dt in zip(weights, head_dims, dtypes):
        n = w.shape[1]
        if hd is None:
            out_specs.append(pl.BlockSpec((tm, n), lambda i: (i, 0)))
            out_shape.append(jax.ShapeDtypeStruct((t, n), dt))
        else:
            out_specs.append(pl.BlockSpec((n // hd, tm, hd), lambda i: (0, i, 0)))
            out_shape.append(jax.ShapeDtypeStruct((n // hd, t, hd), dt))
        offsets.append(total)
        width = -(-n // LANES) * LANES
        padded.append(jnp.pad(w, ((0, 0), (0, width - n))))
        total += width
    w_all = jnp.concatenate(padded, axis=1) if n_out > 1 else padded[0]
    return pl.pallas_call(
        functools.partial(_norm_proj_kernel, head_dims, tuple(offsets)),
        grid=(t // tm,),
        in_specs=[pl.BlockSpec((tm, d), lambda i: (i, 0)), pl.BlockSpec((1, d), lambda i: (0, 0)),
                  pl.BlockSpec(w_all.shape, lambda i: (0, 0))],
        out_specs=out_specs,
        out_shape=out_shape,
        compiler_params=_compiler_params(("parallel",)),
        name="norm_proj",
    )(x, nw, w_all)


def _mix_xattn_kernel(group_sizes, x_ref, *refs):
    n_a = sum(group_sizes)
    n_g = len(group_sizes)
    a_refs, w_refs = refs[:n_a], refs[n_a:n_a + n_g]
    nw_ref, wq_ref, kv_ref, wo_ref, o_ref = refs[n_a + n_g:]
    x = x_ref[...]
    pos = 0
    for size, w_ref in zip(group_sizes, w_refs):
        a = a_refs[pos][...]
        for r in a_refs[pos + 1:pos + size]:
            a = a.astype(F32) + r[...].astype(F32)
        pos += size
        x = x + jnp.dot(a.astype(BF16), w_ref[...], preferred_element_type=F32)
    h = _rms(x, nw_ref[...]).astype(BF16)
    q = jnp.dot(h, wq_ref[...], preferred_element_type=F32).astype(BF16)
    hd = XATTN_HEAD_DIM
    outs = []
    for hh in range(XATTN_HEADS):
        k = kv_ref[:, hh * hd:(hh + 1) * hd].astype(BF16)
        v = kv_ref[:, D_MODEL + hh * hd:D_MODEL + (hh + 1) * hd].astype(BF16)
        s = lax.dot_general(q[:, hh * hd:(hh + 1) * hd], k, _NT, preferred_element_type=F32)
        s = s * (hd ** -0.5)
        p = jnp.exp(s - jnp.max(s, axis=-1, keepdims=True))
        p = p / jnp.sum(p, axis=-1, keepdims=True)
        outs.append(jnp.dot(p.astype(BF16), v, preferred_element_type=F32).astype(BF16))
    o = jnp.concatenate(outs, axis=-1)
    o_ref[...] = x + jnp.dot(o, wo_ref[...], preferred_element_type=F32)


def _mix_xattn(x, groups, weights, nw, wq, kv, wo, bsz, tm=TOKEN_TILE):
    t, d = x.shape
    seq = t // bsz
    nt = seq // tm
    m = kv.shape[1]
    arrays = [a for grp in groups for a in grp]

    def tok(width):
        return pl.BlockSpec((tm, width), lambda b, i: (b * nt + i, 0))

    def const(shape):
        return pl.BlockSpec(shape, lambda b, i: (0, 0))

    return pl.pallas_call(
        functools.partial(_mix_xattn_kernel, tuple(len(grp) for grp in groups)),
        grid=(bsz, nt),
        in_specs=[tok(d)] + [tok(a.shape[1]) for a in arrays] + [const(w.shape) for w in weights]
        + [const((1, d)), const((d, d)), pl.BlockSpec((None, m, 2 * d), lambda b, i: (b, 0, 0)), const((d, d))],
        out_specs=tok(d),
        out_shape=jax.ShapeDtypeStruct((t, d), F32),
        compiler_params=_compiler_params(("parallel", "parallel")),
        name="mix_xattn",
    )(x, *arrays, *weights, nw, wq, kv, wo)


def _retention_kernel(q_ref, k_ref, v_ref, g_ref, cos_ref, sin_ref, dmat_ref, zeta_ref, xi_ref, decay_ref, o_ref, state_ref):
    @pl.when(pl.program_id(1) == 0)
    def _():
        state_ref[...] = jnp.zeros_like(state_ref)

    cos = cos_ref[...]
    sin = sin_ref[...]
    half = RET_DK // 2
    outs = []
    for h in range(RET_HEADS):
        qh = q_ref[:, h * RET_DK:(h + 1) * RET_DK]
        kh = k_ref[:, h * RET_DK:(h + 1) * RET_DK]
        qr = qh * cos + pltpu.roll(qh, half, 1) * sin
        kr = (kh * cos + pltpu.roll(kh, half, 1) * sin) * (RET_DK ** -0.5)
        vh = v_ref[:, h * RET_DV:(h + 1) * RET_DV].astype(BF16)
        scores = lax.dot_general(qr.astype(BF16), kr.astype(BF16), _NT, preferred_element_type=F32) * dmat_ref[h]
        inner = jnp.dot(scores.astype(BF16), vh, preferred_element_type=F32)
        state = state_ref[h]
        cross = jnp.dot((qr * xi_ref[h]).astype(BF16), state.astype(BF16), preferred_element_type=F32)
        kz = (kr * zeta_ref[h]).T.astype(BF16)
        state_ref[h] = decay_ref[h] * state + jnp.dot(kz, vh, preferred_element_type=F32)
        o = inner + cross
        o = o * lax.rsqrt(jnp.mean(o * o, axis=-1, keepdims=True) + NORM_EPS)
        gh = g_ref[:, h * RET_DV:(h + 1) * RET_DV]
        outs.append(gh * jax.nn.sigmoid(gh) * o)
    o_ref[...] = jnp.concatenate(outs, axis=-1).astype(o_ref.dtype)


def _retention_tables(seq):
    L = RET_CHUNK
    pos = jnp.arange(seq, dtype=F32)
    inv_freq = 1.0 / (RET_ROPE_BASE ** jnp.linspace(0.0, 1.0, RET_DK // 2, dtype=F32))
    ang = pos[:, None] * inv_freq[None, :]
    cos, sin = jnp.cos(ang), jnp.sin(ang)
    cos2 = jnp.concatenate([cos, cos], axis=-1)
    sin2 = jnp.concatenate([-sin, sin], axis=-1)
    log_g = jnp.log1p(-jnp.exp2(-5.0 - jnp.arange(RET_HEADS, dtype=F32)))
    idx = jnp.arange(L, dtype=F32)
    diff = idx[:, None] - idx[None, :]
    dmat = jnp.where(diff >= 0, jnp.exp(jnp.maximum(diff, 0.0)[None] * log_g[:, None, None]), 0.0)
    ones = jnp.ones((RET_HEADS, L, RET_DK), F32)
    zeta = jnp.exp((L - 1 - idx)[None, :] * log_g[:, None])[:, :, None] * ones
    xi = jnp.exp((idx + 1)[None, :] * log_g[:, None])[:, :, None] * ones
    decay = jnp.exp(L * log_g)[:, None, None] * ones
    return cos2, sin2, dmat, zeta, xi, decay


CONV_PAD = 8


def _ssd_kernel(xbc_ref, dt_ref, z_ref, convw_ref, convb_ref, dtb_ref, alog_ref, dskip_ref, nw_ref, tri_ref, o_ref, ext_ref, state_ref):
    L = SSM_CHUNK
    hp = SSM_HEAD_DIM
    ns = SSM_D_STATE

    @pl.when(pl.program_id(1) == 0)
    def _():
        ext_ref[0:CONV_PAD, :] = jnp.zeros((CONV_PAD, SSM_CONV_DIM), F32)
        state_ref[...] = jnp.zeros_like(state_ref)

    ext_ref[CONV_PAD:CONV_PAD + L, :] = xbc_ref[...]
    conv = convb_ref[...]
    for w in range(SSM_CONV):
        off = CONV_PAD - (SSM_CONV - 1) + w
        conv = conv + ext_ref[off:off + L, :] * convw_ref[w:w + 1, :]
    ext_ref[0:CONV_PAD, :] = ext_ref[L:L + CONV_PAD, :]
    xa = conv * jax.nn.sigmoid(conv)
    xs = xa[:, :SSM_D_INNER]
    bmat = xa[:, SSM_D_INNER:SSM_D_INNER + SSM_GROUPS * ns]
    cmat = xa[:, SSM_D_INNER + SSM_GROUPS * ns:]

    x = dt_ref[...] + dtb_ref[...]
    dt = jnp.maximum(x, 0.0) + jnp.log1p(jnp.exp(-jnp.abs(x)))
    lane = lax.broadcasted_iota(jnp.int32, (L, LANES), 1)
    dta = jnp.where(lane < SSM_HEADS, dt * -jnp.exp(alog_ref[...]), 0.0)
    cums = jnp.dot(tri_ref[...], dta, preferred_element_type=F32, precision=lax.Precision.HIGHEST)
    cums_t = cums.T
    last = cums[L - 1:L, :]
    decay_to_end = jnp.exp(last - cums)
    decay_from_start = jnp.exp(cums)
    chunk_decay = jnp.exp(last)
    causal = lax.broadcasted_iota(jnp.int32, (L, L), 0) >= lax.broadcasted_iota(jnp.int32, (L, L), 1)
    heads_per_group = SSM_HEADS // SSM_GROUPS
    ys = []
    for g in range(SSM_GROUPS):
        bg = bmat[:, g * ns:(g + 1) * ns].astype(BF16)
        cg = cmat[:, g * ns:(g + 1) * ns].astype(BF16)
        cb = lax.dot_general(cg, bg, _NT, preferred_element_type=F32)
        for e in range(heads_per_group):
            h = g * heads_per_group + e
            xh = xs[:, h * hp:(h + 1) * hp]
            xdt = xh * dt[:, h:h + 1]
            seg = cums[:, h:h + 1] - cums_t[h:h + 1, :]
            lmat = jnp.where(causal, jnp.exp(seg), 0.0)
            y_diag = jnp.dot((cb * lmat).astype(BF16), xdt.astype(BF16), preferred_element_type=F32)
            state = state_ref[h]
            y_off = lax.dot_general(cg, state.astype(BF16), _NT, preferred_element_type=F32) * decay_from_start[:, h:h + 1]
            xw = (xdt * decay_to_end[:, h:h + 1]).astype(BF16)
            new = lax.dot_general(xw, bg, (((0,), (0,)), ((), ())), preferred_element_type=F32)
            state_ref[h] = chunk_decay[:, h:h + 1] * state + new
            ys.append(y_diag + y_off + dskip_ref[:, h * hp:(h + 1) * hp] * xh)
    y = jnp.concatenate(ys, axis=-1)
    z = z_ref[...]
    o_ref[...] = _rms(y * (z * jax.nn.sigmoid(z)), nw_ref[...]).astype(o_ref.dtype)


N_RET_IN = 10
N_SSD_IN = 10


def _even_core_kernel(*refs):
    ret_in, ssd_in = refs[:N_RET_IN], refs[N_RET_IN:N_RET_IN + N_SSD_IN]
    o_ret_ref, o_ssd_ref, ret_state_ref, ext_ref, ssd_state_ref = refs[N_RET_IN + N_SSD_IN:]
    _retention_kernel(*ret_in, o_ret_ref, ret_state_ref)
    _ssd_kernel(*ssd_in, o_ssd_ref, ext_ref, ssd_state_ref)


def _even_core(q, k, v, g, xbc, dt_raw, z, conv_w, conv_b, dt_bias, a_log, d_skip, norm_w):
    bsz, seq, width = q.shape
    L = RET_CHUNK
    assert SSM_CHUNK == L
    tables = _retention_tables(seq)
    pad = LANES - SSM_HEADS
    params = [conv_w, conv_b.reshape(1, -1), jnp.pad(dt_bias, (0, pad)).reshape(1, LANES),
              jnp.pad(a_log, (0, pad)).reshape(1, LANES), jnp.repeat(d_skip, SSM_HEAD_DIM).reshape(1, -1),
              norm_w.reshape(1, -1), jnp.asarray(np.tril(np.ones((L, L), np.float32)))]

    def tok(w):
        return pl.BlockSpec((None, L, w), lambda b, c: (b, c, 0))

    rope = pl.BlockSpec((L, RET_DK), lambda b, c: (c, 0))
    const = pl.BlockSpec((RET_HEADS, L, RET_DK), lambda b, c: (0, 0, 0))
    return pl.pallas_call(
        _even_core_kernel,
        grid=(bsz, seq // L),
        in_specs=[tok(width)] * 4 + [rope, rope, const, const, const, const]
        + [tok(SSM_CONV_DIM), tok(LANES), tok(SSM_D_INNER)] + [pl.BlockSpec(p.shape, lambda b, c: (0, 0)) for p in params],
        out_specs=[tok(width), tok(SSM_D_INNER)],
        out_shape=[jax.ShapeDtypeStruct((bsz, seq, width), BF16), jax.ShapeDtypeStruct((bsz, seq, SSM_D_INNER), BF16)],
        scratch_shapes=[pltpu.VMEM((RET_HEADS, RET_DK, RET_DV), F32),
                        pltpu.VMEM((L + CONV_PAD, SSM_CONV_DIM), F32), pltpu.VMEM((SSM_HEADS, SSM_HEAD_DIM, SSM_D_STATE), F32)],
        compiler_params=_compiler_params(("parallel", "arbitrary")),
        name="even_core",
    )(q, k, v, g, *tables, xbc, dt_raw, z, *params)


def _even_mixer(xt, nw, w_in, w_out, conv_w, conv_b, dt_bias, a_log, d_skip, ssm_norm_w, bsz):
    t, d = xt.shape
    seq = t // bsz
    cuts = [int(c) for c in np.cumsum((0,) + EVEN_SPLITS)]
    ws = [w_in[:, a:b] for a, b in zip(cuts[:-1], cuts[1:])]
    perm = np.concatenate([h * RET_DK + np.concatenate([np.arange(0, RET_DK, 2), np.arange(1, RET_DK, 2)]) for h in range(RET_HEADS)])
    ws[0] = ws[0][:, perm]
    ws[1] = ws[1][:, perm]
    ws[6] = jnp.pad(ws[6], ((0, 0), (0, LANES - SSM_HEADS)))
    dtypes = (F32, F32, BF16, F32, F32, F32, F32)
    q, k, v, g, z, xbc, dt_raw = [a.reshape(bsz, seq, -1) for a in _norm_proj(xt, nw, ws, dtypes=dtypes)]
    o_ret, y = _even_core(q, k, v, g, xbc, dt_raw, z, conv_w, conv_b, dt_bias, a_log, d_skip, ssm_norm_w)
    n_ret = RET_HEADS * RET_DV
    return [[o_ret.reshape(t, n_ret)], [y.reshape(t, SSM_D_INNER)]], [w_out[:n_ret], w_out[n_ret:]]


Q_TILE = ATTN_BLOCK
GROUP_HEADS = NSA_HEADS // NSA_KV_HEADS
HEAD_DIM = NSA_HEAD_DIM
SEL_CHUNK = 512


def _cmp_kv_kernel(r_ref, pe_ref, w1_ref, w2_ref, o_ref):
    r = r_ref[...]
    half = NSA_CMP_STRIDE * HEAD_DIM
    top = jnp.dot((r + pe_ref[0:1, :]).astype(BF16), w1_ref[0:half, :], preferred_element_type=F32)
    bot = jnp.dot((r + pe_ref[1:2, :]).astype(BF16), w1_ref[half:2 * half, :], preferred_element_type=F32)
    hidden = top + pltpu.roll(bot, r.shape[0] - 1, 0)
    act = hidden * jax.nn.sigmoid(hidden)
    o_ref[...] = jnp.dot(act.astype(BF16), w2_ref[...], preferred_element_type=F32)


def _cmp_kv(r, pe, w1, w2):
    ng, bsz, n, width = r.shape
    return pl.pallas_call(
        _cmp_kv_kernel,
        grid=(ng, bsz),
        in_specs=[pl.BlockSpec((None, None, n, width), lambda g, b: (g, b, 0, 0)),
                  pl.BlockSpec(pe.shape, lambda g, b: (0, 0)),
                  pl.BlockSpec(w1.shape, lambda g, b: (0, 0)),
                  pl.BlockSpec(w2.shape, lambda g, b: (0, 0))],
        out_specs=pl.BlockSpec((None, None, n, HEAD_DIM), lambda g, b: (g, b, 0, 0)),
        out_shape=jax.ShapeDtypeStruct((ng, bsz, n, HEAD_DIM), F32),
        compiler_params=_compiler_params(("parallel", "parallel")),
        name="nsa_compress",
    )(r, pe, w1, w2)


def _heads_from_t(o_t):
    pairs = []
    for e in range(0, GROUP_HEADS, 2):
        pair = jnp.concatenate([o_t[:, e * Q_TILE:(e + 1) * Q_TILE], o_t[:, (e + 1) * Q_TILE:(e + 2) * Q_TILE]], axis=0)
        pairs.append(pair.T)
    return jnp.concatenate(pairs, axis=-1)


def _gate_rows(gate_ref, g):
    gate = jax.nn.sigmoid(gate_ref[g])
    gate = jnp.concatenate([gate, jnp.zeros((Q_TILE, LANES - gate.shape[1]), F32)], axis=1)
    return gate.T


def _gate_cols(gate_t, branch):
    return jnp.concatenate([gate_t[3 * e + branch:3 * e + branch + 1, :] for e in range(GROUP_HEADS)], axis=1)


def _tile_heads(x):
    return jnp.concatenate([x] * GROUP_HEADS, axis=1)


def _pv_t(v, p):
    return lax.dot_general(v, p.astype(BF16), (((0,), (0,)), ((), ())), preferred_element_type=F32)


def _band_window(window):
    return (-(-(window - 1) // Q_TILE) + 1) * Q_TILE


def _band_branch(i, q, k_ref, v_ref, g, window, sink=None):
    n_keys = _band_window(window)
    start = pl.multiple_of(jnp.maximum(i * Q_TILE + Q_TILE - n_keys, 0), Q_TILE)
    rel = (i * Q_TILE + lax.broadcasted_iota(jnp.int32, (n_keys, Q_TILE), 1)) - (start + lax.broadcasted_iota(jnp.int32, (n_keys, Q_TILE), 0))
    bias = _tile_heads(jnp.where(rel >= 0, jnp.where(rel < window, 0.0, MASKED), MASKED))
    s = lax.dot_general(k_ref[g, pl.ds(start, n_keys), :], q, _NT, preferred_element_type=F32) + bias
    m = jnp.max(s, axis=0, keepdims=True)
    if sink is not None:
        m = jnp.maximum(m, sink)
    p = jnp.exp2(s - m)
    denom = jnp.sum(p, axis=0, keepdims=True)
    denom = denom + jnp.exp2(sink - m) if sink is not None else jnp.maximum(denom, 1e-30)
    return _pv_t(v_ref[g, pl.ds(start, n_keys), :], p) * (1.0 / denom)


def _nsa_dense_kernel(q_ref, kc_ref, vc_ref, kw_ref, vw_ref, gate_ref, ov_ref, qs_ref, ks_ref, vs_ref, sink_ref,
                      o_ref, sel_ref, swa_ref):
    i = pl.program_id(1)
    rows = GROUP_HEADS * Q_TILE
    width = GROUP_HEADS * HEAD_DIM
    n_cmp = kc_ref.shape[1]
    n = lax.broadcasted_iota(jnp.int32, (n_cmp, Q_TILE), 0)
    t = i * Q_TILE + lax.broadcasted_iota(jnp.int32, (n_cmp, Q_TILE), 1)
    valid = _tile_heads(jnp.where(n * NSA_CMP_STRIDE + (NSA_CMP_BLOCK - 1) <= t, 1.0, 0.0))
    bias = (valid - 1.0) * (-MASKED)
    n_sel = ov_ref.shape[0]
    jb = lax.broadcasted_iota(jnp.int32, (n_sel, Q_TILE), 0)
    cur = (i * Q_TILE + lax.broadcasted_iota(jnp.int32, (n_sel, Q_TILE), 1)) // NSA_SEL_BLOCK
    valid_b = jb <= cur
    forced = jnp.where(valid_b, jnp.where(jb == 0, 1.0, 0.0) + jnp.where(jb == cur, 1.0, 0.0) + jnp.where(jb == cur - 1, 1.0, 0.0), 0.0)
    for g in range(NSA_KV_HEADS):
        q = q_ref[g * GROUP_HEADS:(g + 1) * GROUP_HEADS].reshape(rows, HEAD_DIM)
        s = lax.dot_general(kc_ref[g].astype(BF16), q, _NT, preferred_element_type=F32) + bias
        m = jnp.max(s, axis=0, keepdims=True)
        p = jnp.exp2(s - m) * valid
        p = p * (1.0 / jnp.maximum(jnp.sum(p, axis=0, keepdims=True), 1e-30))
        o_cmp = _pv_t(vc_ref[g].astype(BF16), p)
        o_win = _band_branch(i, q, kw_ref, vw_ref, g, NSA_WINDOW)
        gate_t = _gate_rows(gate_ref, g)
        o_ref[:, g * width:(g + 1) * width] = _heads_from_t(o_cmp * _gate_cols(gate_t, 0) + o_win * _gate_cols(gate_t, 2)).astype(o_ref.dtype)

        p_sum = p[:, 0:Q_TILE]
        for e in range(1, GROUP_HEADS):
            p_sum = p_sum + p[:, e * Q_TILE:(e + 1) * Q_TILE]
        imp = jnp.dot(ov_ref[...], p_sum, preferred_element_type=F32, precision=lax.Precision.HIGHEST)
        score = jnp.where(forced > 0.5, NSA_FORCE_SCORE, jnp.where(valid_b, imp, -jnp.inf))
        sub = 8
        ranks = []
        for v in range(n_sel // sub):
            blk_scores = score[v * sub:(v + 1) * sub, :]
            jb_v = jb[v * sub:(v + 1) * sub, :]
            rank = jnp.zeros((sub, Q_TILE), F32)
            for ii in range(n_sel):
                row = score[ii:ii + 1, :]
                if ii < v * sub:
                    ahead = jnp.where(row >= blk_scores, 1.0, 0.0)
                elif ii >= (v + 1) * sub:
                    ahead = jnp.where(row > blk_scores, 1.0, 0.0)
                else:
                    tie = jnp.where(jb_v > ii, 1.0, 0.0)
                    ahead = jnp.where(row > blk_scores, 1.0, jnp.where(row == blk_scores, tie, 0.0))
                rank = rank + ahead
            ranks.append(rank)
        rank = jnp.concatenate(ranks, axis=0)
        sel_t = jnp.where(valid_b, jnp.where(rank < NSA_TOPN, 1.0, 0.0), 0.0)
        sel_t = jnp.concatenate([sel_t, jnp.zeros((LANES - n_sel, Q_TILE), F32)], axis=0)
        sel_ref[g] = sel_t.T.astype(BF16)

    for g in range(SWA_KV_HEADS):
        q = qs_ref[g * GROUP_HEADS:(g + 1) * GROUP_HEADS].reshape(rows, HEAD_DIM)
        sink = LOG2_E * jnp.concatenate([jnp.broadcast_to(sink_ref[g][:, e:e + 1], (1, Q_TILE)) for e in range(GROUP_HEADS)], axis=1)
        o_swa = _band_branch(i, q, ks_ref, vs_ref, g, SWA_WINDOW, sink)
        swa_ref[:, g * width:(g + 1) * width] = _heads_from_t(o_swa).astype(swa_ref.dtype)


def _nsa_dense(q, k_cmp, v_cmp, kw, vw, gates, overlap_t, q_s, k_s, v_s, sinks):
    nh, bsz, seq, hd = q.shape
    ng, _, n_cmp, _ = k_cmp.shape
    heads = pl.BlockSpec((nh, None, Q_TILE, hd), lambda b, i: (0, b, i, 0))
    cmp_kv = pl.BlockSpec((ng, None, n_cmp, hd), lambda b, i: (0, b, 0, 0))
    seq_kv = pl.BlockSpec((ng, None, seq, hd), lambda b, i: (0, b, 0, 0))
    out = pl.BlockSpec((None, Q_TILE, nh * hd), lambda b, i: (b, i, 0))
    return pl.pallas_call(
        _nsa_dense_kernel,
        grid=(bsz, seq // Q_TILE),
        in_specs=[heads, cmp_kv, cmp_kv, seq_kv, seq_kv,
                  pl.BlockSpec((ng, None, Q_TILE, gates.shape[-1]), lambda b, i: (0, b, i, 0)),
                  pl.BlockSpec(overlap_t.shape, lambda b, i: (0, 0)),
                  heads, seq_kv, seq_kv,
                  pl.BlockSpec(sinks.shape, lambda b, i: (0, 0, 0))],
        out_specs=[out, pl.BlockSpec((ng, None, Q_TILE, LANES), lambda b, i: (0, b, i, 0)), out],
        out_shape=[jax.ShapeDtypeStruct((bsz, seq, nh * hd), BF16),
                   jax.ShapeDtypeStruct((ng, bsz, seq, LANES), BF16),
                   jax.ShapeDtypeStruct((bsz, seq, nh * hd), BF16)],
        compiler_params=_compiler_params(("parallel", "parallel")),
        name="nsa_dense",
    )(q, k_cmp, v_cmp, kw, vw, gates, overlap_t, q_s, k_s, v_s, sinks)


def _nsa_sel_kernel(q_ref, k_ref, v_ref, sel_ref, gate_ref, o_ref, s_ref, m_ref, l_ref, acc_ref):
    i = pl.program_id(1)
    rows = GROUP_HEADS * Q_TILE
    width = GROUP_HEADS * HEAD_DIM
    n_blk = LANES - HEAD_DIM
    kc = SEL_CHUNK
    seq = k_ref.shape[1]
    groups = range(NSA_KV_HEADS)

    def scores(g, start, lhs):
        return lax.dot_general(k_ref[g, pl.ds(start, kc), :], lhs, _NT, preferred_element_type=F32)

    start_d = pl.multiple_of(jnp.maximum(i * Q_TILE + Q_TILE - kc, 0), Q_TILE)
    n_past = (start_d + kc - 1) // kc
    kpos = start_d + lax.broadcasted_iota(jnp.int32, (kc, rows), 0)
    qpos = i * Q_TILE + (lax.broadcasted_iota(jnp.int32, (kc, rows), 1) & (Q_TILE - 1))
    blk = lax.broadcasted_iota(jnp.int32, (Q_TILE, n_blk), 1)
    lhs_past, lhs_none = [], []
    for g in groups:
        q = q_ref[g * GROUP_HEADS:(g + 1) * GROUP_HEADS].reshape(rows, HEAD_DIM)
        sel = sel_ref[g][:, :n_blk].astype(F32)
        neg_diag = (sel - 1.0) * (-MASKED)
        neg_past = jnp.where(blk < start_d // NSA_SEL_BLOCK, neg_diag, MASKED)
        lhs_diag = jnp.concatenate([q, jnp.concatenate([neg_diag.astype(BF16)] * GROUP_HEADS, axis=0)], axis=-1)
        lhs_past.append(jnp.concatenate([q, jnp.concatenate([neg_past.astype(BF16)] * GROUP_HEADS, axis=0)], axis=-1))
        lhs_none.append(jnp.concatenate([q, jnp.full((rows, n_blk), MASKED, BF16)], axis=-1))
        s_ref[g, 0] = jnp.where(kpos <= qpos, scores(g, start_d, lhs_diag), MASKED)

        m_ref[g] = jnp.full((1, rows), MASKED, F32)
        l_ref[g] = jnp.zeros((1, rows), F32)
        acc_ref[g] = jnp.zeros((HEAD_DIM, rows), F32)

    def step(c, cur, nxt, lookahead=True):
        next_start = pl.multiple_of(jnp.minimum(c * kc, seq - kc), kc)
        v_start = pl.multiple_of(jnp.where(c == 0, start_d, (c - 1) * kc), Q_TILE)
        for g in groups:
            if lookahead:
                s_ref[g, nxt] = scores(g, next_start, jnp.where(c < n_past, lhs_past[g], lhs_none[g]))
            s = s_ref[g, cur]
            m = m_ref[g]
            m_new = jnp.maximum(m, jnp.max(s, axis=0, keepdims=True))
            alpha = jnp.exp2(m - m_new)
            p = jnp.exp2(s - m_new)
            m_ref[g] = m_new
            l_ref[g] = alpha * l_ref[g] + jnp.sum(p, axis=0, keepdims=True)
            acc_ref[g] = alpha * acc_ref[g] + _pv_t(v_ref[g, pl.ds(v_start, kc), :], p)

    def body(j, carry):
        step(2 * j, 0, 1)
        step(2 * j + 1, 1, 0)
        return carry

    lax.fori_loop(0, (n_past + 1) // 2, body, 0)

    @pl.when(n_past % 2 == 0)
    def _():
        step(n_past, 0, 1, lookahead=False)

    for g in groups:
        gate = _gate_cols(_gate_rows(gate_ref, g), 1)
        o_ref[:, g * width:(g + 1) * width] = _heads_from_t(acc_ref[g] * (gate / l_ref[g])).astype(o_ref.dtype)


def _nsa_sel(q, k, v, sel, gates):
    nh, bsz, seq, hd = q.shape
    ng = k.shape[0]
    onehot = (np.arange(seq)[:, None] // NSA_SEL_BLOCK == np.arange(LANES - hd)[None, :]).astype(np.float32)
    k_ext = jnp.concatenate([k, jnp.broadcast_to(jnp.asarray(onehot, BF16), (ng, bsz, seq, LANES - hd))], axis=-1)
    return pl.pallas_call(
        _nsa_sel_kernel,
        grid=(bsz, seq // Q_TILE),
        in_specs=[pl.BlockSpec((nh, None, Q_TILE, hd), lambda b, i: (0, b, i, 0)),
                  pl.BlockSpec((ng, None, seq, LANES), lambda b, i: (0, b, 0, 0)),
                  pl.BlockSpec((ng, None, seq, hd), lambda b, i: (0, b, 0, 0)),
                  pl.BlockSpec((ng, None, Q_TILE, LANES), lambda b, i: (0, b, i, 0)),
                  pl.BlockSpec((ng, None, Q_TILE, gates.shape[-1]), lambda b, i: (0, b, i, 0))],
        out_specs=pl.BlockSpec((None, Q_TILE, nh * hd), lambda b, i: (b, i, 0)),
        out_shape=jax.ShapeDtypeStruct((bsz, seq, nh * hd), BF16),
        scratch_shapes=[pltpu.VMEM((ng, 2, SEL_CHUNK, GROUP_HEADS * Q_TILE), F32),
                        pltpu.VMEM((ng, 1, GROUP_HEADS * Q_TILE), F32), pltpu.VMEM((ng, 1, GROUP_HEADS * Q_TILE), F32),
                        pltpu.VMEM((ng, hd, GROUP_HEADS * Q_TILE), F32)],
        compiler_params=_compiler_params(("parallel", "parallel")),
        name="nsa_sel",
    )(q, k_ext, v, sel, gates)


def _overlap_t(seq):
    n_cmp = (seq - NSA_CMP_BLOCK) // NSA_CMP_STRIDE + 1
    n_pad = seq // NSA_CMP_STRIDE
    starts = np.arange(n_pad) * NSA_CMP_STRIDE
    sel_start = np.arange(seq // NSA_SEL_BLOCK) * NSA_SEL_BLOCK
    ov = (starts[None, :] < sel_start[:, None] + NSA_SEL_BLOCK) & (starts[None, :] + NSA_CMP_BLOCK > sel_start[:, None])
    ov = ov & (np.arange(n_pad)[None, :] < n_cmp)
    return jnp.asarray(ov.astype(np.float32))


def _odd_mixer(xt, nw, w_in, w_out, cmp_pe, cmp_w1, cmp_w2, sinks, bsz):
    t, d = xt.shape
    seq = t // bsz
    assert seq // NSA_SEL_BLOCK <= LANES and seq % SEL_CHUNK == 0
    cuts = [int(c) for c in np.cumsum((0,) + ODD_SPLITS)]
    ws = [w_in[:, a:b] for a, b in zip(cuts[:-1], cuts[1:])]
    ws[0] = ws[0] * (HEAD_DIM ** -0.5 * LOG2_E)
    ws[8] = ws[8] * (HEAD_DIM ** -0.5 * LOG2_E)
    ws = [w.astype(BF16) for w in ws]
    hd = HEAD_DIM
    head_dims = (hd, hd, hd, hd, hd, hd, hd, 3 * GROUP_HEADS, hd, hd, hd)
    dtypes = (BF16, F32, F32, BF16, BF16, BF16, BF16, F32, BF16, BF16, BF16)
    q_n, kc, vc, ks, vs, kw, vw, gates, q_s, k_s, v_s = _norm_proj(xt, nw, ws, head_dims, dtypes)

    def heads(a):
        return a.reshape(a.shape[0], bsz, seq, a.shape[-1])

    q_n, ks, vs, kw, vw, gates, q_s, k_s, v_s = map(heads, (q_n, ks, vs, kw, vw, gates, q_s, k_s, v_s))
    ng = NSA_KV_HEADS
    rows16 = NSA_CMP_STRIDE * hd
    pe = cmp_pe.reshape(2, 2, rows16)
    w1 = cmp_w1.astype(BF16)
    w2 = cmp_w2.astype(BF16)
    k_cmp = _cmp_kv(kc.reshape(ng, bsz, seq // NSA_CMP_STRIDE, rows16), pe[0], w1[0], w2[0])
    v_cmp = _cmp_kv(vc.reshape(ng, bsz, seq // NSA_CMP_STRIDE, rows16), pe[1], w1[1], w2[1])
    o_cw, sel, o_swa = _nsa_dense(q_n, k_cmp, v_cmp, kw, vw, gates, _overlap_t(seq), q_s, k_s, v_s,
                                  sinks.reshape(SWA_KV_HEADS, 1, GROUP_HEADS))
    o_sel = _nsa_sel(q_n, ks, vs, sel, gates)
    n_nsa = NSA_HEADS * hd
    return [[o_cw.reshape(t, n_nsa), o_sel.reshape(t, n_nsa)], [o_swa.reshape(t, SWA_HEADS * hd)]], [w_out[:n_nsa], w_out[n_nsa:]]


def kernel(x, mem, norm_w, final_norm_w, mem_norm_w, ffn_w_gate, ffn_w_up, ffn_w_down, xattn_wq, xattn_wkv, xattn_wo, even_w_in, even_w_out, ssm_conv_w, ssm_conv_b, ssm_dt_bias, ssm_a_log, ssm_d, ssm_norm_w, odd_w_in, odd_w_out, nsa_cmp_pe, nsa_cmp_w1, nsa_cmp_w2, swa_sinks):
    bsz, seq, d = x.shape
    t = bsz * seq
    m = mem.shape[1]
    wg = ffn_w_gate.astype(BF16)
    wu = ffn_w_up.astype(BF16)
    wd = ffn_w_down.astype(BF16)
    wq = xattn_wq.astype(BF16)
    wkv = xattn_wkv.astype(BF16)
    wo = xattn_wo.astype(BF16)
    even_in = even_w_in.astype(BF16)
    even_out = even_w_out.astype(BF16)
    odd_out = odd_w_out.astype(BF16)

    xt = x.reshape(t, d)
    mem2 = mem.reshape(bsz * m, d)
    mem_nw = mem_norm_w.reshape(1, d)
    for layer in range(DEPTH):
        i = layer // 2
        xt = _ffn(xt, norm_w[layer, 0].reshape(1, d), wg, wu, wd, layer, 0)
        nw1 = norm_w[layer, 1].reshape(1, d)
        if layer % 2 == 0:
            mix, mix_w = _even_mixer(xt, nw1, even_in[i], even_out[i], ssm_conv_w[i], ssm_conv_b[i], ssm_dt_bias[i], ssm_a_log[i], ssm_d[i], ssm_norm_w[i], bsz)
        else:
            mix, mix_w = _odd_mixer(xt, nw1, odd_w_in[i], odd_out[i], nsa_cmp_pe[i], nsa_cmp_w1[i], nsa_cmp_w2[i], swa_sinks[i], bsz)
        (kv,) = _norm_proj(mem2, mem_nw, [wkv[layer]])
        xt = _mix_xattn(xt, mix, mix_w, norm_w[layer, 2].reshape(1, d), wq[layer], kv.reshape(bsz, m, 2 * d), wo[layer], bsz)
        out_nw = final_norm_w.reshape(1, d) if layer == DEPTH - 1 else None
        xt = _ffn(xt, norm_w[layer, 3].reshape(1, d), wg, wu, wd, layer, 1, out_nw)
    return xt.reshape(bsz, seq, d)
```

```python
import functools

import numpy as np
import jax
import jax.numpy as jnp
from jax import lax
from jax.experimental import pallas as pl
from jax.experimental.pallas import tpu as pltpu

F32 = jnp.float32
BF16 = jnp.bfloat16

D_MODEL = 1024
DEPTH = 4
D_FF = 2816
NORM_EPS = 1e-6
ATTN_BLOCK = 128

RET_HEADS = 4
RET_DK = 128
RET_DV = 128
RET_CHUNK = 128
RET_ROPE_BASE = 10000.0

SSM_HEADS = 8
SSM_HEAD_DIM = 64
SSM_D_INNER = SSM_HEADS * SSM_HEAD_DIM
SSM_D_STATE = 64
SSM_GROUPS = 2
SSM_CONV = 4
SSM_CHUNK = 128
SSM_CONV_DIM = SSM_D_INNER + 2 * SSM_GROUPS * SSM_D_STATE

EVEN_SPLITS = (RET_HEADS * RET_DK, RET_HEADS * RET_DK, RET_HEADS * RET_DV, RET_HEADS * RET_DV, SSM_D_INNER, SSM_CONV_DIM, SSM_HEADS)

NSA_HEADS = 8
NSA_KV_HEADS = 2
NSA_HEAD_DIM = 64
NSA_CMP_BLOCK = 32
NSA_CMP_STRIDE = 16
NSA_CMP_HIDDEN = 256
NSA_SEL_BLOCK = 64
NSA_TOPN = 16
NSA_WINDOW = 512
NSA_FORCE_SCORE = 1e6

SWA_HEADS = 8
SWA_KV_HEADS = 2
SWA_HEAD_DIM = 64
SWA_WINDOW = 128

ODD_SPLITS = (NSA_HEADS * NSA_HEAD_DIM,) + (NSA_KV_HEADS * NSA_HEAD_DIM,) * 6 + (3 * NSA_HEADS, SWA_HEADS * SWA_HEAD_DIM, SWA_KV_HEADS * SWA_HEAD_DIM, SWA_KV_HEADS * SWA_HEAD_DIM)

XATTN_HEADS = 4
XATTN_HEAD_DIM = D_MODEL // XATTN_HEADS

VMEM_LIMIT_BYTES = 52 * 1024 * 1024
TOKEN_TILE = 512
FFN_TILE = 256
LANES = 128
MASKED = -1e30
LOG2_E = float(np.log2(np.e))

_NT = (((1,), (1,)), ((), ()))


def _compiler_params(semantics):
    return pltpu.CompilerParams(dimension_semantics=semantics, vmem_limit_bytes=VMEM_LIMIT_BYTES)


def _rms(x, w):
    return x * lax.rsqrt(jnp.mean(x * x, axis=-1, keepdims=True) + NORM_EPS) * w


def _ffn_kernel(has_out_norm, x_ref, nw_ref, wg_ref, wu_ref, wd_ref, *refs):
    out_nw_ref = refs[0] if has_out_norm else None
    o_ref, a_ref = refs[-2:]
    x = x_ref[...]
    h = _rms(x, nw_ref[...]).astype(BF16)
    for c in range(D_FF // FFN_TILE):
        cols = slice(c * FFN_TILE, (c + 1) * FFN_TILE)
        g = jnp.dot(h, wg_ref[:, cols], preferred_element_type=F32)
        u = jnp.dot(h, wu_ref[:, cols], preferred_element_type=F32)
        a_ref[:, cols] = (g * jax.nn.sigmoid(g) * u).astype(BF16)
    y = x + 0.5 * jnp.dot(a_ref[...], wd_ref[...], preferred_element_type=F32)
    o_ref[...] = _rms(y, out_nw_ref[...]) if has_out_norm else y


def _ffn(x, nw, wg, wu, wd, layer, k, out_nw=None):
    t, d = x.shape
    tm = TOKEN_TILE
    resident = pl.Buffered(1)
    vec = pl.BlockSpec((1, d), lambda i: (0, 0))
    return pl.pallas_call(
        functools.partial(_ffn_kernel, out_nw is not None),
        grid=(t // tm,),
        in_specs=[
            pl.BlockSpec((tm, d), lambda i: (i, 0)),
            vec,
            pl.BlockSpec((None, None, d, D_FF), lambda i: (layer, k, 0, 0), pipeline_mode=resident),
            pl.BlockSpec((None, None, d, D_FF), lambda i: (layer, k, 0, 0), pipeline_mode=resident),
            pl.BlockSpec((None, None, D_FF, d), lambda i: (layer, k, 0, 0), pipeline_mode=resident),
        ] + ([vec] if out_nw is not None else []),
        out_specs=pl.BlockSpec((tm, d), lambda i: (i, 0)),
        out_shape=jax.ShapeDtypeStruct((t, d), F32),
        scratch_shapes=[pltpu.VMEM((tm, D_FF), BF16)],
        compiler_params=_compiler_params(("parallel",)),
        name="ffn",
    )(x, nw, wg, wu, wd, *([out_nw] if out_nw is not None else []))


def _norm_proj_kernel(head_dims, offsets, x_ref, nw_ref, w_ref, *o_refs):
    h = _rms(x_ref[...], nw_ref[...]).astype(BF16)
    y_all = jnp.dot(h, w_ref[...], preferred_element_type=F32)
    for hd, off, o_ref in zip(head_dims, offsets, o_refs):
        if hd is None:
            o_ref[...] = y_all[:, off:off + o_ref.shape[-1]].astype(o_ref.dtype)
        else:
            for j in range(o_ref.shape[0]):
                o_ref[j] = y_all[:, off + j * hd:off + (j + 1) * hd].astype(o_ref.dtype)


def _norm_proj(x, nw, weights, head_dims=None, dtypes=None, tm=TOKEN_TILE):
    t, d = x.shape
    n_out = len(weights)
    head_dims = tuple(head_dims) if head_dims is not None else (None,) * n_out
    dtypes = tuple(dtypes) if dtypes is not None else (F32,) * n_out
    out_specs, out_shape, offsets, padded = [], [], [], []
    total = 0
    for w, hd, dt in zip(weights, head_dims, dtypes):
        n = w.shape[1]
        if hd is None:
            out_specs.append(pl.BlockSpec((tm, n), lambda i: (i, 0)))
            out_shape.append(jax.ShapeDtypeStruct((t, n), dt))
        else:
            out_specs.append(pl.BlockSpec((n // hd, tm, hd), lambda i: (0, i, 0)))
            out_shape.append(jax.ShapeDtypeStruct((n // hd, t, hd), dt))
        offsets.append(total)
        width = -(-n // LANES) * LANES
        padded.append(jnp.pad(w, ((0, 0), (0, width - n))))
        total += width
    w_all = jnp.concatenate(padded, axis=1) if n_out > 1 else padded[0]
    return pl.pallas_call(
        functools.partial(_norm_proj_kernel, head_dims, tuple(offsets)),
        grid=(t // tm,),
        in_specs=[pl.BlockSpec((tm, d), lambda i: (i, 0)), pl.BlockSpec((1, d), lambda i: (0, 0)),
                  pl.BlockSpec(w_all.shape, lambda i: (0, 0))],
        out_specs=out_specs,
        out_shape=out_shape,
        compiler_params=_compiler_params(("parallel",)),
        name="norm_proj",
    )(x, nw, w_all)


def _mix_xattn_kernel(group_sizes, x_ref, *refs):
    n_a = sum(group_sizes)
    n_g = len(group_sizes)
    a_refs, w_refs = refs[:n_a], refs[n_a:n_a + n_g]
    nw_ref, wq_ref, kv_ref, wo_ref, o_ref = refs[n_a + n_g:]
    x = x_ref[...]
    pos = 0
    for size, w_ref in zip(group_sizes, w_refs):
        a = a_refs[pos][...]
        for r in a_refs[pos + 1:pos + size]:
            a = a.astype(F32) + r[...].astype(F32)
        pos += size
        x = x + jnp.dot(a.astype(BF16), w_ref[...], preferred_element_type=F32)
    h = _rms(x, nw_ref[...]).astype(BF16)
    q = jnp.dot(h, wq_ref[...], preferred_element_type=F32).astype(BF16)
    hd = XATTN_HEAD_DIM
    outs = []
    for hh in range(XATTN_HEADS):
        k = kv_ref[:, hh * hd:(hh + 1) * hd].astype(BF16)
        v = kv_ref[:, D_MODEL + hh * hd:D_MODEL + (hh + 1) * hd].astype(BF16)
        s = lax.dot_general(q[:, hh * hd:(hh + 1) * hd], k, _NT, preferred_element_type=F32)
        s = s * (hd ** -0.5)
        p = jnp.exp(s - jnp.max(s, axis=-1, keepdims=True))
        p = p / jnp.sum(p, axis=-1, keepdims=True)
        outs.append(jnp.dot(p.astype(BF16), v, preferred_element_type=F32).astype(BF16))
    o = jnp.concatenate(outs, axis=-1)
    o_ref[...] = x + jnp.dot(o, wo_ref[...], preferred_element_type=F32)


def _mix_xattn(x, groups, weights, nw, wq, kv, wo, bsz, tm=TOKEN_TILE):
    t, d = x.shape
    seq = t // bsz
    nt = seq // tm
    m = kv.shape[1]
    arrays = [a for grp in groups for a in grp]

    def tok(width):
        return pl.BlockSpec((tm, width), lambda b, i: (b * nt + i, 0))

    def const(shape):
        return pl.BlockSpec(shape, lambda b, i: (0, 0))

    return pl.pallas_call(
        functools.partial(_mix_xattn_kernel, tuple(len(grp) for grp in groups)),
        grid=(bsz, nt),
        in_specs=[tok(d)] + [tok(a.shape[1]) for a in arrays] + [const(w.shape) for w in weights]
        + [const((1, d)), const((d, d)), pl.BlockSpec((None, m, 2 * d), lambda b, i: (b, 0, 0)), const((d, d))],
        out_specs=tok(d),
        out_shape=jax.ShapeDtypeStruct((t, d), F32),
        compiler_params=_compiler_params(("parallel", "parallel")),
        name="mix_xattn",
    )(x, *arrays, *weights, nw, wq, kv, wo)


def _retention_kernel(q_ref, k_ref, v_ref, g_ref, cos_ref, sin_ref, dmat_ref, zeta_ref, xi_ref, decay_ref, o_ref, state_ref):
    @pl.when(pl.program_id(1) == 0)
    def _():
        state_ref[...] = jnp.zeros_like(state_ref)

    cos = cos_ref[...]
    sin = sin_ref[...]
    half = RET_DK // 2
    outs = []
    for h in range(RET_HEADS):
        qh = q_ref[:, h * RET_DK:(h + 1) * RET_DK]
        kh = k_ref[:, h * RET_DK:(h + 1) * RET_DK]
        qr = qh * cos + pltpu.roll(qh, half, 1) * sin
        kr = (kh * cos + pltpu.roll(kh, half, 1) * sin) * (RET_DK ** -0.5)
        vh = v_ref[:, h * RET_DV:(h + 1) * RET_DV].astype(BF16)
        scores = lax.dot_general(qr.astype(BF16), kr.astype(BF16), _NT, preferred_element_type=F32) * dmat_ref[h]
        inner = jnp.dot(scores.astype(BF16), vh, preferred_element_type=F32)
        state = state_ref[h]
        cross = jnp.dot((qr * xi_ref[h]).astype(BF16), state.astype(BF16), preferred_element_type=F32)
        kz = (kr * zeta_ref[h]).T.astype(BF16)
        state_ref[h] = decay_ref[h] * state + jnp.dot(kz, vh, preferred_element_type=F32)
        o = inner + cross
        o = o * lax.rsqrt(jnp.mean(o * o, axis=-1, keepdims=True) + NORM_EPS)
        gh = g_ref[:, h * RET_DV:(h + 1) * RET_DV]
        outs.append(gh * jax.nn.sigmoid(gh) * o)
    o_ref[...] = jnp.concatenate(outs, axis=-1).astype(o_ref.dtype)


def _retention_tables(seq):
    L = RET_CHUNK
    pos = jnp.arange(seq, dtype=F32)
    inv_freq = 1.0 / (RET_ROPE_BASE ** jnp.linspace(0.0, 1.0, RET_DK // 2, dtype=F32))
    ang = pos[:, None] * inv_freq[None, :]
    cos, sin = jnp.cos(ang), jnp.sin(ang)
    cos2 = jnp.concatenate([cos, cos], axis=-1)
    sin2 = jnp.concatenate([-sin, sin], axis=-1)
    log_g = jnp.log1p(-jnp.exp2(-5.0 - jnp.arange(RET_HEADS, dtype=F32)))
    idx = jnp.arange(L, dtype=F32)
    diff = idx[:, None] - idx[None, :]
    dmat = jnp.where(diff >= 0, jnp.exp(jnp.maximum(diff, 0.0)[None] * log_g[:, None, None]), 0.0)
    ones = jnp.ones((RET_HEADS, L, RET_DK), F32)
    zeta = jnp.exp((L - 1 - idx)[None, :] * log_g[:, None])[:, :, None] * ones
    xi = jnp.exp((idx + 1)[None, :] * log_g[:, None])[:, :, None] * ones
    decay = jnp.exp(L * log_g)[:, None, None] * ones
    return cos2, sin2, dmat, zeta, xi, decay


CONV_PAD = 8


def _ssd_kernel(xbc_ref, dt_ref, z_ref, convw_ref, convb_ref, dtb_ref, alog_ref, dskip_ref, nw_ref, tri_ref, o_ref, ext_ref, state_ref):
    L = SSM_CHUNK
    hp = SSM_HEAD_DIM
    ns = SSM_D_STATE

    @pl.when(pl.program_id(1) == 0)
    def _():
        ext_ref[0:CONV_PAD, :] = jnp.zeros((CONV_PAD, SSM_CONV_DIM), F32)
        state_ref[...] = jnp.zeros_like(state_ref)

    ext_ref[CONV_PAD:CONV_PAD + L, :] = xbc_ref[...]
    conv = convb_ref[...]
    for w in range(SSM_CONV):
        off = CONV_PAD - (SSM_CONV - 1) + w
        conv = conv + ext_ref[off:off + L, :] * convw_ref[w:w + 1, :]
    ext_ref[0:CONV_PAD, :] = ext_ref[L:L + CONV_PAD, :]
    xa = conv * jax.nn.sigmoid(conv)
    xs = xa[:, :SSM_D_INNER]
    bmat = xa[:, SSM_D_INNER:SSM_D_INNER + SSM_GROUPS * ns]
    cmat = xa[:, SSM_D_INNER + SSM_GROUPS * ns:]

    x = dt_ref[...] + dtb_ref[...]
    dt = jnp.maximum(x, 0.0) + jnp.log1p(jnp.exp(-jnp.abs(x)))
    lane = lax.broadcasted_iota(jnp.int32, (L, LANES), 1)
    dta = jnp.where(lane < SSM_HEADS, dt * -jnp.exp(alog_ref[...]), 0.0)
    cums = jnp.dot(tri_ref[...], dta, preferred_element_type=F32, precision=lax.Precision.HIGHEST)
    cums_t = cums.T
    last = cums[L - 1:L, :]
    decay_to_end = jnp.exp(last - cums)
    decay_from_start = jnp.exp(cums)
    chunk_decay = jnp.exp(last)
    causal = lax.broadcasted_iota(jnp.int32, (L, L), 0) >= lax.broadcasted_iota(jnp.int32, (L, L), 1)
    heads_per_group = SSM_HEADS // SSM_GROUPS
    ys = []
    for g in range(SSM_GROUPS):
        bg = bmat[:, g * ns:(g + 1) * ns].astype(BF16)
        cg = cmat[:, g * ns:(g + 1) * ns].astype(BF16)
        cb = lax.dot_general(cg, bg, _NT, preferred_element_type=F32)
        for e in range(heads_per_group):
            h = g * heads_per_group + e
            xh = xs[:, h * hp:(h + 1) * hp]
            xdt = xh * dt[:, h:h + 1]
            seg = cums[:, h:h + 1] - cums_t[h:h + 1, :]
            lmat = jnp.where(causal, jnp.exp(seg), 0.0)
            y_diag = jnp.dot((cb * lmat).astype(BF16), xdt.astype(BF16), preferred_element_type=F32)
            state = state_ref[h]
            y_off = lax.dot_general(cg, state.astype(BF16), _NT, preferred_element_type=F32) * decay_from_start[:, h:h + 1]
            xw = (xdt * decay_to_end[:, h:h + 1]).astype(BF16)
            new = lax.dot_general(xw, bg, (((0,), (0,)), ((), ())), preferred_element_type=F32)
            state_ref[h] = chunk_decay[:, h:h + 1] * state + new
            ys.append(y_diag + y_off + dskip_ref[:, h * hp:(h + 1) * hp] * xh)
    y = jnp.concatenate(ys, axis=-1)
    z = z_ref[...]
    o_ref[...] = _rms(y * (z * jax.nn.sigmoid(z)), nw_ref[...]).astype(o_ref.dtype)


N_RET_IN = 10
N_SSD_IN = 10


def _even_core_kernel(*refs):
    ret_in, ssd_in = refs[:N_RET_IN], refs[N_RET_IN:N_RET_IN + N_SSD_IN]
    o_ret_ref, o_ssd_ref, ret_state_ref, ext_ref, ssd_state_ref = refs[N_RET_IN + N_SSD_IN:]
    _retention_kernel(*ret_in, o_ret_ref, ret_state_ref)
    _ssd_kernel(*ssd_in, o_ssd_ref, ext_ref, ssd_state_ref)


def _even_core(q, k, v, g, xbc, dt_raw, z, conv_w, conv_b, dt_bias, a_log, d_skip, norm_w):
    bsz, seq, width = q.shape
    L = RET_CHUNK
    assert SSM_CHUNK == L
    tables = _retention_tables(seq)
    pad = LANES - SSM_HEADS
    params = [conv_w, conv_b.reshape(1, -1), jnp.pad(dt_bias, (0, pad)).reshape(1, LANES),
              jnp.pad(a_log, (0, pad)).reshape(1, LANES), jnp.repeat(d_skip, SSM_HEAD_DIM).reshape(1, -1),
              norm_w.reshape(1, -1), jnp.asarray(np.tril(np.ones((L, L), np.float32)))]

    def tok(w):
        return pl.BlockSpec((None, L, w), lambda b, c: (b, c, 0))

    rope = pl.BlockSpec((L, RET_DK), lambda b, c: (c, 0))
    const = pl.BlockSpec((RET_HEADS, L, RET_DK), lambda b, c: (0, 0, 0))
    return pl.pallas_call(
        _even_core_kernel,
        grid=(bsz, seq // L),
        in_specs=[tok(width)] * 4 + [rope, rope, const, const, const, const]
        + [tok(SSM_CONV_DIM), tok(LANES), tok(SSM_D_INNER)] + [pl.BlockSpec(p.shape, lambda b, c: (0, 0)) for p in params],
        out_specs=[tok(width), tok(SSM_D_INNER)],
        out_shape=[jax.ShapeDtypeStruct((bsz, seq, width), BF16), jax.ShapeDtypeStruct((bsz, seq, SSM_D_INNER), BF16)],
        scratch_shapes=[pltpu.VMEM((RET_HEADS, RET_DK, RET_DV), F32),
                        pltpu.VMEM((L + CONV_PAD, SSM_CONV_DIM), F32), pltpu.VMEM((SSM_HEADS, SSM_HEAD_DIM, SSM_D_STATE), F32)],
        compiler_params=_compiler_params(("parallel", "arbitrary")),
        name="even_core",
    )(q, k, v, g, *tables, xbc, dt_raw, z, *params)


def _even_mixer(xt, nw, w_in, w_out, conv_w, conv_b, dt_bias, a_log, d_skip, ssm_norm_w, bsz):
    t, d = xt.shape
    seq = t // bsz
    cuts = [int(c) for c in np.cumsum((0,) + EVEN_SPLITS)]
    ws = [w_in[:, a:b] for a, b in zip(cuts[:-1], cuts[1:])]
    perm = np.concatenate([h * RET_DK + np.concatenate([np.arange(0, RET_DK, 2), np.arange(1, RET_DK, 2)]) for h in range(RET_HEADS)])
    ws[0] = ws[0][:, perm]
    ws[1] = ws[1][:, perm]
    ws[6] = jnp.pad(ws[6], ((0, 0), (0, LANES - SSM_HEADS)))
    dtypes = (F32, F32, BF16, F32, F32, F32, F32)
    q, k, v, g, z, xbc, dt_raw = [a.reshape(bsz, seq, -1) for a in _norm_proj(xt, nw, ws, dtypes=dtypes)]
    o_ret, y = _even_core(q, k, v, g, xbc, dt_raw, z, conv_w, conv_b, dt_bias, a_log, d_skip, ssm_norm_w)
    n_ret = RET_HEADS * RET_DV
    return [[o_ret.reshape(t, n_ret)], [y.reshape(t, SSM_D_INNER)]], [w_out[:n_ret], w_out[n_ret:]]


Q_TILE = ATTN_BLOCK
GROUP_HEADS = NSA_HEADS // NSA_KV_HEADS
HEAD_DIM = NSA_HEAD_DIM
SEL_CHUNK = 512


def _cmp_kv_kernel(r_ref, pe_ref, w1_ref, w2_ref, o_ref):
    r = r_ref[...]
    half = NSA_CMP_STRIDE * HEAD_DIM
    top = jnp.dot((r + pe_ref[0:1, :]).astype(BF16), w1_ref[0:half, :], preferred_element_type=F32)
    bot = jnp.dot((r + pe_ref[1:2, :]).astype(BF16), w1_ref[half:2 * half, :], preferred_element_type=F32)
    hidden = top + pltpu.roll(bot, r.shape[0] - 1, 0)
    act = hidden * jax.nn.sigmoid(hidden)
    o_ref[...] = jnp.dot(act.astype(BF16), w2_ref[...], preferred_element_type=F32)


def _cmp_kv(r, pe, w1, w2):
    ng, bsz, n, width = r.shape
    return pl.pallas_call(
        _cmp_kv_kernel,
        grid=(ng, bsz),
        in_specs=[pl.BlockSpec((None, None, n, width), lambda g, b: (g, b, 0, 0)),
                  pl.BlockSpec(pe.shape, lambda g, b: (0, 0)),
                  pl.BlockSpec(w1.shape, lambda g, b: (0, 0)),
                  pl.BlockSpec(w2.shape, lambda g, b: (0, 0))],
        out_specs=pl.BlockSpec((None, None, n, HEAD_DIM), lambda g, b: (g, b, 0, 0)),
        out_shape=jax.ShapeDtypeStruct((ng, bsz, n, HEAD_DIM), F32),
        compiler_params=_compiler_params(("parallel", "parallel")),
        name="nsa_compress",
    )(r, pe, w1, w2)


def _heads_from_t(o_t):
    pairs = []
    for e in range(0, GROUP_HEADS, 2):
        pair = jnp.concatenate([o_t[:, e * Q_TILE:(e + 1) * Q_TILE], o_t[:, (e + 1) * Q_TILE:(e + 2) * Q_TILE]], axis=0)
        pairs.append(pair.T)
    return jnp.concatenate(pairs, axis=-1)


def _gate_rows(gate_ref, g):
    gate = jax.nn.sigmoid(gate_ref[g])
    gate = jnp.concatenate([gate, jnp.zeros((Q_TILE, LANES - gate.shape[1]), F32)], axis=1)
    return gate.T


def _gate_cols(gate_t, branch):
    return jnp.concatenate([gate_t[3 * e + branch:3 * e + branch + 1, :] for e in range(GROUP_HEADS)], axis=1)


def _tile_heads(x):
    return jnp.concatenate([x] * GROUP_HEADS, axis=1)


def _pv_t(v, p):
    return lax.dot_general(v, p.astype(BF16), (((0,), (0,)), ((), ())), preferred_element_type=F32)


ACC_ROWS = HEAD_DIM + 8


def _with_ones(v):
    pad = jnp.zeros(v.shape[:-1] + (LANES - HEAD_DIM - 1,), v.dtype)
    return jnp.concatenate([v, jnp.ones(v.shape[:-1] + (1,), v.dtype), pad], axis=-1)


def _band_window(window):
    return (-(-(window - 1) // Q_TILE) + 1) * Q_TILE


def _band_branch(i, q, k_ref, v_ref, g, window, sink=None):
    n_keys = _band_window(window)
    start = pl.multiple_of(jnp.maximum(i * Q_TILE + Q_TILE - n_keys, 0), Q_TILE)
    rel = (i * Q_TILE + lax.broadcasted_iota(jnp.int32, (n_keys, Q_TILE), 1)) - (start + lax.broadcasted_iota(jnp.int32, (n_keys, Q_TILE), 0))
    bias = _tile_heads(jnp.where(rel >= 0, jnp.where(rel < window, 0.0, MASKED), MASKED))
    s = lax.dot_general(k_ref[g, pl.ds(start, n_keys), :], q, _NT, preferred_element_type=F32) + bias
    m = jnp.max(s, axis=0, keepdims=True)
    if sink is not None:
        m = jnp.maximum(m, sink)
    p = jnp.exp2(s - m)
    denom = jnp.sum(p, axis=0, keepdims=True)
    denom = denom + jnp.exp2(sink - m) if sink is not None else jnp.maximum(denom, 1e-30)
    return _pv_t(v_ref[g, pl.ds(start, n_keys), :], p) * (1.0 / denom)


def _nsa_dense_kernel(q_ref, kc_ref, vc_ref, kw_ref, vw_ref, gate_ref, ov_ref, qs_ref, ks_ref, vs_ref, sink_ref,
                      o_ref, sel_ref, swa_ref):
    i = pl.program_id(1)
    rows = GROUP_HEADS * Q_TILE
    width = GROUP_HEADS * HEAD_DIM
    n_cmp = kc_ref.shape[1]
    n = lax.broadcasted_iota(jnp.int32, (n_cmp, Q_TILE), 0)
    t = i * Q_TILE + lax.broadcasted_iota(jnp.int32, (n_cmp, Q_TILE), 1)
    valid = _tile_heads(jnp.where(n * NSA_CMP_STRIDE + (NSA_CMP_BLOCK - 1) <= t, 1.0, 0.0))
    bias = (valid - 1.0) * (-MASKED)
    n_sel = ov_ref.shape[0]
    jb = lax.broadcasted_iota(jnp.int32, (n_sel, Q_TILE), 0)
    cur = (i * Q_TILE + lax.broadcasted_iota(jnp.int32, (n_sel, Q_TILE), 1)) // NSA_SEL_BLOCK
    valid_b = jb <= cur
    forced = jnp.where(valid_b, jnp.where(jb == 0, 1.0, 0.0) + jnp.where(jb == cur, 1.0, 0.0) + jnp.where(jb == cur - 1, 1.0, 0.0), 0.0)
    for g in range(NSA_KV_HEADS):
        q = q_ref[g * GROUP_HEADS:(g + 1) * GROUP_HEADS].reshape(rows, HEAD_DIM)
        s = lax.dot_general(kc_ref[g].astype(BF16), q, _NT, preferred_element_type=F32) + bias
        m = jnp.max(s, axis=0, keepdims=True)
        p = jnp.exp2(s - m) * valid
        p = p * (1.0 / jnp.maximum(jnp.sum(p, axis=0, keepdims=True), 1e-30))
        o_cmp = _pv_t(vc_ref[g].astype(BF16), p)
        o_win = _band_branch(i, q, kw_ref, vw_ref, g, NSA_WINDOW)
        gate_t = _gate_rows(gate_ref, g)
        o_ref[:, g * width:(g + 1) * width] = _heads_from_t(o_cmp * _gate_cols(gate_t, 0) + o_win * _gate_cols(gate_t, 2)).astype(o_ref.dtype)

        p_sum = p[:, 0:Q_TILE]
        for e in range(1, GROUP_HEADS):
            p_sum = p_sum + p[:, e * Q_TILE:(e + 1) * Q_TILE]
        imp = jnp.dot(ov_ref[...], p_sum, preferred_element_type=F32, precision=lax.Precision.HIGHEST)
        score = jnp.where(forced > 0.5, NSA_FORCE_SCORE, jnp.where(valid_b, imp, -jnp.inf))
        sub = 8
        ranks = []
        for v in range(n_sel // sub):
            blk_scores = score[v * sub:(v + 1) * sub, :]
            jb_v = jb[v * sub:(v + 1) * sub, :]
            rank = jnp.zeros((sub, Q_TILE), F32)
            for ii in range(n_sel):
                row = score[ii:ii + 1, :]
                if ii < v * sub:
                    ahead = jnp.where(row >= blk_scores, 1.0, 0.0)
                elif ii >= (v + 1) * sub:
                    ahead = jnp.where(row > blk_scores, 1.0, 0.0)
                else:
                    tie = jnp.where(jb_v > ii, 1.0, 0.0)
                    ahead = jnp.where(row > blk_scores, 1.0, jnp.where(row == blk_scores, tie, 0.0))
                rank = rank + ahead
            ranks.append(rank)
        rank = jnp.concatenate(ranks, axis=0)
        sel_t = jnp.where(valid_b, jnp.where(rank < NSA_TOPN, 1.0, 0.0), 0.0)
        sel_t = jnp.concatenate([sel_t, jnp.zeros((LANES - n_sel, Q_TILE), F32)], axis=0)
        sel_ref[g] = sel_t.T.astype(BF16)

    for g in range(SWA_KV_HEADS):
        q = qs_ref[g * GROUP_HEADS:(g + 1) * GROUP_HEADS].reshape(rows, HEAD_DIM)
        sink = LOG2_E * jnp.concatenate([jnp.broadcast_to(sink_ref[g][:, e:e + 1], (1, Q_TILE)) for e in range(GROUP_HEADS)], axis=1)
        o_swa = _band_branch(i, q, ks_ref, vs_ref, g, SWA_WINDOW, sink)
        swa_ref[:, g * width:(g + 1) * width] = _heads_from_t(o_swa).astype(swa_ref.dtype)


def _nsa_dense(q, k_cmp, v_cmp, kw, vw, gates, overlap_t, q_s, k_s, v_s, sinks):
    nh, bsz, seq, hd = q.shape
    ng, _, n_cmp, _ = k_cmp.shape
    heads = pl.BlockSpec((nh, None, Q_TILE, hd), lambda b, i: (0, b, i, 0))
    cmp_kv = pl.BlockSpec((ng, None, n_cmp, hd), lambda b, i: (0, b, 0, 0))
    seq_kv = pl.BlockSpec((ng, None, seq, hd), lambda b, i: (0, b, 0, 0))
    out = pl.BlockSpec((None, Q_TILE, nh * hd), lambda b, i: (b, i, 0))
    return pl.pallas_call(
        _nsa_dense_kernel,
        grid=(bsz, seq // Q_TILE),
        in_specs=[heads, cmp_kv, cmp_kv, seq_kv, seq_kv,
                  pl.BlockSpec((ng, None, Q_TILE, gates.shape[-1]), lambda b, i: (0, b, i, 0)),
                  pl.BlockSpec(overlap_t.shape, lambda b, i: (0, 0)),
                  heads, seq_kv, seq_kv,
                  pl.BlockSpec(sinks.shape, lambda b, i: (0, 0, 0))],
        out_specs=[out, pl.BlockSpec((ng, None, Q_TILE, LANES), lambda b, i: (0, b, i, 0)), out],
        out_shape=[jax.ShapeDtypeStruct((bsz, seq, nh * hd), BF16),
                   jax.ShapeDtypeStruct((ng, bsz, seq, LANES), BF16),
                   jax.ShapeDtypeStruct((bsz, seq, nh * hd), BF16)],
        compiler_params=_compiler_params(("parallel", "parallel")),
        name="nsa_dense",
    )(q, k_cmp, v_cmp, kw, vw, gates, overlap_t, q_s, k_s, v_s, sinks)


def _nsa_sel_kernel(q_ref, k_ref, v_ref, sel_ref, gate_ref, o_ref, s_ref, m_ref, acc_ref):
    i = pl.program_id(1)
    rows = GROUP_HEADS * Q_TILE
    width = GROUP_HEADS * HEAD_DIM
    n_blk = LANES - HEAD_DIM
    kc = SEL_CHUNK
    seq = k_ref.shape[1]
    groups = range(NSA_KV_HEADS)

    def scores(g, start, lhs):
        return lax.dot_general(k_ref[g, pl.ds(start, kc), :], lhs, _NT, preferred_element_type=F32)

    start_d = pl.multiple_of(jnp.maximum(i * Q_TILE + Q_TILE - kc, 0), Q_TILE)
    n_past = (start_d + kc - 1) // kc
    kpos = start_d + lax.broadcasted_iota(jnp.int32, (kc, rows), 0)
    qpos = i * Q_TILE + (lax.broadcasted_iota(jnp.int32, (kc, rows), 1) & (Q_TILE - 1))
    blk = lax.broadcasted_iota(jnp.int32, (Q_TILE, n_blk), 1)
    lhs_past, lhs_none = [], []
    for g in groups:
        q = q_ref[g * GROUP_HEADS:(g + 1) * GROUP_HEADS].reshape(rows, HEAD_DIM)
        sel = sel_ref[g][:, :n_blk].astype(F32)
        neg_diag = (sel - 1.0) * (-MASKED)
        neg_past = jnp.where(blk < start_d // NSA_SEL_BLOCK, neg_diag, MASKED)
        lhs_diag = jnp.concatenate([q, jnp.concatenate([neg_diag.astype(BF16)] * GROUP_HEADS, axis=0)], axis=-1)
        lhs_past.append(jnp.concatenate([q, jnp.concatenate([neg_past.astype(BF16)] * GROUP_HEADS, axis=0)], axis=-1))
        lhs_none.append(jnp.concatenate([q, jnp.full((rows, n_blk), MASKED, BF16)], axis=-1))
        s_ref[g, 0] = jnp.where(kpos <= qpos, scores(g, start_d, lhs_diag), MASKED)

        m_ref[g] = jnp.full((1, rows), MASKED, F32)
        acc_ref[g] = jnp.zeros((ACC_ROWS, rows), F32)

    def step(c, cur, nxt, lookahead=True):
        next_start = pl.multiple_of(jnp.minimum(c * kc, seq - kc), kc)
        v_start = pl.multiple_of(jnp.where(c == 0, start_d, (c - 1) * kc), Q_TILE)
        for g in groups:
            if lookahead:
                s_ref[g, nxt] = scores(g, next_start, jnp.where(c < n_past, lhs_past[g], lhs_none[g]))
            s = s_ref[g, cur]
            m = m_ref[g]
            m_new = jnp.maximum(m, jnp.max(s, axis=0, keepdims=True))
            alpha = jnp.exp2(m - m_new)
            p = jnp.exp2(s - m_new)
            m_ref[g] = m_new
            acc_ref[g] = alpha * acc_ref[g] + _pv_t(v_ref[g, pl.ds(v_start, kc), :], p)[:ACC_ROWS]

    def body(j, carry):
        step(2 * j, 0, 1)
        step(2 * j + 1, 1, 0)
        return carry

    lax.fori_loop(0, (n_past + 1) // 2, body, 0)

    @pl.when(n_past % 2 == 0)
    def _():
        step(n_past, 0, 1, lookahead=False)

    for g in groups:
        gate = _gate_cols(_gate_rows(gate_ref, g), 1)
        acc = acc_ref[g]
        o_ref[:, g * width:(g + 1) * width] = _heads_from_t(acc[:HEAD_DIM] * (gate / acc[HEAD_DIM:HEAD_DIM + 1])).astype(o_ref.dtype)


def _nsa_sel(q, k, v, sel, gates):
    nh, bsz, seq, hd = q.shape
    ng = k.shape[0]
    onehot = (np.arange(seq)[:, None] // NSA_SEL_BLOCK == np.arange(LANES - hd)[None, :]).astype(np.float32)
    k_ext = jnp.concatenate([k, jnp.broadcast_to(jnp.asarray(onehot, BF16), (ng, bsz, seq, LANES - hd))], axis=-1)
    return pl.pallas_call(
        _nsa_sel_kernel,
        grid=(bsz, seq // Q_TILE),
        in_specs=[pl.BlockSpec((nh, None, Q_TILE, hd), lambda b, i: (0, b, i, 0)),
                  pl.BlockSpec((ng, None, seq, LANES), lambda b, i: (0, b, 0, 0)),
                  pl.BlockSpec((ng, None, seq, LANES), lambda b, i: (0, b, 0, 0)),
                  pl.BlockSpec((ng, None, Q_TILE, LANES), lambda b, i: (0, b, i, 0)),
                  pl.BlockSpec((ng, None, Q_TILE, gates.shape[-1]), lambda b, i: (0, b, i, 0))],
        out_specs=pl.BlockSpec((None, Q_TILE, nh * hd), lambda b, i: (b, i, 0)),
        out_shape=jax.ShapeDtypeStruct((bsz, seq, nh * hd), BF16),
        scratch_shapes=[pltpu.VMEM((ng, 2, SEL_CHUNK, GROUP_HEADS * Q_TILE), F32),
                        pltpu.VMEM((ng, 1, GROUP_HEADS * Q_TILE), F32),
                        pltpu.VMEM((ng, ACC_ROWS, GROUP_HEADS * Q_TILE), F32)],
        compiler_params=_compiler_params(("parallel", "parallel")),
        name="nsa_sel",
    )(q, k_ext, _with_ones(v), sel, gates)


def _overlap_t(seq):
    n_cmp = (seq - NSA_CMP_BLOCK) // NSA_CMP_STRIDE + 1
    n_pad = seq // NSA_CMP_STRIDE
    starts = np.arange(n_pad) * NSA_CMP_STRIDE
    sel_start = np.arange(seq // NSA_SEL_BLOCK) * NSA_SEL_BLOCK
    ov = (starts[None, :] < sel_start[:, None] + NSA_SEL_BLOCK) & (starts[None, :] + NSA_CMP_BLOCK > sel_start[:, None])
    ov = ov & (np.arange(n_pad)[None, :] < n_cmp)
    return jnp.asarray(ov.astype(np.float32))


def _odd_mixer(xt, nw, w_in, w_out, cmp_pe, cmp_w1, cmp_w2, sinks, bsz):
    t, d = xt.shape
    seq = t // bsz
    assert seq // NSA_SEL_BLOCK <= LANES and seq % SEL_CHUNK == 0
    cuts = [int(c) for c in np.cumsum((0,) + ODD_SPLITS)]
    ws = [w_in[:, a:b] for a, b in zip(cuts[:-1], cuts[1:])]
    ws[0] = ws[0] * (HEAD_DIM ** -0.5 * LOG2_E)
    ws[8] = ws[8] * (HEAD_DIM ** -0.5 * LOG2_E)
    ws = [w.astype(BF16) for w in ws]
    hd = HEAD_DIM
    head_dims = (hd, hd, hd, hd, hd, hd, hd, 3 * GROUP_HEADS, hd, hd, hd)
    dtypes = (BF16, F32, F32, BF16, BF16, BF16, BF16, F32, BF16, BF16, BF16)
    q_n, kc, vc, ks, vs, kw, vw, gates, q_s, k_s, v_s = _norm_proj(xt, nw, ws, head_dims, dtypes)

    def heads(a):
        return a.reshape(a.shape[0], bsz, seq, a.shape[-1])

    q_n, ks, vs, kw, vw, gates, q_s, k_s, v_s = map(heads, (q_n, ks, vs, kw, vw, gates, q_s, k_s, v_s))
    ng = NSA_KV_HEADS
    rows16 = NSA_CMP_STRIDE * hd
    pe = cmp_pe.reshape(2, 2, rows16)
    w1 = cmp_w1.astype(BF16)
    w2 = cmp_w2.astype(BF16)
    k_cmp = _cmp_kv(kc.reshape(ng, bsz, seq // NSA_CMP_STRIDE, rows16), pe[0], w1[0], w2[0])
    v_cmp = _cmp_kv(vc.reshape(ng, bsz, seq // NSA_CMP_STRIDE, rows16), pe[1], w1[1], w2[1])
    o_cw, sel, o_swa = _nsa_dense(q_n, k_cmp, v_cmp, kw, vw, gates, _overlap_t(seq), q_s, k_s, v_s,
                                  sinks.reshape(SWA_KV_HEADS, 1, GROUP_HEADS))
    o_sel = _nsa_sel(q_n, ks, vs, sel, gates)
    n_nsa = NSA_HEADS * hd
    return [[o_cw.reshape(t, n_nsa), o_sel.reshape(t, n_nsa)], [o_swa.reshape(t, SWA_HEADS * hd)]], [w_out[:n_nsa], w_out[n_nsa:]]


def kernel(x, mem, norm_w, final_norm_w, mem_norm_w, ffn_w_gate, ffn_w_up, ffn_w_down, xattn_wq, xattn_wkv, xattn_wo, even_w_in, even_w_out, ssm_conv_w, ssm_conv_b, ssm_dt_bias, ssm_a_log, ssm_d, ssm_norm_w, odd_w_in, odd_w_out, nsa_cmp_pe, nsa_cmp_w1, nsa_cmp_w2, swa_sinks):
    bsz, seq, d = x.shape
    t = bsz * seq
    m = mem.shape[1]
    wg = ffn_w_gate.astype(BF16)
    wu = ffn_w_up.astype(BF16)
    wd = ffn_w_down.astype(BF16)
    wq = xattn_wq.astype(BF16)
    wkv = xattn_wkv.astype(BF16)
    wo = xattn_wo.astype(BF16)
    even_in = even_w_in.astype(BF16)
    even_out = even_w_out.astype(BF16)
    odd_out = odd_w_out.astype(BF16)

    xt = x.reshape(t, d)
    mem2 = mem.reshape(bsz * m, d)
    mem_nw = mem_norm_w.reshape(1, d)
    for layer in range(DEPTH):
        i = layer // 2
        xt = _ffn(xt, norm_w[layer, 0].reshape(1, d), wg, wu, wd, layer, 0)
        nw1 = norm_w[layer, 1].reshape(1, d)
        if layer % 2 == 0:
            mix, mix_w = _even_mixer(xt, nw1, even_in[i], even_out[i], ssm_conv_w[i], ssm_conv_b[i], ssm_dt_bias[i], ssm_a_log[i], ssm_d[i], ssm_norm_w[i], bsz)
        else:
            mix, mix_w = _odd_mixer(xt, nw1, odd_w_in[i], odd_out[i], nsa_cmp_pe[i], nsa_cmp_w1[i], nsa_cmp_w2[i], swa_sinks[i], bsz)
        (kv,) = _norm_proj(mem2, mem_nw, [wkv[layer]])
        xt = _mix_xattn(xt, mix, mix_w, norm_w[layer, 2].reshape(1, d), wq[layer], kv.reshape(bsz, m, 2 * d), wo[layer], bsz)
        out_nw = final_norm_w.reshape(1, d) if layer == DEPTH - 1 else None
        xt = _ffn(xt, norm_w[layer, 3].reshape(1, d), wg, wu, wd, layer, 1, out_nw)
    return xt.reshape(bsz, seq, d)
```

```python
import functools

import numpy as np
import jax
import jax.numpy as jnp
from jax import lax
from jax.experimental import pallas as pl
from jax.experimental.pallas import tpu as pltpu

F32 = jnp.float32
BF16 = jnp.bfloat16

D_MODEL = 1024
DEPTH = 4
D_FF = 2816
NORM_EPS = 1e-6
ATTN_BLOCK = 128

RET_HEADS = 4
RET_DK = 128
RET_DV = 128
RET_CHUNK = 128
RET_ROPE_BASE = 10000.0

SSM_HEADS = 8
SSM_HEAD_DIM = 64
SSM_D_INNER = SSM_HEADS * SSM_HEAD_DIM
SSM_D_STATE = 64
SSM_GROUPS = 2
SSM_CONV = 4
SSM_CHUNK = 128
SSM_CONV_DIM = SSM_D_INNER + 2 * SSM_GROUPS * SSM_D_STATE

EVEN_SPLITS = (RET_HEADS * RET_DK, RET_HEADS * RET_DK, RET_HEADS * RET_DV, RET_HEADS * RET_DV, SSM_D_INNER, SSM_CONV_DIM, SSM_HEADS)

NSA_HEADS = 8
NSA_KV_HEADS = 2
NSA_HEAD_DIM = 64
NSA_CMP_BLOCK = 32
NSA_CMP_STRIDE = 16
NSA_CMP_HIDDEN = 256
NSA_SEL_BLOCK = 64
NSA_TOPN = 16
NSA_WINDOW = 512
NSA_FORCE_SCORE = 1e6

SWA_HEADS = 8
SWA_KV_HEADS = 2
SWA_HEAD_DIM = 64
SWA_WINDOW = 128

ODD_SPLITS = (NSA_HEADS * NSA_HEAD_DIM,) + (NSA_KV_HEADS * NSA_HEAD_DIM,) * 6 + (3 * NSA_HEADS, SWA_HEADS * SWA_HEAD_DIM, SWA_KV_HEADS * SWA_HEAD_DIM, SWA_KV_HEADS * SWA_HEAD_DIM)

XATTN_HEADS = 4
XATTN_HEAD_DIM = D_MODEL // XATTN_HEADS

VMEM_LIMIT_BYTES = 52 * 1024 * 1024
TOKEN_TILE = 512
FFN_TILE = 256
FFN_DOWN_CHUNK = 1024
LANES = 128
MASKED = -1e30
LOG2_E = float(np.log2(np.e))

_NT = (((1,), (1,)), ((), ()))


def _compiler_params(semantics):
    return pltpu.CompilerParams(dimension_semantics=semantics, vmem_limit_bytes=VMEM_LIMIT_BYTES)


def _rms(x, w):
    return x * lax.rsqrt(jnp.mean(x * x, axis=-1, keepdims=True) + NORM_EPS) * w


def _ffn_kernel(has_out_norm, x_ref, nw_ref, wg_ref, wu_ref, wd_ref, *refs):
    out_nw_ref = refs[0] if has_out_norm else None
    o_ref, a_ref = refs[-2:]
    x = x_ref[...]
    h = _rms(x, nw_ref[...]).astype(BF16)
    for c in range(D_FF // FFN_TILE):
        cols = slice(c * FFN_TILE, (c + 1) * FFN_TILE)
        g = jnp.dot(h, wg_ref[:, cols].astype(BF16), preferred_element_type=F32)
        u = jnp.dot(h, wu_ref[:, cols].astype(BF16), preferred_element_type=F32)
        a_ref[:, cols] = (g * jax.nn.sigmoid(g) * u).astype(BF16)
    acc = None
    for lo in range(0, D_FF, FFN_DOWN_CHUNK):
        hi = min(lo + FFN_DOWN_CHUNK, D_FF)
        part = jnp.dot(a_ref[:, lo:hi], wd_ref[lo:hi, :].astype(BF16), preferred_element_type=F32)
        acc = part if acc is None else acc + part
    y = x + 0.5 * acc
    o_ref[...] = _rms(y, out_nw_ref[...]) if has_out_norm else y


def _ffn(x, nw, wg, wu, wd, layer, k, out_nw=None):
    t, d = x.shape
    tm = TOKEN_TILE
    resident = pl.Buffered(1)
    vec = pl.BlockSpec((1, d), lambda i: (0, 0))
    return pl.pallas_call(
        functools.partial(_ffn_kernel, out_nw is not None),
        grid=(t // tm,),
        in_specs=[
            pl.BlockSpec((tm, d), lambda i: (i, 0)),
            vec,
            pl.BlockSpec((None, None, d, D_FF), lambda i: (layer, k, 0, 0), pipeline_mode=resident),
            pl.BlockSpec((None, None, d, D_FF), lambda i: (layer, k, 0, 0), pipeline_mode=resident),
            pl.BlockSpec((None, None, D_FF, d), lambda i: (layer, k, 0, 0), pipeline_mode=resident),
        ] + ([vec] if out_nw is not None else []),
        out_specs=pl.BlockSpec((tm, d), lambda i: (i, 0)),
        out_shape=jax.ShapeDtypeStruct((t, d), F32),
        scratch_shapes=[pltpu.VMEM((tm, D_FF), BF16)],
        compiler_params=_compiler_params(("parallel",)),
        name="ffn",
    )(x, nw, wg, wu, wd, *([out_nw] if out_nw is not None else []))


def _norm_proj_kernel(head_dims, offsets, x_ref, nw_ref, w_ref, *o_refs):
    h = _rms(x_ref[...], nw_ref[...]).astype(BF16)
    y_all = jnp.dot(h, w_ref[...], preferred_element_type=F32)
    for hd, off, o_ref in zip(head_dims, offsets, o_refs):
        if hd is None:
            o_ref[...] = y_all[:, off:off + o_ref.shape[-1]].astype(o_ref.dtype)
        else:
            for j in range(o_ref.shape[0]):
                o_ref[j] = y_all[:, off + j * hd:off + (j + 1) * hd].astype(o_ref.dtype)


def _norm_proj(x, nw, weights, head_dims=None, dtypes=None, tm=TOKEN_TILE):
    t, d = x.shape
    n_out = len(weights)
    head_dims = tuple(head_dims) if head_dims is not None else (None,) * n_out
    dtypes = tuple(dtypes) if dtypes is not None else (F32,) * n_out
    out_specs, out_shape, offsets, padded = [], [], [], []
    total = 0
    for w, hd, dt in zip(weights, head_dims, dtypes):
        n = w.shape[1]
        if hd is None:
            out_specs.append(pl.BlockSpec((tm, n), lambda i: (i, 0)))
            out_shape.append(jax.ShapeDtypeStruct((t, n), dt))
        else:
            out_specs.append(pl.BlockSpec((n // hd, tm, hd), lambda i: (0, i, 0)))
            out_shape.append(jax.ShapeDtypeStruct((n // hd, t, hd), dt))
        offsets.append(total)
        width = -(-n // LANES) * LANES
        padded.append(jnp.pad(w, ((0, 0), (0, width - n))))
        total += width
    w_all = jnp.concatenate(padded, axis=1) if n_out > 1 else padded[0]
    return pl.pallas_call(
        functools.partial(_norm_proj_kernel, head_dims, tuple(offsets)),
        grid=(t // tm,),
        in_specs=[pl.BlockSpec((tm, d), lambda i: (i, 0)), pl.BlockSpec((1, d), lambda i: (0, 0)),
                  pl.BlockSpec(w_all.shape, lambda i: (0, 0))],
        out_specs=out_specs,
        out_shape=out_shape,
        compiler_params=_compiler_params(("parallel",)),
        name="norm_proj",
    )(x, nw, w_all)


def _mix_xattn_kernel(group_sizes, x_ref, *refs):
    n_a = sum(group_sizes)
    n_g = len(group_sizes)
    a_refs, w_refs = refs[:n_a], refs[n_a:n_a + n_g]
    nw_ref, wq_ref, kv_ref, wo_ref, o_ref = refs[n_a + n_g:]
    x = x_ref[...]
    pos = 0
    for size, w_ref in zip(group_sizes, w_refs):
        a = a_refs[pos][...]
        for r in a_refs[pos + 1:pos + size]:
            a = a.astype(F32) + r[...].astype(F32)
        pos += size
        x = x + jnp.dot(a.astype(BF16), w_ref[...], preferred_element_type=F32)
    h = _rms(x, nw_ref[...]).astype(BF16)
    q = jnp.dot(h, wq_ref[...], preferred_element_type=F32).astype(BF16)
    hd = XATTN_HEAD_DIM
    outs = []
    for hh in range(XATTN_HEADS):
        k = kv_ref[:, hh * hd:(hh + 1) * hd].astype(BF16)
        v = kv_ref[:, D_MODEL + hh * hd:D_MODEL + (hh + 1) * hd].astype(BF16)
        s = lax.dot_general(q[:, hh * hd:(hh + 1) * hd], k, _NT, preferred_element_type=F32)
        s = s * (hd ** -0.5)
        p = jnp.exp(s - jnp.max(s, axis=-1, keepdims=True))
        p = p / jnp.sum(p, axis=-1, keepdims=True)
        outs.append(jnp.dot(p.astype(BF16), v, preferred_element_type=F32).astype(BF16))
    o = jnp.concatenate(outs, axis=-1)
    o_ref[...] = x + jnp.dot(o, wo_ref[...], preferred_element_type=F32)


def _mix_xattn(x, groups, weights, nw, wq, kv, wo, bsz, tm=TOKEN_TILE):
    t, d = x.shape
    seq = t // bsz
    nt = seq // tm
    m = kv.shape[1]
    arrays = [a for grp in groups for a in grp]

    def tok(width):
        return pl.BlockSpec((tm, width), lambda b, i: (b * nt + i, 0))

    def const(shape):
        return pl.BlockSpec(shape, lambda b, i: (0, 0))

    return pl.pallas_call(
        functools.partial(_mix_xattn_kernel, tuple(len(grp) for grp in groups)),
        grid=(bsz, nt),
        in_specs=[tok(d)] + [tok(a.shape[1]) for a in arrays] + [const(w.shape) for w in weights]
        + [const((1, d)), const((d, d)), pl.BlockSpec((None, m, 2 * d), lambda b, i: (b, 0, 0)), const((d, d))],
        out_specs=tok(d),
        out_shape=jax.ShapeDtypeStruct((t, d), F32),
        compiler_params=_compiler_params(("parallel", "parallel")),
        name="mix_xattn",
    )(x, *arrays, *weights, nw, wq, kv, wo)


def _retention_kernel(q_ref, k_ref, v_ref, g_ref, cos_ref, sin_ref, dmat_ref, zeta_ref, xi_ref, decay_ref, o_ref, state_ref):
    @pl.when(pl.program_id(1) == 0)
    def _():
        state_ref[...] = jnp.zeros_like(state_ref)

    cos = cos_ref[...]
    sin = sin_ref[...]
    half = RET_DK // 2
    outs = []
    for h in range(RET_HEADS):
        qh = q_ref[:, h * RET_DK:(h + 1) * RET_DK]
        kh = k_ref[:, h * RET_DK:(h + 1) * RET_DK]
        qr = qh * cos + pltpu.roll(qh, half, 1) * sin
        kr = (kh * cos + pltpu.roll(kh, half, 1) * sin) * (RET_DK ** -0.5)
        vh = v_ref[:, h * RET_DV:(h + 1) * RET_DV].astype(BF16)
        scores = lax.dot_general(qr.astype(BF16), kr.astype(BF16), _NT, preferred_element_type=F32) * dmat_ref[h]
        inner = jnp.dot(scores.astype(BF16), vh, preferred_element_type=F32)
        state = state_ref[h]
        cross = jnp.dot((qr * xi_ref[h]).astype(BF16), state.astype(BF16), preferred_element_type=F32)
        kz = (kr * zeta_ref[h]).T.astype(BF16)
        state_ref[h] = decay_ref[h] * state + jnp.dot(kz, vh, preferred_element_type=F32)
        o = inner + cross
        o = o * lax.rsqrt(jnp.mean(o * o, axis=-1, keepdims=True) + NORM_EPS)
        gh = g_ref[:, h * RET_DV:(h + 1) * RET_DV]
        outs.append(gh * jax.nn.sigmoid(gh) * o)
    o_ref[...] = jnp.concatenate(outs, axis=-1).astype(o_ref.dtype)


def _retention_tables(seq):
    L = RET_CHUNK
    pos = jnp.arange(seq, dtype=F32)
    inv_freq = 1.0 / (RET_ROPE_BASE ** jnp.linspace(0.0, 1.0, RET_DK // 2, dtype=F32))
    ang = pos[:, None] * inv_freq[None, :]
    cos, sin = jnp.cos(ang), jnp.sin(ang)
    cos2 = jnp.concatenate([cos, cos], axis=-1)
    sin2 = jnp.concatenate([-sin, sin], axis=-1)
    log_g = jnp.log1p(-jnp.exp2(-5.0 - jnp.arange(RET_HEADS, dtype=F32)))
    idx = jnp.arange(L, dtype=F32)
    diff = idx[:, None] - idx[None, :]
    dmat = jnp.where(diff >= 0, jnp.exp(jnp.maximum(diff, 0.0)[None] * log_g[:, None, None]), 0.0)
    ones = jnp.ones((RET_HEADS, L, RET_DK), F32)
    zeta = jnp.exp((L - 1 - idx)[None, :] * log_g[:, None])[:, :, None] * ones
    xi = jnp.exp((idx + 1)[None, :] * log_g[:, None])[:, :, None] * ones
    decay = jnp.exp(L * log_g)[:, None, None] * ones
    return cos2, sin2, dmat, zeta, xi, decay


CONV_PAD = 8


def _ssd_kernel(xbc_ref, dt_ref, z_ref, convw_ref, convb_ref, dtb_ref, alog_ref, dskip_ref, nw_ref, tri_ref, o_ref, ext_ref, state_ref):
    L = SSM_CHUNK
    hp = SSM_HEAD_DIM
    ns = SSM_D_STATE

    @pl.when(pl.program_id(1) == 0)
    def _():
        ext_ref[0:CONV_PAD, :] = jnp.zeros((CONV_PAD, SSM_CONV_DIM), F32)
        state_ref[...] = jnp.zeros_like(state_ref)

    ext_ref[CONV_PAD:CONV_PAD + L, :] = xbc_ref[...]
    conv = convb_ref[...]
    for w in range(SSM_CONV):
        off = CONV_PAD - (SSM_CONV - 1) + w
        conv = conv + ext_ref[off:off + L, :] * convw_ref[w:w + 1, :]
    ext_ref[0:CONV_PAD, :] = ext_ref[L:L + CONV_PAD, :]
    xa = conv * jax.nn.sigmoid(conv)
    xs = xa[:, :SSM_D_INNER]
    bmat = xa[:, SSM_D_INNER:SSM_D_INNER + SSM_GROUPS * ns]
    cmat = xa[:, SSM_D_INNER + SSM_GROUPS * ns:]

    x = dt_ref[...] + dtb_ref[...]
    dt = jnp.maximum(x, 0.0) + jnp.log1p(jnp.exp(-jnp.abs(x)))
    lane = lax.broadcasted_iota(jnp.int32, (L, LANES), 1)
    dta = jnp.where(lane < SSM_HEADS, dt * -jnp.exp(alog_ref[...]), 0.0)
    cums = jnp.dot(tri_ref[...], dta, preferred_element_type=F32, precision=lax.Precision.HIGHEST)
    cums_t = cums.T
    last = cums[L - 1:L, :]
    decay_to_end = jnp.exp(last - cums)
    decay_from_start = jnp.exp(cums)
    chunk_decay = jnp.exp(last)
    causal = lax.broadcasted_iota(jnp.int32, (L, L), 0) >= lax.broadcasted_iota(jnp.int32, (L, L), 1)
    heads_per_group = SSM_HEADS // SSM_GROUPS
    ys = []
    for g in range(SSM_GROUPS):
        bg = bmat[:, g * ns:(g + 1) * ns].astype(BF16)
        cg = cmat[:, g * ns:(g + 1) * ns].astype(BF16)
        cb = lax.dot_general(cg, bg, _NT, preferred_element_type=F32)
        for e in range(heads_per_group):
            h = g * heads_per_group + e
            xh = xs[:, h * hp:(h + 1) * hp]
            xdt = xh * dt[:, h:h + 1]
            seg = cums[:, h:h + 1] - cums_t[h:h + 1, :]
            lmat = jnp.where(causal, jnp.exp(seg), 0.0)
            y_diag = jnp.dot((cb * lmat).astype(BF16), xdt.astype(BF16), preferred_element_type=F32)
            state = state_ref[h]
            y_off = lax.dot_general(cg, state.astype(BF16), _NT, preferred_element_type=F32) * decay_from_start[:, h:h + 1]
            xw = (xdt * decay_to_end[:, h:h + 1]).astype(BF16)
            new = lax.dot_general(xw, bg, (((0,), (0,)), ((), ())), preferred_element_type=F32)
            state_ref[h] = chunk_decay[:, h:h + 1] * state + new
            ys.append(y_diag + y_off + dskip_ref[:, h * hp:(h + 1) * hp] * xh)
    y = jnp.concatenate(ys, axis=-1)
    z = z_ref[...]
    o_ref[...] = _rms(y * (z * jax.nn.sigmoid(z)), nw_ref[...]).astype(o_ref.dtype)


N_RET_IN = 10
N_SSD_IN = 10


def _even_core_kernel(*refs):
    ret_in, ssd_in = refs[:N_RET_IN], refs[N_RET_IN:N_RET_IN + N_SSD_IN]
    o_ret_ref, o_ssd_ref, ret_state_ref, ext_ref, ssd_state_ref = refs[N_RET_IN + N_SSD_IN:]
    _retention_kernel(*ret_in, o_ret_ref, ret_state_ref)
    _ssd_kernel(*ssd_in, o_ssd_ref, ext_ref, ssd_state_ref)


def _even_core(q, k, v, g, xbc, dt_raw, z, conv_w, conv_b, dt_bias, a_log, d_skip, norm_w):
    bsz, seq, width = q.shape
    L = RET_CHUNK
    assert SSM_CHUNK == L
    tables = _retention_tables(seq)
    pad = LANES - SSM_HEADS
    params = [conv_w, conv_b.reshape(1, -1), jnp.pad(dt_bias, (0, pad)).reshape(1, LANES),
              jnp.pad(a_log, (0, pad)).reshape(1, LANES), jnp.repeat(d_skip, SSM_HEAD_DIM).reshape(1, -1),
              norm_w.reshape(1, -1), jnp.asarray(np.tril(np.ones((L, L), np.float32)))]

    def tok(w):
        return pl.BlockSpec((None, L, w), lambda b, c: (b, c, 0))

    rope = pl.BlockSpec((L, RET_DK), lambda b, c: (c, 0))
    const = pl.BlockSpec((RET_HEADS, L, RET_DK), lambda b, c: (0, 0, 0))
    return pl.pallas_call(
        _even_core_kernel,
        grid=(bsz, seq // L),
        in_specs=[tok(width)] * 4 + [rope, rope, const, const, const, const]
        + [tok(SSM_CONV_DIM), tok(LANES), tok(SSM_D_INNER)] + [pl.BlockSpec(p.shape, lambda b, c: (0, 0)) for p in params],
        out_specs=[tok(width), tok(SSM_D_INNER)],
        out_shape=[jax.ShapeDtypeStruct((bsz, seq, width), BF16), jax.ShapeDtypeStruct((bsz, seq, SSM_D_INNER), BF16)],
        scratch_shapes=[pltpu.VMEM((RET_HEADS, RET_DK, RET_DV), F32),
                        pltpu.VMEM((L + CONV_PAD, SSM_CONV_DIM), F32), pltpu.VMEM((SSM_HEADS, SSM_HEAD_DIM, SSM_D_STATE), F32)],
        compiler_params=_compiler_params(("parallel", "arbitrary")),
        name="even_core",
    )(q, k, v, g, *tables, xbc, dt_raw, z, *params)


def _even_mixer(xt, nw, w_in, w_out, conv_w, conv_b, dt_bias, a_log, d_skip, ssm_norm_w, bsz):
    t, d = xt.shape
    seq = t // bsz
    cuts = [int(c) for c in np.cumsum((0,) + EVEN_SPLITS)]
    ws = [w_in[:, a:b] for a, b in zip(cuts[:-1], cuts[1:])]
    perm = np.concatenate([h * RET_DK + np.concatenate([np.arange(0, RET_DK, 2), np.arange(1, RET_DK, 2)]) for h in range(RET_HEADS)])
    ws[0] = ws[0][:, perm]
    ws[1] = ws[1][:, perm]
    ws[6] = jnp.pad(ws[6], ((0, 0), (0, LANES - SSM_HEADS)))
    dtypes = (F32, F32, BF16, F32, F32, F32, F32)
    q, k, v, g, z, xbc, dt_raw = [a.reshape(bsz, seq, -1) for a in _norm_proj(xt, nw, ws, dtypes=dtypes)]
    o_ret, y = _even_core(q, k, v, g, xbc, dt_raw, z, conv_w, conv_b, dt_bias, a_log, d_skip, ssm_norm_w)
    n_ret = RET_HEADS * RET_DV
    return [[o_ret.reshape(t, n_ret)], [y.reshape(t, SSM_D_INNER)]], [w_out[:n_ret], w_out[n_ret:]]


Q_TILE = ATTN_BLOCK
GROUP_HEADS = NSA_HEADS // NSA_KV_HEADS
HEAD_DIM = NSA_HEAD_DIM
SEL_CHUNK = 512


def _cmp_kv_kernel(r_ref, pe_ref, w1_ref, w2_ref, o_ref):
    r = r_ref[...]
    half = NSA_CMP_STRIDE * HEAD_DIM
    top = jnp.dot((r + pe_ref[0:1, :]).astype(BF16), w1_ref[0:half, :], preferred_element_type=F32)
    bot = jnp.dot((r + pe_ref[1:2, :]).astype(BF16), w1_ref[half:2 * half, :], preferred_element_type=F32)
    hidden = top + pltpu.roll(bot, r.shape[0] - 1, 0)
    act = hidden * jax.nn.sigmoid(hidden)
    o_ref[...] = jnp.dot(act.astype(BF16), w2_ref[...], preferred_element_type=F32)


def _cmp_kv(r, pe, w1, w2):
    ng, bsz, n, width = r.shape
    return pl.pallas_call(
        _cmp_kv_kernel,
        grid=(ng, bsz),
        in_specs=[pl.BlockSpec((None, None, n, width), lambda g, b: (g, b, 0, 0)),
                  pl.BlockSpec(pe.shape, lambda g, b: (0, 0)),
                  pl.BlockSpec(w1.shape, lambda g, b: (0, 0)),
                  pl.BlockSpec(w2.shape, lambda g, b: (0, 0))],
        out_specs=pl.BlockSpec((None, None, n, HEAD_DIM), lambda g, b: (g, b, 0, 0)),
        out_shape=jax.ShapeDtypeStruct((ng, bsz, n, HEAD_DIM), F32),
        compiler_params=_compiler_params(("parallel", "parallel")),
        name="nsa_compress",
    )(r, pe, w1, w2)


def _heads_from_t(o_t):
    pairs = []
    for e in range(0, GROUP_HEADS, 2):
        pair = jnp.concatenate([o_t[:, e * Q_TILE:(e + 1) * Q_TILE], o_t[:, (e + 1) * Q_TILE:(e + 2) * Q_TILE]], axis=0)
        pairs.append(pair.T)
    return jnp.concatenate(pairs, axis=-1)


def _gate_rows(gate_ref, g):
    gate = jax.nn.sigmoid(gate_ref[g])
    gate = jnp.concatenate([gate, jnp.zeros((Q_TILE, LANES - gate.shape[1]), F32)], axis=1)
    return gate.T


def _gate_cols(gate_t, branch):
    return jnp.concatenate([gate_t[3 * e + branch:3 * e + branch + 1, :] for e in range(GROUP_HEADS)], axis=1)


def _tile_heads(x):
    return jnp.concatenate([x] * GROUP_HEADS, axis=1)


def _pv_t(v, p):
    return lax.dot_general(v, p.astype(BF16), (((0,), (0,)), ((), ())), preferred_element_type=F32)


ACC_ROWS = HEAD_DIM + 8


def _with_ones(v):
    pad = jnp.zeros(v.shape[:-1] + (LANES - HEAD_DIM - 1,), v.dtype)
    return jnp.concatenate([v, jnp.ones(v.shape[:-1] + (1,), v.dtype), pad], axis=-1)


def _band_window(window):
    return (-(-(window - 1) // Q_TILE) + 1) * Q_TILE


def _band_branch(i, q, k_ref, v_ref, g, window, sink=None):
    n_keys = _band_window(window)
    start = pl.multiple_of(jnp.maximum(i * Q_TILE + Q_TILE - n_keys, 0), Q_TILE)
    rel = (i * Q_TILE + lax.broadcasted_iota(jnp.int32, (n_keys, Q_TILE), 1)) - (start + lax.broadcasted_iota(jnp.int32, (n_keys, Q_TILE), 0))
    bias = _tile_heads(jnp.where(rel >= 0, jnp.where(rel < window, 0.0, MASKED), MASKED))
    s = lax.dot_general(k_ref[g, pl.ds(start, n_keys), :], q, _NT, preferred_element_type=F32) + bias
    m = jnp.max(s, axis=0, keepdims=True)
    if sink is not None:
        m = jnp.maximum(m, sink)
    p = jnp.exp2(s - m)
    denom = jnp.sum(p, axis=0, keepdims=True)
    denom = denom + jnp.exp2(sink - m) if sink is not None else jnp.maximum(denom, 1e-30)
    return _pv_t(v_ref[g, pl.ds(start, n_keys), :], p) * (1.0 / denom)


def _nsa_dense_kernel(q_ref, kc_ref, vc_ref, kw_ref, vw_ref, gate_ref, ov_ref, qs_ref, ks_ref, vs_ref, sink_ref,
                      o_ref, sel_ref, swa_ref):
    i = pl.program_id(1)
    rows = GROUP_HEADS * Q_TILE
    width = GROUP_HEADS * HEAD_DIM
    n_cmp = kc_ref.shape[1]
    n = lax.broadcasted_iota(jnp.int32, (n_cmp, Q_TILE), 0)
    t = i * Q_TILE + lax.broadcasted_iota(jnp.int32, (n_cmp, Q_TILE), 1)
    valid = _tile_heads(jnp.where(n * NSA_CMP_STRIDE + (NSA_CMP_BLOCK - 1) <= t, 1.0, 0.0))
    bias = (valid - 1.0) * (-MASKED)
    n_sel = ov_ref.shape[0]
    jb = lax.broadcasted_iota(jnp.int32, (n_sel, Q_TILE), 0)
    cur = (i * Q_TILE + lax.broadcasted_iota(jnp.int32, (n_sel, Q_TILE), 1)) // NSA_SEL_BLOCK
    valid_b = jb <= cur
    forced = jnp.where(valid_b, jnp.where(jb == 0, 1.0, 0.0) + jnp.where(jb == cur, 1.0, 0.0) + jnp.where(jb == cur - 1, 1.0, 0.0), 0.0)
    for g in range(NSA_KV_HEADS):
        q = q_ref[g * GROUP_HEADS:(g + 1) * GROUP_HEADS].reshape(rows, HEAD_DIM)
        s = lax.dot_general(kc_ref[g].astype(BF16), q, _NT, preferred_element_type=F32) + bias
        m = jnp.max(s, axis=0, keepdims=True)
        p = jnp.exp2(s - m) * valid
        p = p * (1.0 / jnp.maximum(jnp.sum(p, axis=0, keepdims=True), 1e-30))
        o_cmp = _pv_t(vc_ref[g].astype(BF16), p)
        o_win = _band_branch(i, q, kw_ref, vw_ref, g, NSA_WINDOW)
        gate_t = _gate_rows(gate_ref, g)
        o_ref[:, g * width:(g + 1) * width] = _heads_from_t(o_cmp * _gate_cols(gate_t, 0) + o_win * _gate_cols(gate_t, 2)).astype(o_ref.dtype)

        p_sum = p[:, 0:Q_TILE]
        for e in range(1, GROUP_HEADS):
            p_sum = p_sum + p[:, e * Q_TILE:(e + 1) * Q_TILE]
        imp = jnp.dot(ov_ref[...], p_sum, preferred_element_type=F32, precision=lax.Precision.HIGHEST)
        score = jnp.where(forced > 0.5, NSA_FORCE_SCORE, jnp.where(valid_b, imp, -jnp.inf))
        sub = 8
        ranks = []
        for v in range(n_sel // sub):
            blk_scores = score[v * sub:(v + 1) * sub, :]
            jb_v = jb[v * sub:(v + 1) * sub, :]
            rank = jnp.zeros((sub, Q_TILE), F32)
            for ii in range(n_sel):
                row = score[ii:ii + 1, :]
                if ii < v * sub:
                    ahead = jnp.where(row >= blk_scores, 1.0, 0.0)
                elif ii >= (v + 1) * sub:
                    ahead = jnp.where(row > blk_scores, 1.0, 0.0)
                else:
                    tie = jnp.where(jb_v > ii, 1.0, 0.0)
                    ahead = jnp.where(row > blk_scores, 1.0, jnp.where(row == blk_scores, tie, 0.0))
                rank = rank + ahead
            ranks.append(rank)
        rank = jnp.concatenate(ranks, axis=0)
        sel_t = jnp.where(valid_b, jnp.where(rank < NSA_TOPN, 1.0, 0.0), 0.0)
        sel_t = jnp.concatenate([sel_t, jnp.zeros((LANES - n_sel, Q_TILE), F32)], axis=0)
        sel_ref[g] = sel_t.T.astype(BF16)

    for g in range(SWA_KV_HEADS):
        q = qs_ref[g * GROUP_HEADS:(g + 1) * GROUP_HEADS].reshape(rows, HEAD_DIM)
        sink = LOG2_E * jnp.concatenate([jnp.broadcast_to(sink_ref[g][:, e:e + 1], (1, Q_TILE)) for e in range(GROUP_HEADS)], axis=1)
        o_swa = _band_branch(i, q, ks_ref, vs_ref, g, SWA_WINDOW, sink)
        swa_ref[:, g * width:(g + 1) * width] = _heads_from_t(o_swa).astype(swa_ref.dtype)


def _nsa_dense(q, k_cmp, v_cmp, kw, vw, gates, overlap_t, q_s, k_s, v_s, sinks):
    nh, bsz, seq, hd = q.shape
    ng, _, n_cmp, _ = k_cmp.shape
    heads = pl.BlockSpec((nh, None, Q_TILE, hd), lambda b, i: (0, b, i, 0))
    cmp_kv = pl.BlockSpec((ng, None, n_cmp, hd), lambda b, i: (0, b, 0, 0))
    seq_kv = pl.BlockSpec((ng, None, seq, hd), lambda b, i: (0, b, 0, 0))
    out = pl.BlockSpec((None, Q_TILE, nh * hd), lambda b, i: (b, i, 0))
    return pl.pallas_call(
        _nsa_dense_kernel,
        grid=(bsz, seq // Q_TILE),
        in_specs=[heads, cmp_kv, cmp_kv, seq_kv, seq_kv,
                  pl.BlockSpec((ng, None, Q_TILE, gates.shape[-1]), lambda b, i: (0, b, i, 0)),
                  pl.BlockSpec(overlap_t.shape, lambda b, i: (0, 0)),
                  heads, seq_kv, seq_kv,
                  pl.BlockSpec(sinks.shape, lambda b, i: (0, 0, 0))],
        out_specs=[out, pl.BlockSpec((ng, None, Q_TILE, LANES), lambda b, i: (0, b, i, 0)), out],
        out_shape=[jax.ShapeDtypeStruct((bsz, seq, nh * hd), BF16),
                   jax.ShapeDtypeStruct((ng, bsz, seq, LANES), BF16),
                   jax.ShapeDtypeStruct((bsz, seq, nh * hd), BF16)],
        compiler_params=_compiler_params(("parallel", "parallel")),
        name="nsa_dense",
    )(q, k_cmp, v_cmp, kw, vw, gates, overlap_t, q_s, k_s, v_s, sinks)


def _nsa_sel_kernel(q_ref, k_ref, v_ref, sel_ref, gate_ref, o_ref, s_ref, m_ref, acc_ref):
    i = pl.program_id(1)
    rows = GROUP_HEADS * Q_TILE
    width = GROUP_HEADS * HEAD_DIM
    n_blk = LANES - HEAD_DIM
    kc = SEL_CHUNK
    seq = k_ref.shape[1]
    groups = range(NSA_KV_HEADS)

    def scores(g, start, lhs):
        return lax.dot_general(k_ref[g, pl.ds(start, kc), :], lhs, _NT, preferred_element_type=F32)

    start_d = pl.multiple_of(jnp.maximum(i * Q_TILE + Q_TILE - kc, 0), Q_TILE)
    n_past = (start_d + kc - 1) // kc
    kpos = start_d + lax.broadcasted_iota(jnp.int32, (kc, rows), 0)
    qpos = i * Q_TILE + (lax.broadcasted_iota(jnp.int32, (kc, rows), 1) & (Q_TILE - 1))
    blk = lax.broadcasted_iota(jnp.int32, (Q_TILE, n_blk), 1)
    lhs_past, lhs_none = [], []
    for g in groups:
        q = q_ref[g * GROUP_HEADS:(g + 1) * GROUP_HEADS].reshape(rows, HEAD_DIM)
        sel = sel_ref[g][:, :n_blk].astype(F32)
        neg_diag = (sel - 1.0) * (-MASKED)
        neg_past = jnp.where(blk < start_d // NSA_SEL_BLOCK, neg_diag, MASKED)
        lhs_diag = jnp.concatenate([q, jnp.concatenate([neg_diag.astype(BF16)] * GROUP_HEADS, axis=0)], axis=-1)
        lhs_past.append(jnp.concatenate([q, jnp.concatenate([neg_past.astype(BF16)] * GROUP_HEADS, axis=0)], axis=-1))
        lhs_none.append(jnp.concatenate([q, jnp.full((rows, n_blk), MASKED, BF16)], axis=-1))
        s_ref[g, 0] = jnp.where(kpos <= qpos, scores(g, start_d, lhs_diag), MASKED)

        m_ref[g] = jnp.full((1, rows), MASKED, F32)
        acc_ref[g] = jnp.zeros((ACC_ROWS, rows), F32)

    def step(c, cur, nxt, lookahead=True):
        next_start = pl.multiple_of(jnp.minimum(c * kc, seq - kc), kc)
        v_start = pl.multiple_of(jnp.where(c == 0, start_d, (c - 1) * kc), Q_TILE)
        for g in groups:
            if lookahead:
                s_ref[g, nxt] = scores(g, next_start, jnp.where(c < n_past, lhs_past[g], lhs_none[g]))
            s = s_ref[g, cur]
            m = m_ref[g]
            m_new = jnp.maximum(m, jnp.max(s, axis=0, keepdims=True))
            alpha = jnp.exp2(m - m_new)
            p = jnp.exp2(s - m_new)
            m_ref[g] = m_new
            acc_ref[g] = alpha * acc_ref[g] + _pv_t(v_ref[g, pl.ds(v_start, kc), :], p)[:ACC_ROWS]

    def body(j, carry):
        step(2 * j, 0, 1)
        step(2 * j + 1, 1, 0)
        return carry

    lax.fori_loop(0, (n_past + 1) // 2, body, 0)

    @pl.when(n_past % 2 == 0)
    def _():
        step(n_past, 0, 1, lookahead=False)

    for g in groups:
        gate = _gate_cols(_gate_rows(gate_ref, g), 1)
        acc = acc_ref[g]
        o_ref[:, g * width:(g + 1) * width] = _heads_from_t(acc[:HEAD_DIM] * (gate / acc[HEAD_DIM:HEAD_DIM + 1])).astype(o_ref.dtype)


def _nsa_sel(q, k, v, sel, gates):
    nh, bsz, seq, hd = q.shape
    ng = k.shape[0]
    onehot = (np.arange(seq)[:, None] // NSA_SEL_BLOCK == np.arange(LANES - hd)[None, :]).astype(np.float32)
    k_ext = jnp.concatenate([k, jnp.broadcast_to(jnp.asarray(onehot, BF16), (ng, bsz, seq, LANES - hd))], axis=-1)
    return pl.pallas_call(
        _nsa_sel_kernel,
        grid=(bsz, seq // Q_TILE),
        in_specs=[pl.BlockSpec((nh, None, Q_TILE, hd), lambda b, i: (0, b, i, 0)),
                  pl.BlockSpec((ng, None, seq, LANES), lambda b, i: (0, b, 0, 0)),
                  pl.BlockSpec((ng, None, seq, LANES), lambda b, i: (0, b, 0, 0)),
                  pl.BlockSpec((ng, None, Q_TILE, LANES), lambda b, i: (0, b, i, 0)),
                  pl.BlockSpec((ng, None, Q_TILE, gates.shape[-1]), lambda b, i: (0, b, i, 0))],
        out_specs=pl.BlockSpec((None, Q_TILE, nh * hd), lambda b, i: (b, i, 0)),
        out_shape=jax.ShapeDtypeStruct((bsz, seq, nh * hd), BF16),
        scratch_shapes=[pltpu.VMEM((ng, 2, SEL_CHUNK, GROUP_HEADS * Q_TILE), F32),
                        pltpu.VMEM((ng, 1, GROUP_HEADS * Q_TILE), F32),
                        pltpu.VMEM((ng, ACC_ROWS, GROUP_HEADS * Q_TILE), F32)],
        compiler_params=_compiler_params(("parallel", "parallel")),
        name="nsa_sel",
    )(q, k_ext, _with_ones(v), sel, gates)


def _overlap_t(seq):
    n_cmp = (seq - NSA_CMP_BLOCK) // NSA_CMP_STRIDE + 1
    n_pad = seq // NSA_CMP_STRIDE
    starts = np.arange(n_pad) * NSA_CMP_STRIDE
    sel_start = np.arange(seq // NSA_SEL_BLOCK) * NSA_SEL_BLOCK
    ov = (starts[None, :] < sel_start[:, None] + NSA_SEL_BLOCK) & (starts[None, :] + NSA_CMP_BLOCK > sel_start[:, None])
    ov = ov & (np.arange(n_pad)[None, :] < n_cmp)
    return jnp.asarray(ov.astype(np.float32))


def _odd_mixer(xt, nw, w_in, w_out, cmp_pe, cmp_w1, cmp_w2, sinks, bsz):
    t, d = xt.shape
    seq = t // bsz
    assert seq // NSA_SEL_BLOCK <= LANES and seq % SEL_CHUNK == 0
    cuts = [int(c) for c in np.cumsum((0,) + ODD_SPLITS)]
    ws = [w_in[:, a:b] for a, b in zip(cuts[:-1], cuts[1:])]
    ws[0] = ws[0] * (HEAD_DIM ** -0.5 * LOG2_E)
    ws[8] = ws[8] * (HEAD_DIM ** -0.5 * LOG2_E)
    ws = [w.astype(BF16) for w in ws]
    hd = HEAD_DIM
    head_dims = (hd, hd, hd, hd, hd, hd, hd, 3 * GROUP_HEADS, hd, hd, hd)
    dtypes = (BF16, F32, F32, BF16, BF16, BF16, BF16, F32, BF16, BF16, BF16)
    q_n, kc, vc, ks, vs, kw, vw, gates, q_s, k_s, v_s = _norm_proj(xt, nw, ws, head_dims, dtypes)

    def heads(a):
        return a.reshape(a.shape[0], bsz, seq, a.shape[-1])

    q_n, ks, vs, kw, vw, gates, q_s, k_s, v_s = map(heads, (q_n, ks, vs, kw, vw, gates, q_s, k_s, v_s))
    ng = NSA_KV_HEADS
    rows16 = NSA_CMP_STRIDE * hd
    pe = cmp_pe.reshape(2, 2, rows16)
    w1 = cmp_w1.astype(BF16)
    w2 = cmp_w2.astype(BF16)
    k_cmp = _cmp_kv(kc.reshape(ng, bsz, seq // NSA_CMP_STRIDE, rows16), pe[0], w1[0], w2[0])
    v_cmp = _cmp_kv(vc.reshape(ng, bsz, seq // NSA_CMP_STRIDE, rows16), pe[1], w1[1], w2[1])
    o_cw, sel, o_swa = _nsa_dense(q_n, k_cmp, v_cmp, kw, vw, gates, _overlap_t(seq), q_s, k_s, v_s,
                                  sinks.reshape(SWA_KV_HEADS, 1, GROUP_HEADS))
    o_sel = _nsa_sel(q_n, ks, vs, sel, gates)
    n_nsa = NSA_HEADS * hd
    return [[o_cw.reshape(t, n_nsa), o_sel.reshape(t, n_nsa)], [o_swa.reshape(t, SWA_HEADS * hd)]], [w_out[:n_nsa], w_out[n_nsa:]]


def kernel(x, mem, norm_w, final_norm_w, mem_norm_w, ffn_w_gate, ffn_w_up, ffn_w_down, xattn_wq, xattn_wkv, xattn_wo, even_w_in, even_w_out, ssm_conv_w, ssm_conv_b, ssm_dt_bias, ssm_a_log, ssm_d, ssm_norm_w, odd_w_in, odd_w_out, nsa_cmp_pe, nsa_cmp_w1, nsa_cmp_w2, swa_sinks):
    bsz, seq, d = x.shape
    t = bsz * seq
    m = mem.shape[1]
    wg, wu, wd = ffn_w_gate, ffn_w_up, ffn_w_down
    wq = xattn_wq.astype(BF16)
    wkv = xattn_wkv.astype(BF16)
    wo = xattn_wo.astype(BF16)
    even_in = even_w_in.astype(BF16)
    even_out = even_w_out.astype(BF16)
    odd_out = odd_w_out.astype(BF16)

    xt = x.reshape(t, d)
    mem2 = mem.reshape(bsz * m, d)
    mem_nw = mem_norm_w.reshape(1, d)
    for layer in range(DEPTH):
        i = layer // 2
        xt = _ffn(xt, norm_w[layer, 0].reshape(1, d), wg, wu, wd, layer, 0)
        nw1 = norm_w[layer, 1].reshape(1, d)
        if layer % 2 == 0:
            mix, mix_w = _even_mixer(xt, nw1, even_in[i], even_out[i], ssm_conv_w[i], ssm_conv_b[i], ssm_dt_bias[i], ssm_a_log[i], ssm_d[i], ssm_norm_w[i], bsz)
        else:
            mix, mix_w = _odd_mixer(xt, nw1, odd_w_in[i], odd_out[i], nsa_cmp_pe[i], nsa_cmp_w1[i], nsa_cmp_w2[i], swa_sinks[i], bsz)
        (kv,) = _norm_proj(mem2, mem_nw, [wkv[layer]])
        xt = _mix_xattn(xt, mix, mix_w, norm_w[layer, 2].reshape(1, d), wq[layer], kv.reshape(bsz, m, 2 * d), wo[layer], bsz)
        out_nw = final_norm_w.reshape(1, d) if layer == DEPTH - 1 else None
        xt = _ffn(xt, norm_w[layer, 3].reshape(1, d), wg, wu, wd, layer, 1, out_nw)
    return xt.reshape(bsz, seq, d)
```

```python
import functools

import numpy as np
import jax
import jax.numpy as jnp
from jax import lax
from jax.experimental import pallas as pl
from jax.experimental.pallas import tpu as pltpu

F32 = jnp.float32
BF16 = jnp.bfloat16

D_MODEL = 1024
DEPTH = 4
D_FF = 2816
NORM_EPS = 1e-6
ATTN_BLOCK = 128

RET_HEADS = 4
RET_DK = 128
RET_DV = 128
RET_CHUNK = 128
RET_ROPE_BASE = 10000.0

SSM_HEADS = 8
SSM_HEAD_DIM = 64
SSM_D_INNER = SSM_HEADS * SSM_HEAD_DIM
SSM_D_STATE = 64
SSM_GROUPS = 2
SSM_CONV = 4
SSM_CHUNK = 128
SSM_CONV_DIM = SSM_D_INNER + 2 * SSM_GROUPS * SSM_D_STATE

EVEN_SPLITS = (RET_HEADS * RET_DK, RET_HEADS * RET_DK, RET_HEADS * RET_DV, RET_HEADS * RET_DV, SSM_D_INNER, SSM_CONV_DIM, SSM_HEADS)

NSA_HEADS = 8
NSA_KV_HEADS = 2
NSA_HEAD_DIM = 64
NSA_CMP_BLOCK = 32
NSA_CMP_STRIDE = 16
NSA_CMP_HIDDEN = 256
NSA_SEL_BLOCK = 64
NSA_TOPN = 16
NSA_WINDOW = 512
NSA_FORCE_SCORE = 1e6

SWA_HEADS = 8
SWA_KV_HEADS = 2
SWA_HEAD_DIM = 64
SWA_WINDOW = 128

ODD_SPLITS = (NSA_HEADS * NSA_HEAD_DIM,) + (NSA_KV_HEADS * NSA_HEAD_DIM,) * 6 + (3 * NSA_HEADS, SWA_HEADS * SWA_HEAD_DIM, SWA_KV_HEADS * SWA_HEAD_DIM, SWA_KV_HEADS * SWA_HEAD_DIM)

XATTN_HEADS = 4
XATTN_HEAD_DIM = D_MODEL // XATTN_HEADS

VMEM_LIMIT_BYTES = 52 * 1024 * 1024
TOKEN_TILE = 512
FFN_TILE = 256
FFN_DOWN_CHUNK = 1024
LANES = 128
SUBLANES = 8
MASKED = -1e30
LOG2_E = float(np.log2(np.e))

_NT = (((1,), (1,)), ((), ()))


def _compiler_params(semantics):
    return pltpu.CompilerParams(dimension_semantics=semantics, vmem_limit_bytes=VMEM_LIMIT_BYTES)


def _rms(x, w):
    return x * lax.rsqrt(jnp.mean(x * x, axis=-1, keepdims=True) + NORM_EPS) * w


def _ffn_kernel(has_out_norm, x_ref, nw_ref, wg_ref, wu_ref, wd_ref, *refs):
    out_nw_ref = refs[0] if has_out_norm else None
    o_ref, a_ref = refs[-2:]
    x = x_ref[...]
    h = _rms(x, nw_ref[...]).astype(BF16)
    for c in range(D_FF // FFN_TILE):
        cols = slice(c * FFN_TILE, (c + 1) * FFN_TILE)
        g = jnp.dot(h, wg_ref[:, cols].astype(BF16), preferred_element_type=F32)
        u = jnp.dot(h, wu_ref[:, cols].astype(BF16), preferred_element_type=F32)
        a_ref[:, cols] = (g * jax.nn.sigmoid(g) * u).astype(BF16)
    acc = None
    for lo in range(0, D_FF, FFN_DOWN_CHUNK):
        hi = min(lo + FFN_DOWN_CHUNK, D_FF)
        part = jnp.dot(a_ref[:, lo:hi], wd_ref[lo:hi, :].astype(BF16), preferred_element_type=F32)
        acc = part if acc is None else acc + part
    y = x + 0.5 * acc
    o_ref[...] = _rms(y, out_nw_ref[...]) if has_out_norm else y


def _ffn(x, nw, wg, wu, wd, layer, k, out_nw=None):
    t, d = x.shape
    tm = TOKEN_TILE
    resident = pl.Buffered(1)
    vec = pl.BlockSpec((1, d), lambda i: (0, 0))
    return pl.pallas_call(
        functools.partial(_ffn_kernel, out_nw is not None),
        grid=(t // tm,),
        in_specs=[
            pl.BlockSpec((tm, d), lambda i: (i, 0)),
            vec,
            pl.BlockSpec((None, None, d, D_FF), lambda i: (layer, k, 0, 0), pipeline_mode=resident),
            pl.BlockSpec((None, None, d, D_FF), lambda i: (layer, k, 0, 0), pipeline_mode=resident),
            pl.BlockSpec((None, None, D_FF, d), lambda i: (layer, k, 0, 0), pipeline_mode=resident),
        ] + ([vec] if out_nw is not None else []),
        out_specs=pl.BlockSpec((tm, d), lambda i: (i, 0)),
        out_shape=jax.ShapeDtypeStruct((t, d), F32),
        scratch_shapes=[pltpu.VMEM((tm, D_FF), BF16)],
        compiler_params=_compiler_params(("parallel",)),
        name="ffn",
    )(x, nw, wg, wu, wd, *([out_nw] if out_nw is not None else []))


def _norm_proj_kernel(head_dims, offsets, x_ref, nw_ref, w_ref, *o_refs):
    h = _rms(x_ref[...], nw_ref[...]).astype(BF16)
    y_all = jnp.dot(h, w_ref[...], preferred_element_type=F32)
    for hd, off, o_ref in zip(head_dims, offsets, o_refs):
        if hd is None:
            o_ref[...] = y_all[:, off:off + o_ref.shape[-1]].astype(o_ref.dtype)
        else:
            for j in range(o_ref.shape[0]):
                o_ref[j] = y_all[:, off + j * hd:off + (j + 1) * hd].astype(o_ref.dtype)


def _norm_proj(x, nw, weights, head_dims=None, dtypes=None, tm=TOKEN_TILE):
    t, d = x.shape
    n_out = len(weights)
    head_dims = tuple(head_dims) if head_dims is not None else (None,) * n_out
    dtypes = tuple(dtypes) if dtypes is not None else (F32,) * n_out
    out_specs, out_shape, offsets, padded = [], [], [], []
    total = 0
    for w, hd, dt in zip(weights, head_dims, dtypes):
        n = w.shape[1]
        if hd is None:
            out_specs.append(pl.BlockSpec((tm, n), lambda i: (i, 0)))
            out_shape.append(jax.ShapeDtypeStruct((t, n), dt))
        else:
            out_specs.append(pl.BlockSpec((n // hd, tm, hd), lambda i: (0, i, 0)))
            out_shape.append(jax.ShapeDtypeStruct((n // hd, t, hd), dt))
        offsets.append(total)
        width = -(-n // LANES) * LANES
        padded.append(jnp.pad(w, ((0, 0), (0, width - n))))
        total += width
    w_all = jnp.concatenate(padded, axis=1) if n_out > 1 else padded[0]
    return pl.pallas_call(
        functools.partial(_norm_proj_kernel, head_dims, tuple(offsets)),
        grid=(t // tm,),
        in_specs=[pl.BlockSpec((tm, d), lambda i: (i, 0)), pl.BlockSpec((1, d), lambda i: (0, 0)),
                  pl.BlockSpec(w_all.shape, lambda i: (0, 0))],
        out_specs=out_specs,
        out_shape=out_shape,
        compiler_params=_compiler_params(("parallel",)),
        name="norm_proj",
    )(x, nw, w_all)


def _mix_xattn_kernel(group_sizes, x_ref, *refs):
    n_a = sum(group_sizes)
    n_g = len(group_sizes)
    a_refs, w_refs = refs[:n_a], refs[n_a:n_a + n_g]
    nw_ref, wq_ref, kv_ref, wo_ref, o_ref = refs[n_a + n_g:]
    x = x_ref[...]
    pos = 0
    for size, w_ref in zip(group_sizes, w_refs):
        a = a_refs[pos][...]
        for r in a_refs[pos + 1:pos + size]:
            a = a.astype(F32) + r[...].astype(F32)
        pos += size
        x = x + jnp.dot(a.astype(BF16), w_ref[...], preferred_element_type=F32)
    h = _rms(x, nw_ref[...]).astype(BF16)
    q = jnp.dot(h, wq_ref[...], preferred_element_type=F32).astype(BF16)
    hd = XATTN_HEAD_DIM
    outs = []
    for hh in range(XATTN_HEADS):
        k = kv_ref[:, hh * hd:(hh + 1) * hd].astype(BF16)
        v = kv_ref[:, D_MODEL + hh * hd:D_MODEL + (hh + 1) * hd].astype(BF16)
        s = lax.dot_general(q[:, hh * hd:(hh + 1) * hd], k, _NT, preferred_element_type=F32)
        s = s * (hd ** -0.5)
        p = jnp.exp(s - jnp.max(s, axis=-1, keepdims=True))
        p = p / jnp.sum(p, axis=-1, keepdims=True)
        outs.append(jnp.dot(p.astype(BF16), v, preferred_element_type=F32).astype(BF16))
    o = jnp.concatenate(outs, axis=-1)
    o_ref[...] = x + jnp.dot(o, wo_ref[...], preferred_element_type=F32)


def _mix_xattn(x, groups, weights, nw, wq, kv, wo, bsz, tm=TOKEN_TILE):
    t, d = x.shape
    seq = t // bsz
    nt = seq // tm
    m = kv.shape[1]
    arrays = [a for grp in groups for a in grp]

    def tok(width):
        return pl.BlockSpec((tm, width), lambda b, i: (b * nt + i, 0))

    def const(shape):
        return pl.BlockSpec(shape, lambda b, i: (0, 0))

    return pl.pallas_call(
        functools.partial(_mix_xattn_kernel, tuple(len(grp) for grp in groups)),
        grid=(bsz, nt),
        in_specs=[tok(d)] + [tok(a.shape[1]) for a in arrays] + [const(w.shape) for w in weights]
        + [const((1, d)), const((d, d)), pl.BlockSpec((None, m, 2 * d), lambda b, i: (b, 0, 0)), const((d, d))],
        out_specs=tok(d),
        out_shape=jax.ShapeDtypeStruct((t, d), F32),
        compiler_params=_compiler_params(("parallel", "parallel")),
        name="mix_xattn",
    )(x, *arrays, *weights, nw, wq, kv, wo)


def _retention_kernel(q_ref, k_ref, v_ref, g_ref, cos_ref, sin_ref, dmat_ref, zeta_ref, xi_ref, decay_ref, o_ref, state_ref):
    @pl.when(pl.program_id(1) == 0)
    def _():
        state_ref[...] = jnp.zeros_like(state_ref)

    cos = cos_ref[...]
    sin = sin_ref[...]
    half = RET_DK // 2
    outs = []
    for h in range(RET_HEADS):
        qh = q_ref[:, h * RET_DK:(h + 1) * RET_DK]
        kh = k_ref[:, h * RET_DK:(h + 1) * RET_DK]
        qr = qh * cos + pltpu.roll(qh, half, 1) * sin
        kr = (kh * cos + pltpu.roll(kh, half, 1) * sin) * (RET_DK ** -0.5)
        vh = v_ref[:, h * RET_DV:(h + 1) * RET_DV].astype(BF16)
        scores = lax.dot_general(qr.astype(BF16), kr.astype(BF16), _NT, preferred_element_type=F32) * dmat_ref[h]
        inner = jnp.dot(scores.astype(BF16), vh, preferred_element_type=F32)
        state = state_ref[h]
        cross = jnp.dot((qr * xi_ref[h]).astype(BF16), state.astype(BF16), preferred_element_type=F32)
        kz = (kr * zeta_ref[h]).T.astype(BF16)
        state_ref[h] = decay_ref[h] * state + jnp.dot(kz, vh, preferred_element_type=F32)
        o = inner + cross
        o = o * lax.rsqrt(jnp.mean(o * o, axis=-1, keepdims=True) + NORM_EPS)
        gh = g_ref[:, h * RET_DV:(h + 1) * RET_DV]
        outs.append(gh * jax.nn.sigmoid(gh) * o)
    o_ref[...] = jnp.concatenate(outs, axis=-1).astype(o_ref.dtype)


def _retention_tables(seq):
    L = RET_CHUNK
    pos = jnp.arange(seq, dtype=F32)
    inv_freq = 1.0 / (RET_ROPE_BASE ** jnp.linspace(0.0, 1.0, RET_DK // 2, dtype=F32))
    ang = pos[:, None] * inv_freq[None, :]
    cos, sin = jnp.cos(ang), jnp.sin(ang)
    cos2 = jnp.concatenate([cos, cos], axis=-1)
    sin2 = jnp.concatenate([-sin, sin], axis=-1)
    log_g = jnp.log1p(-jnp.exp2(-5.0 - jnp.arange(RET_HEADS, dtype=F32)))
    idx = jnp.arange(L, dtype=F32)
    diff = idx[:, None] - idx[None, :]
    dmat = jnp.where(diff >= 0, jnp.exp(jnp.maximum(diff, 0.0)[None] * log_g[:, None, None]), 0.0)
    ones = jnp.ones((RET_HEADS, L, RET_DK), F32)
    zeta = jnp.exp((L - 1 - idx)[None, :] * log_g[:, None])[:, :, None] * ones
    xi = jnp.exp((idx + 1)[None, :] * log_g[:, None])[:, :, None] * ones
    decay = jnp.exp(L * log_g)[:, None, None] * ones
    return cos2, sin2, dmat, zeta, xi, decay


CONV_PAD = 8


def _ssd_kernel(xbc_ref, dt_ref, z_ref, convw_ref, convb_ref, dtb_ref, alog_ref, dskip_ref, nw_ref, tri_ref, o_ref, ext_ref, state_ref):
    L = SSM_CHUNK
    hp = SSM_HEAD_DIM
    ns = SSM_D_STATE

    @pl.when(pl.program_id(1) == 0)
    def _():
        ext_ref[0:CONV_PAD, :] = jnp.zeros((CONV_PAD, SSM_CONV_DIM), F32)
        state_ref[...] = jnp.zeros_like(state_ref)

    ext_ref[CONV_PAD:CONV_PAD + L, :] = xbc_ref[...]
    conv = convb_ref[...]
    for w in range(SSM_CONV):
        off = CONV_PAD - (SSM_CONV - 1) + w
        conv = conv + ext_ref[off:off + L, :] * convw_ref[w:w + 1, :]
    ext_ref[0:CONV_PAD, :] = ext_ref[L:L + CONV_PAD, :]
    xa = conv * jax.nn.sigmoid(conv)
    xs = xa[:, :SSM_D_INNER]
    bmat = xa[:, SSM_D_INNER:SSM_D_INNER + SSM_GROUPS * ns]
    cmat = xa[:, SSM_D_INNER + SSM_GROUPS * ns:]

    x = dt_ref[...] + dtb_ref[...]
    dt = jnp.maximum(x, 0.0) + jnp.log1p(jnp.exp(-jnp.abs(x)))
    lane = lax.broadcasted_iota(jnp.int32, (L, LANES), 1)
    dta = jnp.where(lane < SSM_HEADS, dt * -jnp.exp(alog_ref[...]), 0.0)
    cums = jnp.dot(tri_ref[...], dta, preferred_element_type=F32, precision=lax.Precision.HIGHEST)
    cums_t = cums.T
    last = cums[L - 1:L, :]
    decay_to_end = jnp.exp(last - cums)
    decay_from_start = jnp.exp(cums)
    chunk_decay = jnp.exp(last)
    causal = lax.broadcasted_iota(jnp.int32, (L, L), 0) >= lax.broadcasted_iota(jnp.int32, (L, L), 1)
    heads_per_group = SSM_HEADS // SSM_GROUPS
    ys = []
    for g in range(SSM_GROUPS):
        bg = bmat[:, g * ns:(g + 1) * ns].astype(BF16)
        cg = cmat[:, g * ns:(g + 1) * ns].astype(BF16)
        cb = lax.dot_general(cg, bg, _NT, preferred_element_type=F32)
        for e in range(heads_per_group):
            h = g * heads_per_group + e
            xh = xs[:, h * hp:(h + 1) * hp]
            xdt = xh * dt[:, h:h + 1]
            seg = cums[:, h:h + 1] - cums_t[h:h + 1, :]
            lmat = jnp.where(causal, jnp.exp(seg), 0.0)
            y_diag = jnp.dot((cb * lmat).astype(BF16), xdt.astype(BF16), preferred_element_type=F32)
            state = state_ref[h]
            y_off = lax.dot_general(cg, state.astype(BF16), _NT, preferred_element_type=F32) * decay_from_start[:, h:h + 1]
            xw = (xdt * decay_to_end[:, h:h + 1]).astype(BF16)
            new = lax.dot_general(xw, bg, (((0,), (0,)), ((), ())), preferred_element_type=F32)
            state_ref[h] = chunk_decay[:, h:h + 1] * state + new
            ys.append(y_diag + y_off + dskip_ref[:, h * hp:(h + 1) * hp] * xh)
    y = jnp.concatenate(ys, axis=-1)
    z = z_ref[...]
    o_ref[...] = _rms(y * (z * jax.nn.sigmoid(z)), nw_ref[...]).astype(o_ref.dtype)


N_RET_IN = 10
N_SSD_IN = 10


def _even_core_kernel(*refs):
    ret_in, ssd_in = refs[:N_RET_IN], refs[N_RET_IN:N_RET_IN + N_SSD_IN]
    o_ret_ref, o_ssd_ref, ret_state_ref, ext_ref, ssd_state_ref = refs[N_RET_IN + N_SSD_IN:]
    _retention_kernel(*ret_in, o_ret_ref, ret_state_ref)
    _ssd_kernel(*ssd_in, o_ssd_ref, ext_ref, ssd_state_ref)


def _even_core(q, k, v, g, xbc, dt_raw, z, conv_w, conv_b, dt_bias, a_log, d_skip, norm_w):
    bsz, seq, width = q.shape
    L = RET_CHUNK
    assert SSM_CHUNK == L
    tables = _retention_tables(seq)
    pad = LANES - SSM_HEADS
    params = [conv_w, conv_b.reshape(1, -1), jnp.pad(dt_bias, (0, pad)).reshape(1, LANES),
              jnp.pad(a_log, (0, pad)).reshape(1, LANES), jnp.repeat(d_skip, SSM_HEAD_DIM).reshape(1, -1),
              norm_w.reshape(1, -1), jnp.asarray(np.tril(np.ones((L, L), np.float32)))]

    def tok(w):
        return pl.BlockSpec((None, L, w), lambda b, c: (b, c, 0))

    rope = pl.BlockSpec((L, RET_DK), lambda b, c: (c, 0))
    const = pl.BlockSpec((RET_HEADS, L, RET_DK), lambda b, c: (0, 0, 0))
    return pl.pallas_call(
        _even_core_kernel,
        grid=(bsz, seq // L),
        in_specs=[tok(width)] * 4 + [rope, rope, const, const, const, const]
        + [tok(SSM_CONV_DIM), tok(LANES), tok(SSM_D_INNER)] + [pl.BlockSpec(p.shape, lambda b, c: (0, 0)) for p in params],
        out_specs=[tok(width), tok(SSM_D_INNER)],
        out_shape=[jax.ShapeDtypeStruct((bsz, seq, width), BF16), jax.ShapeDtypeStruct((bsz, seq, SSM_D_INNER), BF16)],
        scratch_shapes=[pltpu.VMEM((RET_HEADS, RET_DK, RET_DV), F32),
                        pltpu.VMEM((L + CONV_PAD, SSM_CONV_DIM), F32), pltpu.VMEM((SSM_HEADS, SSM_HEAD_DIM, SSM_D_STATE), F32)],
        compiler_params=_compiler_params(("parallel", "arbitrary")),
        name="even_core",
    )(q, k, v, g, *tables, xbc, dt_raw, z, *params)


def _even_mixer(xt, nw, w_in, w_out, conv_w, conv_b, dt_bias, a_log, d_skip, ssm_norm_w, bsz):
    t, d = xt.shape
    seq = t // bsz
    cuts = [int(c) for c in np.cumsum((0,) + EVEN_SPLITS)]
    ws = [w_in[:, a:b] for a, b in zip(cuts[:-1], cuts[1:])]
    perm = np.concatenate([h * RET_DK + np.concatenate([np.arange(0, RET_DK, 2), np.arange(1, RET_DK, 2)]) for h in range(RET_HEADS)])
    ws[0] = ws[0][:, perm]
    ws[1] = ws[1][:, perm]
    ws[6] = jnp.pad(ws[6], ((0, 0), (0, LANES - SSM_HEADS)))
    dtypes = (F32, F32, BF16, F32, F32, F32, F32)
    q, k, v, g, z, xbc, dt_raw = [a.reshape(bsz, seq, -1) for a in _norm_proj(xt, nw, ws, dtypes=dtypes)]
    o_ret, y = _even_core(q, k, v, g, xbc, dt_raw, z, conv_w, conv_b, dt_bias, a_log, d_skip, ssm_norm_w)
    n_ret = RET_HEADS * RET_DV
    return [[o_ret.reshape(t, n_ret)], [y.reshape(t, SSM_D_INNER)]], [w_out[:n_ret], w_out[n_ret:]]


Q_TILE = ATTN_BLOCK
GROUP_HEADS = NSA_HEADS // NSA_KV_HEADS
HEAD_DIM = NSA_HEAD_DIM
SEL_CHUNK = 512


def _cmp_kv_kernel(x_ref, pe_ref, w1_ref, w2_ref, o_ref):
    n = x_ref.shape[0] // NSA_CMP_STRIDE
    top = bot = None
    for r in range(NSA_CMP_STRIDE):
        rows = x_ref[pl.ds(r, n, stride=NSA_CMP_STRIDE), :]
        lo = r * HEAD_DIM
        hi = (NSA_CMP_STRIDE + r) * HEAD_DIM
        a = jnp.dot((rows + pe_ref[r:r + 1, :]).astype(BF16), w1_ref[lo:lo + HEAD_DIM, :], preferred_element_type=F32)
        b = jnp.dot((rows + pe_ref[NSA_CMP_STRIDE + r:NSA_CMP_STRIDE + r + 1, :]).astype(BF16), w1_ref[hi:hi + HEAD_DIM, :], preferred_element_type=F32)
        top = a if top is None else top + a
        bot = b if bot is None else bot + b
    hidden = top + pltpu.roll(bot, n - 1, 0)
    act = hidden * jax.nn.sigmoid(hidden)
    o_ref[...] = jnp.dot(act.astype(BF16), w2_ref[...], preferred_element_type=F32)


def _cmp_kv(x, pe, w1, w2):
    ng, bsz, seq, hd = x.shape
    n = seq // NSA_CMP_STRIDE
    return pl.pallas_call(
        _cmp_kv_kernel,
        grid=(ng, bsz),
        in_specs=[pl.BlockSpec((None, None, seq, hd), lambda g, b: (g, b, 0, 0)),
                  pl.BlockSpec(pe.shape, lambda g, b: (0, 0)),
                  pl.BlockSpec(w1.shape, lambda g, b: (0, 0)),
                  pl.BlockSpec(w2.shape, lambda g, b: (0, 0))],
        out_specs=pl.BlockSpec((None, None, n, hd), lambda g, b: (g, b, 0, 0)),
        out_shape=jax.ShapeDtypeStruct((ng, bsz, n, hd), F32),
        compiler_params=_compiler_params(("parallel", "parallel")),
        name="nsa_compress",
    )(x, pe, w1, w2)


def _heads_from_t(o_t):
    pairs = []
    for e in range(0, GROUP_HEADS, 2):
        pair = jnp.concatenate([o_t[:, e * Q_TILE:(e + 1) * Q_TILE], o_t[:, (e + 1) * Q_TILE:(e + 2) * Q_TILE]], axis=0)
        pairs.append(pair.T)
    return jnp.concatenate(pairs, axis=-1)


def _gate_rows(gate_ref, g):
    gate = jax.nn.sigmoid(gate_ref[g])
    gate = jnp.concatenate([gate, jnp.zeros((Q_TILE, LANES - gate.shape[1]), F32)], axis=1)
    return gate.T


def _gate_cols(gate_t, branch):
    return jnp.concatenate([gate_t[3 * e + branch:3 * e + branch + 1, :] for e in range(GROUP_HEADS)], axis=1)


def _tile_heads(x):
    return jnp.concatenate([x] * GROUP_HEADS, axis=1)


def _pv_t(v, p):
    return lax.dot_general(v, p.astype(BF16), (((0,), (0,)), ((), ())), preferred_element_type=F32)


ACC_ROWS = HEAD_DIM + 8


def _with_ones(v):
    pad = jnp.zeros(v.shape[:-1] + (LANES - HEAD_DIM - 1,), v.dtype)
    return jnp.concatenate([v, jnp.ones(v.shape[:-1] + (1,), v.dtype), pad], axis=-1)


def _band_window(window):
    return (-(-(window - 1) // Q_TILE) + 1) * Q_TILE


def _band_branch(i, q, k_ref, v_ref, g, window, sink=None):
    n_keys = _band_window(window)
    start = pl.multiple_of(jnp.maximum(i * Q_TILE + Q_TILE - n_keys, 0), Q_TILE)
    rel = (i * Q_TILE + lax.broadcasted_iota(jnp.int32, (n_keys, Q_TILE), 1)) - (start + lax.broadcasted_iota(jnp.int32, (n_keys, Q_TILE), 0))
    bias = _tile_heads(jnp.where(rel >= 0, jnp.where(rel < window, 0.0, MASKED), MASKED))
    s = lax.dot_general(k_ref[g, pl.ds(start, n_keys), :], q, _NT, preferred_element_type=F32) + bias
    m = jnp.max(s, axis=0, keepdims=True)
    if sink is not None:
        m = jnp.maximum(m, sink)
    p = jnp.exp2(s - m)
    denom = jnp.sum(p, axis=0, keepdims=True)
    denom = denom + jnp.exp2(sink - m) if sink is not None else jnp.maximum(denom, 1e-30)
    return _pv_t(v_ref[g, pl.ds(start, n_keys), :], p) * (1.0 / denom)


def _nsa_dense_kernel(q_ref, kc_ref, vc_ref, kw_ref, vw_ref, gate_ref, ov_ref, qs_ref, ks_ref, vs_ref, sink_ref,
                      o_ref, sel_ref, swa_ref):
    i = pl.program_id(1)
    rows = GROUP_HEADS * Q_TILE
    width = GROUP_HEADS * HEAD_DIM
    n_cmp = kc_ref.shape[1]
    n = lax.broadcasted_iota(jnp.int32, (n_cmp, Q_TILE), 0)
    t = i * Q_TILE + lax.broadcasted_iota(jnp.int32, (n_cmp, Q_TILE), 1)
    valid = _tile_heads(jnp.where(n * NSA_CMP_STRIDE + (NSA_CMP_BLOCK - 1) <= t, 1.0, 0.0))
    bias = (valid - 1.0) * (-MASKED)
    n_sel = ov_ref.shape[0]
    jb = lax.broadcasted_iota(jnp.int32, (n_sel, Q_TILE), 0)
    cur = (i * Q_TILE + lax.broadcasted_iota(jnp.int32, (n_sel, Q_TILE), 1)) // NSA_SEL_BLOCK
    valid_b = jb <= cur
    forced = jnp.where(valid_b, jnp.where(jb == 0, 1.0, 0.0) + jnp.where(jb == cur, 1.0, 0.0) + jnp.where(jb == cur - 1, 1.0, 0.0), 0.0)
    for g in range(NSA_KV_HEADS):
        q = q_ref[g * GROUP_HEADS:(g + 1) * GROUP_HEADS].reshape(rows, HEAD_DIM)
        s = lax.dot_general(kc_ref[g].astype(BF16), q, _NT, preferred_element_type=F32) + bias
        m = jnp.max(s, axis=0, keepdims=True)
        p = jnp.exp2(s - m) * valid
        p = p * (1.0 / jnp.maximum(jnp.sum(p, axis=0, keepdims=True), 1e-30))
        o_cmp = _pv_t(vc_ref[g].astype(BF16), p)
        o_win = _band_branch(i, q, kw_ref, vw_ref, g, NSA_WINDOW)
        gate_t = _gate_rows(gate_ref, g)
        o_ref[:, g * width:(g + 1) * width] = _heads_from_t(o_cmp * _gate_cols(gate_t, 0) + o_win * _gate_cols(gate_t, 2)).astype(o_ref.dtype)

        p_sum = p[:, 0:Q_TILE]
        for e in range(1, GROUP_HEADS):
            p_sum = p_sum + p[:, e * Q_TILE:(e + 1) * Q_TILE]
        imp = jnp.dot(ov_ref[...], p_sum, preferred_element_type=F32, precision=lax.Precision.HIGHEST)
        score = jnp.where(forced > 0.5, NSA_FORCE_SCORE, jnp.where(valid_b, imp, -jnp.inf))
        sub = SUBLANES
        ranks = []
        for v in range(n_sel // sub):
            blk_scores = score[v * sub:(v + 1) * sub, :]
            jb_v = jb[v * sub:(v + 1) * sub, :]
            rank = jnp.zeros((sub, Q_TILE), F32)
            for ii in range(n_sel):
                row = score[ii:ii + 1, :]
                if ii < v * sub:
                    ahead = jnp.where(row >= blk_scores, 1.0, 0.0)
                elif ii >= (v + 1) * sub:
                    ahead = jnp.where(row > blk_scores, 1.0, 0.0)
                else:
                    tie = jnp.where(jb_v > ii, 1.0, 0.0)
                    ahead = jnp.where(row > blk_scores, 1.0, jnp.where(row == blk_scores, tie, 0.0))
                rank = rank + ahead
            ranks.append(rank)
        rank = jnp.concatenate(ranks, axis=0)
        sel_t = jnp.where(valid_b, jnp.where(rank < NSA_TOPN, 1.0, 0.0), 0.0)
        sel_t = jnp.concatenate([sel_t, jnp.zeros((LANES - n_sel, Q_TILE), F32)], axis=0)
        sel_ref[g] = sel_t.T.astype(BF16)

    for g in range(SWA_KV_HEADS):
        q = qs_ref[g * GROUP_HEADS:(g + 1) * GROUP_HEADS].reshape(rows, HEAD_DIM)
        sink = LOG2_E * jnp.concatenate([jnp.broadcast_to(sink_ref[g][:, e:e + 1], (1, Q_TILE)) for e in range(GROUP_HEADS)], axis=1)
        o_swa = _band_branch(i, q, ks_ref, vs_ref, g, SWA_WINDOW, sink)
        swa_ref[:, g * width:(g + 1) * width] = _heads_from_t(o_swa).astype(swa_ref.dtype)


def _nsa_dense(q, k_cmp, v_cmp, kw, vw, gates, overlap_t, q_s, k_s, v_s, sinks):
    nh, bsz, seq, hd = q.shape
    ng, _, n_cmp, _ = k_cmp.shape
    heads = pl.BlockSpec((nh, None, Q_TILE, hd), lambda b, i: (0, b, i, 0))
    cmp_kv = pl.BlockSpec((ng, None, n_cmp, hd), lambda b, i: (0, b, 0, 0))
    seq_kv = pl.BlockSpec((ng, None, seq, hd), lambda b, i: (0, b, 0, 0))
    out = pl.BlockSpec((None, Q_TILE, nh * hd), lambda b, i: (b, i, 0))
    return pl.pallas_call(
        _nsa_dense_kernel,
        grid=(bsz, seq // Q_TILE),
        in_specs=[heads, cmp_kv, cmp_kv, seq_kv, seq_kv,
                  pl.BlockSpec((ng, None, Q_TILE, gates.shape[-1]), lambda b, i: (0, b, i, 0)),
                  pl.BlockSpec(overlap_t.shape, lambda b, i: (0, 0)),
                  heads, seq_kv, seq_kv,
                  pl.BlockSpec(sinks.shape, lambda b, i: (0, 0, 0))],
        out_specs=[out, pl.BlockSpec((ng, None, Q_TILE, LANES), lambda b, i: (0, b, i, 0)), out],
        out_shape=[jax.ShapeDtypeStruct((bsz, seq, nh * hd), BF16),
                   jax.ShapeDtypeStruct((ng, bsz, seq, LANES), BF16),
                   jax.ShapeDtypeStruct((bsz, seq, nh * hd), BF16)],
        compiler_params=_compiler_params(("parallel", "parallel")),
        name="nsa_dense",
    )(q, k_cmp, v_cmp, kw, vw, gates, overlap_t, q_s, k_s, v_s, sinks)


def _nsa_sel_kernel(q_ref, k_ref, v_ref, sel_ref, gate_ref, o_ref, s_ref, m_ref, acc_ref):
    i = pl.program_id(1)
    rows = GROUP_HEADS * Q_TILE
    width = GROUP_HEADS * HEAD_DIM
    n_blk = LANES - HEAD_DIM
    kc = SEL_CHUNK
    seq = k_ref.shape[1]
    groups = range(NSA_KV_HEADS)

    def scores(g, start, lhs):
        return lax.dot_general(k_ref[g, pl.ds(start, kc), :], lhs, _NT, preferred_element_type=F32)

    start_d = pl.multiple_of(jnp.maximum(i * Q_TILE + Q_TILE - kc, 0), Q_TILE)
    n_past = (start_d + kc - 1) // kc
    kpos = start_d + lax.broadcasted_iota(jnp.int32, (kc, rows), 0)
    qpos = i * Q_TILE + (lax.broadcasted_iota(jnp.int32, (kc, rows), 1) & (Q_TILE - 1))
    blk = lax.broadcasted_iota(jnp.int32, (Q_TILE, n_blk), 1)
    lhs_past, lhs_none = [], []
    for g in groups:
        q = q_ref[g * GROUP_HEADS:(g + 1) * GROUP_HEADS].reshape(rows, HEAD_DIM)
        sel = sel_ref[g][:, :n_blk].astype(F32)
        neg_diag = (sel - 1.0) * (-MASKED)
        neg_past = jnp.where(blk < start_d // NSA_SEL_BLOCK, neg_diag, MASKED)
        lhs_diag = jnp.concatenate([q, jnp.concatenate([neg_diag.astype(BF16)] * GROUP_HEADS, axis=0)], axis=-1)
        lhs_past.append(jnp.concatenate([q, jnp.concatenate([neg_past.astype(BF16)] * GROUP_HEADS, axis=0)], axis=-1))
        lhs_none.append(jnp.concatenate([q, jnp.full((rows, n_blk), MASKED, BF16)], axis=-1))
        s_ref[g, 0] = jnp.where(kpos <= qpos, scores(g, start_d, lhs_diag), MASKED)

        m_ref[g] = jnp.full((1, rows), MASKED, F32)
        acc_ref[g] = jnp.zeros((ACC_ROWS, rows), F32)

    def step(c, cur, nxt, lookahead=True):
        next_start = pl.multiple_of(jnp.minimum(c * kc, seq - kc), kc)
        v_start = pl.multiple_of(jnp.where(c == 0, start_d, (c - 1) * kc), Q_TILE)
        for g in groups:
            if lookahead:
                s_ref[g, nxt] = scores(g, next_start, jnp.where(c < n_past, lhs_past[g], lhs_none[g]))
            s = s_ref[g, cur]
            m = m_ref[g]
            m_new = jnp.maximum(m, jnp.max(s, axis=0, keepdims=True))
            alpha = jnp.exp2(m - m_new)
            p = jnp.exp2(s - m_new)
            m_ref[g] = m_new
            acc_ref[g] = alpha * acc_ref[g] + _pv_t(v_ref[g, pl.ds(v_start, kc), :], p)[:ACC_ROWS]

    def body(j, carry):
        step(2 * j, 0, 1)
        step(2 * j + 1, 1, 0)
        return carry

    lax.fori_loop(0, (n_past + 1) // 2, body, 0)

    @pl.when(n_past % 2 == 0)
    def _():
        step(n_past, 0, 1, lookahead=False)

    for g in groups:
        gate = _gate_cols(_gate_rows(gate_ref, g), 1)
        acc = acc_ref[g]
        o_ref[:, g * width:(g + 1) * width] = _heads_from_t(acc[:HEAD_DIM] * (gate / acc[HEAD_DIM:HEAD_DIM + 1])).astype(o_ref.dtype)


def _nsa_sel(q, k, v, sel, gates):
    nh, bsz, seq, hd = q.shape
    ng = k.shape[0]
    onehot = (np.arange(seq)[:, None] // NSA_SEL_BLOCK == np.arange(LANES - hd)[None, :]).astype(np.float32)
    k_ext = jnp.concatenate([k, jnp.broadcast_to(jnp.asarray(onehot, BF16), (ng, bsz, seq, LANES - hd))], axis=-1)
    return pl.pallas_call(
        _nsa_sel_kernel,
        grid=(bsz, seq // Q_TILE),
        in_specs=[pl.BlockSpec((nh, None, Q_TILE, hd), lambda b, i: (0, b, i, 0)),
                  pl.BlockSpec((ng, None, seq, LANES), lambda b, i: (0, b, 0, 0)),
                  pl.BlockSpec((ng, None, seq, LANES), lambda b, i: (0, b, 0, 0)),
                  pl.BlockSpec((ng, None, Q_TILE, LANES), lambda b, i: (0, b, i, 0)),
                  pl.BlockSpec((ng, None, Q_TILE, gates.shape[-1]), lambda b, i: (0, b, i, 0))],
        out_specs=pl.BlockSpec((None, Q_TILE, nh * hd), lambda b, i: (b, i, 0)),
        out_shape=jax.ShapeDtypeStruct((bsz, seq, nh * hd), BF16),
        scratch_shapes=[pltpu.VMEM((ng, 2, SEL_CHUNK, GROUP_HEADS * Q_TILE), F32),
                        pltpu.VMEM((ng, 1, GROUP_HEADS * Q_TILE), F32),
                        pltpu.VMEM((ng, ACC_ROWS, GROUP_HEADS * Q_TILE), F32)],
        compiler_params=_compiler_params(("parallel", "parallel")),
        name="nsa_sel",
    )(q, k_ext, _with_ones(v), sel, gates)


def _overlap_t(seq):
    n_cmp = (seq - NSA_CMP_BLOCK) // NSA_CMP_STRIDE + 1
    n_pad = seq // NSA_CMP_STRIDE
    starts = np.arange(n_pad) * NSA_CMP_STRIDE
    sel_start = np.arange(seq // NSA_SEL_BLOCK) * NSA_SEL_BLOCK
    ov = (starts[None, :] < sel_start[:, None] + NSA_SEL_BLOCK) & (starts[None, :] + NSA_CMP_BLOCK > sel_start[:, None])
    ov = ov & (np.arange(n_pad)[None, :] < n_cmp)
    return jnp.asarray(ov.astype(np.float32))


def _odd_mixer(xt, nw, w_in, w_out, cmp_pe, cmp_w1, cmp_w2, sinks, bsz):
    t, d = xt.shape
    seq = t // bsz
    assert seq // NSA_SEL_BLOCK <= LANES and seq % SEL_CHUNK == 0
    cuts = [int(c) for c in np.cumsum((0,) + ODD_SPLITS)]
    ws = [w_in[:, a:b] for a, b in zip(cuts[:-1], cuts[1:])]
    ws[0] = ws[0] * (HEAD_DIM ** -0.5 * LOG2_E)
    ws[8] = ws[8] * (HEAD_DIM ** -0.5 * LOG2_E)
    ws = [w.astype(BF16) for w in ws]
    hd = HEAD_DIM
    head_dims = (hd, hd, hd, hd, hd, hd, hd, 3 * GROUP_HEADS, hd, hd, hd)
    dtypes = (BF16, F32, F32, BF16, BF16, BF16, BF16, F32, BF16, BF16, BF16)
    q_n, kc, vc, ks, vs, kw, vw, gates, q_s, k_s, v_s = _norm_proj(xt, nw, ws, head_dims, dtypes)

    def heads(a):
        return a.reshape(a.shape[0], bsz, seq, a.shape[-1])

    q_n, kc, vc, ks, vs, kw, vw, gates, q_s, k_s, v_s = map(heads, (q_n, kc, vc, ks, vs, kw, vw, gates, q_s, k_s, v_s))
    w1 = cmp_w1.astype(BF16)
    w2 = cmp_w2.astype(BF16)
    k_cmp = _cmp_kv(kc, cmp_pe[0], w1[0], w2[0])
    v_cmp = _cmp_kv(vc, cmp_pe[1], w1[1], w2[1])
    o_cw, sel, o_swa = _nsa_dense(q_n, k_cmp, v_cmp, kw, vw, gates, _overlap_t(seq), q_s, k_s, v_s,
                                  sinks.reshape(SWA_KV_HEADS, 1, GROUP_HEADS))
    o_sel = _nsa_sel(q_n, ks, vs, sel, gates)
    n_nsa = NSA_HEADS * hd
    return [[o_cw.reshape(t, n_nsa), o_sel.reshape(t, n_nsa)], [o_swa.reshape(t, SWA_HEADS * hd)]], [w_out[:n_nsa], w_out[n_nsa:]]


def kernel(x, mem, norm_w, final_norm_w, mem_norm_w, ffn_w_gate, ffn_w_up, ffn_w_down, xattn_wq, xattn_wkv, xattn_wo, even_w_in, even_w_out, ssm_conv_w, ssm_conv_b, ssm_dt_bias, ssm_a_log, ssm_d, ssm_norm_w, odd_w_in, odd_w_out, nsa_cmp_pe, nsa_cmp_w1, nsa_cmp_w2, swa_sinks):
    bsz, seq, d = x.shape
    t = bsz * seq
    m = mem.shape[1]
    wg, wu, wd = ffn_w_gate, ffn_w_up, ffn_w_down
    wq = xattn_wq.astype(BF16)
    wkv = xattn_wkv.astype(BF16)
    wo = xattn_wo.astype(BF16)
    even_in = even_w_in.astype(BF16)
    even_out = even_w_out.astype(BF16)
    odd_out = odd_w_out.astype(BF16)

    xt = x.reshape(t, d)
    mem2 = mem.reshape(bsz * m, d)
    mem_nw = mem_norm_w.reshape(1, d)
    for layer in range(DEPTH):
        i = layer // 2
        xt = _ffn(xt, norm_w[layer, 0].reshape(1, d), wg, wu, wd, layer, 0)
        nw1 = norm_w[layer, 1].reshape(1, d)
        if layer % 2 == 0:
            mix, mix_w = _even_mixer(xt, nw1, even_in[i], even_out[i], ssm_conv_w[i], ssm_conv_b[i], ssm_dt_bias[i], ssm_a_log[i], ssm_d[i], ssm_norm_w[i], bsz)
        else:
            mix, mix_w = _odd_mixer(xt, nw1, odd_w_in[i], odd_out[i], nsa_cmp_pe[i], nsa_cmp_w1[i], nsa_cmp_w2[i], swa_sinks[i], bsz)
        (kv,) = _norm_proj(mem2, mem_nw, [wkv[layer]])
        xt = _mix_xattn(xt, mix, mix_w, norm_w[layer, 2].reshape(1, d), wq[layer], kv.reshape(bsz, m, 2 * d), wo[layer], bsz)
        out_nw = final_norm_w.reshape(1, d) if layer == DEPTH - 1 else None
        xt = _ffn(xt, norm_w[layer, 3].reshape(1, d), wg, wu, wd, layer, 1, out_nw)
    return xt.reshape(bsz, seq, d)
```

```python
import functools

import numpy as np
import jax
import jax.numpy as jnp
from jax import lax
from jax.experimental import pallas as pl
from jax.experimental.pallas import tpu as pltpu

F32 = jnp.float32
BF16 = jnp.bfloat16

D_MODEL = 1024
DEPTH = 4
D_FF = 2816
NORM_EPS = 1e-6
ATTN_BLOCK = 128

RET_HEADS = 4
RET_DK = 128
RET_DV = 128
RET_CHUNK = 128
RET_ROPE_BASE = 10000.0

SSM_HEADS = 8
SSM_HEAD_DIM = 64
SSM_D_INNER = SSM_HEADS * SSM_HEAD_DIM
SSM_D_STATE = 64
SSM_GROUPS = 2
SSM_CONV = 4
SSM_CHUNK = 128
SSM_CONV_DIM = SSM_D_INNER + 2 * SSM_GROUPS * SSM_D_STATE

EVEN_SPLITS = (RET_HEADS * RET_DK, RET_HEADS * RET_DK, RET_HEADS * RET_DV, RET_HEADS * RET_DV, SSM_D_INNER, SSM_CONV_DIM, SSM_HEADS)

NSA_HEADS = 8
NSA_KV_HEADS = 2
NSA_HEAD_DIM = 64
NSA_CMP_BLOCK = 32
NSA_CMP_STRIDE = 16
NSA_CMP_HIDDEN = 256
NSA_SEL_BLOCK = 64
NSA_TOPN = 16
NSA_WINDOW = 512
NSA_FORCE_SCORE = 1e6

SWA_HEADS = 8
SWA_KV_HEADS = 2
SWA_HEAD_DIM = 64
SWA_WINDOW = 128

ODD_SPLITS = (NSA_HEADS * NSA_HEAD_DIM,) + (NSA_KV_HEADS * NSA_HEAD_DIM,) * 6 + (3 * NSA_HEADS, SWA_HEADS * SWA_HEAD_DIM, SWA_KV_HEADS * SWA_HEAD_DIM, SWA_KV_HEADS * SWA_HEAD_DIM)

XATTN_HEADS = 4
XATTN_HEAD_DIM = D_MODEL // XATTN_HEADS

VMEM_LIMIT_BYTES = 52 * 1024 * 1024
TOKEN_TILE = 512
FFN_TILE = 256
FFN_DOWN_CHUNK = 1024
LANES = 128
SUBLANES = 8
MASKED = -1e30
LOG2_E = float(np.log2(np.e))

_NT = (((1,), (1,)), ((), ()))


def _compiler_params(semantics):
    return pltpu.CompilerParams(dimension_semantics=semantics, vmem_limit_bytes=VMEM_LIMIT_BYTES)


def _rms(x, w):
    return x * lax.rsqrt(jnp.mean(x * x, axis=-1, keepdims=True) + NORM_EPS) * w


def _ffn_kernel(has_out_norm, x_ref, nw_ref, wg_ref, wu_ref, wd_ref, *refs):
    out_nw_ref = refs[0] if has_out_norm else None
    o_ref, a_ref = refs[-2:]
    x = x_ref[...]
    h = _rms(x, nw_ref[...]).astype(BF16)
    for c in range(D_FF // FFN_TILE):
        cols = slice(c * FFN_TILE, (c + 1) * FFN_TILE)
        g = jnp.dot(h, wg_ref[:, cols].astype(BF16), preferred_element_type=F32)
        u = jnp.dot(h, wu_ref[:, cols].astype(BF16), preferred_element_type=F32)
        a_ref[:, cols] = (g * jax.nn.sigmoid(g) * u).astype(BF16)
    acc = None
    for lo in range(0, D_FF, FFN_DOWN_CHUNK):
        hi = min(lo + FFN_DOWN_CHUNK, D_FF)
        part = jnp.dot(a_ref[:, lo:hi], wd_ref[lo:hi, :].astype(BF16), preferred_element_type=F32)
        acc = part if acc is None else acc + part
    y = x + 0.5 * acc
    o_ref[...] = _rms(y, out_nw_ref[...]) if has_out_norm else y


def _ffn(x, nw, wg, wu, wd, layer, k, out_nw=None):
    t, d = x.shape
    tm = TOKEN_TILE
    resident = pl.Buffered(1)
    vec = pl.BlockSpec((1, d), lambda i: (0, 0))
    return pl.pallas_call(
        functools.partial(_ffn_kernel, out_nw is not None),
        grid=(t // tm,),
        in_specs=[
            pl.BlockSpec((tm, d), lambda i: (i, 0)),
            vec,
            pl.BlockSpec((None, None, d, D_FF), lambda i: (layer, k, 0, 0), pipeline_mode=resident),
            pl.BlockSpec((None, None, d, D_FF), lambda i: (layer, k, 0, 0), pipeline_mode=resident),
            pl.BlockSpec((None, None, D_FF, d), lambda i: (layer, k, 0, 0), pipeline_mode=resident),
        ] + ([vec] if out_nw is not None else []),
        out_specs=pl.BlockSpec((tm, d), lambda i: (i, 0)),
        out_shape=jax.ShapeDtypeStruct((t, d), F32),
        scratch_shapes=[pltpu.VMEM((tm, D_FF), BF16)],
        compiler_params=_compiler_params(("parallel",)),
        name="ffn",
    )(x, nw, wg, wu, wd, *([out_nw] if out_nw is not None else []))


def _norm_proj_kernel(head_dims, offsets, x_ref, nw_ref, w_ref, *o_refs):
    h = _rms(x_ref[...], nw_ref[...]).astype(BF16)
    y_all = jnp.dot(h, w_ref[...], preferred_element_type=F32)
    for hd, off, o_ref in zip(head_dims, offsets, o_refs):
        if hd is None:
            o_ref[...] = y_all[:, off:off + o_ref.shape[-1]].astype(o_ref.dtype)
        else:
            for j in range(o_ref.shape[0]):
                o_ref[j] = y_all[:, off + j * hd:off + (j + 1) * hd].astype(o_ref.dtype)


def _norm_proj(x, nw, weights, head_dims=None, dtypes=None, tm=TOKEN_TILE):
    t, d = x.shape
    n_out = len(weights)
    head_dims = tuple(head_dims) if head_dims is not None else (None,) * n_out
    dtypes = tuple(dtypes) if dtypes is not None else (F32,) * n_out
    out_specs, out_shape, offsets, padded = [], [], [], []
    total = 0
    for w, hd, dt in zip(weights, head_dims, dtypes):
        n = w.shape[1]
        if hd is None:
            out_specs.append(pl.BlockSpec((tm, n), lambda i: (i, 0)))
            out_shape.append(jax.ShapeDtypeStruct((t, n), dt))
        else:
            out_specs.append(pl.BlockSpec((n // hd, tm, hd), lambda i: (0, i, 0)))
            out_shape.append(jax.ShapeDtypeStruct((n // hd, t, hd), dt))
        offsets.append(total)
        width = -(-n // LANES) * LANES
        padded.append(jnp.pad(w, ((0, 0), (0, width - n))))
        total += width
    w_all = jnp.concatenate(padded, axis=1) if n_out > 1 else padded[0]
    return pl.pallas_call(
        functools.partial(_norm_proj_kernel, head_dims, tuple(offsets)),
        grid=(t // tm,),
        in_specs=[pl.BlockSpec((tm, d), lambda i: (i, 0)), pl.BlockSpec((1, d), lambda i: (0, 0)),
                  pl.BlockSpec(w_all.shape, lambda i: (0, 0))],
        out_specs=out_specs,
        out_shape=out_shape,
        compiler_params=_compiler_params(("parallel",)),
        name="norm_proj",
    )(x, nw, w_all)


def _mix_xattn_kernel(group_sizes, x_ref, *refs):
    n_a = sum(group_sizes)
    n_g = len(group_sizes)
    a_refs, w_refs = refs[:n_a], refs[n_a:n_a + n_g]
    nw_ref, wq_ref, kv_ref, wo_ref, o_ref = refs[n_a + n_g:]
    x = x_ref[...]
    pos = 0
    for size, w_ref in zip(group_sizes, w_refs):
        a = a_refs[pos][...]
        for r in a_refs[pos + 1:pos + size]:
            a = a.astype(F32) + r[...].astype(F32)
        pos += size
        x = x + jnp.dot(a.astype(BF16), w_ref[...], preferred_element_type=F32)
    h = _rms(x, nw_ref[...]).astype(BF16)
    q = jnp.dot(h, wq_ref[...].astype(BF16), preferred_element_type=F32).astype(BF16)
    hd = XATTN_HEAD_DIM
    outs = []
    for hh in range(XATTN_HEADS):
        k = kv_ref[:, hh * hd:(hh + 1) * hd].astype(BF16)
        v = kv_ref[:, D_MODEL + hh * hd:D_MODEL + (hh + 1) * hd].astype(BF16)
        s = lax.dot_general(q[:, hh * hd:(hh + 1) * hd], k, _NT, preferred_element_type=F32)
        s = s * (hd ** -0.5)
        p = jnp.exp(s - jnp.max(s, axis=-1, keepdims=True))
        p = p / jnp.sum(p, axis=-1, keepdims=True)
        outs.append(jnp.dot(p.astype(BF16), v, preferred_element_type=F32).astype(BF16))
    o = jnp.concatenate(outs, axis=-1)
    o_ref[...] = x + jnp.dot(o, wo_ref[...].astype(BF16), preferred_element_type=F32)


def _mix_xattn(x, groups, weights, nw, wq, kv, wo, layer, bsz, tm=TOKEN_TILE):
    t, d = x.shape
    seq = t // bsz
    nt = seq // tm
    m = kv.shape[1]
    arrays = [a for grp in groups for a in grp]

    def tok(width):
        return pl.BlockSpec((tm, width), lambda b, i: (b * nt + i, 0))

    def const(shape):
        return pl.BlockSpec(shape, lambda b, i: (0, 0))

    layer_w = pl.BlockSpec((None, d, d), lambda b, i: (layer, 0, 0))
    return pl.pallas_call(
        functools.partial(_mix_xattn_kernel, tuple(len(grp) for grp in groups)),
        grid=(bsz, nt),
        in_specs=[tok(d)] + [tok(a.shape[1]) for a in arrays] + [const(w.shape) for w in weights]
        + [const((1, d)), layer_w, pl.BlockSpec((None, m, 2 * d), lambda b, i: (b, 0, 0)), layer_w],
        out_specs=tok(d),
        out_shape=jax.ShapeDtypeStruct((t, d), F32),
        compiler_params=_compiler_params(("parallel", "parallel")),
        name="mix_xattn",
    )(x, *arrays, *weights, nw, wq, kv, wo)


def _retention_kernel(q_ref, k_ref, v_ref, g_ref, cos_ref, sin_ref, dmat_ref, zeta_ref, xi_ref, decay_ref, o_ref, state_ref):
    @pl.when(pl.program_id(1) == 0)
    def _():
        state_ref[...] = jnp.zeros_like(state_ref)

    cos = cos_ref[...]
    sin = sin_ref[...]
    half = RET_DK // 2
    outs = []
    for h in range(RET_HEADS):
        qh = q_ref[:, h * RET_DK:(h + 1) * RET_DK]
        kh = k_ref[:, h * RET_DK:(h + 1) * RET_DK]
        qr = qh * cos + pltpu.roll(qh, half, 1) * sin
        kr = (kh * cos + pltpu.roll(kh, half, 1) * sin) * (RET_DK ** -0.5)
        vh = v_ref[:, h * RET_DV:(h + 1) * RET_DV].astype(BF16)
        scores = lax.dot_general(qr.astype(BF16), kr.astype(BF16), _NT, preferred_element_type=F32) * dmat_ref[h]
        inner = jnp.dot(scores.astype(BF16), vh, preferred_element_type=F32)
        state = state_ref[h]
        cross = jnp.dot((qr * xi_ref[h]).astype(BF16), state.astype(BF16), preferred_element_type=F32)
        kz = (kr * zeta_ref[h]).T.astype(BF16)
        state_ref[h] = decay_ref[h] * state + jnp.dot(kz, vh, preferred_element_type=F32)
        o = inner + cross
        o = o * lax.rsqrt(jnp.mean(o * o, axis=-1, keepdims=True) + NORM_EPS)
        gh = g_ref[:, h * RET_DV:(h + 1) * RET_DV]
        outs.append(gh * jax.nn.sigmoid(gh) * o)
    o_ref[...] = jnp.concatenate(outs, axis=-1).astype(o_ref.dtype)


def _retention_tables(seq):
    L = RET_CHUNK
    pos = jnp.arange(seq, dtype=F32)
    inv_freq = 1.0 / (RET_ROPE_BASE ** jnp.linspace(0.0, 1.0, RET_DK // 2, dtype=F32))
    ang = pos[:, None] * inv_freq[None, :]
    cos, sin = jnp.cos(ang), jnp.sin(ang)
    cos2 = jnp.concatenate([cos, cos], axis=-1)
    sin2 = jnp.concatenate([-sin, sin], axis=-1)
    log_g = jnp.log1p(-jnp.exp2(-5.0 - jnp.arange(RET_HEADS, dtype=F32)))
    idx = jnp.arange(L, dtype=F32)
    diff = idx[:, None] - idx[None, :]
    dmat = jnp.where(diff >= 0, jnp.exp(jnp.maximum(diff, 0.0)[None] * log_g[:, None, None]), 0.0)
    ones = jnp.ones((RET_HEADS, L, RET_DK), F32)
    zeta = jnp.exp((L - 1 - idx)[None, :] * log_g[:, None])[:, :, None] * ones
    xi = jnp.exp((idx + 1)[None, :] * log_g[:, None])[:, :, None] * ones
    decay = jnp.exp(L * log_g)[:, None, None] * ones
    return cos2, sin2, dmat, zeta, xi, decay


CONV_PAD = 8


def _ssd_kernel(xbc_ref, dt_ref, z_ref, convw_ref, convb_ref, dtb_ref, alog_ref, dskip_ref, nw_ref, tri_ref, o_ref, ext_ref, state_ref):
    L = SSM_CHUNK
    hp = SSM_HEAD_DIM
    ns = SSM_D_STATE

    @pl.when(pl.program_id(1) == 0)
    def _():
        ext_ref[0:CONV_PAD, :] = jnp.zeros((CONV_PAD, SSM_CONV_DIM), F32)
        state_ref[...] = jnp.zeros_like(state_ref)

    ext_ref[CONV_PAD:CONV_PAD + L, :] = xbc_ref[...]
    conv = convb_ref[...]
    for w in range(SSM_CONV):
        off = CONV_PAD - (SSM_CONV - 1) + w
        conv = conv + ext_ref[off:off + L, :] * convw_ref[w:w + 1, :]
    ext_ref[0:CONV_PAD, :] = ext_ref[L:L + CONV_PAD, :]
    xa = conv * jax.nn.sigmoid(conv)
    xs = xa[:, :SSM_D_INNER]
    bmat = xa[:, SSM_D_INNER:SSM_D_INNER + SSM_GROUPS * ns]
    cmat = xa[:, SSM_D_INNER + SSM_GROUPS * ns:]

    x = dt_ref[...] + dtb_ref[...]
    dt = jnp.maximum(x, 0.0) + jnp.log1p(jnp.exp(-jnp.abs(x)))
    lane = lax.broadcasted_iota(jnp.int32, (L, LANES), 1)
    dta = jnp.where(lane < SSM_HEADS, dt * -jnp.exp(alog_ref[...]), 0.0)
    cums = jnp.dot(tri_ref[...], dta, preferred_element_type=F32, precision=lax.Precision.HIGHEST)
    cums_t = cums.T
    last = cums[L - 1:L, :]
    decay_to_end = jnp.exp(last - cums)
    decay_from_start = jnp.exp(cums)
    chunk_decay = jnp.exp(last)
    causal = lax.broadcasted_iota(jnp.int32, (L, L), 0) >= lax.broadcasted_iota(jnp.int32, (L, L), 1)
    heads_per_group = SSM_HEADS // SSM_GROUPS
    ys = []
    for g in range(SSM_GROUPS):
        bg = bmat[:, g * ns:(g + 1) * ns].astype(BF16)
        cg = cmat[:, g * ns:(g + 1) * ns].astype(BF16)
        cb = lax.dot_general(cg, bg, _NT, preferred_element_type=F32)
        for e in range(heads_per_group):
            h = g * heads_per_group + e
            xh = xs[:, h * hp:(h + 1) * hp]
            xdt = xh * dt[:, h:h + 1]
            seg = cums[:, h:h + 1] - cums_t[h:h + 1, :]
            lmat = jnp.where(causal, jnp.exp(seg), 0.0)
            y_diag = jnp.dot((cb * lmat).astype(BF16), xdt.astype(BF16), preferred_element_type=F32)
            state = state_ref[h]
            y_off = lax.dot_general(cg, state.astype(BF16), _NT, preferred_element_type=F32) * decay_from_start[:, h:h + 1]
            xw = (xdt * decay_to_end[:, h:h + 1]).astype(BF16)
            new = lax.dot_general(xw, bg, (((0,), (0,)), ((), ())), preferred_element_type=F32)
            state_ref[h] = chunk_decay[:, h:h + 1] * state + new
            ys.append(y_diag + y_off + dskip_ref[:, h * hp:(h + 1) * hp] * xh)
    y = jnp.concatenate(ys, axis=-1)
    z = z_ref[...]
    o_ref[...] = _rms(y * (z * jax.nn.sigmoid(z)), nw_ref[...]).astype(o_ref.dtype)


N_RET_IN = 10
N_SSD_IN = 10


def _even_core_kernel(*refs):
    ret_in, ssd_in = refs[:N_RET_IN], refs[N_RET_IN:N_RET_IN + N_SSD_IN]
    o_ret_ref, o_ssd_ref, ret_state_ref, ext_ref, ssd_state_ref = refs[N_RET_IN + N_SSD_IN:]
    _retention_kernel(*ret_in, o_ret_ref, ret_state_ref)
    _ssd_kernel(*ssd_in, o_ssd_ref, ext_ref, ssd_state_ref)


def _even_core(q, k, v, g, xbc, dt_raw, z, conv_w, conv_b, dt_bias, a_log, d_skip, norm_w):
    bsz, seq, width = q.shape
    L = RET_CHUNK
    assert SSM_CHUNK == L
    tables = _retention_tables(seq)
    pad = LANES - SSM_HEADS
    params = [conv_w, conv_b.reshape(1, -1), jnp.pad(dt_bias, (0, pad)).reshape(1, LANES),
              jnp.pad(a_log, (0, pad)).reshape(1, LANES), jnp.repeat(d_skip, SSM_HEAD_DIM).reshape(1, -1),
              norm_w.reshape(1, -1), jnp.asarray(np.tril(np.ones((L, L), np.float32)))]

    def tok(w):
        return pl.BlockSpec((None, L, w), lambda b, c: (b, c, 0))

    rope = pl.BlockSpec((L, RET_DK), lambda b, c: (c, 0))
    const = pl.BlockSpec((RET_HEADS, L, RET_DK), lambda b, c: (0, 0, 0))
    return pl.pallas_call(
        _even_core_kernel,
        grid=(bsz, seq // L),
        in_specs=[tok(width)] * 4 + [rope, rope, const, const, const, const]
        + [tok(SSM_CONV_DIM), tok(LANES), tok(SSM_D_INNER)] + [pl.BlockSpec(p.shape, lambda b, c: (0, 0)) for p in params],
        out_specs=[tok(width), tok(SSM_D_INNER)],
        out_shape=[jax.ShapeDtypeStruct((bsz, seq, width), BF16), jax.ShapeDtypeStruct((bsz, seq, SSM_D_INNER), BF16)],
        scratch_shapes=[pltpu.VMEM((RET_HEADS, RET_DK, RET_DV), F32),
                        pltpu.VMEM((L + CONV_PAD, SSM_CONV_DIM), F32), pltpu.VMEM((SSM_HEADS, SSM_HEAD_DIM, SSM_D_STATE), F32)],
        compiler_params=_compiler_params(("parallel", "arbitrary")),
        name="even_core",
    )(q, k, v, g, *tables, xbc, dt_raw, z, *params)


def _even_mixer(xt, nw, w_in, w_out, conv_w, conv_b, dt_bias, a_log, d_skip, ssm_norm_w, bsz):
    t, d = xt.shape
    seq = t // bsz
    cuts = [int(c) for c in np.cumsum((0,) + EVEN_SPLITS)]
    ws = [w_in[:, a:b].astype(BF16) for a, b in zip(cuts[:-1], cuts[1:])]
    perm = np.concatenate([h * RET_DK + np.concatenate([np.arange(0, RET_DK, 2), np.arange(1, RET_DK, 2)]) for h in range(RET_HEADS)])
    ws[0] = ws[0][:, perm]
    ws[1] = ws[1][:, perm]
    ws[6] = jnp.pad(ws[6], ((0, 0), (0, LANES - SSM_HEADS)))
    dtypes = (F32, F32, BF16, F32, F32, F32, F32)
    q, k, v, g, z, xbc, dt_raw = [a.reshape(bsz, seq, -1) for a in _norm_proj(xt, nw, ws, dtypes=dtypes)]
    o_ret, y = _even_core(q, k, v, g, xbc, dt_raw, z, conv_w, conv_b, dt_bias, a_log, d_skip, ssm_norm_w)
    n_ret = RET_HEADS * RET_DV
    return [[o_ret.reshape(t, n_ret)], [y.reshape(t, SSM_D_INNER)]], [w_out[:n_ret], w_out[n_ret:]]


Q_TILE = ATTN_BLOCK
GROUP_HEADS = NSA_HEADS // NSA_KV_HEADS
HEAD_DIM = NSA_HEAD_DIM
SEL_CHUNK = 512


def _cmp_kv_kernel(x_ref, pe_ref, w1_ref, w2_ref, o_ref):
    n = x_ref.shape[0] // NSA_CMP_STRIDE
    top = bot = None
    for r in range(NSA_CMP_STRIDE):
        rows = x_ref[pl.ds(r, n, stride=NSA_CMP_STRIDE), :]
        lo = r * HEAD_DIM
        hi = (NSA_CMP_STRIDE + r) * HEAD_DIM
        a = jnp.dot((rows + pe_ref[r:r + 1, :]).astype(BF16), w1_ref[lo:lo + HEAD_DIM, :], preferred_element_type=F32)
        b = jnp.dot((rows + pe_ref[NSA_CMP_STRIDE + r:NSA_CMP_STRIDE + r + 1, :]).astype(BF16), w1_ref[hi:hi + HEAD_DIM, :], preferred_element_type=F32)
        top = a if top is None else top + a
        bot = b if bot is None else bot + b
    hidden = top + pltpu.roll(bot, n - 1, 0)
    act = hidden * jax.nn.sigmoid(hidden)
    o_ref[...] = jnp.dot(act.astype(BF16), w2_ref[...], preferred_element_type=F32)


def _cmp_kv(x, pe, w1, w2):
    ng, bsz, seq, hd = x.shape
    n = seq // NSA_CMP_STRIDE
    return pl.pallas_call(
        _cmp_kv_kernel,
        grid=(ng, bsz),
        in_specs=[pl.BlockSpec((None, None, seq, hd), lambda g, b: (g, b, 0, 0)),
                  pl.BlockSpec(pe.shape, lambda g, b: (0, 0)),
                  pl.BlockSpec(w1.shape, lambda g, b: (0, 0)),
                  pl.BlockSpec(w2.shape, lambda g, b: (0, 0))],
        out_specs=pl.BlockSpec((None, None, n, hd), lambda g, b: (g, b, 0, 0)),
        out_shape=jax.ShapeDtypeStruct((ng, bsz, n, hd), F32),
        compiler_params=_compiler_params(("parallel", "parallel")),
        name="nsa_compress",
    )(x, pe, w1, w2)


def _heads_from_t(o_t):
    pairs = []
    for e in range(0, GROUP_HEADS, 2):
        pair = jnp.concatenate([o_t[:, e * Q_TILE:(e + 1) * Q_TILE], o_t[:, (e + 1) * Q_TILE:(e + 2) * Q_TILE]], axis=0)
        pairs.append(pair.T)
    return jnp.concatenate(pairs, axis=-1)


def _gate_rows(gate_ref, g):
    gate = jax.nn.sigmoid(gate_ref[g])
    gate = jnp.concatenate([gate, jnp.zeros((Q_TILE, LANES - gate.shape[1]), F32)], axis=1)
    return gate.T


def _gate_cols(gate_t, branch):
    return jnp.concatenate([gate_t[3 * e + branch:3 * e + branch + 1, :] for e in range(GROUP_HEADS)], axis=1)


def _tile_heads(x):
    return jnp.concatenate([x] * GROUP_HEADS, axis=1)


def _pv_t(v, p):
    return lax.dot_general(v, p.astype(BF16), (((0,), (0,)), ((), ())), preferred_element_type=F32)


ACC_ROWS = HEAD_DIM + 8


def _with_ones(v):
    pad = jnp.zeros(v.shape[:-1] + (LANES - HEAD_DIM - 1,), v.dtype)
    return jnp.concatenate([v, jnp.ones(v.shape[:-1] + (1,), v.dtype), pad], axis=-1)


def _band_window(window):
    return (-(-(window - 1) // Q_TILE) + 1) * Q_TILE


def _band_branch(i, q, k_ref, v_ref, g, window, sink=None):
    n_keys = _band_window(window)
    start = pl.multiple_of(jnp.maximum(i * Q_TILE + Q_TILE - n_keys, 0), Q_TILE)
    rel = (i * Q_TILE + lax.broadcasted_iota(jnp.int32, (n_keys, Q_TILE), 1)) - (start + lax.broadcasted_iota(jnp.int32, (n_keys, Q_TILE), 0))
    bias = _tile_heads(jnp.where(rel >= 0, jnp.where(rel < window, 0.0, MASKED), MASKED))
    s = lax.dot_general(k_ref[g, pl.ds(start, n_keys), :], q, _NT, preferred_element_type=F32) + bias
    m = jnp.max(s, axis=0, keepdims=True)
    if sink is not None:
        m = jnp.maximum(m, sink)
    p = jnp.exp2(s - m)
    denom = jnp.sum(p, axis=0, keepdims=True)
    denom = denom + jnp.exp2(sink - m) if sink is not None else jnp.maximum(denom, 1e-30)
    return _pv_t(v_ref[g, pl.ds(start, n_keys), :], p) * (1.0 / denom)


def _nsa_dense_kernel(q_ref, kc_ref, vc_ref, kw_ref, vw_ref, gate_ref, ov_ref, qs_ref, ks_ref, vs_ref, sink_ref,
                      o_ref, sel_ref, swa_ref):
    i = pl.program_id(1)
    rows = GROUP_HEADS * Q_TILE
    width = GROUP_HEADS * HEAD_DIM
    n_cmp = kc_ref.shape[1]
    n = lax.broadcasted_iota(jnp.int32, (n_cmp, Q_TILE), 0)
    t = i * Q_TILE + lax.broadcasted_iota(jnp.int32, (n_cmp, Q_TILE), 1)
    valid = _tile_heads(jnp.where(n * NSA_CMP_STRIDE + (NSA_CMP_BLOCK - 1) <= t, 1.0, 0.0))
    bias = (valid - 1.0) * (-MASKED)
    n_sel = ov_ref.shape[0]
    jb = lax.broadcasted_iota(jnp.int32, (n_sel, Q_TILE), 0)
    cur = (i * Q_TILE + lax.broadcasted_iota(jnp.int32, (n_sel, Q_TILE), 1)) // NSA_SEL_BLOCK
    valid_b = jb <= cur
    forced = jnp.where(valid_b, jnp.where(jb == 0, 1.0, 0.0) + jnp.where(jb == cur, 1.0, 0.0) + jnp.where(jb == cur - 1, 1.0, 0.0), 0.0)
    for g in range(NSA_KV_HEADS):
        q = q_ref[g * GROUP_HEADS:(g + 1) * GROUP_HEADS].reshape(rows, HEAD_DIM)
        s = lax.dot_general(kc_ref[g].astype(BF16), q, _NT, preferred_element_type=F32) + bias
        m = jnp.max(s, axis=0, keepdims=True)
        p = jnp.exp2(s - m) * valid
        p = p * (1.0 / jnp.maximum(jnp.sum(p, axis=0, keepdims=True), 1e-30))
        o_cmp = _pv_t(vc_ref[g].astype(BF16), p)
        o_win = _band_branch(i, q, kw_ref, vw_ref, g, NSA_WINDOW)
        gate_t = _gate_rows(gate_ref, g)
        o_ref[:, g * width:(g + 1) * width] = _heads_from_t(o_cmp * _gate_cols(gate_t, 0) + o_win * _gate_cols(gate_t, 2)).astype(o_ref.dtype)

        p_sum = p[:, 0:Q_TILE]
        for e in range(1, GROUP_HEADS):
            p_sum = p_sum + p[:, e * Q_TILE:(e + 1) * Q_TILE]
        imp = jnp.dot(ov_ref[...], p_sum, preferred_element_type=F32, precision=lax.Precision.HIGHEST)
        score = jnp.where(forced > 0.5, NSA_FORCE_SCORE, jnp.where(valid_b, imp, -jnp.inf))
        sub = SUBLANES
        ranks = []
        for v in range(n_sel // sub):
            blk_scores = score[v * sub:(v + 1) * sub, :]
            jb_v = jb[v * sub:(v + 1) * sub, :]
            rank = jnp.zeros((sub, Q_TILE), F32)
            for ii in range(n_sel):
                row = score[ii:ii + 1, :]
                if ii < v * sub:
                    ahead = jnp.where(row >= blk_scores, 1.0, 0.0)
                elif ii >= (v + 1) * sub:
                    ahead = jnp.where(row > blk_scores, 1.0, 0.0)
                else:
                    tie = jnp.where(jb_v > ii, 1.0, 0.0)
                    ahead = jnp.where(row > blk_scores, 1.0, jnp.where(row == blk_scores, tie, 0.0))
                rank = rank + ahead
            ranks.append(rank)
        rank = jnp.concatenate(ranks, axis=0)
        sel_t = jnp.where(valid_b, jnp.where(rank < NSA_TOPN, 1.0, 0.0), 0.0)
        sel_t = jnp.concatenate([sel_t, jnp.zeros((LANES - n_sel, Q_TILE), F32)], axis=0)
        sel_ref[g] = sel_t.T.astype(BF16)

    for g in range(SWA_KV_HEADS):
        q = qs_ref[g * GROUP_HEADS:(g + 1) * GROUP_HEADS].reshape(rows, HEAD_DIM)
        sink = LOG2_E * jnp.concatenate([jnp.broadcast_to(sink_ref[g][:, e:e + 1], (1, Q_TILE)) for e in range(GROUP_HEADS)], axis=1)
        o_swa = _band_branch(i, q, ks_ref, vs_ref, g, SWA_WINDOW, sink)
        swa_ref[:, g * width:(g + 1) * width] = _heads_from_t(o_swa).astype(swa_ref.dtype)


def _nsa_dense(q, k_cmp, v_cmp, kw, vw, gates, overlap_t, q_s, k_s, v_s, sinks):
    nh, bsz, seq, hd = q.shape
    ng, _, n_cmp, _ = k_cmp.shape
    heads = pl.BlockSpec((nh, None, Q_TILE, hd), lambda b, i: (0, b, i, 0))
    cmp_kv = pl.BlockSpec((ng, None, n_cmp, hd), lambda b, i: (0, b, 0, 0))
    seq_kv = pl.BlockSpec((ng, None, seq, hd), lambda b, i: (0, b, 0, 0))
    out = pl.BlockSpec((None, Q_TILE, nh * hd), lambda b, i: (b, i, 0))
    return pl.pallas_call(
        _nsa_dense_kernel,
        grid=(bsz, seq // Q_TILE),
        in_specs=[heads, cmp_kv, cmp_kv, seq_kv, seq_kv,
                  pl.BlockSpec((ng, None, Q_TILE, gates.shape[-1]), lambda b, i: (0, b, i, 0)),
                  pl.BlockSpec(overlap_t.shape, lambda b, i: (0, 0)),
                  heads, seq_kv, seq_kv,
                  pl.BlockSpec(sinks.shape, lambda b, i: (0, 0, 0))],
        out_specs=[out, pl.BlockSpec((ng, None, Q_TILE, LANES), lambda b, i: (0, b, i, 0)), out],
        out_shape=[jax.ShapeDtypeStruct((bsz, seq, nh * hd), BF16),
                   jax.ShapeDtypeStruct((ng, bsz, seq, LANES), BF16),
                   jax.ShapeDtypeStruct((bsz, seq, nh * hd), BF16)],
        compiler_params=_compiler_params(("parallel", "parallel")),
        name="nsa_dense",
    )(q, k_cmp, v_cmp, kw, vw, gates, overlap_t, q_s, k_s, v_s, sinks)


def _nsa_sel_kernel(q_ref, k_ref, v_ref, sel_ref, gate_ref, o_ref, s_ref, m_ref, acc_ref):
    i = pl.program_id(1)
    rows = GROUP_HEADS * Q_TILE
    width = GROUP_HEADS * HEAD_DIM
    n_blk = LANES - HEAD_DIM
    kc = SEL_CHUNK
    seq = k_ref.shape[1]
    groups = range(NSA_KV_HEADS)

    def scores(g, start, lhs):
        return lax.dot_general(k_ref[g, pl.ds(start, kc), :], lhs, _NT, preferred_element_type=F32)

    start_d = pl.multiple_of(jnp.maximum(i * Q_TILE + Q_TILE - kc, 0), Q_TILE)
    n_past = (start_d + kc - 1) // kc
    kpos = start_d + lax.broadcasted_iota(jnp.int32, (kc, rows), 0)
    qpos = i * Q_TILE + (lax.broadcasted_iota(jnp.int32, (kc, rows), 1) & (Q_TILE - 1))
    blk = lax.broadcasted_iota(jnp.int32, (Q_TILE, n_blk), 1)
    lhs_past, lhs_none = [], []
    for g in groups:
        q = q_ref[g * GROUP_HEADS:(g + 1) * GROUP_HEADS].reshape(rows, HEAD_DIM)
        sel = sel_ref[g][:, :n_blk].astype(F32)
        neg_diag = (sel - 1.0) * (-MASKED)
        neg_past = jnp.where(blk < start_d // NSA_SEL_BLOCK, neg_diag, MASKED)
        lhs_diag = jnp.concatenate([q, jnp.concatenate([neg_diag.astype(BF16)] * GROUP_HEADS, axis=0)], axis=-1)
        lhs_past.append(jnp.concatenate([q, jnp.concatenate([neg_past.astype(BF16)] * GROUP_HEADS, axis=0)], axis=-1))
        lhs_none.append(jnp.concatenate([q, jnp.full((rows, n_blk), MASKED, BF16)], axis=-1))
        s_ref[g, 0] = jnp.where(kpos <= qpos, scores(g, start_d, lhs_diag), MASKED)

        m_ref[g] = jnp.full((1, rows), MASKED, F32)
        acc_ref[g] = jnp.zeros((ACC_ROWS, rows), F32)

    def step(c, cur, nxt, lookahead=True):
        next_start = pl.multiple_of(jnp.minimum(c * kc, seq - kc), kc)
        v_start = pl.multiple_of(jnp.where(c == 0, start_d, (c - 1) * kc), Q_TILE)
        for g in groups:
            if lookahead:
                s_ref[g, nxt] = scores(g, next_start, jnp.where(c < n_past, lhs_past[g], lhs_none[g]))
            s = s_ref[g, cur]
            m = m_ref[g]
            m_new = jnp.maximum(m, jnp.max(s, axis=0, keepdims=True))
            alpha = jnp.exp2(m - m_new)
            p = jnp.exp2(s - m_new)
            m_ref[g] = m_new
            acc_ref[g] = alpha * acc_ref[g] + _pv_t(v_ref[g, pl.ds(v_start, kc), :], p)[:ACC_ROWS]

    def body(j, carry):
        step(2 * j, 0, 1)
        step(2 * j + 1, 1, 0)
        return carry

    lax.fori_loop(0, (n_past + 1) // 2, body, 0)

    @pl.when(n_past % 2 == 0)
    def _():
        step(n_past, 0, 1, lookahead=False)

    for g in groups:
        gate = _gate_cols(_gate_rows(gate_ref, g), 1)
        acc = acc_ref[g]
        o_ref[:, g * width:(g + 1) * width] = _heads_from_t(acc[:HEAD_DIM] * (gate / acc[HEAD_DIM:HEAD_DIM + 1])).astype(o_ref.dtype)


def _nsa_sel(q, k, v, sel, gates):
    nh, bsz, seq, hd = q.shape
    ng = k.shape[0]
    onehot = (np.arange(seq)[:, None] // NSA_SEL_BLOCK == np.arange(LANES - hd)[None, :]).astype(np.float32)
    k_ext = jnp.concatenate([k, jnp.broadcast_to(jnp.asarray(onehot, BF16), (ng, bsz, seq, LANES - hd))], axis=-1)
    return pl.pallas_call(
        _nsa_sel_kernel,
        grid=(bsz, seq // Q_TILE),
        in_specs=[pl.BlockSpec((nh, None, Q_TILE, hd), lambda b, i: (0, b, i, 0)),
                  pl.BlockSpec((ng, None, seq, LANES), lambda b, i: (0, b, 0, 0)),
                  pl.BlockSpec((ng, None, seq, LANES), lambda b, i: (0, b, 0, 0)),
                  pl.BlockSpec((ng, None, Q_TILE, LANES), lambda b, i: (0, b, i, 0)),
                  pl.BlockSpec((ng, None, Q_TILE, gates.shape[-1]), lambda b, i: (0, b, i, 0))],
        out_specs=pl.BlockSpec((None, Q_TILE, nh * hd), lambda b, i: (b, i, 0)),
        out_shape=jax.ShapeDtypeStruct((bsz, seq, nh * hd), BF16),
        scratch_shapes=[pltpu.VMEM((ng, 2, SEL_CHUNK, GROUP_HEADS * Q_TILE), F32),
                        pltpu.VMEM((ng, 1, GROUP_HEADS * Q_TILE), F32),
                        pltpu.VMEM((ng, ACC_ROWS, GROUP_HEADS * Q_TILE), F32)],
        compiler_params=_compiler_params(("parallel", "parallel")),
        name="nsa_sel",
    )(q, k_ext, _with_ones(v), sel, gates)


def _overlap_t(seq):
    n_cmp = (seq - NSA_CMP_BLOCK) // NSA_CMP_STRIDE + 1
    n_pad = seq // NSA_CMP_STRIDE
    starts = np.arange(n_pad) * NSA_CMP_STRIDE
    sel_start = np.arange(seq // NSA_SEL_BLOCK) * NSA_SEL_BLOCK
    ov = (starts[None, :] < sel_start[:, None] + NSA_SEL_BLOCK) & (starts[None, :] + NSA_CMP_BLOCK > sel_start[:, None])
    ov = ov & (np.arange(n_pad)[None, :] < n_cmp)
    return jnp.asarray(ov.astype(np.float32))


def _odd_mixer(xt, nw, w_in, w_out, cmp_pe, cmp_w1, cmp_w2, sinks, bsz):
    t, d = xt.shape
    seq = t // bsz
    assert seq // NSA_SEL_BLOCK <= LANES and seq % SEL_CHUNK == 0
    cuts = [int(c) for c in np.cumsum((0,) + ODD_SPLITS)]
    ws = [w_in[:, a:b] for a, b in zip(cuts[:-1], cuts[1:])]
    ws[0] = ws[0] * (HEAD_DIM ** -0.5 * LOG2_E)
    ws[8] = ws[8] * (HEAD_DIM ** -0.5 * LOG2_E)
    ws = [w.astype(BF16) for w in ws]
    hd = HEAD_DIM
    head_dims = (hd, hd, hd, hd, hd, hd, hd, 3 * GROUP_HEADS, hd, hd, hd)
    dtypes = (BF16, F32, F32, BF16, BF16, BF16, BF16, F32, BF16, BF16, BF16)
    q_n, kc, vc, ks, vs, kw, vw, gates, q_s, k_s, v_s = _norm_proj(xt, nw, ws, head_dims, dtypes)

    def heads(a):
        return a.reshape(a.shape[0], bsz, seq, a.shape[-1])

    q_n, kc, vc, ks, vs, kw, vw, gates, q_s, k_s, v_s = map(heads, (q_n, kc, vc, ks, vs, kw, vw, gates, q_s, k_s, v_s))
    w1 = cmp_w1.astype(BF16)
    w2 = cmp_w2.astype(BF16)
    k_cmp = _cmp_kv(kc, cmp_pe[0], w1[0], w2[0])
    v_cmp = _cmp_kv(vc, cmp_pe[1], w1[1], w2[1])
    o_cw, sel, o_swa = _nsa_dense(q_n, k_cmp, v_cmp, kw, vw, gates, _overlap_t(seq), q_s, k_s, v_s,
                                  sinks.reshape(SWA_KV_HEADS, 1, GROUP_HEADS))
    o_sel = _nsa_sel(q_n, ks, vs, sel, gates)
    n_nsa = NSA_HEADS * hd
    return [[o_cw.reshape(t, n_nsa), o_sel.reshape(t, n_nsa)], [o_swa.reshape(t, SWA_HEADS * hd)]], [w_out[:n_nsa], w_out[n_nsa:]]


def kernel(x, mem, norm_w, final_norm_w, mem_norm_w, ffn_w_gate, ffn_w_up, ffn_w_down, xattn_wq, xattn_wkv, xattn_wo, even_w_in, even_w_out, ssm_conv_w, ssm_conv_b, ssm_dt_bias, ssm_a_log, ssm_d, ssm_norm_w, odd_w_in, odd_w_out, nsa_cmp_pe, nsa_cmp_w1, nsa_cmp_w2, swa_sinks):
    bsz, seq, d = x.shape
    t = bsz * seq
    m = mem.shape[1]
    wg, wu, wd = ffn_w_gate, ffn_w_up, ffn_w_down
    wkv = xattn_wkv.astype(BF16)
    even_out = even_w_out.astype(BF16)
    odd_out = odd_w_out.astype(BF16)

    xt = x.reshape(t, d)
    mem2 = mem.reshape(bsz * m, d)
    mem_nw = mem_norm_w.reshape(1, d)
    for layer in range(DEPTH):
        i = layer // 2
        xt = _ffn(xt, norm_w[layer, 0].reshape(1, d), wg, wu, wd, layer, 0)
        nw1 = norm_w[layer, 1].reshape(1, d)
        if layer % 2 == 0:
            mix, mix_w = _even_mixer(xt, nw1, even_w_in[i], even_out[i], ssm_conv_w[i], ssm_conv_b[i], ssm_dt_bias[i], ssm_a_log[i], ssm_d[i], ssm_norm_w[i], bsz)
        else:
            mix, mix_w = _odd_mixer(xt, nw1, odd_w_in[i], odd_out[i], nsa_cmp_pe[i], nsa_cmp_w1[i], nsa_cmp_w2[i], swa_sinks[i], bsz)
        (kv,) = _norm_proj(mem2, mem_nw, [wkv[layer]])
        xt = _mix_xattn(xt, mix, mix_w, norm_w[layer, 2].reshape(1, d), xattn_wq, kv.reshape(bsz, m, 2 * d), xattn_wo, layer, bsz)
        out_nw = final_norm_w.reshape(1, d) if layer == DEPTH - 1 else None
        xt = _ffn(xt, norm_w[layer, 3].reshape(1, d), wg, wu, wd, layer, 1, out_nw)
    return xt.reshape(bsz, seq, d)
```

```python
import functools

import numpy as np
import jax
import jax.numpy as jnp
from jax import lax
from jax.experimental import pallas as pl
from jax.experimental.pallas import tpu as pltpu

F32 = jnp.float32
BF16 = jnp.bfloat16

D_MODEL = 1024
DEPTH = 4
D_FF = 2816
NORM_EPS = 1e-6
ATTN_BLOCK = 128

RET_HEADS = 4
RET_DK = 128
RET_DV = 128
RET_CHUNK = 128
RET_ROPE_BASE = 10000.0

SSM_HEADS = 8
SSM_HEAD_DIM = 64
SSM_D_INNER = SSM_HEADS * SSM_HEAD_DIM
SSM_D_STATE = 64
SSM_GROUPS = 2
SSM_CONV = 4
SSM_CHUNK = 128
SSM_CONV_DIM = SSM_D_INNER + 2 * SSM_GROUPS * SSM_D_STATE

EVEN_SPLITS = (RET_HEADS * RET_DK, RET_HEADS * RET_DK, RET_HEADS * RET_DV, RET_HEADS * RET_DV, SSM_D_INNER, SSM_CONV_DIM, SSM_HEADS)

NSA_HEADS = 8
NSA_KV_HEADS = 2
NSA_HEAD_DIM = 64
NSA_CMP_BLOCK = 32
NSA_CMP_STRIDE = 16
NSA_CMP_HIDDEN = 256
NSA_SEL_BLOCK = 64
NSA_TOPN = 16
NSA_WINDOW = 512
NSA_FORCE_SCORE = 1e6

SWA_HEADS = 8
SWA_KV_HEADS = 2
SWA_HEAD_DIM = 64
SWA_WINDOW = 128

ODD_SPLITS = (NSA_HEADS * NSA_HEAD_DIM,) + (NSA_KV_HEADS * NSA_HEAD_DIM,) * 6 + (3 * NSA_HEADS, SWA_HEADS * SWA_HEAD_DIM, SWA_KV_HEADS * SWA_HEAD_DIM, SWA_KV_HEADS * SWA_HEAD_DIM)

XATTN_HEADS = 4
XATTN_HEAD_DIM = D_MODEL // XATTN_HEADS

VMEM_LIMIT_BYTES = 52 * 1024 * 1024
TOKEN_TILE = 512
FFN_TILE = 256
FFN_DOWN_CHUNK = 1024
LANES = 128
SUBLANES = 8
MASKED = -1e30
LOG2_E = float(np.log2(np.e))

_NT = (((1,), (1,)), ((), ()))


def _compiler_params(semantics):
    return pltpu.CompilerParams(dimension_semantics=semantics, vmem_limit_bytes=VMEM_LIMIT_BYTES)


def _rms(x, w):
    return x * lax.rsqrt(jnp.mean(x * x, axis=-1, keepdims=True) + NORM_EPS) * w


def _ffn_kernel(has_out_norm, x_ref, nw_ref, wg_ref, wu_ref, wd_ref, *refs):
    out_nw_ref = refs[0] if has_out_norm else None
    o_ref, a_ref = refs[-2:]
    x = x_ref[...]
    h = _rms(x, nw_ref[...]).astype(BF16)
    for c in range(D_FF // FFN_TILE):
        cols = slice(c * FFN_TILE, (c + 1) * FFN_TILE)
        g = jnp.dot(h, wg_ref[:, cols].astype(BF16), preferred_element_type=F32)
        u = jnp.dot(h, wu_ref[:, cols].astype(BF16), preferred_element_type=F32)
        a_ref[:, cols] = (g * jax.nn.sigmoid(g) * u).astype(BF16)
    acc = None
    for lo in range(0, D_FF, FFN_DOWN_CHUNK):
        hi = min(lo + FFN_DOWN_CHUNK, D_FF)
        part = jnp.dot(a_ref[:, lo:hi], wd_ref[lo:hi, :].astype(BF16), preferred_element_type=F32)
        acc = part if acc is None else acc + part
    y = x + 0.5 * acc
    o_ref[...] = _rms(y, out_nw_ref[...]) if has_out_norm else y


def _ffn(x, nw, wg, wu, wd, layer, k, out_nw=None):
    t, d = x.shape
    tm = TOKEN_TILE
    resident = pl.Buffered(1)
    vec = pl.BlockSpec((1, d), lambda i: (0, 0))
    return pl.pallas_call(
        functools.partial(_ffn_kernel, out_nw is not None),
        grid=(t // tm,),
        in_specs=[
            pl.BlockSpec((tm, d), lambda i: (i, 0)),
            vec,
            pl.BlockSpec((None, None, d, D_FF), lambda i: (layer, k, 0, 0), pipeline_mode=resident),
            pl.BlockSpec((None, None, d, D_FF), lambda i: (layer, k, 0, 0), pipeline_mode=resident),
            pl.BlockSpec((None, None, D_FF, d), lambda i: (layer, k, 0, 0), pipeline_mode=resident),
        ] + ([vec] if out_nw is not None else []),
        out_specs=pl.BlockSpec((tm, d), lambda i: (i, 0)),
        out_shape=jax.ShapeDtypeStruct((t, d), F32),
        scratch_shapes=[pltpu.VMEM((tm, D_FF), BF16)],
        compiler_params=_compiler_params(("parallel",)),
        name="ffn",
    )(x, nw, wg, wu, wd, *([out_nw] if out_nw is not None else []))


def _norm_proj_kernel(head_dims, offsets, x_ref, nw_ref, w_ref, *o_refs):
    h = _rms(x_ref[...], nw_ref[...]).astype(BF16)
    y_all = jnp.dot(h, w_ref[...], preferred_element_type=F32)
    for hd, off, o_ref in zip(head_dims, offsets, o_refs):
        if hd is None:
            o_ref[...] = y_all[:, off:off + o_ref.shape[-1]].astype(o_ref.dtype)
        else:
            for j in range(o_ref.shape[0]):
                o_ref[j] = y_all[:, off + j * hd:off + (j + 1) * hd].astype(o_ref.dtype)


def _norm_proj(x, nw, weights, head_dims=None, dtypes=None, tm=TOKEN_TILE):
    t, d = x.shape
    n_out = len(weights)
    head_dims = tuple(head_dims) if head_dims is not None else (None,) * n_out
    dtypes = tuple(dtypes) if dtypes is not None else (F32,) * n_out
    out_specs, out_shape, offsets, padded = [], [], [], []
    total = 0
    for w, hd, dt in zip(weights, head_dims, dtypes):
        n = w.shape[1]
        if hd is None:
            out_specs.append(pl.BlockSpec((tm, n), lambda i: (i, 0)))
            out_shape.append(jax.ShapeDtypeStruct((t, n), dt))
        else:
            out_specs.append(pl.BlockSpec((n // hd, tm, hd), lambda i: (0, i, 0)))
            out_shape.append(jax.ShapeDtypeStruct((n // hd, t, hd), dt))
        offsets.append(total)
        width = -(-n // LANES) * LANES
        padded.append(jnp.pad(w, ((0, 0), (0, width - n))))
        total += width
    w_all = jnp.concatenate(padded, axis=1) if n_out > 1 else padded[0]
    return pl.pallas_call(
        functools.partial(_norm_proj_kernel, head_dims, tuple(offsets)),
        grid=(t // tm,),
        in_specs=[pl.BlockSpec((tm, d), lambda i: (i, 0)), pl.BlockSpec((1, d), lambda i: (0, 0)),
                  pl.BlockSpec(w_all.shape, lambda i: (0, 0))],
        out_specs=out_specs,
        out_shape=out_shape,
        compiler_params=_compiler_params(("parallel",)),
        name="norm_proj",
    )(x, nw, w_all)


def _mix_xattn_kernel(group_sizes, x_ref, *refs):
    n_a = sum(group_sizes)
    n_g = len(group_sizes)
    a_refs, w_refs = refs[:n_a], refs[n_a:n_a + n_g]
    nw_ref, wq_ref, kv_ref, wo_ref, o_ref = refs[n_a + n_g:]
    x = x_ref[...]
    pos = 0
    for size, w_ref in zip(group_sizes, w_refs):
        a = a_refs[pos][...]
        for r in a_refs[pos + 1:pos + size]:
            a = a.astype(F32) + r[...].astype(F32)
        pos += size
        x = x + jnp.dot(a.astype(BF16), w_ref[...], preferred_element_type=F32)
    h = _rms(x, nw_ref[...]).astype(BF16)
    q = jnp.dot(h, wq_ref[...].astype(BF16), preferred_element_type=F32).astype(BF16)
    hd = XATTN_HEAD_DIM
    outs = []
    for hh in range(XATTN_HEADS):
        k = kv_ref[:, hh * hd:(hh + 1) * hd].astype(BF16)
        v = kv_ref[:, D_MODEL + hh * hd:D_MODEL + (hh + 1) * hd].astype(BF16)
        s = lax.dot_general(q[:, hh * hd:(hh + 1) * hd], k, _NT, preferred_element_type=F32)
        s = s * (hd ** -0.5)
        p = jnp.exp(s - jnp.max(s, axis=-1, keepdims=True))
        p = p / jnp.sum(p, axis=-1, keepdims=True)
        outs.append(jnp.dot(p.astype(BF16), v, preferred_element_type=F32).astype(BF16))
    o = jnp.concatenate(outs, axis=-1)
    o_ref[...] = x + jnp.dot(o, wo_ref[...].astype(BF16), preferred_element_type=F32)


def _mix_xattn(x, groups, weights, nw, wq, kv, wo, layer, bsz, tm=2 * TOKEN_TILE):
    t, d = x.shape
    seq = t // bsz
    nt = seq // tm
    m = kv.shape[1]
    arrays = [a for grp in groups for a in grp]

    def tok(width):
        return pl.BlockSpec((tm, width), lambda b, i: (b * nt + i, 0))

    def const(shape):
        return pl.BlockSpec(shape, lambda b, i: (0, 0))

    layer_w = pl.BlockSpec((None, d, d), lambda b, i: (layer, 0, 0))
    return pl.pallas_call(
        functools.partial(_mix_xattn_kernel, tuple(len(grp) for grp in groups)),
        grid=(bsz, nt),
        in_specs=[tok(d)] + [tok(a.shape[1]) for a in arrays] + [const(w.shape) for w in weights]
        + [const((1, d)), layer_w, pl.BlockSpec((None, m, 2 * d), lambda b, i: (b, 0, 0)), layer_w],
        out_specs=tok(d),
        out_shape=jax.ShapeDtypeStruct((t, d), F32),
        compiler_params=_compiler_params(("parallel", "parallel")),
        name="mix_xattn",
    )(x, *arrays, *weights, nw, wq, kv, wo)


def _retention_kernel(q_ref, k_ref, v_ref, g_ref, cos_ref, sin_ref, dmat_ref, zeta_ref, xi_ref, decay_ref, o_ref, state_ref):
    @pl.when(pl.program_id(1) == 0)
    def _():
        state_ref[...] = jnp.zeros_like(state_ref)

    cos = cos_ref[...]
    sin = sin_ref[...]
    half = RET_DK // 2
    outs = []
    for h in range(RET_HEADS):
        qh = q_ref[:, h * RET_DK:(h + 1) * RET_DK]
        kh = k_ref[:, h * RET_DK:(h + 1) * RET_DK]
        qr = qh * cos + pltpu.roll(qh, half, 1) * sin
        kr = (kh * cos + pltpu.roll(kh, half, 1) * sin) * (RET_DK ** -0.5)
        vh = v_ref[:, h * RET_DV:(h + 1) * RET_DV].astype(BF16)
        scores = lax.dot_general(qr.astype(BF16), kr.astype(BF16), _NT, preferred_element_type=F32) * dmat_ref[h]
        inner = jnp.dot(scores.astype(BF16), vh, preferred_element_type=F32)
        state = state_ref[h]
        cross = jnp.dot((qr * xi_ref[h]).astype(BF16), state.astype(BF16), preferred_element_type=F32)
        kz = (kr * zeta_ref[h]).T.astype(BF16)
        state_ref[h] = decay_ref[h] * state + jnp.dot(kz, vh, preferred_element_type=F32)
        o = inner + cross
        o = o * lax.rsqrt(jnp.mean(o * o, axis=-1, keepdims=True) + NORM_EPS)
        gh = g_ref[:, h * RET_DV:(h + 1) * RET_DV]
        outs.append(gh * jax.nn.sigmoid(gh) * o)
    o_ref[...] = jnp.concatenate(outs, axis=-1).astype(o_ref.dtype)


def _retention_tables(seq):
    L = RET_CHUNK
    pos = jnp.arange(seq, dtype=F32)
    inv_freq = 1.0 / (RET_ROPE_BASE ** jnp.linspace(0.0, 1.0, RET_DK // 2, dtype=F32))
    ang = pos[:, None] * inv_freq[None, :]
    cos, sin = jnp.cos(ang), jnp.sin(ang)
    cos2 = jnp.concatenate([cos, cos], axis=-1)
    sin2 = jnp.concatenate([-sin, sin], axis=-1)
    log_g = jnp.log1p(-jnp.exp2(-5.0 - jnp.arange(RET_HEADS, dtype=F32)))
    idx = jnp.arange(L, dtype=F32)
    diff = idx[:, None] - idx[None, :]
    dmat = jnp.where(diff >= 0, jnp.exp(jnp.maximum(diff, 0.0)[None] * log_g[:, None, None]), 0.0)
    ones = jnp.ones((RET_HEADS, L, RET_DK), F32)
    zeta = jnp.exp((L - 1 - idx)[None, :] * log_g[:, None])[:, :, None] * ones
    xi = jnp.exp((idx + 1)[None, :] * log_g[:, None])[:, :, None] * ones
    decay = jnp.exp(L * log_g)[:, None, None] * ones
    return cos2, sin2, dmat, zeta, xi, decay


CONV_PAD = SUBLANES


def _ssd_kernel(xbc_ref, dt_ref, z_ref, convw_ref, convb_ref, dtb_ref, alog_ref, dskip_ref, nw_ref, tri_ref, o_ref, ext_ref, state_ref):
    L = SSM_CHUNK
    hp = SSM_HEAD_DIM
    ns = SSM_D_STATE

    @pl.when(pl.program_id(1) == 0)
    def _():
        ext_ref[0:CONV_PAD, :] = jnp.zeros((CONV_PAD, SSM_CONV_DIM), F32)
        state_ref[...] = jnp.zeros_like(state_ref)

    ext_ref[CONV_PAD:CONV_PAD + L, :] = xbc_ref[...]
    conv = convb_ref[...]
    for w in range(SSM_CONV):
        off = CONV_PAD - (SSM_CONV - 1) + w
        conv = conv + ext_ref[off:off + L, :] * convw_ref[w:w + 1, :]
    ext_ref[0:CONV_PAD, :] = ext_ref[L:L + CONV_PAD, :]
    xa = conv * jax.nn.sigmoid(conv)
    xs = xa[:, :SSM_D_INNER]
    bmat = xa[:, SSM_D_INNER:SSM_D_INNER + SSM_GROUPS * ns]
    cmat = xa[:, SSM_D_INNER + SSM_GROUPS * ns:]

    x = dt_ref[...] + dtb_ref[...]
    dt = jnp.maximum(x, 0.0) + jnp.log1p(jnp.exp(-jnp.abs(x)))
    lane = lax.broadcasted_iota(jnp.int32, (L, LANES), 1)
    dta = jnp.where(lane < SSM_HEADS, dt * -jnp.exp(alog_ref[...]), 0.0)
    cums = jnp.dot(tri_ref[...], dta, preferred_element_type=F32, precision=lax.Precision.HIGHEST)
    cums_t = cums.T
    last = cums[L - 1:L, :]
    decay_to_end = jnp.exp(last - cums)
    decay_from_start = jnp.exp(cums)
    chunk_decay = jnp.exp(last)
    causal = lax.broadcasted_iota(jnp.int32, (L, L), 0) >= lax.broadcasted_iota(jnp.int32, (L, L), 1)
    heads_per_group = SSM_HEADS // SSM_GROUPS
    ys = []
    for g in range(SSM_GROUPS):
        bg = bmat[:, g * ns:(g + 1) * ns].astype(BF16)
        cg = cmat[:, g * ns:(g + 1) * ns].astype(BF16)
        cb = lax.dot_general(cg, bg, _NT, preferred_element_type=F32)
        for e in range(heads_per_group):
            h = g * heads_per_group + e
            xh = xs[:, h * hp:(h + 1) * hp]
            xdt = xh * dt[:, h:h + 1]
            seg = cums[:, h:h + 1] - cums_t[h:h + 1, :]
            lmat = jnp.where(causal, jnp.exp(seg), 0.0)
            y_diag = jnp.dot((cb * lmat).astype(BF16), xdt.astype(BF16), preferred_element_type=F32)
            state = state_ref[h]
            y_off = lax.dot_general(cg, state.astype(BF16), _NT, preferred_element_type=F32) * decay_from_start[:, h:h + 1]
            xw = (xdt * decay_to_end[:, h:h + 1]).astype(BF16)
            new = lax.dot_general(xw, bg, (((0,), (0,)), ((), ())), preferred_element_type=F32)
            state_ref[h] = chunk_decay[:, h:h + 1] * state + new
            ys.append(y_diag + y_off + dskip_ref[:, h * hp:(h + 1) * hp] * xh)
    y = jnp.concatenate(ys, axis=-1)
    z = z_ref[...]
    o_ref[...] = _rms(y * (z * jax.nn.sigmoid(z)), nw_ref[...]).astype(o_ref.dtype)


N_RET_IN = 10
N_SSD_IN = 10


def _even_core_kernel(*refs):
    ret_in, ssd_in = refs[:N_RET_IN], refs[N_RET_IN:N_RET_IN + N_SSD_IN]
    o_ret_ref, o_ssd_ref, ret_state_ref, ext_ref, ssd_state_ref = refs[N_RET_IN + N_SSD_IN:]
    _retention_kernel(*ret_in, o_ret_ref, ret_state_ref)
    _ssd_kernel(*ssd_in, o_ssd_ref, ext_ref, ssd_state_ref)


def _even_core(q, k, v, g, xbc, dt_raw, z, conv_w, conv_b, dt_bias, a_log, d_skip, norm_w):
    bsz, seq, width = q.shape
    L = RET_CHUNK
    assert SSM_CHUNK == L
    tables = _retention_tables(seq)
    pad = LANES - SSM_HEADS
    params = [conv_w, conv_b.reshape(1, -1), jnp.pad(dt_bias, (0, pad)).reshape(1, LANES),
              jnp.pad(a_log, (0, pad)).reshape(1, LANES), jnp.repeat(d_skip, SSM_HEAD_DIM).reshape(1, -1),
              norm_w.reshape(1, -1), jnp.asarray(np.tril(np.ones((L, L), np.float32)))]

    def tok(w):
        return pl.BlockSpec((None, L, w), lambda b, c: (b, c, 0))

    rope = pl.BlockSpec((L, RET_DK), lambda b, c: (c, 0))
    const = pl.BlockSpec((RET_HEADS, L, RET_DK), lambda b, c: (0, 0, 0))
    return pl.pallas_call(
        _even_core_kernel,
        grid=(bsz, seq // L),
        in_specs=[tok(width)] * 4 + [rope, rope, const, const, const, const]
        + [tok(SSM_CONV_DIM), tok(LANES), tok(SSM_D_INNER)] + [pl.BlockSpec(p.shape, lambda b, c: (0, 0)) for p in params],
        out_specs=[tok(width), tok(SSM_D_INNER)],
        out_shape=[jax.ShapeDtypeStruct((bsz, seq, width), BF16), jax.ShapeDtypeStruct((bsz, seq, SSM_D_INNER), BF16)],
        scratch_shapes=[pltpu.VMEM((RET_HEADS, RET_DK, RET_DV), F32),
                        pltpu.VMEM((L + CONV_PAD, SSM_CONV_DIM), F32), pltpu.VMEM((SSM_HEADS, SSM_HEAD_DIM, SSM_D_STATE), F32)],
        compiler_params=_compiler_params(("parallel", "arbitrary")),
        name="even_core",
    )(q, k, v, g, *tables, xbc, dt_raw, z, *params)


def _even_mixer(xt, nw, w_in, w_out, conv_w, conv_b, dt_bias, a_log, d_skip, ssm_norm_w, bsz):
    t, d = xt.shape
    seq = t // bsz
    cuts = [int(c) for c in np.cumsum((0,) + EVEN_SPLITS)]
    ws = [w_in[:, a:b].astype(BF16) for a, b in zip(cuts[:-1], cuts[1:])]
    perm = np.concatenate([h * RET_DK + np.concatenate([np.arange(0, RET_DK, 2), np.arange(1, RET_DK, 2)]) for h in range(RET_HEADS)])
    ws[0] = ws[0][:, perm]
    ws[1] = ws[1][:, perm]
    ws[6] = jnp.pad(ws[6], ((0, 0), (0, LANES - SSM_HEADS)))
    dtypes = (F32, F32, BF16, F32, F32, F32, F32)
    q, k, v, g, z, xbc, dt_raw = [a.reshape(bsz, seq, -1) for a in _norm_proj(xt, nw, ws, dtypes=dtypes)]
    o_ret, y = _even_core(q, k, v, g, xbc, dt_raw, z, conv_w, conv_b, dt_bias, a_log, d_skip, ssm_norm_w)
    n_ret = RET_HEADS * RET_DV
    return [[o_ret.reshape(t, n_ret)], [y.reshape(t, SSM_D_INNER)]], [w_out[:n_ret], w_out[n_ret:]]


Q_TILE = ATTN_BLOCK
GROUP_HEADS = NSA_HEADS // NSA_KV_HEADS
HEAD_DIM = NSA_HEAD_DIM
SEL_CHUNK = 512


def _cmp_kv_kernel(x_ref, pe_ref, w1_ref, w2_ref, o_ref):
    n = x_ref.shape[0] // NSA_CMP_STRIDE
    top = bot = None
    for r in range(NSA_CMP_STRIDE):
        rows = x_ref[pl.ds(r, n, stride=NSA_CMP_STRIDE), :]
        lo = r * HEAD_DIM
        hi = (NSA_CMP_STRIDE + r) * HEAD_DIM
        a = jnp.dot((rows + pe_ref[r:r + 1, :]).astype(BF16), w1_ref[lo:lo + HEAD_DIM, :], preferred_element_type=F32)
        b = jnp.dot((rows + pe_ref[NSA_CMP_STRIDE + r:NSA_CMP_STRIDE + r + 1, :]).astype(BF16), w1_ref[hi:hi + HEAD_DIM, :], preferred_element_type=F32)
        top = a if top is None else top + a
        bot = b if bot is None else bot + b
    hidden = top + pltpu.roll(bot, n - 1, 0)
    act = hidden * jax.nn.sigmoid(hidden)
    o_ref[...] = jnp.dot(act.astype(BF16), w2_ref[...], preferred_element_type=F32)


def _cmp_kv(x, pe, w1, w2):
    ng, bsz, seq, hd = x.shape
    n = seq // NSA_CMP_STRIDE
    return pl.pallas_call(
        _cmp_kv_kernel,
        grid=(ng, bsz),
        in_specs=[pl.BlockSpec((None, None, seq, hd), lambda g, b: (g, b, 0, 0)),
                  pl.BlockSpec(pe.shape, lambda g, b: (0, 0)),
                  pl.BlockSpec(w1.shape, lambda g, b: (0, 0)),
                  pl.BlockSpec(w2.shape, lambda g, b: (0, 0))],
        out_specs=pl.BlockSpec((None, None, n, hd), lambda g, b: (g, b, 0, 0)),
        out_shape=jax.ShapeDtypeStruct((ng, bsz, n, hd), F32),
        compiler_params=_compiler_params(("parallel", "parallel")),
        name="nsa_compress",
    )(x, pe, w1, w2)


def _heads_from_t(o_t):
    pairs = []
    for e in range(0, GROUP_HEADS, 2):
        pair = jnp.concatenate([o_t[:, e * Q_TILE:(e + 1) * Q_TILE], o_t[:, (e + 1) * Q_TILE:(e + 2) * Q_TILE]], axis=0)
        pairs.append(pair.T)
    return jnp.concatenate(pairs, axis=-1)


def _gate_rows(gate_ref, g):
    gate = jax.nn.sigmoid(gate_ref[g])
    gate = jnp.concatenate([gate, jnp.zeros((Q_TILE, LANES - gate.shape[1]), F32)], axis=1)
    return gate.T


def _gate_cols(gate_t, branch):
    return jnp.concatenate([gate_t[3 * e + branch:3 * e + branch + 1, :] for e in range(GROUP_HEADS)], axis=1)


def _tile_heads(x):
    return jnp.concatenate([x] * GROUP_HEADS, axis=1)


def _pv_t(v, p):
    return lax.dot_general(v, p.astype(BF16), (((0,), (0,)), ((), ())), preferred_element_type=F32)


ACC_ROWS = HEAD_DIM + SUBLANES


def _with_ones(v):
    pad = jnp.zeros(v.shape[:-1] + (LANES - HEAD_DIM - 1,), v.dtype)
    return jnp.concatenate([v, jnp.ones(v.shape[:-1] + (1,), v.dtype), pad], axis=-1)


def _band_window(window):
    return (-(-(window - 1) // Q_TILE) + 1) * Q_TILE


def _band_branch(i, q, k_ref, v_ref, g, window, sink=None):
    n_keys = _band_window(window)
    start = pl.multiple_of(jnp.maximum(i * Q_TILE + Q_TILE - n_keys, 0), Q_TILE)
    rel = (i * Q_TILE + lax.broadcasted_iota(jnp.int32, (n_keys, Q_TILE), 1)) - (start + lax.broadcasted_iota(jnp.int32, (n_keys, Q_TILE), 0))
    bias = _tile_heads(jnp.where(rel >= 0, jnp.where(rel < window, 0.0, MASKED), MASKED))
    s = lax.dot_general(k_ref[g, pl.ds(start, n_keys), :], q, _NT, preferred_element_type=F32) + bias
    m = jnp.max(s, axis=0, keepdims=True)
    if sink is not None:
        m = jnp.maximum(m, sink)
    p = jnp.exp2(s - m)
    denom = jnp.sum(p, axis=0, keepdims=True)
    denom = denom + jnp.exp2(sink - m) if sink is not None else jnp.maximum(denom, 1e-30)
    return _pv_t(v_ref[g, pl.ds(start, n_keys), :], p) * (1.0 / denom)


def _nsa_dense_kernel(q_ref, kc_ref, vc_ref, kw_ref, vw_ref, gate_ref, ov_ref, qs_ref, ks_ref, vs_ref, sink_ref,
                      o_ref, sel_ref, swa_ref):
    i = pl.program_id(1)
    rows = GROUP_HEADS * Q_TILE
    width = GROUP_HEADS * HEAD_DIM
    n_cmp = kc_ref.shape[1]
    n = lax.broadcasted_iota(jnp.int32, (n_cmp, Q_TILE), 0)
    t = i * Q_TILE + lax.broadcasted_iota(jnp.int32, (n_cmp, Q_TILE), 1)
    valid = _tile_heads(jnp.where(n * NSA_CMP_STRIDE + (NSA_CMP_BLOCK - 1) <= t, 1.0, 0.0))
    bias = (valid - 1.0) * (-MASKED)
    n_sel = ov_ref.shape[0]
    jb = lax.broadcasted_iota(jnp.int32, (n_sel, Q_TILE), 0)
    cur = (i * Q_TILE + lax.broadcasted_iota(jnp.int32, (n_sel, Q_TILE), 1)) // NSA_SEL_BLOCK
    valid_b = jb <= cur
    forced = jnp.where(valid_b, jnp.where(jb == 0, 1.0, 0.0) + jnp.where(jb == cur, 1.0, 0.0) + jnp.where(jb == cur - 1, 1.0, 0.0), 0.0)
    for g in range(NSA_KV_HEADS):
        q = q_ref[g * GROUP_HEADS:(g + 1) * GROUP_HEADS].reshape(rows, HEAD_DIM)
        s = lax.dot_general(kc_ref[g].astype(BF16), q, _NT, preferred_element_type=F32) + bias
        m = jnp.max(s, axis=0, keepdims=True)
        p = jnp.exp2(s - m) * valid
        p = p * (1.0 / jnp.maximum(jnp.sum(p, axis=0, keepdims=True), 1e-30))
        o_cmp = _pv_t(vc_ref[g].astype(BF16), p)
        o_win = _band_branch(i, q, kw_ref, vw_ref, g, NSA_WINDOW)
        gate_t = _gate_rows(gate_ref, g)
        o_ref[:, g * width:(g + 1) * width] = _heads_from_t(o_cmp * _gate_cols(gate_t, 0) + o_win * _gate_cols(gate_t, 2)).astype(o_ref.dtype)

        p_sum = p[:, 0:Q_TILE]
        for e in range(1, GROUP_HEADS):
            p_sum = p_sum + p[:, e * Q_TILE:(e + 1) * Q_TILE]
        imp = jnp.dot(ov_ref[...], p_sum, preferred_element_type=F32, precision=lax.Precision.HIGHEST)
        score = jnp.where(forced > 0.5, NSA_FORCE_SCORE, jnp.where(valid_b, imp, -jnp.inf))
        sub = SUBLANES
        ranks = []
        for v in range(n_sel // sub):
            blk_scores = score[v * sub:(v + 1) * sub, :]
            jb_v = jb[v * sub:(v + 1) * sub, :]
            rank = jnp.zeros((sub, Q_TILE), F32)
            for ii in range(n_sel):
                row = score[ii:ii + 1, :]
                if ii < v * sub:
                    ahead = jnp.where(row >= blk_scores, 1.0, 0.0)
                elif ii >= (v + 1) * sub:
                    ahead = jnp.where(row > blk_scores, 1.0, 0.0)
                else:
                    tie = jnp.where(jb_v > ii, 1.0, 0.0)
                    ahead = jnp.where(row > blk_scores, 1.0, jnp.where(row == blk_scores, tie, 0.0))
                rank = rank + ahead
            ranks.append(rank)
        rank = jnp.concatenate(ranks, axis=0)
        sel_t = jnp.where(valid_b, jnp.where(rank < NSA_TOPN, 1.0, 0.0), 0.0)
        sel_t = jnp.concatenate([sel_t, jnp.zeros((LANES - n_sel, Q_TILE), F32)], axis=0)
        sel_ref[g] = sel_t.T.astype(BF16)

    for g in range(SWA_KV_HEADS):
        q = qs_ref[g * GROUP_HEADS:(g + 1) * GROUP_HEADS].reshape(rows, HEAD_DIM)
        sink = LOG2_E * jnp.concatenate([jnp.broadcast_to(sink_ref[g][:, e:e + 1], (1, Q_TILE)) for e in range(GROUP_HEADS)], axis=1)
        o_swa = _band_branch(i, q, ks_ref, vs_ref, g, SWA_WINDOW, sink)
        swa_ref[:, g * width:(g + 1) * width] = _heads_from_t(o_swa).astype(swa_ref.dtype)


def _nsa_dense(q, k_cmp, v_cmp, kw, vw, gates, overlap_t, q_s, k_s, v_s, sinks):
    nh, bsz, seq, hd = q.shape
    ng, _, n_cmp, _ = k_cmp.shape
    heads = pl.BlockSpec((nh, None, Q_TILE, hd), lambda b, i: (0, b, i, 0))
    cmp_kv = pl.BlockSpec((ng, None, n_cmp, hd), lambda b, i: (0, b, 0, 0))
    seq_kv = pl.BlockSpec((ng, None, seq, hd), lambda b, i: (0, b, 0, 0))
    out = pl.BlockSpec((None, Q_TILE, nh * hd), lambda b, i: (b, i, 0))
    return pl.pallas_call(
        _nsa_dense_kernel,
        grid=(bsz, seq // Q_TILE),
        in_specs=[heads, cmp_kv, cmp_kv, seq_kv, seq_kv,
                  pl.BlockSpec((ng, None, Q_TILE, gates.shape[-1]), lambda b, i: (0, b, i, 0)),
                  pl.BlockSpec(overlap_t.shape, lambda b, i: (0, 0)),
                  heads, seq_kv, seq_kv,
                  pl.BlockSpec(sinks.shape, lambda b, i: (0, 0, 0))],
        out_specs=[out, pl.BlockSpec((ng, None, Q_TILE, LANES), lambda b, i: (0, b, i, 0)), out],
        out_shape=[jax.ShapeDtypeStruct((bsz, seq, nh * hd), BF16),
                   jax.ShapeDtypeStruct((ng, bsz, seq, LANES), BF16),
                   jax.ShapeDtypeStruct((bsz, seq, nh * hd), BF16)],
        compiler_params=_compiler_params(("parallel", "parallel")),
        name="nsa_dense",
    )(q, k_cmp, v_cmp, kw, vw, gates, overlap_t, q_s, k_s, v_s, sinks)


def _nsa_sel_kernel(q_ref, k_ref, v_ref, sel_ref, gate_ref, o_ref, s_ref, m_ref, acc_ref):
    i = pl.program_id(1)
    rows = GROUP_HEADS * Q_TILE
    width = GROUP_HEADS * HEAD_DIM
    n_blk = LANES - HEAD_DIM
    kc = SEL_CHUNK
    seq = k_ref.shape[1]
    groups = range(NSA_KV_HEADS)

    def scores(g, start, lhs):
        return lax.dot_general(k_ref[g, pl.ds(start, kc), :], lhs, _NT, preferred_element_type=F32)

    start_d = pl.multiple_of(jnp.maximum(i * Q_TILE + Q_TILE - kc, 0), Q_TILE)
    n_past = (start_d + kc - 1) // kc
    kpos = start_d + lax.broadcasted_iota(jnp.int32, (kc, rows), 0)
    qpos = i * Q_TILE + (lax.broadcasted_iota(jnp.int32, (kc, rows), 1) & (Q_TILE - 1))
    blk = lax.broadcasted_iota(jnp.int32, (Q_TILE, n_blk), 1)
    lhs_past, lhs_none = [], []
    for g in groups:
        q = q_ref[g * GROUP_HEADS:(g + 1) * GROUP_HEADS].reshape(rows, HEAD_DIM)
        sel = sel_ref[g][:, :n_blk].astype(F32)
        neg_diag = (sel - 1.0) * (-MASKED)
        neg_past = jnp.where(blk < start_d // NSA_SEL_BLOCK, neg_diag, MASKED)
        lhs_diag = jnp.concatenate([q, jnp.concatenate([neg_diag.astype(BF16)] * GROUP_HEADS, axis=0)], axis=-1)
        lhs_past.append(jnp.concatenate([q, jnp.concatenate([neg_past.astype(BF16)] * GROUP_HEADS, axis=0)], axis=-1))
        lhs_none.append(jnp.concatenate([q, jnp.full((rows, n_blk), MASKED, BF16)], axis=-1))
        s_ref[g, 0] = jnp.where(kpos <= qpos, scores(g, start_d, lhs_diag), MASKED)

        m_ref[g] = jnp.full((1, rows), MASKED, F32)
        acc_ref[g] = jnp.zeros((ACC_ROWS, rows), F32)

    def step(c, cur, nxt, lookahead=True):
        next_start = pl.multiple_of(jnp.minimum(c * kc, seq - kc), kc)
        v_start = pl.multiple_of(jnp.where(c == 0, start_d, (c - 1) * kc), Q_TILE)
        for g in groups:
            if lookahead:
                s_ref[g, nxt] = scores(g, next_start, jnp.where(c < n_past, lhs_past[g], lhs_none[g]))
            s = s_ref[g, cur]
            m = m_ref[g]
            m_new = jnp.maximum(m, jnp.max(s, axis=0, keepdims=True))
            alpha = jnp.exp2(m - m_new)
            p = jnp.exp2(s - m_new)
            m_ref[g] = m_new
            acc_ref[g] = alpha * acc_ref[g] + _pv_t(v_ref[g, pl.ds(v_start, kc), :], p)[:ACC_ROWS]

    def body(j, carry):
        step(2 * j, 0, 1)
        step(2 * j + 1, 1, 0)
        return carry

    lax.fori_loop(0, (n_past + 1) // 2, body, 0)

    @pl.when(n_past % 2 == 0)
    def _():
        step(n_past, 0, 1, lookahead=False)

    for g in groups:
        gate = _gate_cols(_gate_rows(gate_ref, g), 1)
        acc = acc_ref[g]
        o_ref[:, g * width:(g + 1) * width] = _heads_from_t(acc[:HEAD_DIM] * (gate / acc[HEAD_DIM:HEAD_DIM + 1])).astype(o_ref.dtype)


def _nsa_sel(q, k, v, sel, gates):
    nh, bsz, seq, hd = q.shape
    ng = k.shape[0]
    onehot = (np.arange(seq)[:, None] // NSA_SEL_BLOCK == np.arange(LANES - hd)[None, :]).astype(np.float32)
    k_ext = jnp.concatenate([k, jnp.broadcast_to(jnp.asarray(onehot, BF16), (ng, bsz, seq, LANES - hd))], axis=-1)
    return pl.pallas_call(
        _nsa_sel_kernel,
        grid=(bsz, seq // Q_TILE),
        in_specs=[pl.BlockSpec((nh, None, Q_TILE, hd), lambda b, i: (0, b, i, 0)),
                  pl.BlockSpec((ng, None, seq, LANES), lambda b, i: (0, b, 0, 0)),
                  pl.BlockSpec((ng, None, seq, LANES), lambda b, i: (0, b, 0, 0)),
                  pl.BlockSpec((ng, None, Q_TILE, LANES), lambda b, i: (0, b, i, 0)),
                  pl.BlockSpec((ng, None, Q_TILE, gates.shape[-1]), lambda b, i: (0, b, i, 0))],
        out_specs=pl.BlockSpec((None, Q_TILE, nh * hd), lambda b, i: (b, i, 0)),
        out_shape=jax.ShapeDtypeStruct((bsz, seq, nh * hd), BF16),
        scratch_shapes=[pltpu.VMEM((ng, 2, SEL_CHUNK, GROUP_HEADS * Q_TILE), F32),
                        pltpu.VMEM((ng, 1, GROUP_HEADS * Q_TILE), F32),
                        pltpu.VMEM((ng, ACC_ROWS, GROUP_HEADS * Q_TILE), F32)],
        compiler_params=_compiler_params(("parallel", "parallel")),
        name="nsa_sel",
    )(q, k_ext, _with_ones(v), sel, gates)


def _overlap_t(seq):
    n_cmp = (seq - NSA_CMP_BLOCK) // NSA_CMP_STRIDE + 1
    n_pad = seq // NSA_CMP_STRIDE
    starts = np.arange(n_pad) * NSA_CMP_STRIDE
    sel_start = np.arange(seq // NSA_SEL_BLOCK) * NSA_SEL_BLOCK
    ov = (starts[None, :] < sel_start[:, None] + NSA_SEL_BLOCK) & (starts[None, :] + NSA_CMP_BLOCK > sel_start[:, None])
    ov = ov & (np.arange(n_pad)[None, :] < n_cmp)
    return jnp.asarray(ov.astype(np.float32))


def _odd_mixer(xt, nw, w_in, w_out, cmp_pe, cmp_w1, cmp_w2, sinks, bsz):
    t, d = xt.shape
    seq = t // bsz
    assert seq // NSA_SEL_BLOCK <= LANES and seq % SEL_CHUNK == 0
    cuts = [int(c) for c in np.cumsum((0,) + ODD_SPLITS)]
    ws = [w_in[:, a:b] for a, b in zip(cuts[:-1], cuts[1:])]
    ws[0] = ws[0] * (HEAD_DIM ** -0.5 * LOG2_E)
    ws[8] = ws[8] * (HEAD_DIM ** -0.5 * LOG2_E)
    ws = [w.astype(BF16) for w in ws]
    hd = HEAD_DIM
    head_dims = (hd, hd, hd, hd, hd, hd, hd, 3 * GROUP_HEADS, hd, hd, hd)
    dtypes = (BF16, F32, F32, BF16, BF16, BF16, BF16, F32, BF16, BF16, BF16)
    q_n, kc, vc, ks, vs, kw, vw, gates, q_s, k_s, v_s = _norm_proj(xt, nw, ws, head_dims, dtypes)

    def heads(a):
        return a.reshape(a.shape[0], bsz, seq, a.shape[-1])

    q_n, kc, vc, ks, vs, kw, vw, gates, q_s, k_s, v_s = map(heads, (q_n, kc, vc, ks, vs, kw, vw, gates, q_s, k_s, v_s))
    w1 = cmp_w1.astype(BF16)
    w2 = cmp_w2.astype(BF16)
    k_cmp = _cmp_kv(kc, cmp_pe[0], w1[0], w2[0])
    v_cmp = _cmp_kv(vc, cmp_pe[1], w1[1], w2[1])
    o_cw, sel, o_swa = _nsa_dense(q_n, k_cmp, v_cmp, kw, vw, gates, _overlap_t(seq), q_s, k_s, v_s,
                                  sinks.reshape(SWA_KV_HEADS, 1, GROUP_HEADS))
    o_sel = _nsa_sel(q_n, ks, vs, sel, gates)
    n_nsa = NSA_HEADS * hd
    return [[o_cw.reshape(t, n_nsa), o_sel.reshape(t, n_nsa)], [o_swa.reshape(t, SWA_HEADS * hd)]], [w_out[:n_nsa], w_out[n_nsa:]]


def kernel(x, mem, norm_w, final_norm_w, mem_norm_w, ffn_w_gate, ffn_w_up, ffn_w_down, xattn_wq, xattn_wkv, xattn_wo, even_w_in, even_w_out, ssm_conv_w, ssm_conv_b, ssm_dt_bias, ssm_a_log, ssm_d, ssm_norm_w, odd_w_in, odd_w_out, nsa_cmp_pe, nsa_cmp_w1, nsa_cmp_w2, swa_sinks):
    bsz, seq, d = x.shape
    t = bsz * seq
    m = mem.shape[1]
    wg, wu, wd = ffn_w_gate, ffn_w_up, ffn_w_down
    wkv = xattn_wkv.astype(BF16)
    even_out = even_w_out.astype(BF16)
    odd_out = odd_w_out.astype(BF16)

    xt = x.reshape(t, d)
    mem2 = mem.reshape(bsz * m, d)
    mem_nw = mem_norm_w.reshape(1, d)
    for layer in range(DEPTH):
        i = layer // 2
        xt = _ffn(xt, norm_w[layer, 0].reshape(1, d), wg, wu, wd, layer, 0)
        nw1 = norm_w[layer, 1].reshape(1, d)
        if layer % 2 == 0:
            mix, mix_w = _even_mixer(xt, nw1, even_w_in[i], even_out[i], ssm_conv_w[i], ssm_conv_b[i], ssm_dt_bias[i], ssm_a_log[i], ssm_d[i], ssm_norm_w[i], bsz)
        else:
            mix, mix_w = _odd_mixer(xt, nw1, odd_w_in[i], odd_out[i], nsa_cmp_pe[i], nsa_cmp_w1[i], nsa_cmp_w2[i], swa_sinks[i], bsz)
        (kv,) = _norm_proj(mem2, mem_nw, [wkv[layer]])
        xt = _mix_xattn(xt, mix, mix_w, norm_w[layer, 2].reshape(1, d), xattn_wq, kv.reshape(bsz, m, 2 * d), xattn_wo, layer, bsz)
        out_nw = final_norm_w.reshape(1, d) if layer == DEPTH - 1 else None
        xt = _ffn(xt, norm_w[layer, 3].reshape(1, d), wg, wu, wd, layer, 1, out_nw)
    return xt.reshape(bsz, seq, d)
```

```python
import functools

import numpy as np
import jax
import jax.numpy as jnp
from jax import lax
from jax.experimental import pallas as pl
from jax.experimental.pallas import tpu as pltpu

F32 = jnp.float32
BF16 = jnp.bfloat16

D_MODEL = 1024
DEPTH = 4
D_FF = 2816
NORM_EPS = 1e-6
ATTN_BLOCK = 128

RET_HEADS = 4
RET_DK = 128
RET_DV = 128
RET_CHUNK = 128
RET_ROPE_BASE = 10000.0

SSM_HEADS = 8
SSM_HEAD_DIM = 64
SSM_D_INNER = SSM_HEADS * SSM_HEAD_DIM
SSM_D_STATE = 64
SSM_GROUPS = 2
SSM_CONV = 4
SSM_CHUNK = 128
SSM_CONV_DIM = SSM_D_INNER + 2 * SSM_GROUPS * SSM_D_STATE

EVEN_SPLITS = (RET_HEADS * RET_DK, RET_HEADS * RET_DK, RET_HEADS * RET_DV, RET_HEADS * RET_DV, SSM_D_INNER, SSM_CONV_DIM, SSM_HEADS)

NSA_HEADS = 8
NSA_KV_HEADS = 2
NSA_HEAD_DIM = 64
NSA_CMP_BLOCK = 32
NSA_CMP_STRIDE = 16
NSA_CMP_HIDDEN = 256
NSA_SEL_BLOCK = 64
NSA_TOPN = 16
NSA_WINDOW = 512
NSA_FORCE_SCORE = 1e6

SWA_HEADS = 8
SWA_KV_HEADS = 2
SWA_HEAD_DIM = 64
SWA_WINDOW = 128

ODD_SPLITS = (NSA_HEADS * NSA_HEAD_DIM,) + (NSA_KV_HEADS * NSA_HEAD_DIM,) * 6 + (3 * NSA_HEADS, SWA_HEADS * SWA_HEAD_DIM, SWA_KV_HEADS * SWA_HEAD_DIM, SWA_KV_HEADS * SWA_HEAD_DIM)

XATTN_HEADS = 4
XATTN_HEAD_DIM = D_MODEL // XATTN_HEADS

VMEM_LIMIT_BYTES = 52 * 1024 * 1024
TOKEN_TILE = 512
FFN_TILE = 256
FFN_DOWN_CHUNK = 1024
LANES = 128
SUBLANES = 8
MASKED = -1e30
LOG2_E = float(np.log2(np.e))

_NT = (((1,), (1,)), ((), ()))


def _compiler_params(semantics):
    return pltpu.CompilerParams(dimension_semantics=semantics, vmem_limit_bytes=VMEM_LIMIT_BYTES)


def _rms(x, w):
    return x * lax.rsqrt(jnp.mean(x * x, axis=-1, keepdims=True) + NORM_EPS) * w


def _ffn_kernel(has_out_norm, x_ref, nw_ref, wg_ref, wu_ref, wd_ref, *refs):
    out_nw_ref = refs[0] if has_out_norm else None
    o_ref, a_ref = refs[-2:]
    x = x_ref[...]
    h = _rms(x, nw_ref[...]).astype(BF16)
    for c in range(D_FF // FFN_TILE):
        cols = slice(c * FFN_TILE, (c + 1) * FFN_TILE)
        g = jnp.dot(h, wg_ref[:, cols].astype(BF16), preferred_element_type=F32)
        u = jnp.dot(h, wu_ref[:, cols].astype(BF16), preferred_element_type=F32)
        a_ref[:, cols] = (g * jax.nn.sigmoid(g) * u).astype(BF16)
    acc = None
    for lo in range(0, D_FF, FFN_DOWN_CHUNK):
        hi = min(lo + FFN_DOWN_CHUNK, D_FF)
        part = jnp.dot(a_ref[:, lo:hi], wd_ref[lo:hi, :].astype(BF16), preferred_element_type=F32)
        acc = part if acc is None else acc + part
    y = x + 0.5 * acc
    o_ref[...] = _rms(y, out_nw_ref[...]) if has_out_norm else y


def _ffn(x, nw, wg, wu, wd, layer, k, out_nw=None):
    t, d = x.shape
    tm = TOKEN_TILE
    resident = pl.Buffered(1)
    vec = pl.BlockSpec((1, d), lambda i: (0, 0))
    return pl.pallas_call(
        functools.partial(_ffn_kernel, out_nw is not None),
        grid=(t // tm,),
        in_specs=[
            pl.BlockSpec((tm, d), lambda i: (i, 0)),
            vec,
            pl.BlockSpec((None, None, d, D_FF), lambda i: (layer, k, 0, 0), pipeline_mode=resident),
            pl.BlockSpec((None, None, d, D_FF), lambda i: (layer, k, 0, 0), pipeline_mode=resident),
            pl.BlockSpec((None, None, D_FF, d), lambda i: (layer, k, 0, 0), pipeline_mode=resident),
        ] + ([vec] if out_nw is not None else []),
        out_specs=pl.BlockSpec((tm, d), lambda i: (i, 0)),
        out_shape=jax.ShapeDtypeStruct((t, d), F32),
        scratch_shapes=[pltpu.VMEM((tm, D_FF), BF16)],
        compiler_params=_compiler_params(("parallel",)),
        name="ffn",
    )(x, nw, wg, wu, wd, *([out_nw] if out_nw is not None else []))


def _norm_proj_kernel(head_dims, offsets, x_ref, nw_ref, w_ref, *o_refs):
    h = _rms(x_ref[...], nw_ref[...]).astype(BF16)
    y_all = jnp.dot(h, w_ref[...], preferred_element_type=F32)
    for hd, off, o_ref in zip(head_dims, offsets, o_refs):
        if hd is None:
            o_ref[...] = y_all[:, off:off + o_ref.shape[-1]].astype(o_ref.dtype)
        else:
            for j in range(o_ref.shape[0]):
                o_ref[j] = y_all[:, off + j * hd:off + (j + 1) * hd].astype(o_ref.dtype)


def _norm_proj(x, nw, weights, head_dims=None, dtypes=None, tm=TOKEN_TILE):
    t, d = x.shape
    n_out = len(weights)
    head_dims = tuple(head_dims) if head_dims is not None else (None,) * n_out
    dtypes = tuple(dtypes) if dtypes is not None else (F32,) * n_out
    out_specs, out_shape, offsets, padded = [], [], [], []
    total = 0
    for w, hd, dt in zip(weights, head_dims, dtypes):
        n = w.shape[1]
        if hd is None:
            out_specs.append(pl.BlockSpec((tm, n), lambda i: (i, 0)))
            out_shape.append(jax.ShapeDtypeStruct((t, n), dt))
        else:
            out_specs.append(pl.BlockSpec((n // hd, tm, hd), lambda i: (0, i, 0)))
            out_shape.append(jax.ShapeDtypeStruct((n // hd, t, hd), dt))
        offsets.append(total)
        width = -(-n // LANES) * LANES
        padded.append(jnp.pad(w, ((0, 0), (0, width - n))))
        total += width
    w_all = jnp.concatenate(padded, axis=1) if n_out > 1 else padded[0]
    return pl.pallas_call(
        functools.partial(_norm_proj_kernel, head_dims, tuple(offsets)),
        grid=(t // tm,),
        in_specs=[pl.BlockSpec((tm, d), lambda i: (i, 0)), pl.BlockSpec((1, d), lambda i: (0, 0)),
                  pl.BlockSpec(w_all.shape, lambda i: (0, 0))],
        out_specs=out_specs,
        out_shape=out_shape,
        compiler_params=_compiler_params(("parallel",)),
        name="norm_proj",
    )(x, nw, w_all)


def _mix_xattn_kernel(group_sizes, x_ref, *refs):
    n_a = sum(group_sizes)
    n_g = len(group_sizes)
    a_refs, w_refs = refs[:n_a], refs[n_a:n_a + n_g]
    nw_ref, wq_ref, kv_ref, wo_ref, o_ref = refs[n_a + n_g:]
    x = x_ref[...]
    pos = 0
    for size, w_ref in zip(group_sizes, w_refs):
        a = a_refs[pos][...]
        for r in a_refs[pos + 1:pos + size]:
            a = a.astype(F32) + r[...].astype(F32)
        pos += size
        x = x + jnp.dot(a.astype(BF16), w_ref[...], preferred_element_type=F32)
    h = _rms(x, nw_ref[...]).astype(BF16)
    q = jnp.dot(h, wq_ref[...].astype(BF16), preferred_element_type=F32).astype(BF16)
    hd = XATTN_HEAD_DIM
    outs = []
    for hh in range(XATTN_HEADS):
        k = kv_ref[:, hh * hd:(hh + 1) * hd].astype(BF16)
        v = kv_ref[:, D_MODEL + hh * hd:D_MODEL + (hh + 1) * hd].astype(BF16)
        s = lax.dot_general(q[:, hh * hd:(hh + 1) * hd], k, _NT, preferred_element_type=F32)
        s = s * (hd ** -0.5)
        p = jnp.exp(s - jnp.max(s, axis=-1, keepdims=True))
        p = p / jnp.sum(p, axis=-1, keepdims=True)
        outs.append(jnp.dot(p.astype(BF16), v, preferred_element_type=F32).astype(BF16))
    o = jnp.concatenate(outs, axis=-1)
    o_ref[...] = x + jnp.dot(o, wo_ref[...].astype(BF16), preferred_element_type=F32)


def _mix_xattn(x, groups, weights, nw, wq, kv, wo, layer, bsz, tm=2 * TOKEN_TILE):
    t, d = x.shape
    seq = t // bsz
    nt = seq // tm
    m = kv.shape[1]
    arrays = [a for grp in groups for a in grp]

    def tok(width):
        return pl.BlockSpec((tm, width), lambda b, i: (b * nt + i, 0))

    def const(shape):
        return pl.BlockSpec(shape, lambda b, i: (0, 0))

    layer_w = pl.BlockSpec((None, d, d), lambda b, i: (layer, 0, 0))
    return pl.pallas_call(
        functools.partial(_mix_xattn_kernel, tuple(len(grp) for grp in groups)),
        grid=(bsz, nt),
        in_specs=[tok(d)] + [tok(a.shape[1]) for a in arrays] + [const(w.shape) for w in weights]
        + [const((1, d)), layer_w, pl.BlockSpec((None, m, 2 * d), lambda b, i: (b, 0, 0)), layer_w],
        out_specs=tok(d),
        out_shape=jax.ShapeDtypeStruct((t, d), F32),
        compiler_params=_compiler_params(("parallel", "parallel")),
        name="mix_xattn",
    )(x, *arrays, *weights, nw, wq, kv, wo)


def _retention_kernel(q_ref, k_ref, v_ref, g_ref, cos_ref, sin_ref, dmat_ref, zeta_ref, xi_ref, decay_ref, o_ref, state_ref):
    @pl.when(pl.program_id(1) == 0)
    def _():
        state_ref[...] = jnp.zeros_like(state_ref)

    cos = cos_ref[...]
    sin = sin_ref[...]
    half = RET_DK // 2
    outs = []
    for h in range(RET_HEADS):
        qh = q_ref[:, h * RET_DK:(h + 1) * RET_DK]
        kh = k_ref[:, h * RET_DK:(h + 1) * RET_DK]
        qr = qh * cos + pltpu.roll(qh, half, 1) * sin
        kr = (kh * cos + pltpu.roll(kh, half, 1) * sin) * (RET_DK ** -0.5)
        vh = v_ref[:, h * RET_DV:(h + 1) * RET_DV].astype(BF16)
        scores = lax.dot_general(qr.astype(BF16), kr.astype(BF16), _NT, preferred_element_type=F32) * dmat_ref[h]
        inner = jnp.dot(scores.astype(BF16), vh, preferred_element_type=F32)
        state = state_ref[h]
        cross = jnp.dot((qr * xi_ref[h]).astype(BF16), state.astype(BF16), preferred_element_type=F32)
        kz = (kr * zeta_ref[h]).T.astype(BF16)
        state_ref[h] = decay_ref[h] * state + jnp.dot(kz, vh, preferred_element_type=F32)
        o = inner + cross
        o = o * lax.rsqrt(jnp.mean(o * o, axis=-1, keepdims=True) + NORM_EPS)
        gh = g_ref[:, h * RET_DV:(h + 1) * RET_DV]
        outs.append(gh * jax.nn.sigmoid(gh) * o)
    o_ref[...] = jnp.concatenate(outs, axis=-1).astype(o_ref.dtype)


def _retention_tables(seq):
    L = RET_CHUNK
    pos = jnp.arange(seq, dtype=F32)
    inv_freq = 1.0 / (RET_ROPE_BASE ** jnp.linspace(0.0, 1.0, RET_DK // 2, dtype=F32))
    ang = pos[:, None] * inv_freq[None, :]
    cos, sin = jnp.cos(ang), jnp.sin(ang)
    cos2 = jnp.concatenate([cos, cos], axis=-1)
    sin2 = jnp.concatenate([-sin, sin], axis=-1)
    log_g = jnp.log1p(-jnp.exp2(-5.0 - jnp.arange(RET_HEADS, dtype=F32)))
    idx = jnp.arange(L, dtype=F32)
    diff = idx[:, None] - idx[None, :]
    dmat = jnp.where(diff >= 0, jnp.exp(jnp.maximum(diff, 0.0)[None] * log_g[:, None, None]), 0.0)
    ones = jnp.ones((RET_HEADS, L, RET_DK), F32)
    zeta = jnp.exp((L - 1 - idx)[None, :] * log_g[:, None])[:, :, None] * ones
    xi = jnp.exp((idx + 1)[None, :] * log_g[:, None])[:, :, None] * ones
    decay = jnp.exp(L * log_g)[:, None, None] * ones
    return cos2, sin2, dmat, zeta, xi, decay


CONV_PAD = SUBLANES


def _ssd_kernel(xbc_ref, dt_ref, z_ref, convw_ref, convb_ref, dtb_ref, alog_ref, dskip_ref, nw_ref, tri_ref, o_ref, ext_ref, state_ref):
    L = SSM_CHUNK
    hp = SSM_HEAD_DIM
    ns = SSM_D_STATE

    @pl.when(pl.program_id(1) == 0)
    def _():
        ext_ref[0:CONV_PAD, :] = jnp.zeros((CONV_PAD, SSM_CONV_DIM), F32)
        state_ref[...] = jnp.zeros_like(state_ref)

    ext_ref[CONV_PAD:CONV_PAD + L, :] = xbc_ref[...]
    conv = convb_ref[...]
    for w in range(SSM_CONV):
        off = CONV_PAD - (SSM_CONV - 1) + w
        conv = conv + ext_ref[off:off + L, :] * convw_ref[w:w + 1, :]
    ext_ref[0:CONV_PAD, :] = ext_ref[L:L + CONV_PAD, :]
    xa = conv * jax.nn.sigmoid(conv)
    xs = xa[:, :SSM_D_INNER]
    bmat = xa[:, SSM_D_INNER:SSM_D_INNER + SSM_GROUPS * ns]
    cmat = xa[:, SSM_D_INNER + SSM_GROUPS * ns:]

    x = dt_ref[...] + dtb_ref[...]
    dt = jnp.maximum(x, 0.0) + jnp.log1p(jnp.exp(-jnp.abs(x)))
    lane = lax.broadcasted_iota(jnp.int32, (L, LANES), 1)
    dta = jnp.where(lane < SSM_HEADS, dt * -jnp.exp(alog_ref[...]), 0.0)
    cums = jnp.dot(tri_ref[...], dta, preferred_element_type=F32, precision=lax.Precision.HIGHEST)
    cums_t = cums.T
    last = cums[L - 1:L, :]
    decay_to_end = jnp.exp(last - cums)
    decay_from_start = jnp.exp(cums)
    chunk_decay = jnp.exp(last)
    causal = lax.broadcasted_iota(jnp.int32, (L, L), 0) >= lax.broadcasted_iota(jnp.int32, (L, L), 1)
    heads_per_group = SSM_HEADS // SSM_GROUPS
    ys = []
    for g in range(SSM_GROUPS):
        bg = bmat[:, g * ns:(g + 1) * ns].astype(BF16)
        cg = cmat[:, g * ns:(g + 1) * ns].astype(BF16)
        cb = lax.dot_general(cg, bg, _NT, preferred_element_type=F32)
        for e in range(heads_per_group):
            h = g * heads_per_group + e
            xh = xs[:, h * hp:(h + 1) * hp]
            xdt = xh * dt[:, h:h + 1]
            seg = cums[:, h:h + 1] - cums_t[h:h + 1, :]
            lmat = jnp.where(causal, jnp.exp(seg), 0.0)
            y_diag = jnp.dot((cb * lmat).astype(BF16), xdt.astype(BF16), preferred_element_type=F32)
            state = state_ref[h]
            y_off = lax.dot_general(cg, state.astype(BF16), _NT, preferred_element_type=F32) * decay_from_start[:, h:h + 1]
            xw = (xdt * decay_to_end[:, h:h + 1]).astype(BF16)
            new = lax.dot_general(xw, bg, (((0,), (0,)), ((), ())), preferred_element_type=F32)
            state_ref[h] = chunk_decay[:, h:h + 1] * state + new
            ys.append(y_diag + y_off + dskip_ref[:, h * hp:(h + 1) * hp] * xh)
    y = jnp.concatenate(ys, axis=-1)
    z = z_ref[...]
    o_ref[...] = _rms(y * (z * jax.nn.sigmoid(z)), nw_ref[...]).astype(o_ref.dtype)


N_RET_IN = 10
N_SSD_IN = 10


def _even_core_kernel(*refs):
    ret_in, ssd_in = refs[:N_RET_IN], refs[N_RET_IN:N_RET_IN + N_SSD_IN]
    o_ret_ref, o_ssd_ref, ret_state_ref, ext_ref, ssd_state_ref = refs[N_RET_IN + N_SSD_IN:]
    _retention_kernel(*ret_in, o_ret_ref, ret_state_ref)
    _ssd_kernel(*ssd_in, o_ssd_ref, ext_ref, ssd_state_ref)


def _even_core(q, k, v, g, xbc, dt_raw, z, conv_w, conv_b, dt_bias, a_log, d_skip, norm_w):
    bsz, seq, width = q.shape
    L = RET_CHUNK
    assert SSM_CHUNK == L
    tables = _retention_tables(seq)
    pad = LANES - SSM_HEADS
    params = [conv_w, conv_b.reshape(1, -1), jnp.pad(dt_bias, (0, pad)).reshape(1, LANES),
              jnp.pad(a_log, (0, pad)).reshape(1, LANES), jnp.repeat(d_skip, SSM_HEAD_DIM).reshape(1, -1),
              norm_w.reshape(1, -1), jnp.asarray(np.tril(np.ones((L, L), np.float32)))]

    def tok(w):
        return pl.BlockSpec((None, L, w), lambda b, c: (b, c, 0))

    rope = pl.BlockSpec((L, RET_DK), lambda b, c: (c, 0))
    const = pl.BlockSpec((RET_HEADS, L, RET_DK), lambda b, c: (0, 0, 0))
    return pl.pallas_call(
        _even_core_kernel,
        grid=(bsz, seq // L),
        in_specs=[tok(width)] * 4 + [rope, rope, const, const, const, const]
        + [tok(SSM_CONV_DIM), tok(LANES), tok(SSM_D_INNER)] + [pl.BlockSpec(p.shape, lambda b, c: (0, 0)) for p in params],
        out_specs=[tok(width), tok(SSM_D_INNER)],
        out_shape=[jax.ShapeDtypeStruct((bsz, seq, width), BF16), jax.ShapeDtypeStruct((bsz, seq, SSM_D_INNER), BF16)],
        scratch_shapes=[pltpu.VMEM((RET_HEADS, RET_DK, RET_DV), F32),
                        pltpu.VMEM((L + CONV_PAD, SSM_CONV_DIM), F32), pltpu.VMEM((SSM_HEADS, SSM_HEAD_DIM, SSM_D_STATE), F32)],
        compiler_params=_compiler_params(("parallel", "arbitrary")),
        name="even_core",
    )(q, k, v, g, *tables, xbc, dt_raw, z, *params)


def _even_mixer(xt, nw, w_in, w_out, conv_w, conv_b, dt_bias, a_log, d_skip, ssm_norm_w, bsz):
    t, d = xt.shape
    seq = t // bsz
    cuts = [int(c) for c in np.cumsum((0,) + EVEN_SPLITS)]
    ws = [w_in[:, a:b].astype(BF16) for a, b in zip(cuts[:-1], cuts[1:])]
    perm = np.concatenate([h * RET_DK + np.concatenate([np.arange(0, RET_DK, 2), np.arange(1, RET_DK, 2)]) for h in range(RET_HEADS)])
    ws[0] = ws[0][:, perm]
    ws[1] = ws[1][:, perm]
    ws[6] = jnp.pad(ws[6], ((0, 0), (0, LANES - SSM_HEADS)))
    dtypes = (F32, F32, BF16, F32, F32, F32, F32)
    q, k, v, g, z, xbc, dt_raw = [a.reshape(bsz, seq, -1) for a in _norm_proj(xt, nw, ws, dtypes=dtypes)]
    o_ret, y = _even_core(q, k, v, g, xbc, dt_raw, z, conv_w, conv_b, dt_bias, a_log, d_skip, ssm_norm_w)
    n_ret = RET_HEADS * RET_DV
    return [[o_ret.reshape(t, n_ret)], [y.reshape(t, SSM_D_INNER)]], [w_out[:n_ret], w_out[n_ret:]]


Q_TILE = 2 * ATTN_BLOCK
GROUP_HEADS = NSA_HEADS // NSA_KV_HEADS
HEAD_DIM = NSA_HEAD_DIM
SEL_CHUNK = 512


def _cmp_kv_kernel(x_ref, pe_ref, w1_ref, w2_ref, o_ref):
    n = x_ref.shape[0] // NSA_CMP_STRIDE
    top = bot = None
    for r in range(NSA_CMP_STRIDE):
        rows = x_ref[pl.ds(r, n, stride=NSA_CMP_STRIDE), :]
        lo = r * HEAD_DIM
        hi = (NSA_CMP_STRIDE + r) * HEAD_DIM
        a = jnp.dot((rows + pe_ref[r:r + 1, :]).astype(BF16), w1_ref[lo:lo + HEAD_DIM, :], preferred_element_type=F32)
        b = jnp.dot((rows + pe_ref[NSA_CMP_STRIDE + r:NSA_CMP_STRIDE + r + 1, :]).astype(BF16), w1_ref[hi:hi + HEAD_DIM, :], preferred_element_type=F32)
        top = a if top is None else top + a
        bot = b if bot is None else bot + b
    hidden = top + pltpu.roll(bot, n - 1, 0)
    act = hidden * jax.nn.sigmoid(hidden)
    o_ref[...] = jnp.dot(act.astype(BF16), w2_ref[...], preferred_element_type=F32)


def _cmp_kv(x, pe, w1, w2):
    ng, bsz, seq, hd = x.shape
    n = seq // NSA_CMP_STRIDE
    return pl.pallas_call(
        _cmp_kv_kernel,
        grid=(ng, bsz),
        in_specs=[pl.BlockSpec((None, None, seq, hd), lambda g, b: (g, b, 0, 0)),
                  pl.BlockSpec(pe.shape, lambda g, b: (0, 0)),
                  pl.BlockSpec(w1.shape, lambda g, b: (0, 0)),
                  pl.BlockSpec(w2.shape, lambda g, b: (0, 0))],
        out_specs=pl.BlockSpec((None, None, n, hd), lambda g, b: (g, b, 0, 0)),
        out_shape=jax.ShapeDtypeStruct((ng, bsz, n, hd), F32),
        compiler_params=_compiler_params(("parallel", "parallel")),
        name="nsa_compress",
    )(x, pe, w1, w2)


def _heads_from_t(o_t):
    pairs = []
    for e in range(0, GROUP_HEADS, 2):
        pair = jnp.concatenate([o_t[:, e * Q_TILE:(e + 1) * Q_TILE], o_t[:, (e + 1) * Q_TILE:(e + 2) * Q_TILE]], axis=0)
        pairs.append(pair.T)
    return jnp.concatenate(pairs, axis=-1)


def _gate_rows(gate_ref, g):
    gate = jax.nn.sigmoid(gate_ref[g])
    gate = jnp.concatenate([gate, jnp.zeros((Q_TILE, LANES - gate.shape[1]), F32)], axis=1)
    return gate.T


def _gate_cols(gate_t, branch):
    return jnp.concatenate([gate_t[3 * e + branch:3 * e + branch + 1, :] for e in range(GROUP_HEADS)], axis=1)


def _tile_heads(x):
    return jnp.concatenate([x] * GROUP_HEADS, axis=1)


def _pv_t(v, p):
    return lax.dot_general(v, p.astype(BF16), (((0,), (0,)), ((), ())), preferred_element_type=F32)


ACC_ROWS = HEAD_DIM + SUBLANES


def _with_ones(v):
    pad = jnp.zeros(v.shape[:-1] + (LANES - HEAD_DIM - 1,), v.dtype)
    return jnp.concatenate([v, jnp.ones(v.shape[:-1] + (1,), v.dtype), pad], axis=-1)


def _band_window(window):
    return (-(-(window - 1) // Q_TILE) + 1) * Q_TILE


def _band_branch(i, q, k_ref, v_ref, g, window, sink=None):
    n_keys = _band_window(window)
    start = pl.multiple_of(jnp.maximum(i * Q_TILE + Q_TILE - n_keys, 0), Q_TILE)
    rel = (i * Q_TILE + lax.broadcasted_iota(jnp.int32, (n_keys, Q_TILE), 1)) - (start + lax.broadcasted_iota(jnp.int32, (n_keys, Q_TILE), 0))
    bias = _tile_heads(jnp.where(rel >= 0, jnp.where(rel < window, 0.0, MASKED), MASKED))
    s = lax.dot_general(k_ref[g, pl.ds(start, n_keys), :], q, _NT, preferred_element_type=F32) + bias
    m = jnp.max(s, axis=0, keepdims=True)
    if sink is not None:
        m = jnp.maximum(m, sink)
    p = jnp.exp2(s - m)
    denom = jnp.sum(p, axis=0, keepdims=True)
    denom = denom + jnp.exp2(sink - m) if sink is not None else jnp.maximum(denom, 1e-30)
    return _pv_t(v_ref[g, pl.ds(start, n_keys), :], p) * (1.0 / denom)


def _nsa_dense_kernel(q_ref, kc_ref, vc_ref, kw_ref, vw_ref, gate_ref, ov_ref, qs_ref, ks_ref, vs_ref, sink_ref,
                      o_ref, sel_ref, swa_ref):
    i = pl.program_id(1)
    rows = GROUP_HEADS * Q_TILE
    width = GROUP_HEADS * HEAD_DIM
    n_cmp = kc_ref.shape[1]
    n = lax.broadcasted_iota(jnp.int32, (n_cmp, Q_TILE), 0)
    t = i * Q_TILE + lax.broadcasted_iota(jnp.int32, (n_cmp, Q_TILE), 1)
    valid = _tile_heads(jnp.where(n * NSA_CMP_STRIDE + (NSA_CMP_BLOCK - 1) <= t, 1.0, 0.0))
    bias = (valid - 1.0) * (-MASKED)
    n_sel = ov_ref.shape[0]
    jb = lax.broadcasted_iota(jnp.int32, (n_sel, Q_TILE), 0)
    cur = (i * Q_TILE + lax.broadcasted_iota(jnp.int32, (n_sel, Q_TILE), 1)) // NSA_SEL_BLOCK
    valid_b = jb <= cur
    forced = jnp.where(valid_b, jnp.where(jb == 0, 1.0, 0.0) + jnp.where(jb == cur, 1.0, 0.0) + jnp.where(jb == cur - 1, 1.0, 0.0), 0.0)
    for g in range(NSA_KV_HEADS):
        q = q_ref[g * GROUP_HEADS:(g + 1) * GROUP_HEADS].reshape(rows, HEAD_DIM)
        s = lax.dot_general(kc_ref[g].astype(BF16), q, _NT, preferred_element_type=F32) + bias
        m = jnp.max(s, axis=0, keepdims=True)
        p = jnp.exp2(s - m) * valid
        p = p * (1.0 / jnp.maximum(jnp.sum(p, axis=0, keepdims=True), 1e-30))
        o_cmp = _pv_t(vc_ref[g].astype(BF16), p)
        o_win = _band_branch(i, q, kw_ref, vw_ref, g, NSA_WINDOW)
        gate_t = _gate_rows(gate_ref, g)
        o_ref[:, g * width:(g + 1) * width] = _heads_from_t(o_cmp * _gate_cols(gate_t, 0) + o_win * _gate_cols(gate_t, 2)).astype(o_ref.dtype)

        p_sum = p[:, 0:Q_TILE]
        for e in range(1, GROUP_HEADS):
            p_sum = p_sum + p[:, e * Q_TILE:(e + 1) * Q_TILE]
        imp = jnp.dot(ov_ref[...], p_sum, preferred_element_type=F32, precision=lax.Precision.HIGHEST)
        score = jnp.where(forced > 0.5, NSA_FORCE_SCORE, jnp.where(valid_b, imp, -jnp.inf))
        sub = SUBLANES
        ranks = []
        for v in range(n_sel // sub):
            blk_scores = score[v * sub:(v + 1) * sub, :]
            jb_v = v * sub + lax.broadcasted_iota(jnp.int32, (sub, Q_TILE), 0)
            rank = jnp.zeros((sub, Q_TILE), F32)
            for ii in range(n_sel):
                row = score[ii:ii + 1, :]
                if ii < v * sub:
                    ahead = jnp.where(row >= blk_scores, 1.0, 0.0)
                elif ii >= (v + 1) * sub:
                    ahead = jnp.where(row > blk_scores, 1.0, 0.0)
                else:
                    tie = jnp.where(jb_v > ii, 1.0, 0.0)
                    ahead = jnp.where(row > blk_scores, 1.0, jnp.where(row == blk_scores, tie, 0.0))
                rank = rank + ahead
            ranks.append(rank)
        rank = jnp.concatenate(ranks, axis=0)
        sel_t = jnp.where(valid_b, jnp.where(rank < NSA_TOPN, 1.0, 0.0), 0.0)
        sel_t = jnp.concatenate([sel_t, jnp.zeros((LANES - n_sel, Q_TILE), F32)], axis=0)
        sel_ref[g] = sel_t.T.astype(BF16)

    for g in range(SWA_KV_HEADS):
        q = qs_ref[g * GROUP_HEADS:(g + 1) * GROUP_HEADS].reshape(rows, HEAD_DIM)
        sink = LOG2_E * jnp.concatenate([jnp.broadcast_to(sink_ref[g][:, e:e + 1], (1, Q_TILE)) for e in range(GROUP_HEADS)], axis=1)
        o_swa = _band_branch(i, q, ks_ref, vs_ref, g, SWA_WINDOW, sink)
        swa_ref[:, g * width:(g + 1) * width] = _heads_from_t(o_swa).astype(swa_ref.dtype)


def _nsa_dense(q, k_cmp, v_cmp, kw, vw, gates, overlap_t, q_s, k_s, v_s, sinks):
    nh, bsz, seq, hd = q.shape
    ng, _, n_cmp, _ = k_cmp.shape
    heads = pl.BlockSpec((nh, None, Q_TILE, hd), lambda b, i: (0, b, i, 0))
    cmp_kv = pl.BlockSpec((ng, None, n_cmp, hd), lambda b, i: (0, b, 0, 0))
    seq_kv = pl.BlockSpec((ng, None, seq, hd), lambda b, i: (0, b, 0, 0))
    out = pl.BlockSpec((None, Q_TILE, nh * hd), lambda b, i: (b, i, 0))
    return pl.pallas_call(
        _nsa_dense_kernel,
        grid=(bsz, seq // Q_TILE),
        in_specs=[heads, cmp_kv, cmp_kv, seq_kv, seq_kv,
                  pl.BlockSpec((ng, None, Q_TILE, gates.shape[-1]), lambda b, i: (0, b, i, 0)),
                  pl.BlockSpec(overlap_t.shape, lambda b, i: (0, 0)),
                  heads, seq_kv, seq_kv,
                  pl.BlockSpec(sinks.shape, lambda b, i: (0, 0, 0))],
        out_specs=[out, pl.BlockSpec((ng, None, Q_TILE, LANES), lambda b, i: (0, b, i, 0)), out],
        out_shape=[jax.ShapeDtypeStruct((bsz, seq, nh * hd), BF16),
                   jax.ShapeDtypeStruct((ng, bsz, seq, LANES), BF16),
                   jax.ShapeDtypeStruct((bsz, seq, nh * hd), BF16)],
        compiler_params=_compiler_params(("parallel", "parallel")),
        name="nsa_dense",
    )(q, k_cmp, v_cmp, kw, vw, gates, overlap_t, q_s, k_s, v_s, sinks)


def _nsa_sel_kernel(q_ref, k_ref, v_ref, sel_ref, gate_ref, o_ref, s_ref, m_ref, acc_ref):
    i = pl.program_id(1)
    rows = GROUP_HEADS * Q_TILE
    width = GROUP_HEADS * HEAD_DIM
    n_blk = LANES - HEAD_DIM
    kc = SEL_CHUNK
    seq = k_ref.shape[1]
    groups = range(NSA_KV_HEADS)

    def scores(g, start, lhs):
        return lax.dot_general(k_ref[g, pl.ds(start, kc), :], lhs, _NT, preferred_element_type=F32)

    start_d = pl.multiple_of(jnp.maximum(i * Q_TILE + Q_TILE - kc, 0), Q_TILE)
    n_past = (start_d + kc - 1) // kc
    kpos = start_d + lax.broadcasted_iota(jnp.int32, (kc, rows), 0)
    qpos = i * Q_TILE + (lax.broadcasted_iota(jnp.int32, (kc, rows), 1) & (Q_TILE - 1))
    blk = lax.broadcasted_iota(jnp.int32, (Q_TILE, n_blk), 1)
    lhs_past, lhs_none = [], []
    for g in groups:
        q = q_ref[g * GROUP_HEADS:(g + 1) * GROUP_HEADS].reshape(rows, HEAD_DIM)
        sel = sel_ref[g][:, :n_blk].astype(F32)
        neg_diag = (sel - 1.0) * (-MASKED)
        neg_past = jnp.where(blk < start_d // NSA_SEL_BLOCK, neg_diag, MASKED)
        lhs_diag = jnp.concatenate([q, jnp.concatenate([neg_diag.astype(BF16)] * GROUP_HEADS, axis=0)], axis=-1)
        lhs_past.append(jnp.concatenate([q, jnp.concatenate([neg_past.astype(BF16)] * GROUP_HEADS, axis=0)], axis=-1))
        lhs_none.append(jnp.concatenate([q, jnp.full((rows, n_blk), MASKED, BF16)], axis=-1))
        s_ref[g, 0] = jnp.where(kpos <= qpos, scores(g, start_d, lhs_diag), MASKED)

        m_ref[g] = jnp.full((1, rows), MASKED, F32)
        acc_ref[g] = jnp.zeros((ACC_ROWS, rows), F32)

    def step(c, cur, nxt, lookahead=True):
        next_start = pl.multiple_of(jnp.minimum(c * kc, seq - kc), kc)
        v_start = pl.multiple_of(jnp.where(c == 0, start_d, (c - 1) * kc), Q_TILE)
        for g in groups:
            if lookahead:
                s_ref[g, nxt] = scores(g, next_start, jnp.where(c < n_past, lhs_past[g], lhs_none[g]))
            s = s_ref[g, cur]
            m = m_ref[g]
            m_new = jnp.maximum(m, jnp.max(s, axis=0, keepdims=True))
            alpha = jnp.exp2(m - m_new)
            p = jnp.exp2(s - m_new)
            m_ref[g] = m_new
            acc_ref[g] = alpha * acc_ref[g] + _pv_t(v_ref[g, pl.ds(v_start, kc), :], p)[:ACC_ROWS]

    def body(j, carry):
        step(2 * j, 0, 1)
        step(2 * j + 1, 1, 0)
        return carry

    lax.fori_loop(0, (n_past + 1) // 2, body, 0)

    @pl.when(n_past % 2 == 0)
    def _():
        step(n_past, 0, 1, lookahead=False)

    for g in groups:
        gate = _gate_cols(_gate_rows(gate_ref, g), 1)
        acc = acc_ref[g]
        o_ref[:, g * width:(g + 1) * width] = _heads_from_t(acc[:HEAD_DIM] * (gate / acc[HEAD_DIM:HEAD_DIM + 1])).astype(o_ref.dtype)


def _nsa_sel(q, k, v, sel, gates):
    nh, bsz, seq, hd = q.shape
    ng = k.shape[0]
    onehot = (np.arange(seq)[:, None] // NSA_SEL_BLOCK == np.arange(LANES - hd)[None, :]).astype(np.float32)
    k_ext = jnp.concatenate([k, jnp.broadcast_to(jnp.asarray(onehot, BF16), (ng, bsz, seq, LANES - hd))], axis=-1)
    return pl.pallas_call(
        _nsa_sel_kernel,
        grid=(bsz, seq // Q_TILE),
        in_specs=[pl.BlockSpec((nh, None, Q_TILE, hd), lambda b, i: (0, b, i, 0)),
                  pl.BlockSpec((ng, None, seq, LANES), lambda b, i: (0, b, 0, 0)),
                  pl.BlockSpec((ng, None, seq, LANES), lambda b, i: (0, b, 0, 0)),
                  pl.BlockSpec((ng, None, Q_TILE, LANES), lambda b, i: (0, b, i, 0)),
                  pl.BlockSpec((ng, None, Q_TILE, gates.shape[-1]), lambda b, i: (0, b, i, 0))],
        out_specs=pl.BlockSpec((None, Q_TILE, nh * hd), lambda b, i: (b, i, 0)),
        out_shape=jax.ShapeDtypeStruct((bsz, seq, nh * hd), BF16),
        scratch_shapes=[pltpu.VMEM((ng, 2, SEL_CHUNK, GROUP_HEADS * Q_TILE), F32),
                        pltpu.VMEM((ng, 1, GROUP_HEADS * Q_TILE), F32),
                        pltpu.VMEM((ng, ACC_ROWS, GROUP_HEADS * Q_TILE), F32)],
        compiler_params=_compiler_params(("parallel", "parallel")),
        name="nsa_sel",
    )(q, k_ext, _with_ones(v), sel, gates)


def _overlap_t(seq):
    n_cmp = (seq - NSA_CMP_BLOCK) // NSA_CMP_STRIDE + 1
    n_pad = seq // NSA_CMP_STRIDE
    starts = np.arange(n_pad) * NSA_CMP_STRIDE
    sel_start = np.arange(seq // NSA_SEL_BLOCK) * NSA_SEL_BLOCK
    ov = (starts[None, :] < sel_start[:, None] + NSA_SEL_BLOCK) & (starts[None, :] + NSA_CMP_BLOCK > sel_start[:, None])
    ov = ov & (np.arange(n_pad)[None, :] < n_cmp)
    return jnp.asarray(ov.astype(np.float32))


def _odd_mixer(xt, nw, w_in, w_out, cmp_pe, cmp_w1, cmp_w2, sinks, bsz):
    t, d = xt.shape
    seq = t // bsz
    assert seq // NSA_SEL_BLOCK <= LANES and seq % SEL_CHUNK == 0
    cuts = [int(c) for c in np.cumsum((0,) + ODD_SPLITS)]
    ws = [w_in[:, a:b] for a, b in zip(cuts[:-1], cuts[1:])]
    ws[0] = ws[0] * (HEAD_DIM ** -0.5 * LOG2_E)
    ws[8] = ws[8] * (HEAD_DIM ** -0.5 * LOG2_E)
    ws = [w.astype(BF16) for w in ws]
    hd = HEAD_DIM
    head_dims = (hd, hd, hd, hd, hd, hd, hd, 3 * GROUP_HEADS, hd, hd, hd)
    dtypes = (BF16, F32, F32, BF16, BF16, BF16, BF16, F32, BF16, BF16, BF16)
    q_n, kc, vc, ks, vs, kw, vw, gates, q_s, k_s, v_s = _norm_proj(xt, nw, ws, head_dims, dtypes)

    def heads(a):
        return a.reshape(a.shape[0], bsz, seq, a.shape[-1])

    q_n, kc, vc, ks, vs, kw, vw, gates, q_s, k_s, v_s = map(heads, (q_n, kc, vc, ks, vs, kw, vw, gates, q_s, k_s, v_s))
    w1 = cmp_w1.astype(BF16)
    w2 = cmp_w2.astype(BF16)
    k_cmp = _cmp_kv(kc, cmp_pe[0], w1[0], w2[0])
    v_cmp = _cmp_kv(vc, cmp_pe[1], w1[1], w2[1])
    o_cw, sel, o_swa = _nsa_dense(q_n, k_cmp, v_cmp, kw, vw, gates, _overlap_t(seq), q_s, k_s, v_s,
                                  sinks.reshape(SWA_KV_HEADS, 1, GROUP_HEADS))
    o_sel = _nsa_sel(q_n, ks, vs, sel, gates)
    n_nsa = NSA_HEADS * hd
    return [[o_cw.reshape(t, n_nsa), o_sel.reshape(t, n_nsa)], [o_swa.reshape(t, SWA_HEADS * hd)]], [w_out[:n_nsa], w_out[n_nsa:]]


def kernel(x, mem, norm_w, final_norm_w, mem_norm_w, ffn_w_gate, ffn_w_up, ffn_w_down, xattn_wq, xattn_wkv, xattn_wo, even_w_in, even_w_out, ssm_conv_w, ssm_conv_b, ssm_dt_bias, ssm_a_log, ssm_d, ssm_norm_w, odd_w_in, odd_w_out, nsa_cmp_pe, nsa_cmp_w1, nsa_cmp_w2, swa_sinks):
    bsz, seq, d = x.shape
    t = bsz * seq
    m = mem.shape[1]
    wg, wu, wd = ffn_w_gate, ffn_w_up, ffn_w_down
    wkv = xattn_wkv.astype(BF16)
    even_out = even_w_out.astype(BF16)
    odd_out = odd_w_out.astype(BF16)

    xt = x.reshape(t, d)
    mem2 = mem.reshape(bsz * m, d)
    mem_nw = mem_norm_w.reshape(1, d)
    for layer in range(DEPTH):
        i = layer // 2
        xt = _ffn(xt, norm_w[layer, 0].reshape(1, d), wg, wu, wd, layer, 0)
        nw1 = norm_w[layer, 1].reshape(1, d)
        if layer % 2 == 0:
            mix, mix_w = _even_mixer(xt, nw1, even_w_in[i], even_out[i], ssm_conv_w[i], ssm_conv_b[i], ssm_dt_bias[i], ssm_a_log[i], ssm_d[i], ssm_norm_w[i], bsz)
        else:
            mix, mix_w = _odd_mixer(xt, nw1, odd_w_in[i], odd_out[i], nsa_cmp_pe[i], nsa_cmp_w1[i], nsa_cmp_w2[i], swa_sinks[i], bsz)
        (kv,) = _norm_proj(mem2, mem_nw, [wkv[layer]])
        xt = _mix_xattn(xt, mix, mix_w, norm_w[layer, 2].reshape(1, d), xattn_wq, kv.reshape(bsz, m, 2 * d), xattn_wo, layer, bsz)
        out_nw = final_norm_w.reshape(1, d) if layer == DEPTH - 1 else None
        xt = _ffn(xt, norm_w[layer, 3].reshape(1, d), wg, wu, wd, layer, 1, out_nw)
    return xt.reshape(bsz, seq, d)
```

```python
import functools

import numpy as np
import jax
import jax.numpy as jnp
from jax import lax
from jax.experimental import pallas as pl
from jax.experimental.pallas import tpu as pltpu

F32 = jnp.float32
BF16 = jnp.bfloat16

D_MODEL = 1024
DEPTH = 4
D_FF = 2816
NORM_EPS = 1e-6
ATTN_BLOCK = 128

RET_HEADS = 4
RET_DK = 128
RET_DV = 128
RET_CHUNK = 128
RET_ROPE_BASE = 10000.0

SSM_HEADS = 8
SSM_HEAD_DIM = 64
SSM_D_INNER = SSM_HEADS * SSM_HEAD_DIM
SSM_D_STATE = 64
SSM_GROUPS = 2
SSM_CONV = 4
SSM_CHUNK = 128
SSM_CONV_DIM = SSM_D_INNER + 2 * SSM_GROUPS * SSM_D_STATE

EVEN_SPLITS = (RET_HEADS * RET_DK, RET_HEADS * RET_DK, RET_HEADS * RET_DV, RET_HEADS * RET_DV, SSM_D_INNER, SSM_CONV_DIM, SSM_HEADS)

NSA_HEADS = 8
NSA_KV_HEADS = 2
NSA_HEAD_DIM = 64
NSA_CMP_BLOCK = 32
NSA_CMP_STRIDE = 16
NSA_CMP_HIDDEN = 256
NSA_SEL_BLOCK = 64
NSA_TOPN = 16
NSA_WINDOW = 512
NSA_FORCE_SCORE = 1e6

SWA_HEADS = 8
SWA_KV_HEADS = 2
SWA_HEAD_DIM = 64
SWA_WINDOW = 128

ODD_SPLITS = (NSA_HEADS * NSA_HEAD_DIM,) + (NSA_KV_HEADS * NSA_HEAD_DIM,) * 6 + (3 * NSA_HEADS, SWA_HEADS * SWA_HEAD_DIM, SWA_KV_HEADS * SWA_HEAD_DIM, SWA_KV_HEADS * SWA_HEAD_DIM)

XATTN_HEADS = 4
XATTN_HEAD_DIM = D_MODEL // XATTN_HEADS

VMEM_LIMIT_BYTES = 52 * 1024 * 1024
TOKEN_TILE = 512
FFN_TILE = 256
FFN_DOWN_CHUNK = 1024
LANES = 128
SUBLANES = 8
MASKED = -1e30
LOG2_E = float(np.log2(np.e))

_NT = (((1,), (1,)), ((), ()))


def _compiler_params(semantics):
    return pltpu.CompilerParams(dimension_semantics=semantics, vmem_limit_bytes=VMEM_LIMIT_BYTES)


def _rms(x, w):
    return x * lax.rsqrt(jnp.mean(x * x, axis=-1, keepdims=True) + NORM_EPS) * w


def _ffn_kernel(has_out_norm, x_ref, nw_ref, wg_ref, wu_ref, wd_ref, *refs):
    out_nw_ref = refs[0] if has_out_norm else None
    o_ref, a_ref = refs[-2:]
    x = x_ref[...]
    h = _rms(x, nw_ref[...]).astype(BF16)
    for c in range(D_FF // FFN_TILE):
        cols = slice(c * FFN_TILE, (c + 1) * FFN_TILE)
        g = jnp.dot(h, wg_ref[:, cols].astype(BF16), preferred_element_type=F32)
        u = jnp.dot(h, wu_ref[:, cols].astype(BF16), preferred_element_type=F32)
        a_ref[:, cols] = (g * jax.nn.sigmoid(g) * u).astype(BF16)
    acc = None
    for lo in range(0, D_FF, FFN_DOWN_CHUNK):
        hi = min(lo + FFN_DOWN_CHUNK, D_FF)
        part = jnp.dot(a_ref[:, lo:hi], wd_ref[lo:hi, :].astype(BF16), preferred_element_type=F32)
        acc = part if acc is None else acc + part
    y = x + 0.5 * acc
    o_ref[...] = _rms(y, out_nw_ref[...]) if has_out_norm else y


def _ffn(x, nw, wg, wu, wd, layer, k, out_nw=None):
    t, d = x.shape
    tm = TOKEN_TILE
    resident = pl.Buffered(1)
    vec = pl.BlockSpec((1, d), lambda i: (0, 0))
    return pl.pallas_call(
        functools.partial(_ffn_kernel, out_nw is not None),
        grid=(t // tm,),
        in_specs=[
            pl.BlockSpec((tm, d), lambda i: (i, 0)),
            vec,
            pl.BlockSpec((None, None, d, D_FF), lambda i: (layer, k, 0, 0), pipeline_mode=resident),
            pl.BlockSpec((None, None, d, D_FF), lambda i: (layer, k, 0, 0), pipeline_mode=resident),
            pl.BlockSpec((None, None, D_FF, d), lambda i: (layer, k, 0, 0), pipeline_mode=resident),
        ] + ([vec] if out_nw is not None else []),
        out_specs=pl.BlockSpec((tm, d), lambda i: (i, 0)),
        out_shape=jax.ShapeDtypeStruct((t, d), F32),
        scratch_shapes=[pltpu.VMEM((tm, D_FF), BF16)],
        compiler_params=_compiler_params(("parallel",)),
        name="ffn",
    )(x, nw, wg, wu, wd, *([out_nw] if out_nw is not None else []))


def _norm_proj_kernel(head_dims, offsets, x_ref, nw_ref, w_ref, *o_refs):
    h = _rms(x_ref[...], nw_ref[...]).astype(BF16)
    y_all = jnp.dot(h, w_ref[...], preferred_element_type=F32)
    for hd, off, o_ref in zip(head_dims, offsets, o_refs):
        if hd is None:
            o_ref[...] = y_all[:, off:off + o_ref.shape[-1]].astype(o_ref.dtype)
        else:
            for j in range(o_ref.shape[0]):
                o_ref[j] = y_all[:, off + j * hd:off + (j + 1) * hd].astype(o_ref.dtype)


def _norm_proj(x, nw, weights, head_dims=None, dtypes=None, tm=TOKEN_TILE):
    t, d = x.shape
    n_out = len(weights)
    head_dims = tuple(head_dims) if head_dims is not None else (None,) * n_out
    dtypes = tuple(dtypes) if dtypes is not None else (F32,) * n_out
    out_specs, out_shape, offsets, padded = [], [], [], []
    total = 0
    for w, hd, dt in zip(weights, head_dims, dtypes):
        n = w.shape[1]
        if hd is None:
            out_specs.append(pl.BlockSpec((tm, n), lambda i: (i, 0)))
            out_shape.append(jax.ShapeDtypeStruct((t, n), dt))
        else:
            out_specs.append(pl.BlockSpec((n // hd, tm, hd), lambda i: (0, i, 0)))
            out_shape.append(jax.ShapeDtypeStruct((n // hd, t, hd), dt))
        offsets.append(total)
        width = -(-n // LANES) * LANES
        padded.append(jnp.pad(w, ((0, 0), (0, width - n))))
        total += width
    w_all = jnp.concatenate(padded, axis=1) if n_out > 1 else padded[0]
    return pl.pallas_call(
        functools.partial(_norm_proj_kernel, head_dims, tuple(offsets)),
        grid=(t // tm,),
        in_specs=[pl.BlockSpec((tm, d), lambda i: (i, 0)), pl.BlockSpec((1, d), lambda i: (0, 0)),
                  pl.BlockSpec(w_all.shape, lambda i: (0, 0))],
        out_specs=out_specs,
        out_shape=out_shape,
        compiler_params=_compiler_params(("parallel",)),
        name="norm_proj",
    )(x, nw, w_all)


def _mix_xattn_kernel(group_sizes, x_ref, *refs):
    n_a = sum(group_sizes)
    n_g = len(group_sizes)
    a_refs, w_refs = refs[:n_a], refs[n_a:n_a + n_g]
    nw_ref, wq_ref, kv_ref, wo_ref, o_ref = refs[n_a + n_g:]
    x = x_ref[...]
    pos = 0
    for size, w_ref in zip(group_sizes, w_refs):
        a = a_refs[pos][...]
        for r in a_refs[pos + 1:pos + size]:
            a = a.astype(F32) + r[...].astype(F32)
        pos += size
        x = x + jnp.dot(a.astype(BF16), w_ref[...], preferred_element_type=F32)
    h = _rms(x, nw_ref[...]).astype(BF16)
    q = jnp.dot(h, wq_ref[...].astype(BF16), preferred_element_type=F32).astype(BF16)
    hd = XATTN_HEAD_DIM
    outs = []
    for hh in range(XATTN_HEADS):
        k = kv_ref[:, hh * hd:(hh + 1) * hd].astype(BF16)
        v = kv_ref[:, D_MODEL + hh * hd:D_MODEL + (hh + 1) * hd].astype(BF16)
        s = lax.dot_general(q[:, hh * hd:(hh + 1) * hd], k, _NT, preferred_element_type=F32)
        s = s * (hd ** -0.5)
        p = jnp.exp(s - jnp.max(s, axis=-1, keepdims=True))
        p = p / jnp.sum(p, axis=-1, keepdims=True)
        outs.append(jnp.dot(p.astype(BF16), v, preferred_element_type=F32).astype(BF16))
    o = jnp.concatenate(outs, axis=-1)
    o_ref[...] = x + jnp.dot(o, wo_ref[...].astype(BF16), preferred_element_type=F32)


def _mix_xattn(x, groups, weights, nw, wq, kv, wo, layer, bsz, tm=2 * TOKEN_TILE):
    t, d = x.shape
    seq = t // bsz
    nt = seq // tm
    m = kv.shape[1]
    arrays = [a for grp in groups for a in grp]

    def tok(width):
        return pl.BlockSpec((tm, width), lambda b, i: (b * nt + i, 0))

    def const(shape):
        return pl.BlockSpec(shape, lambda b, i: (0, 0))

    layer_w = pl.BlockSpec((None, d, d), lambda b, i: (layer, 0, 0))
    return pl.pallas_call(
        functools.partial(_mix_xattn_kernel, tuple(len(grp) for grp in groups)),
        grid=(bsz, nt),
        in_specs=[tok(d)] + [tok(a.shape[1]) for a in arrays] + [const(w.shape) for w in weights]
        + [const((1, d)), layer_w, pl.BlockSpec((None, m, 2 * d), lambda b, i: (b, 0, 0)), layer_w],
        out_specs=tok(d),
        out_shape=jax.ShapeDtypeStruct((t, d), F32),
        compiler_params=_compiler_params(("parallel", "parallel")),
        name="mix_xattn",
    )(x, *arrays, *weights, nw, wq, kv, wo)


def _retention_kernel(q_ref, k_ref, v_ref, g_ref, cos_ref, sin_ref, dmat_ref, zeta_ref, xi_ref, decay_ref, o_ref, state_ref):
    @pl.when(pl.program_id(1) == 0)
    def _():
        state_ref[...] = jnp.zeros_like(state_ref)

    cos = cos_ref[...]
    sin = sin_ref[...]
    half = RET_DK // 2
    outs = []
    for h in range(RET_HEADS):
        qh = q_ref[:, h * RET_DK:(h + 1) * RET_DK]
        kh = k_ref[:, h * RET_DK:(h + 1) * RET_DK]
        qr = qh * cos + pltpu.roll(qh, half, 1) * sin
        kr = (kh * cos + pltpu.roll(kh, half, 1) * sin) * (RET_DK ** -0.5)
        vh = v_ref[:, h * RET_DV:(h + 1) * RET_DV].astype(BF16)
        scores = lax.dot_general(qr.astype(BF16), kr.astype(BF16), _NT, preferred_element_type=F32) * dmat_ref[h]
        inner = jnp.dot(scores.astype(BF16), vh, preferred_element_type=F32)
        state = state_ref[h]
        cross = jnp.dot((qr * xi_ref[h]).astype(BF16), state.astype(BF16), preferred_element_type=F32)
        kz = (kr * zeta_ref[h]).T.astype(BF16)
        state_ref[h] = decay_ref[h] * state + jnp.dot(kz, vh, preferred_element_type=F32)
        o = inner + cross
        o = o * lax.rsqrt(jnp.mean(o * o, axis=-1, keepdims=True) + NORM_EPS)
        gh = g_ref[:, h * RET_DV:(h + 1) * RET_DV]
        outs.append(gh * jax.nn.sigmoid(gh) * o)
    o_ref[...] = jnp.concatenate(outs, axis=-1).astype(o_ref.dtype)


def _retention_tables(seq):
    L = RET_CHUNK
    pos = jnp.arange(seq, dtype=F32)
    inv_freq = 1.0 / (RET_ROPE_BASE ** jnp.linspace(0.0, 1.0, RET_DK // 2, dtype=F32))
    ang = pos[:, None] * inv_freq[None, :]
    cos, sin = jnp.cos(ang), jnp.sin(ang)
    cos2 = jnp.concatenate([cos, cos], axis=-1)
    sin2 = jnp.concatenate([-sin, sin], axis=-1)
    log_g = jnp.log1p(-jnp.exp2(-5.0 - jnp.arange(RET_HEADS, dtype=F32)))
    idx = jnp.arange(L, dtype=F32)
    diff = idx[:, None] - idx[None, :]
    dmat = jnp.where(diff >= 0, jnp.exp(jnp.maximum(diff, 0.0)[None] * log_g[:, None, None]), 0.0)
    ones = jnp.ones((RET_HEADS, L, RET_DK), F32)
    zeta = jnp.exp((L - 1 - idx)[None, :] * log_g[:, None])[:, :, None] * ones
    xi = jnp.exp((idx + 1)[None, :] * log_g[:, None])[:, :, None] * ones
    decay = jnp.exp(L * log_g)[:, None, None] * ones
    return cos2, sin2, dmat, zeta, xi, decay


CONV_PAD = SUBLANES


def _ssd_kernel(xbc_ref, dt_ref, z_ref, convw_ref, convb_ref, dtb_ref, alog_ref, dskip_ref, nw_ref, tri_ref, o_ref, ext_ref, state_ref):
    L = SSM_CHUNK
    hp = SSM_HEAD_DIM
    ns = SSM_D_STATE

    @pl.when(pl.program_id(1) == 0)
    def _():
        ext_ref[0:CONV_PAD, :] = jnp.zeros((CONV_PAD, SSM_CONV_DIM), F32)
        state_ref[...] = jnp.zeros_like(state_ref)

    ext_ref[CONV_PAD:CONV_PAD + L, :] = xbc_ref[...]
    conv = convb_ref[...]
    for w in range(SSM_CONV):
        off = CONV_PAD - (SSM_CONV - 1) + w
        conv = conv + ext_ref[off:off + L, :] * convw_ref[w:w + 1, :]
    ext_ref[0:CONV_PAD, :] = ext_ref[L:L + CONV_PAD, :]
    xa = conv * jax.nn.sigmoid(conv)
    xs = xa[:, :SSM_D_INNER]
    bmat = xa[:, SSM_D_INNER:SSM_D_INNER + SSM_GROUPS * ns]
    cmat = xa[:, SSM_D_INNER + SSM_GROUPS * ns:]

    x = dt_ref[...] + dtb_ref[...]
    dt = jnp.maximum(x, 0.0) + jnp.log1p(jnp.exp(-jnp.abs(x)))
    lane = lax.broadcasted_iota(jnp.int32, (L, LANES), 1)
    dta = jnp.where(lane < SSM_HEADS, dt * -jnp.exp(alog_ref[...]), 0.0)
    cums = jnp.dot(tri_ref[...], dta, preferred_element_type=F32, precision=lax.Precision.HIGHEST)
    cums_t = cums.T
    last = cums[L - 1:L, :]
    decay_to_end = jnp.exp(last - cums)
    decay_from_start = jnp.exp(cums)
    chunk_decay = jnp.exp(last)
    causal = lax.broadcasted_iota(jnp.int32, (L, L), 0) >= lax.broadcasted_iota(jnp.int32, (L, L), 1)
    heads_per_group = SSM_HEADS // SSM_GROUPS
    ys = []
    for g in range(SSM_GROUPS):
        bg = bmat[:, g * ns:(g + 1) * ns].astype(BF16)
        cg = cmat[:, g * ns:(g + 1) * ns].astype(BF16)
        cb = lax.dot_general(cg, bg, _NT, preferred_element_type=F32)
        for e in range(heads_per_group):
            h = g * heads_per_group + e
            xh = xs[:, h * hp:(h + 1) * hp]
            xdt = xh * dt[:, h:h + 1]
            seg = cums[:, h:h + 1] - cums_t[h:h + 1, :]
            lmat = jnp.where(causal, jnp.exp(seg), 0.0)
            y_diag = jnp.dot((cb * lmat).astype(BF16), xdt.astype(BF16), preferred_element_type=F32)
            state = state_ref[h]
            y_off = lax.dot_general(cg, state.astype(BF16), _NT, preferred_element_type=F32) * decay_from_start[:, h:h + 1]
            xw = (xdt * decay_to_end[:, h:h + 1]).astype(BF16)
            new = lax.dot_general(xw, bg, (((0,), (0,)), ((), ())), preferred_element_type=F32)
            state_ref[h] = chunk_decay[:, h:h + 1] * state + new
            ys.append(y_diag + y_off + dskip_ref[:, h * hp:(h + 1) * hp] * xh)
    y = jnp.concatenate(ys, axis=-1)
    z = z_ref[...]
    o_ref[...] = _rms(y * (z * jax.nn.sigmoid(z)), nw_ref[...]).astype(o_ref.dtype)


N_RET_IN = 10
N_SSD_IN = 10


def _even_core_kernel(*refs):
    ret_in, ssd_in = refs[:N_RET_IN], refs[N_RET_IN:N_RET_IN + N_SSD_IN]
    o_ret_ref, o_ssd_ref, ret_state_ref, ext_ref, ssd_state_ref = refs[N_RET_IN + N_SSD_IN:]
    _retention_kernel(*ret_in, o_ret_ref, ret_state_ref)
    _ssd_kernel(*ssd_in, o_ssd_ref, ext_ref, ssd_state_ref)


def _even_core(q, k, v, g, xbc, dt_raw, z, conv_w, conv_b, dt_bias, a_log, d_skip, norm_w):
    bsz, seq, width = q.shape
    L = RET_CHUNK
    assert SSM_CHUNK == L
    tables = _retention_tables(seq)
    pad = LANES - SSM_HEADS
    params = [conv_w, conv_b.reshape(1, -1), jnp.pad(dt_bias, (0, pad)).reshape(1, LANES),
              jnp.pad(a_log, (0, pad)).reshape(1, LANES), jnp.repeat(d_skip, SSM_HEAD_DIM).reshape(1, -1),
              norm_w.reshape(1, -1), jnp.asarray(np.tril(np.ones((L, L), np.float32)))]

    def tok(w):
        return pl.BlockSpec((None, L, w), lambda b, c: (b, c, 0))

    rope = pl.BlockSpec((L, RET_DK), lambda b, c: (c, 0))
    const = pl.BlockSpec((RET_HEADS, L, RET_DK), lambda b, c: (0, 0, 0))
    return pl.pallas_call(
        _even_core_kernel,
        grid=(bsz, seq // L),
        in_specs=[tok(width)] * 4 + [rope, rope, const, const, const, const]
        + [tok(SSM_CONV_DIM), tok(LANES), tok(SSM_D_INNER)] + [pl.BlockSpec(p.shape, lambda b, c: (0, 0)) for p in params],
        out_specs=[tok(width), tok(SSM_D_INNER)],
        out_shape=[jax.ShapeDtypeStruct((bsz, seq, width), BF16), jax.ShapeDtypeStruct((bsz, seq, SSM_D_INNER), BF16)],
        scratch_shapes=[pltpu.VMEM((RET_HEADS, RET_DK, RET_DV), F32),
                        pltpu.VMEM((L + CONV_PAD, SSM_CONV_DIM), F32), pltpu.VMEM((SSM_HEADS, SSM_HEAD_DIM, SSM_D_STATE), F32)],
        compiler_params=_compiler_params(("parallel", "arbitrary")),
        name="even_core",
    )(q, k, v, g, *tables, xbc, dt_raw, z, *params)


def _even_mixer(xt, nw, w_in, w_out, conv_w, conv_b, dt_bias, a_log, d_skip, ssm_norm_w, bsz):
    t, d = xt.shape
    seq = t // bsz
    cuts = [int(c) for c in np.cumsum((0,) + EVEN_SPLITS)]
    ws = [w_in[:, a:b].astype(BF16) for a, b in zip(cuts[:-1], cuts[1:])]
    perm = np.concatenate([h * RET_DK + np.concatenate([np.arange(0, RET_DK, 2), np.arange(1, RET_DK, 2)]) for h in range(RET_HEADS)])
    ws[0] = ws[0][:, perm]
    ws[1] = ws[1][:, perm]
    ws[6] = jnp.pad(ws[6], ((0, 0), (0, LANES - SSM_HEADS)))
    dtypes = (F32, F32, BF16, F32, F32, F32, F32)
    q, k, v, g, z, xbc, dt_raw = [a.reshape(bsz, seq, -1) for a in _norm_proj(xt, nw, ws, dtypes=dtypes)]
    o_ret, y = _even_core(q, k, v, g, xbc, dt_raw, z, conv_w, conv_b, dt_bias, a_log, d_skip, ssm_norm_w)
    n_ret = RET_HEADS * RET_DV
    return [[o_ret.reshape(t, n_ret)], [y.reshape(t, SSM_D_INNER)]], [w_out[:n_ret], w_out[n_ret:]]


Q_TILE = 2 * ATTN_BLOCK
GROUP_HEADS = NSA_HEADS // NSA_KV_HEADS
HEAD_DIM = NSA_HEAD_DIM
SEL_CHUNK = 512


def _cmp_kv_kernel(x_ref, pe_ref, w1_ref, w2_ref, o_ref):
    n = x_ref.shape[0] // NSA_CMP_STRIDE
    top = bot = None
    for r in range(NSA_CMP_STRIDE):
        rows = x_ref[pl.ds(r, n, stride=NSA_CMP_STRIDE), :]
        lo = r * HEAD_DIM
        hi = (NSA_CMP_STRIDE + r) * HEAD_DIM
        a = jnp.dot((rows + pe_ref[r:r + 1, :]).astype(BF16), w1_ref[lo:lo + HEAD_DIM, :], preferred_element_type=F32)
        b = jnp.dot((rows + pe_ref[NSA_CMP_STRIDE + r:NSA_CMP_STRIDE + r + 1, :]).astype(BF16), w1_ref[hi:hi + HEAD_DIM, :], preferred_element_type=F32)
        top = a if top is None else top + a
        bot = b if bot is None else bot + b
    hidden = top + pltpu.roll(bot, n - 1, 0)
    act = hidden * jax.nn.sigmoid(hidden)
    o_ref[...] = jnp.dot(act.astype(BF16), w2_ref[...], preferred_element_type=F32)


def _cmp_kv(x, pe, w1, w2):
    ng, bsz, seq, hd = x.shape
    n = seq // NSA_CMP_STRIDE
    return pl.pallas_call(
        _cmp_kv_kernel,
        grid=(ng, bsz),
        in_specs=[pl.BlockSpec((None, None, seq, hd), lambda g, b: (g, b, 0, 0)),
                  pl.BlockSpec(pe.shape, lambda g, b: (0, 0)),
                  pl.BlockSpec(w1.shape, lambda g, b: (0, 0)),
                  pl.BlockSpec(w2.shape, lambda g, b: (0, 0))],
        out_specs=pl.BlockSpec((None, None, n, hd), lambda g, b: (g, b, 0, 0)),
        out_shape=jax.ShapeDtypeStruct((ng, bsz, n, hd), F32),
        compiler_params=_compiler_params(("parallel", "parallel")),
        name="nsa_compress",
    )(x, pe, w1, w2)


def _heads_from_t(o_t):
    pairs = []
    for e in range(0, GROUP_HEADS, 2):
        pair = jnp.concatenate([o_t[:, e * Q_TILE:(e + 1) * Q_TILE], o_t[:, (e + 1) * Q_TILE:(e + 2) * Q_TILE]], axis=0)
        pairs.append(pair.T)
    return jnp.concatenate(pairs, axis=-1)


def _gate_rows(gate_ref, g):
    gate = jax.nn.sigmoid(gate_ref[g])
    gate = jnp.concatenate([gate, jnp.zeros((Q_TILE, LANES - gate.shape[1]), F32)], axis=1)
    return gate.T


def _gate_cols(gate_t, branch):
    return jnp.concatenate([gate_t[3 * e + branch:3 * e + branch + 1, :] for e in range(GROUP_HEADS)], axis=1)


def _tile_heads(x):
    return jnp.concatenate([x] * GROUP_HEADS, axis=1)


def _pv_t(v, p):
    return lax.dot_general(v, p.astype(BF16), (((0,), (0,)), ((), ())), preferred_element_type=F32)


ACC_ROWS = HEAD_DIM + SUBLANES


def _with_ones(v):
    pad = jnp.zeros(v.shape[:-1] + (LANES - HEAD_DIM - 1,), v.dtype)
    return jnp.concatenate([v, jnp.ones(v.shape[:-1] + (1,), v.dtype), pad], axis=-1)


def _band_window(window):
    return (-(-(window - 1) // Q_TILE) + 1) * Q_TILE


def _band_branch(i, q, k_ref, v_ref, g, window, sink=None):
    n_keys = _band_window(window)
    start = pl.multiple_of(jnp.maximum(i * Q_TILE + Q_TILE - n_keys, 0), Q_TILE)
    rel = (i * Q_TILE + lax.broadcasted_iota(jnp.int32, (n_keys, Q_TILE), 1)) - (start + lax.broadcasted_iota(jnp.int32, (n_keys, Q_TILE), 0))
    bias = _tile_heads(jnp.where(rel >= 0, jnp.where(rel < window, 0.0, MASKED), MASKED))
    s = lax.dot_general(k_ref[g, pl.ds(start, n_keys), :], q, _NT, preferred_element_type=F32) + bias
    m = jnp.max(s, axis=0, keepdims=True)
    if sink is not None:
        m = jnp.maximum(m, sink)
    p = jnp.exp2(s - m)
    denom = jnp.sum(p, axis=0, keepdims=True)
    denom = denom + jnp.exp2(sink - m) if sink is not None else jnp.maximum(denom, 1e-30)
    return _pv_t(v_ref[g, pl.ds(start, n_keys), :], p) * (1.0 / denom)


def _nsa_dense_kernel(q_ref, kc_ref, vc_ref, kw_ref, vw_ref, gate_ref, ov_ref, qs_ref, ks_ref, vs_ref, sink_ref,
                      o_ref, sel_ref, swa_ref):
    i = pl.program_id(1)
    rows = GROUP_HEADS * Q_TILE
    width = GROUP_HEADS * HEAD_DIM
    n_cmp = kc_ref.shape[1]
    n = lax.broadcasted_iota(jnp.int32, (n_cmp, Q_TILE), 0)
    t = i * Q_TILE + lax.broadcasted_iota(jnp.int32, (n_cmp, Q_TILE), 1)
    valid = _tile_heads(jnp.where(n * NSA_CMP_STRIDE + (NSA_CMP_BLOCK - 1) <= t, 1.0, 0.0))
    bias = (valid - 1.0) * (-MASKED)
    n_sel = ov_ref.shape[0]
    jb = lax.broadcasted_iota(jnp.int32, (n_sel, Q_TILE), 0)
    cur = (i * Q_TILE + lax.broadcasted_iota(jnp.int32, (n_sel, Q_TILE), 1)) // NSA_SEL_BLOCK
    valid_b = jb <= cur
    forced = jnp.where(valid_b, jnp.where(jb == 0, 1.0, 0.0) + jnp.where(jb == cur, 1.0, 0.0) + jnp.where(jb == cur - 1, 1.0, 0.0), 0.0)
    for g in range(NSA_KV_HEADS):
        q = q_ref[g * GROUP_HEADS:(g + 1) * GROUP_HEADS].reshape(rows, HEAD_DIM)
        s = lax.dot_general(kc_ref[g].astype(BF16), q, _NT, preferred_element_type=F32) + bias
        m = jnp.max(s, axis=0, keepdims=True)
        p = jnp.exp2(s - m) * valid
        p = p * (1.0 / jnp.maximum(jnp.sum(p, axis=0, keepdims=True), 1e-30))
        o_cmp = _pv_t(vc_ref[g].astype(BF16), p)
        o_win = _band_branch(i, q, kw_ref, vw_ref, g, NSA_WINDOW)
        gate_t = _gate_rows(gate_ref, g)
        o_ref[:, g * width:(g + 1) * width] = _heads_from_t(o_cmp * _gate_cols(gate_t, 0) + o_win * _gate_cols(gate_t, 2)).astype(o_ref.dtype)

        p_sum = p[:, 0:Q_TILE]
        for e in range(1, GROUP_HEADS):
            p_sum = p_sum + p[:, e * Q_TILE:(e + 1) * Q_TILE]
        imp = jnp.dot(ov_ref[...], p_sum, preferred_element_type=F32, precision=lax.Precision.HIGHEST)
        score = jnp.where(forced > 0.5, NSA_FORCE_SCORE, jnp.where(valid_b, imp, -jnp.inf))
        sub = SUBLANES
        ranks = []
        for v in range(n_sel // sub):
            blk_scores = score[v * sub:(v + 1) * sub, :]
            jb_v = v * sub + lax.broadcasted_iota(jnp.int32, (sub, Q_TILE), 0)
            rank = jnp.zeros((sub, Q_TILE), F32)
            for ii in range(n_sel):
                row = score[ii:ii + 1, :]
                if ii < v * sub:
                    ahead = jnp.where(row >= blk_scores, 1.0, 0.0)
                elif ii >= (v + 1) * sub:
                    ahead = jnp.where(row > blk_scores, 1.0, 0.0)
                else:
                    tie = jnp.where(jb_v > ii, 1.0, 0.0)
                    ahead = jnp.where(row > blk_scores, 1.0, jnp.where(row == blk_scores, tie, 0.0))
                rank = rank + ahead
            ranks.append(rank)
        rank = jnp.concatenate(ranks, axis=0)
        sel_t = jnp.where(valid_b, jnp.where(rank < NSA_TOPN, 1.0, 0.0), 0.0)
        sel_t = jnp.concatenate([sel_t, jnp.zeros((LANES - n_sel, Q_TILE), F32)], axis=0)
        sel_ref[g] = sel_t.T.astype(BF16)

    for g in range(SWA_KV_HEADS):
        q = qs_ref[g * GROUP_HEADS:(g + 1) * GROUP_HEADS].reshape(rows, HEAD_DIM)
        sink = LOG2_E * jnp.concatenate([jnp.broadcast_to(sink_ref[g][:, e:e + 1], (1, Q_TILE)) for e in range(GROUP_HEADS)], axis=1)
        o_swa = _band_branch(i, q, ks_ref, vs_ref, g, SWA_WINDOW, sink)
        swa_ref[:, g * width:(g + 1) * width] = _heads_from_t(o_swa).astype(swa_ref.dtype)


def _nsa_dense(q, k_cmp, v_cmp, kw, vw, gates, overlap_t, q_s, k_s, v_s, sinks):
    nh, bsz, seq, hd = q.shape
    ng, _, n_cmp, _ = k_cmp.shape
    heads = pl.BlockSpec((nh, None, Q_TILE, hd), lambda b, i: (0, b, i, 0))
    cmp_kv = pl.BlockSpec((ng, None, n_cmp, hd), lambda b, i: (0, b, 0, 0))
    seq_kv = pl.BlockSpec((ng, None, seq, hd), lambda b, i: (0, b, 0, 0))
    out = pl.BlockSpec((None, Q_TILE, nh * hd), lambda b, i: (b, i, 0))
    return pl.pallas_call(
        _nsa_dense_kernel,
        grid=(bsz, seq // Q_TILE),
        in_specs=[heads, cmp_kv, cmp_kv, seq_kv, seq_kv,
                  pl.BlockSpec((ng, None, Q_TILE, gates.shape[-1]), lambda b, i: (0, b, i, 0)),
                  pl.BlockSpec(overlap_t.shape, lambda b, i: (0, 0)),
                  heads, seq_kv, seq_kv,
                  pl.BlockSpec(sinks.shape, lambda b, i: (0, 0, 0))],
        out_specs=[out, pl.BlockSpec((ng, None, Q_TILE, LANES), lambda b, i: (0, b, i, 0)), out],
        out_shape=[jax.ShapeDtypeStruct((bsz, seq, nh * hd), BF16),
                   jax.ShapeDtypeStruct((ng, bsz, seq, LANES), BF16),
                   jax.ShapeDtypeStruct((bsz, seq, nh * hd), BF16)],
        compiler_params=_compiler_params(("parallel", "parallel")),
        name="nsa_dense",
    )(q, k_cmp, v_cmp, kw, vw, gates, overlap_t, q_s, k_s, v_s, sinks)


def _nsa_sel_kernel(q_ref, k_ref, v_ref, sel_ref, gate_ref, o_ref, s_ref, m_ref, acc_ref):
    i = pl.program_id(1)
    rows = GROUP_HEADS * Q_TILE
    width = GROUP_HEADS * HEAD_DIM
    n_blk = LANES - HEAD_DIM
    kc = SEL_CHUNK
    seq = k_ref.shape[1]
    groups = range(NSA_KV_HEADS)

    def scores(g, start, lhs):
        return lax.dot_general(k_ref[g, pl.ds(start, kc), :], lhs, _NT, preferred_element_type=F32)

    start_d = pl.multiple_of(jnp.maximum(i * Q_TILE + Q_TILE - kc, 0), Q_TILE)
    n_past = (start_d + kc - 1) // kc
    kpos = start_d + lax.broadcasted_iota(jnp.int32, (kc, rows), 0)
    qpos = i * Q_TILE + (lax.broadcasted_iota(jnp.int32, (kc, rows), 1) & (Q_TILE - 1))
    blk = lax.broadcasted_iota(jnp.int32, (Q_TILE, n_blk), 1)
    lhs_past, lhs_none = [], []
    for g in groups:
        q = q_ref[g * GROUP_HEADS:(g + 1) * GROUP_HEADS].reshape(rows, HEAD_DIM)
        sel = sel_ref[g][:, :n_blk].astype(F32)
        neg_diag = (sel - 1.0) * (-MASKED)
        neg_past = jnp.where(blk < start_d // NSA_SEL_BLOCK, neg_diag, MASKED)
        lhs_diag = jnp.concatenate([q, jnp.concatenate([neg_diag.astype(BF16)] * GROUP_HEADS, axis=0)], axis=-1)
        lhs_past.append(jnp.concatenate([q, jnp.concatenate([neg_past.astype(BF16)] * GROUP_HEADS, axis=0)], axis=-1))
        lhs_none.append(jnp.concatenate([q, jnp.full((rows, n_blk), MASKED, BF16)], axis=-1))
        s_ref[g, 0] = jnp.where(kpos <= qpos, scores(g, start_d, lhs_diag), MASKED)

        m_ref[g] = jnp.full((1, rows), MASKED, F32)
        acc_ref[g] = jnp.zeros((ACC_ROWS, rows), F32)

    def step(c, cur, nxt, lookahead=True):
        next_start = pl.multiple_of(jnp.minimum(c * kc, seq - kc), kc)
        v_start = pl.multiple_of(jnp.where(c == 0, start_d, (c - 1) * kc), Q_TILE)
        for g in groups:
            if lookahead:
                s_ref[g, nxt] = scores(g, next_start, jnp.where(c < n_past, lhs_past[g], lhs_none[g]))
            s = s_ref[g, cur]
            m = m_ref[g]
            m_new = jnp.maximum(m, jnp.max(s, axis=0, keepdims=True))
            alpha = jnp.exp2(m - m_new)
            p = jnp.exp2(s - m_new)
            m_ref[g] = m_new
            acc_ref[g] = alpha * acc_ref[g] + _pv_t(v_ref[g, pl.ds(v_start, kc), :], p)[:ACC_ROWS]

    def body(j, carry):
        step(2 * j, 0, 1)
        step(2 * j + 1, 1, 0)
        return carry

    lax.fori_loop(0, (n_past + 1) // 2, body, 0)

    @pl.when(n_past % 2 == 0)
    def _():
        step(n_past, 0, 1, lookahead=False)

    for g in groups:
        gate = _gate_cols(_gate_rows(gate_ref, g), 1)
        acc = acc_ref[g]
        o_ref[:, g * width:(g + 1) * width] = _heads_from_t(acc[:HEAD_DIM] * (gate / acc[HEAD_DIM:HEAD_DIM + 1])).astype(o_ref.dtype)


def _nsa_sel(q, k, v, sel, gates):
    nh, bsz, seq, hd = q.shape
    ng = k.shape[0]
    onehot = (np.arange(seq)[:, None] // NSA_SEL_BLOCK == np.arange(LANES - hd)[None, :]).astype(np.float32)
    k_ext = jnp.concatenate([k, jnp.broadcast_to(jnp.asarray(onehot, BF16), (ng, bsz, seq, LANES - hd))], axis=-1)
    return pl.pallas_call(
        _nsa_sel_kernel,
        grid=(bsz, seq // Q_TILE),
        in_specs=[pl.BlockSpec((nh, None, Q_TILE, hd), lambda b, i: (0, b, i, 0)),
                  pl.BlockSpec((ng, None, seq, LANES), lambda b, i: (0, b, 0, 0)),
                  pl.BlockSpec((ng, None, seq, LANES), lambda b, i: (0, b, 0, 0)),
                  pl.BlockSpec((ng, None, Q_TILE, LANES), lambda b, i: (0, b, i, 0)),
                  pl.BlockSpec((ng, None, Q_TILE, gates.shape[-1]), lambda b, i: (0, b, i, 0))],
        out_specs=pl.BlockSpec((None, Q_TILE, nh * hd), lambda b, i: (b, i, 0)),
        out_shape=jax.ShapeDtypeStruct((bsz, seq, nh * hd), BF16),
        scratch_shapes=[pltpu.VMEM((ng, 2, SEL_CHUNK, GROUP_HEADS * Q_TILE), F32),
                        pltpu.VMEM((ng, 1, GROUP_HEADS * Q_TILE), F32),
                        pltpu.VMEM((ng, ACC_ROWS, GROUP_HEADS * Q_TILE), F32)],
        compiler_params=_compiler_params(("parallel", "parallel")),
        name="nsa_sel",
    )(q, k_ext, _with_ones(v), sel, gates)


def _overlap_t(seq):
    n_cmp = (seq - NSA_CMP_BLOCK) // NSA_CMP_STRIDE + 1
    n_pad = seq // NSA_CMP_STRIDE
    starts = np.arange(n_pad) * NSA_CMP_STRIDE
    sel_start = np.arange(seq // NSA_SEL_BLOCK) * NSA_SEL_BLOCK
    ov = (starts[None, :] < sel_start[:, None] + NSA_SEL_BLOCK) & (starts[None, :] + NSA_CMP_BLOCK > sel_start[:, None])
    ov = ov & (np.arange(n_pad)[None, :] < n_cmp)
    return jnp.asarray(ov.astype(np.float32))


def _odd_mixer(xt, nw, w_in, w_out, cmp_pe, cmp_w1, cmp_w2, sinks, bsz):
    t, d = xt.shape
    seq = t // bsz
    assert seq // NSA_SEL_BLOCK <= LANES - HEAD_DIM and seq % SEL_CHUNK == 0 and seq % Q_TILE == 0
    assert _band_window(NSA_WINDOW) <= seq and SEL_CHUNK % Q_TILE == 0
    cuts = [int(c) for c in np.cumsum((0,) + ODD_SPLITS)]
    ws = [w_in[:, a:b] for a, b in zip(cuts[:-1], cuts[1:])]
    ws[0] = ws[0] * (HEAD_DIM ** -0.5 * LOG2_E)
    ws[8] = ws[8] * (HEAD_DIM ** -0.5 * LOG2_E)
    ws = [w.astype(BF16) for w in ws]
    hd = HEAD_DIM
    head_dims = (hd, hd, hd, hd, hd, hd, hd, 3 * GROUP_HEADS, hd, hd, hd)
    dtypes = (BF16, F32, F32, BF16, BF16, BF16, BF16, F32, BF16, BF16, BF16)
    q_n, kc, vc, ks, vs, kw, vw, gates, q_s, k_s, v_s = _norm_proj(xt, nw, ws, head_dims, dtypes)

    def heads(a):
        return a.reshape(a.shape[0], bsz, seq, a.shape[-1])

    q_n, kc, vc, ks, vs, kw, vw, gates, q_s, k_s, v_s = map(heads, (q_n, kc, vc, ks, vs, kw, vw, gates, q_s, k_s, v_s))
    w1 = cmp_w1.astype(BF16)
    w2 = cmp_w2.astype(BF16)
    k_cmp = _cmp_kv(kc, cmp_pe[0], w1[0], w2[0])
    v_cmp = _cmp_kv(vc, cmp_pe[1], w1[1], w2[1])
    o_cw, sel, o_swa = _nsa_dense(q_n, k_cmp, v_cmp, kw, vw, gates, _overlap_t(seq), q_s, k_s, v_s,
                                  sinks.reshape(SWA_KV_HEADS, 1, GROUP_HEADS))
    o_sel = _nsa_sel(q_n, ks, vs, sel, gates)
    n_nsa = NSA_HEADS * hd
    return [[o_cw.reshape(t, n_nsa), o_sel.reshape(t, n_nsa)], [o_swa.reshape(t, SWA_HEADS * hd)]], [w_out[:n_nsa], w_out[n_nsa:]]


def kernel(x, mem, norm_w, final_norm_w, mem_norm_w, ffn_w_gate, ffn_w_up, ffn_w_down, xattn_wq, xattn_wkv, xattn_wo, even_w_in, even_w_out, ssm_conv_w, ssm_conv_b, ssm_dt_bias, ssm_a_log, ssm_d, ssm_norm_w, odd_w_in, odd_w_out, nsa_cmp_pe, nsa_cmp_w1, nsa_cmp_w2, swa_sinks):
    bsz, seq, d = x.shape
    t = bsz * seq
    m = mem.shape[1]
    wg, wu, wd = ffn_w_gate, ffn_w_up, ffn_w_down
    wkv = xattn_wkv.astype(BF16)
    even_out = even_w_out.astype(BF16)
    odd_out = odd_w_out.astype(BF16)

    xt = x.reshape(t, d)
    mem2 = mem.reshape(bsz * m, d)
    mem_nw = mem_norm_w.reshape(1, d)
    for layer in range(DEPTH):
        i = layer // 2
        xt = _ffn(xt, norm_w[layer, 0].reshape(1, d), wg, wu, wd, layer, 0)
        nw1 = norm_w[layer, 1].reshape(1, d)
        if layer % 2 == 0:
            mix, mix_w = _even_mixer(xt, nw1, even_w_in[i], even_out[i], ssm_conv_w[i], ssm_conv_b[i], ssm_dt_bias[i], ssm_a_log[i], ssm_d[i], ssm_norm_w[i], bsz)
        else:
            mix, mix_w = _odd_mixer(xt, nw1, odd_w_in[i], odd_out[i], nsa_cmp_pe[i], nsa_cmp_w1[i], nsa_cmp_w2[i], swa_sinks[i], bsz)
        (kv,) = _norm_proj(mem2, mem_nw, [wkv[layer]])
        xt = _mix_xattn(xt, mix, mix_w, norm_w[layer, 2].reshape(1, d), xattn_wq, kv.reshape(bsz, m, 2 * d), xattn_wo, layer, bsz)
        out_nw = final_norm_w.reshape(1, d) if layer == DEPTH - 1 else None
        xt = _ffn(xt, norm_w[layer, 3].reshape(1, d), wg, wu, wd, layer, 1, out_nw)
    return xt.reshape(bsz, seq, d)
```

```python
import functools

import numpy as np
import jax
import jax.numpy as jnp
from jax import lax
from jax.experimental import pallas as pl
from jax.experimental.pallas import tpu as pltpu

F32 = jnp.float32
BF16 = jnp.bfloat16

D_MODEL = 1024
DEPTH = 4
D_FF = 2816
NORM_EPS = 1e-6
ATTN_BLOCK = 128

RET_HEADS = 4
RET_DK = 128
RET_DV = 128
RET_CHUNK = 128
RET_ROPE_BASE = 10000.0

SSM_HEADS = 8
SSM_HEAD_DIM = 64
SSM_D_INNER = SSM_HEADS * SSM_HEAD_DIM
SSM_D_STATE = 64
SSM_GROUPS = 2
SSM_CONV = 4
SSM_CHUNK = 128
SSM_CONV_DIM = SSM_D_INNER + 2 * SSM_GROUPS * SSM_D_STATE

EVEN_SPLITS = (RET_HEADS * RET_DK, RET_HEADS * RET_DK, RET_HEADS * RET_DV, RET_HEADS * RET_DV, SSM_D_INNER, SSM_CONV_DIM, SSM_HEADS)

NSA_HEADS = 8
NSA_KV_HEADS = 2
NSA_HEAD_DIM = 64
NSA_CMP_BLOCK = 32
NSA_CMP_STRIDE = 16
NSA_CMP_HIDDEN = 256
NSA_SEL_BLOCK = 64
NSA_TOPN = 16
NSA_WINDOW = 512
NSA_FORCE_SCORE = 1e6

SWA_HEADS = 8
SWA_KV_HEADS = 2
SWA_HEAD_DIM = 64
SWA_WINDOW = 128

ODD_SPLITS = (NSA_HEADS * NSA_HEAD_DIM,) + (NSA_KV_HEADS * NSA_HEAD_DIM,) * 6 + (3 * NSA_HEADS, SWA_HEADS * SWA_HEAD_DIM, SWA_KV_HEADS * SWA_HEAD_DIM, SWA_KV_HEADS * SWA_HEAD_DIM)

XATTN_HEADS = 4
XATTN_HEAD_DIM = D_MODEL // XATTN_HEADS

VMEM_LIMIT_BYTES = 52 * 1024 * 1024
TOKEN_TILE = 512
FFN_TILE = 256
FFN_DOWN_CHUNK = 1024
LANES = 128
SUBLANES = 8
MASKED = -1e30
LOG2_E = float(np.log2(np.e))

_NT = (((1,), (1,)), ((), ()))


def _compiler_params(semantics):
    return pltpu.CompilerParams(dimension_semantics=semantics, vmem_limit_bytes=VMEM_LIMIT_BYTES)


def _rms(x, w):
    return x * lax.rsqrt(jnp.mean(x * x, axis=-1, keepdims=True) + NORM_EPS) * w


def _ffn_kernel(has_out_norm, x_ref, nw_ref, wg_ref, wu_ref, wd_ref, *refs):
    out_nw_ref = refs[0] if has_out_norm else None
    o_ref, a_ref = refs[-2:]
    x = x_ref[...]
    h = _rms(x, nw_ref[...]).astype(BF16)
    for c in range(D_FF // FFN_TILE):
        cols = slice(c * FFN_TILE, (c + 1) * FFN_TILE)
        g = jnp.dot(h, wg_ref[:, cols].astype(BF16), preferred_element_type=F32)
        u = jnp.dot(h, wu_ref[:, cols].astype(BF16), preferred_element_type=F32)
        a_ref[:, cols] = (g * jax.nn.sigmoid(g) * u).astype(BF16)
    acc = None
    for lo in range(0, D_FF, FFN_DOWN_CHUNK):
        hi = min(lo + FFN_DOWN_CHUNK, D_FF)
        part = jnp.dot(a_ref[:, lo:hi], wd_ref[lo:hi, :].astype(BF16), preferred_element_type=F32)
        acc = part if acc is None else acc + part
    y = x + 0.5 * acc
    o_ref[...] = _rms(y, out_nw_ref[...]) if has_out_norm else y


def _ffn(x, nw, wg, wu, wd, layer, k, out_nw=None):
    t, d = x.shape
    tm = TOKEN_TILE
    resident = pl.Buffered(1)
    vec = pl.BlockSpec((1, d), lambda i: (0, 0))
    return pl.pallas_call(
        functools.partial(_ffn_kernel, out_nw is not None),
        grid=(t // tm,),
        in_specs=[
            pl.BlockSpec((tm, d), lambda i: (i, 0)),
            vec,
            pl.BlockSpec((None, None, d, D_FF), lambda i: (layer, k, 0, 0), pipeline_mode=resident),
            pl.BlockSpec((None, None, d, D_FF), lambda i: (layer, k, 0, 0), pipeline_mode=resident),
            pl.BlockSpec((None, None, D_FF, d), lambda i: (layer, k, 0, 0), pipeline_mode=resident),
        ] + ([vec] if out_nw is not None else []),
        out_specs=pl.BlockSpec((tm, d), lambda i: (i, 0)),
        out_shape=jax.ShapeDtypeStruct((t, d), F32),
        scratch_shapes=[pltpu.VMEM((tm, D_FF), BF16)],
        compiler_params=_compiler_params(("parallel",)),
        name="ffn",
    )(x, nw, wg, wu, wd, *([out_nw] if out_nw is not None else []))


def _norm_proj_kernel(head_dims, offsets, x_ref, nw_ref, w_ref, *o_refs):
    h = _rms(x_ref[...], nw_ref[...]).astype(BF16)
    y_all = jnp.dot(h, w_ref[...], preferred_element_type=F32)
    for hd, off, o_ref in zip(head_dims, offsets, o_refs):
        if hd is None:
            o_ref[...] = y_all[:, off:off + o_ref.shape[-1]].astype(o_ref.dtype)
        else:
            for j in range(o_ref.shape[0]):
                o_ref[j] = y_all[:, off + j * hd:off + (j + 1) * hd].astype(o_ref.dtype)


def _norm_proj(x, nw, weights, head_dims=None, dtypes=None, tm=TOKEN_TILE):
    t, d = x.shape
    n_out = len(weights)
    head_dims = tuple(head_dims) if head_dims is not None else (None,) * n_out
    dtypes = tuple(dtypes) if dtypes is not None else (F32,) * n_out
    out_specs, out_shape, offsets, padded = [], [], [], []
    total = 0
    for w, hd, dt in zip(weights, head_dims, dtypes):
        n = w.shape[1]
        if hd is None:
            out_specs.append(pl.BlockSpec((tm, n), lambda i: (i, 0)))
            out_shape.append(jax.ShapeDtypeStruct((t, n), dt))
        else:
            out_specs.append(pl.BlockSpec((n // hd, tm, hd), lambda i: (0, i, 0)))
            out_shape.append(jax.ShapeDtypeStruct((n // hd, t, hd), dt))
        offsets.append(total)
        width = -(-n // LANES) * LANES
        padded.append(jnp.pad(w, ((0, 0), (0, width - n))))
        total += width
    w_all = jnp.concatenate(padded, axis=1) if n_out > 1 else padded[0]
    return pl.pallas_call(
        functools.partial(_norm_proj_kernel, head_dims, tuple(offsets)),
        grid=(t // tm,),
        in_specs=[pl.BlockSpec((tm, d), lambda i: (i, 0)), pl.BlockSpec((1, d), lambda i: (0, 0)),
                  pl.BlockSpec(w_all.shape, lambda i: (0, 0))],
        out_specs=out_specs,
        out_shape=out_shape,
        compiler_params=_compiler_params(("parallel",)),
        name="norm_proj",
    )(x, nw, w_all)


def _mix_xattn_kernel(group_sizes, x_ref, *refs):
    n_a = sum(group_sizes)
    n_g = len(group_sizes)
    a_refs, w_refs = refs[:n_a], refs[n_a:n_a + n_g]
    nw_ref, wq_ref, kv_ref, wo_ref, o_ref = refs[n_a + n_g:]
    x = x_ref[...]
    pos = 0
    for size, w_ref in zip(group_sizes, w_refs):
        a = a_refs[pos][...]
        for r in a_refs[pos + 1:pos + size]:
            a = a.astype(F32) + r[...].astype(F32)
        pos += size
        x = x + jnp.dot(a.astype(BF16), w_ref[...], preferred_element_type=F32)
    h = _rms(x, nw_ref[...]).astype(BF16)
    q = jnp.dot(h, wq_ref[...].astype(BF16), preferred_element_type=F32).astype(BF16)
    hd = XATTN_HEAD_DIM
    outs = []
    for hh in range(XATTN_HEADS):
        k = kv_ref[:, hh * hd:(hh + 1) * hd].astype(BF16)
        v = kv_ref[:, D_MODEL + hh * hd:D_MODEL + (hh + 1) * hd].astype(BF16)
        s = lax.dot_general(q[:, hh * hd:(hh + 1) * hd], k, _NT, preferred_element_type=F32)
        s = s * (hd ** -0.5)
        p = jnp.exp(s - jnp.max(s, axis=-1, keepdims=True))
        p = p / jnp.sum(p, axis=-1, keepdims=True)
        outs.append(jnp.dot(p.astype(BF16), v, preferred_element_type=F32).astype(BF16))
    o = jnp.concatenate(outs, axis=-1)
    o_ref[...] = x + jnp.dot(o, wo_ref[...].astype(BF16), preferred_element_type=F32)


def _mix_xattn(x, groups, weights, nw, wq, kv, wo, layer, bsz, tm=2 * TOKEN_TILE):
    t, d = x.shape
    seq = t // bsz
    nt = seq // tm
    m = kv.shape[1]
    arrays = [a for grp in groups for a in grp]

    def tok(width):
        return pl.BlockSpec((tm, width), lambda b, i: (b * nt + i, 0))

    def const(shape):
        return pl.BlockSpec(shape, lambda b, i: (0, 0))

    layer_w = pl.BlockSpec((None, d, d), lambda b, i: (layer, 0, 0))
    return pl.pallas_call(
        functools.partial(_mix_xattn_kernel, tuple(len(grp) for grp in groups)),
        grid=(bsz, nt),
        in_specs=[tok(d)] + [tok(a.shape[1]) for a in arrays] + [const(w.shape) for w in weights]
        + [const((1, d)), layer_w, pl.BlockSpec((None, m, 2 * d), lambda b, i: (b, 0, 0)), layer_w],
        out_specs=tok(d),
        out_shape=jax.ShapeDtypeStruct((t, d), F32),
        compiler_params=_compiler_params(("parallel", "parallel")),
        name="mix_xattn",
    )(x, *arrays, *weights, nw, wq, kv, wo)


def _retention_kernel(q_ref, k_ref, v_ref, g_ref, cos_ref, sin_ref, dmat_ref, zeta_ref, xi_ref, decay_ref, o_ref, state_ref):
    @pl.when(pl.program_id(1) == 0)
    def _():
        state_ref[...] = jnp.zeros_like(state_ref)

    cos = cos_ref[...]
    sin = sin_ref[...]
    half = RET_DK // 2
    outs = []
    for h in range(RET_HEADS):
        qh = q_ref[:, h * RET_DK:(h + 1) * RET_DK]
        kh = k_ref[:, h * RET_DK:(h + 1) * RET_DK]
        qr = qh * cos + pltpu.roll(qh, half, 1) * sin
        kr = (kh * cos + pltpu.roll(kh, half, 1) * sin) * (RET_DK ** -0.5)
        vh = v_ref[:, h * RET_DV:(h + 1) * RET_DV].astype(BF16)
        scores = lax.dot_general(qr.astype(BF16), kr.astype(BF16), _NT, preferred_element_type=F32) * dmat_ref[h]
        inner = jnp.dot(scores.astype(BF16), vh, preferred_element_type=F32)
        state = state_ref[h]
        cross = jnp.dot((qr * xi_ref[h]).astype(BF16), state.astype(BF16), preferred_element_type=F32)
        kz = (kr * zeta_ref[h]).T.astype(BF16)
        state_ref[h] = decay_ref[h] * state + jnp.dot(kz, vh, preferred_element_type=F32)
        o = inner + cross
        o = o * lax.rsqrt(jnp.mean(o * o, axis=-1, keepdims=True) + NORM_EPS)
        gh = g_ref[:, h * RET_DV:(h + 1) * RET_DV]
        outs.append(gh * jax.nn.sigmoid(gh) * o)
    o_ref[...] = jnp.concatenate(outs, axis=-1).astype(o_ref.dtype)


def _retention_tables(seq):
    L = RET_CHUNK
    pos = jnp.arange(seq, dtype=F32)
    inv_freq = 1.0 / (RET_ROPE_BASE ** jnp.linspace(0.0, 1.0, RET_DK // 2, dtype=F32))
    ang = pos[:, None] * inv_freq[None, :]
    cos, sin = jnp.cos(ang), jnp.sin(ang)
    cos2 = jnp.concatenate([cos, cos], axis=-1)
    sin2 = jnp.concatenate([-sin, sin], axis=-1)
    log_g = jnp.log1p(-jnp.exp2(-5.0 - jnp.arange(RET_HEADS, dtype=F32)))
    idx = jnp.arange(L, dtype=F32)
    diff = idx[:, None] - idx[None, :]
    dmat = jnp.where(diff >= 0, jnp.exp(jnp.maximum(diff, 0.0)[None] * log_g[:, None, None]), 0.0)
    ones = jnp.ones((RET_HEADS, L, RET_DK), F32)
    zeta = jnp.exp((L - 1 - idx)[None, :] * log_g[:, None])[:, :, None] * ones
    xi = jnp.exp((idx + 1)[None, :] * log_g[:, None])[:, :, None] * ones
    decay = jnp.exp(L * log_g)[:, None, None] * ones
    return cos2, sin2, dmat, zeta, xi, decay


CONV_PAD = SUBLANES


def _ssd_kernel(xbc_ref, dt_ref, z_ref, convw_ref, convb_ref, dtb_ref, alog_ref, dskip_ref, nw_ref, tri_ref, o_ref, ext_ref, state_ref):
    L = SSM_CHUNK
    hp = SSM_HEAD_DIM
    ns = SSM_D_STATE

    @pl.when(pl.program_id(1) == 0)
    def _():
        ext_ref[0:CONV_PAD, :] = jnp.zeros((CONV_PAD, SSM_CONV_DIM), F32)
        state_ref[...] = jnp.zeros_like(state_ref)

    ext_ref[CONV_PAD:CONV_PAD + L, :] = xbc_ref[...]
    conv = convb_ref[...]
    for w in range(SSM_CONV):
        off = CONV_PAD - (SSM_CONV - 1) + w
        conv = conv + ext_ref[off:off + L, :] * convw_ref[w:w + 1, :]
    ext_ref[0:CONV_PAD, :] = ext_ref[L:L + CONV_PAD, :]
    xa = conv * jax.nn.sigmoid(conv)
    xs = xa[:, :SSM_D_INNER]
    bmat = xa[:, SSM_D_INNER:SSM_D_INNER + SSM_GROUPS * ns]
    cmat = xa[:, SSM_D_INNER + SSM_GROUPS * ns:]

    x = dt_ref[...] + dtb_ref[...]
    dt = jnp.maximum(x, 0.0) + jnp.log1p(jnp.exp(-jnp.abs(x)))
    lane = lax.broadcasted_iota(jnp.int32, (L, LANES), 1)
    dta = jnp.where(lane < SSM_HEADS, dt * -jnp.exp(alog_ref[...]), 0.0)
    cums = jnp.dot(tri_ref[...], dta, preferred_element_type=F32, precision=lax.Precision.HIGHEST)
    cums_t = cums.T
    last = cums[L - 1:L, :]
    decay_to_end = jnp.exp(last - cums)
    decay_from_start = jnp.exp(cums)
    chunk_decay = jnp.exp(last)
    causal = lax.broadcasted_iota(jnp.int32, (L, L), 0) >= lax.broadcasted_iota(jnp.int32, (L, L), 1)
    heads_per_group = SSM_HEADS // SSM_GROUPS
    ys = []
    for g in range(SSM_GROUPS):
        bg = bmat[:, g * ns:(g + 1) * ns].astype(BF16)
        cg = cmat[:, g * ns:(g + 1) * ns].astype(BF16)
        cb = lax.dot_general(cg, bg, _NT, preferred_element_type=F32)
        for e in range(heads_per_group):
            h = g * heads_per_group + e
            xh = xs[:, h * hp:(h + 1) * hp]
            xdt = xh * dt[:, h:h + 1]
            seg = cums[:, h:h + 1] - cums_t[h:h + 1, :]
            lmat = jnp.where(causal, jnp.exp(seg), 0.0)
            y_diag = jnp.dot((cb * lmat).astype(BF16), xdt.astype(BF16), preferred_element_type=F32)
            state = state_ref[h]
            y_off = lax.dot_general(cg, state.astype(BF16), _NT, preferred_element_type=F32) * decay_from_start[:, h:h + 1]
            xw = (xdt * decay_to_end[:, h:h + 1]).astype(BF16)
            new = lax.dot_general(xw, bg, (((0,), (0,)), ((), ())), preferred_element_type=F32)
            state_ref[h] = chunk_decay[:, h:h + 1] * state + new
            ys.append(y_diag + y_off + dskip_ref[:, h * hp:(h + 1) * hp] * xh)
    y = jnp.concatenate(ys, axis=-1)
    z = z_ref[...]
    o_ref[...] = _rms(y * (z * jax.nn.sigmoid(z)), nw_ref[...]).astype(o_ref.dtype)


N_RET_IN = 10
N_SSD_IN = 10


def _even_core_kernel(*refs):
    ret_in, ssd_in = refs[:N_RET_IN], refs[N_RET_IN:N_RET_IN + N_SSD_IN]
    o_ret_ref, o_ssd_ref, ret_state_ref, ext_ref, ssd_state_ref = refs[N_RET_IN + N_SSD_IN:]
    _retention_kernel(*ret_in, o_ret_ref, ret_state_ref)
    _ssd_kernel(*ssd_in, o_ssd_ref, ext_ref, ssd_state_ref)


def _even_core(q, k, v, g, xbc, dt_raw, z, conv_w, conv_b, dt_bias, a_log, d_skip, norm_w):
    bsz, seq, width = q.shape
    L = RET_CHUNK
    assert SSM_CHUNK == L
    tables = _retention_tables(seq)
    pad = LANES - SSM_HEADS
    params = [conv_w, conv_b.reshape(1, -1), jnp.pad(dt_bias, (0, pad)).reshape(1, LANES),
              jnp.pad(a_log, (0, pad)).reshape(1, LANES), jnp.repeat(d_skip, SSM_HEAD_DIM).reshape(1, -1),
              norm_w.reshape(1, -1), jnp.asarray(np.tril(np.ones((L, L), np.float32)))]

    def tok(w):
        return pl.BlockSpec((None, L, w), lambda b, c: (b, c, 0))

    rope = pl.BlockSpec((L, RET_DK), lambda b, c: (c, 0))
    const = pl.BlockSpec((RET_HEADS, L, RET_DK), lambda b, c: (0, 0, 0))
    return pl.pallas_call(
        _even_core_kernel,
        grid=(bsz, seq // L),
        in_specs=[tok(width)] * 4 + [rope, rope, const, const, const, const]
        + [tok(SSM_CONV_DIM), tok(LANES), tok(SSM_D_INNER)] + [pl.BlockSpec(p.shape, lambda b, c: (0, 0)) for p in params],
        out_specs=[tok(width), tok(SSM_D_INNER)],
        out_shape=[jax.ShapeDtypeStruct((bsz, seq, width), BF16), jax.ShapeDtypeStruct((bsz, seq, SSM_D_INNER), BF16)],
        scratch_shapes=[pltpu.VMEM((RET_HEADS, RET_DK, RET_DV), F32),
                        pltpu.VMEM((L + CONV_PAD, SSM_CONV_DIM), F32), pltpu.VMEM((SSM_HEADS, SSM_HEAD_DIM, SSM_D_STATE), F32)],
        compiler_params=_compiler_params(("parallel", "arbitrary")),
        name="even_core",
    )(q, k, v, g, *tables, xbc, dt_raw, z, *params)


def _even_mixer(xt, nw, w_in, w_out, conv_w, conv_b, dt_bias, a_log, d_skip, ssm_norm_w, bsz):
    t, d = xt.shape
    seq = t // bsz
    cuts = [int(c) for c in np.cumsum((0,) + EVEN_SPLITS)]
    ws = [w_in[:, a:b].astype(BF16) for a, b in zip(cuts[:-1], cuts[1:])]
    perm = np.concatenate([h * RET_DK + np.concatenate([np.arange(0, RET_DK, 2), np.arange(1, RET_DK, 2)]) for h in range(RET_HEADS)])
    ws[0] = ws[0][:, perm]
    ws[1] = ws[1][:, perm]
    ws[6] = jnp.pad(ws[6], ((0, 0), (0, LANES - SSM_HEADS)))
    dtypes = (F32, F32, BF16, F32, F32, F32, F32)
    q, k, v, g, z, xbc, dt_raw = [a.reshape(bsz, seq, -1) for a in _norm_proj(xt, nw, ws, dtypes=dtypes)]
    o_ret, y = _even_core(q, k, v, g, xbc, dt_raw, z, conv_w, conv_b, dt_bias, a_log, d_skip, ssm_norm_w)
    n_ret = RET_HEADS * RET_DV
    return [[o_ret.reshape(t, n_ret)], [y.reshape(t, SSM_D_INNER)]], [w_out[:n_ret], w_out[n_ret:]]


Q_TILE = 2 * ATTN_BLOCK
GROUP_HEADS = NSA_HEADS // NSA_KV_HEADS
HEAD_DIM = NSA_HEAD_DIM
SEL_CHUNK = 512


def _cmp_kv_kernel(xk_ref, xv_ref, pe_ref, w1_ref, w2_ref, ok_ref, ov_ref):
    for j, (x_ref, o_ref) in enumerate(((xk_ref, ok_ref), (xv_ref, ov_ref))):
        n = x_ref.shape[0] // NSA_CMP_STRIDE
        top = bot = None
        for r in range(NSA_CMP_STRIDE):
            rows = x_ref[pl.ds(r, n, stride=NSA_CMP_STRIDE), :]
            lo = r * HEAD_DIM
            hi = (NSA_CMP_STRIDE + r) * HEAD_DIM
            a = jnp.dot((rows + pe_ref[j, r:r + 1, :]).astype(BF16), w1_ref[j, lo:lo + HEAD_DIM, :], preferred_element_type=F32)
            b = jnp.dot((rows + pe_ref[j, NSA_CMP_STRIDE + r:NSA_CMP_STRIDE + r + 1, :]).astype(BF16), w1_ref[j, hi:hi + HEAD_DIM, :], preferred_element_type=F32)
            top = a if top is None else top + a
            bot = b if bot is None else bot + b
        hidden = top + pltpu.roll(bot, n - 1, 0)
        act = hidden * jax.nn.sigmoid(hidden)
        o_ref[...] = jnp.dot(act.astype(BF16), w2_ref[j], preferred_element_type=F32)


def _cmp_kv(xk, xv, pe, w1, w2):
    ng, bsz, seq, hd = xk.shape
    n = seq // NSA_CMP_STRIDE
    tok = pl.BlockSpec((None, None, seq, hd), lambda g, b: (g, b, 0, 0))
    out = pl.BlockSpec((None, None, n, hd), lambda g, b: (g, b, 0, 0))
    return pl.pallas_call(
        _cmp_kv_kernel,
        grid=(ng, bsz),
        in_specs=[tok, tok] + [pl.BlockSpec(a.shape, lambda g, b: (0, 0, 0)) for a in (pe, w1, w2)],
        out_specs=[out, out],
        out_shape=[jax.ShapeDtypeStruct((ng, bsz, n, hd), F32)] * 2,
        compiler_params=_compiler_params(("parallel", "parallel")),
        name="nsa_compress",
    )(xk, xv, pe, w1, w2)


def _heads_from_t(o_t):
    pairs = []
    for e in range(0, GROUP_HEADS, 2):
        pair = jnp.concatenate([o_t[:, e * Q_TILE:(e + 1) * Q_TILE], o_t[:, (e + 1) * Q_TILE:(e + 2) * Q_TILE]], axis=0)
        pairs.append(pair.T)
    return jnp.concatenate(pairs, axis=-1)


def _gate_rows(gate_ref, g):
    gate = jax.nn.sigmoid(gate_ref[g])
    gate = jnp.concatenate([gate, jnp.zeros((Q_TILE, LANES - gate.shape[1]), F32)], axis=1)
    return gate.T


def _gate_cols(gate_t, branch):
    return jnp.concatenate([gate_t[3 * e + branch:3 * e + branch + 1, :] for e in range(GROUP_HEADS)], axis=1)


def _tile_heads(x):
    return jnp.concatenate([x] * GROUP_HEADS, axis=1)


def _pv_t(v, p):
    return lax.dot_general(v, p.astype(BF16), (((0,), (0,)), ((), ())), preferred_element_type=F32)


ACC_ROWS = HEAD_DIM + SUBLANES


def _with_ones(v):
    pad = jnp.zeros(v.shape[:-1] + (LANES - HEAD_DIM - 1,), v.dtype)
    return jnp.concatenate([v, jnp.ones(v.shape[:-1] + (1,), v.dtype), pad], axis=-1)


def _band_window(window):
    return (-(-(window - 1) // Q_TILE) + 1) * Q_TILE


def _band_branch(i, q, k_ref, v_ref, g, window, sink=None):
    n_keys = _band_window(window)
    start = pl.multiple_of(jnp.maximum(i * Q_TILE + Q_TILE - n_keys, 0), Q_TILE)
    rel = (i * Q_TILE + lax.broadcasted_iota(jnp.int32, (n_keys, Q_TILE), 1)) - (start + lax.broadcasted_iota(jnp.int32, (n_keys, Q_TILE), 0))
    bias = _tile_heads(jnp.where(rel >= 0, jnp.where(rel < window, 0.0, MASKED), MASKED))
    s = lax.dot_general(k_ref[g, pl.ds(start, n_keys), :], q, _NT, preferred_element_type=F32) + bias
    m = jnp.max(s, axis=0, keepdims=True)
    if sink is not None:
        m = jnp.maximum(m, sink)
    p = jnp.exp2(s - m)
    denom = jnp.sum(p, axis=0, keepdims=True)
    denom = denom + jnp.exp2(sink - m) if sink is not None else jnp.maximum(denom, 1e-30)
    return _pv_t(v_ref[g, pl.ds(start, n_keys), :], p) * (1.0 / denom)


def _nsa_dense_kernel(q_ref, kc_ref, vc_ref, kw_ref, vw_ref, gate_ref, ov_ref, qs_ref, ks_ref, vs_ref, sink_ref,
                      o_ref, sel_ref, swa_ref):
    i = pl.program_id(1)
    rows = GROUP_HEADS * Q_TILE
    width = GROUP_HEADS * HEAD_DIM
    n_cmp = kc_ref.shape[1]
    n = lax.broadcasted_iota(jnp.int32, (n_cmp, Q_TILE), 0)
    t = i * Q_TILE + lax.broadcasted_iota(jnp.int32, (n_cmp, Q_TILE), 1)
    valid = _tile_heads(jnp.where(n * NSA_CMP_STRIDE + (NSA_CMP_BLOCK - 1) <= t, 1.0, 0.0))
    bias = (valid - 1.0) * (-MASKED)
    n_sel = ov_ref.shape[0]
    jb = lax.broadcasted_iota(jnp.int32, (n_sel, Q_TILE), 0)
    cur = (i * Q_TILE + lax.broadcasted_iota(jnp.int32, (n_sel, Q_TILE), 1)) // NSA_SEL_BLOCK
    valid_b = jb <= cur
    forced = jnp.where(valid_b, jnp.where(jb == 0, 1.0, 0.0) + jnp.where(jb == cur, 1.0, 0.0) + jnp.where(jb == cur - 1, 1.0, 0.0), 0.0)
    for g in range(NSA_KV_HEADS):
        q = q_ref[g * GROUP_HEADS:(g + 1) * GROUP_HEADS].reshape(rows, HEAD_DIM)
        s = lax.dot_general(kc_ref[g].astype(BF16), q, _NT, preferred_element_type=F32) + bias
        m = jnp.max(s, axis=0, keepdims=True)
        p = jnp.exp2(s - m) * valid
        p = p * (1.0 / jnp.maximum(jnp.sum(p, axis=0, keepdims=True), 1e-30))
        o_cmp = _pv_t(vc_ref[g].astype(BF16), p)
        o_win = _band_branch(i, q, kw_ref, vw_ref, g, NSA_WINDOW)
        gate_t = _gate_rows(gate_ref, g)
        o_ref[:, g * width:(g + 1) * width] = _heads_from_t(o_cmp * _gate_cols(gate_t, 0) + o_win * _gate_cols(gate_t, 2)).astype(o_ref.dtype)

        p_sum = p[:, 0:Q_TILE]
        for e in range(1, GROUP_HEADS):
            p_sum = p_sum + p[:, e * Q_TILE:(e + 1) * Q_TILE]
        imp = jnp.dot(ov_ref[...], p_sum, preferred_element_type=F32, precision=lax.Precision.HIGHEST)
        score = jnp.where(forced > 0.5, NSA_FORCE_SCORE, jnp.where(valid_b, imp, -jnp.inf))
        sub = SUBLANES
        ranks = []
        for v in range(n_sel // sub):
            blk_scores = score[v * sub:(v + 1) * sub, :]
            jb_v = v * sub + lax.broadcasted_iota(jnp.int32, (sub, Q_TILE), 0)
            rank = jnp.zeros((sub, Q_TILE), F32)
            for ii in range(n_sel):
                row = score[ii:ii + 1, :]
                if ii < v * sub:
                    ahead = jnp.where(row >= blk_scores, 1.0, 0.0)
                elif ii >= (v + 1) * sub:
                    ahead = jnp.where(row > blk_scores, 1.0, 0.0)
                else:
                    tie = jnp.where(jb_v > ii, 1.0, 0.0)
                    ahead = jnp.where(row > blk_scores, 1.0, jnp.where(row == blk_scores, tie, 0.0))
                rank = rank + ahead
            ranks.append(rank)
        rank = jnp.concatenate(ranks, axis=0)
        sel_t = jnp.where(valid_b, jnp.where(rank < NSA_TOPN, 1.0, 0.0), 0.0)
        sel_t = jnp.concatenate([sel_t, jnp.zeros((LANES - n_sel, Q_TILE), F32)], axis=0)
        sel_ref[g] = sel_t.T.astype(BF16)

    for g in range(SWA_KV_HEADS):
        q = qs_ref[g * GROUP_HEADS:(g + 1) * GROUP_HEADS].reshape(rows, HEAD_DIM)
        sink = LOG2_E * jnp.concatenate([jnp.broadcast_to(sink_ref[g][:, e:e + 1], (1, Q_TILE)) for e in range(GROUP_HEADS)], axis=1)
        o_swa = _band_branch(i, q, ks_ref, vs_ref, g, SWA_WINDOW, sink)
        swa_ref[:, g * width:(g + 1) * width] = _heads_from_t(o_swa).astype(swa_ref.dtype)


def _nsa_dense(q, k_cmp, v_cmp, kw, vw, gates, overlap_t, q_s, k_s, v_s, sinks):
    nh, bsz, seq, hd = q.shape
    ng, _, n_cmp, _ = k_cmp.shape
    heads = pl.BlockSpec((nh, None, Q_TILE, hd), lambda b, i: (0, b, i, 0))
    cmp_kv = pl.BlockSpec((ng, None, n_cmp, hd), lambda b, i: (0, b, 0, 0))
    seq_kv = pl.BlockSpec((ng, None, seq, hd), lambda b, i: (0, b, 0, 0))
    out = pl.BlockSpec((None, Q_TILE, nh * hd), lambda b, i: (b, i, 0))
    return pl.pallas_call(
        _nsa_dense_kernel,
        grid=(bsz, seq // Q_TILE),
        in_specs=[heads, cmp_kv, cmp_kv, seq_kv, seq_kv,
                  pl.BlockSpec((ng, None, Q_TILE, gates.shape[-1]), lambda b, i: (0, b, i, 0)),
                  pl.BlockSpec(overlap_t.shape, lambda b, i: (0, 0)),
                  heads, seq_kv, seq_kv,
                  pl.BlockSpec(sinks.shape, lambda b, i: (0, 0, 0))],
        out_specs=[out, pl.BlockSpec((ng, None, Q_TILE, LANES), lambda b, i: (0, b, i, 0)), out],
        out_shape=[jax.ShapeDtypeStruct((bsz, seq, nh * hd), BF16),
                   jax.ShapeDtypeStruct((ng, bsz, seq, LANES), BF16),
                   jax.ShapeDtypeStruct((bsz, seq, nh * hd), BF16)],
        compiler_params=_compiler_params(("parallel", "parallel")),
        name="nsa_dense",
    )(q, k_cmp, v_cmp, kw, vw, gates, overlap_t, q_s, k_s, v_s, sinks)


def _nsa_sel_kernel(q_ref, k_ref, v_ref, sel_ref, gate_ref, o_ref, s_ref, m_ref, acc_ref):
    i = pl.program_id(1)
    rows = GROUP_HEADS * Q_TILE
    width = GROUP_HEADS * HEAD_DIM
    n_blk = LANES - HEAD_DIM
    kc = SEL_CHUNK
    seq = k_ref.shape[1]
    groups = range(NSA_KV_HEADS)

    def scores(g, start, lhs):
        return lax.dot_general(k_ref[g, pl.ds(start, kc), :], lhs, _NT, preferred_element_type=F32)

    start_d = pl.multiple_of(jnp.maximum(i * Q_TILE + Q_TILE - kc, 0), Q_TILE)
    n_past = (start_d + kc - 1) // kc
    kpos = start_d + lax.broadcasted_iota(jnp.int32, (kc, rows), 0)
    qpos = i * Q_TILE + (lax.broadcasted_iota(jnp.int32, (kc, rows), 1) & (Q_TILE - 1))
    blk = lax.broadcasted_iota(jnp.int32, (Q_TILE, n_blk), 1)
    lhs_past, lhs_none = [], []
    for g in groups:
        q = q_ref[g * GROUP_HEADS:(g + 1) * GROUP_HEADS].reshape(rows, HEAD_DIM)
        sel = sel_ref[g][:, :n_blk].astype(F32)
        neg_diag = (sel - 1.0) * (-MASKED)
        neg_past = jnp.where(blk < start_d // NSA_SEL_BLOCK, neg_diag, MASKED)
        lhs_diag = jnp.concatenate([q, jnp.concatenate([neg_diag.astype(BF16)] * GROUP_HEADS, axis=0)], axis=-1)
        lhs_past.append(jnp.concatenate([q, jnp.concatenate([neg_past.astype(BF16)] * GROUP_HEADS, axis=0)], axis=-1))
        lhs_none.append(jnp.concatenate([q, jnp.full((rows, n_blk), MASKED, BF16)], axis=-1))
        s_ref[g, 0] = jnp.where(kpos <= qpos, scores(g, start_d, lhs_diag), MASKED)

        m_ref[g] = jnp.full((1, rows), MASKED, F32)
        acc_ref[g] = jnp.zeros((ACC_ROWS, rows), F32)

    def step(c, cur, nxt, lookahead=True):
        next_start = pl.multiple_of(jnp.minimum(c * kc, seq - kc), kc)
        v_start = pl.multiple_of(jnp.where(c == 0, start_d, (c - 1) * kc), Q_TILE)
        for g in groups:
            if lookahead:
                s_ref[g, nxt] = scores(g, next_start, jnp.where(c < n_past, lhs_past[g], lhs_none[g]))
            s = s_ref[g, cur]
            m = m_ref[g]
            m_new = jnp.maximum(m, jnp.max(s, axis=0, keepdims=True))
            alpha = jnp.exp2(m - m_new)
            p = jnp.exp2(s - m_new)
            m_ref[g] = m_new
            acc_ref[g] = alpha * acc_ref[g] + _pv_t(v_ref[g, pl.ds(v_start, kc), :], p)[:ACC_ROWS]

    def body(j, carry):
        step(2 * j, 0, 1)
        step(2 * j + 1, 1, 0)
        return carry

    lax.fori_loop(0, (n_past + 1) // 2, body, 0)

    @pl.when(n_past % 2 == 0)
    def _():
        step(n_past, 0, 1, lookahead=False)

    for g in groups:
        gate = _gate_cols(_gate_rows(gate_ref, g), 1)
        acc = acc_ref[g]
        o_ref[:, g * width:(g + 1) * width] = _heads_from_t(acc[:HEAD_DIM] * (gate / acc[HEAD_DIM:HEAD_DIM + 1])).astype(o_ref.dtype)


def _nsa_sel(q, k, v, sel, gates):
    nh, bsz, seq, hd = q.shape
    ng = k.shape[0]
    onehot = (np.arange(seq)[:, None] // NSA_SEL_BLOCK == np.arange(LANES - hd)[None, :]).astype(np.float32)
    k_ext = jnp.concatenate([k, jnp.broadcast_to(jnp.asarray(onehot, BF16), (ng, bsz, seq, LANES - hd))], axis=-1)
    return pl.pallas_call(
        _nsa_sel_kernel,
        grid=(bsz, seq // Q_TILE),
        in_specs=[pl.BlockSpec((nh, None, Q_TILE, hd), lambda b, i: (0, b, i, 0)),
                  pl.BlockSpec((ng, None, seq, LANES), lambda b, i: (0, b, 0, 0)),
                  pl.BlockSpec((ng, None, seq, LANES), lambda b, i: (0, b, 0, 0)),
                  pl.BlockSpec((ng, None, Q_TILE, LANES), lambda b, i: (0, b, i, 0)),
                  pl.BlockSpec((ng, None, Q_TILE, gates.shape[-1]), lambda b, i: (0, b, i, 0))],
        out_specs=pl.BlockSpec((None, Q_TILE, nh * hd), lambda b, i: (b, i, 0)),
        out_shape=jax.ShapeDtypeStruct((bsz, seq, nh * hd), BF16),
        scratch_shapes=[pltpu.VMEM((ng, 2, SEL_CHUNK, GROUP_HEADS * Q_TILE), F32),
                        pltpu.VMEM((ng, 1, GROUP_HEADS * Q_TILE), F32),
                        pltpu.VMEM((ng, ACC_ROWS, GROUP_HEADS * Q_TILE), F32)],
        compiler_params=_compiler_params(("parallel", "parallel")),
        name="nsa_sel",
    )(q, k_ext, _with_ones(v), sel, gates)


def _overlap_t(seq):
    n_cmp = (seq - NSA_CMP_BLOCK) // NSA_CMP_STRIDE + 1
    n_pad = seq // NSA_CMP_STRIDE
    starts = np.arange(n_pad) * NSA_CMP_STRIDE
    sel_start = np.arange(seq // NSA_SEL_BLOCK) * NSA_SEL_BLOCK
    ov = (starts[None, :] < sel_start[:, None] + NSA_SEL_BLOCK) & (starts[None, :] + NSA_CMP_BLOCK > sel_start[:, None])
    ov = ov & (np.arange(n_pad)[None, :] < n_cmp)
    return jnp.asarray(ov.astype(np.float32))


def _odd_mixer(xt, nw, w_in, w_out, cmp_pe, cmp_w1, cmp_w2, sinks, bsz):
    t, d = xt.shape
    seq = t // bsz
    assert seq // NSA_SEL_BLOCK <= LANES - HEAD_DIM and seq % SEL_CHUNK == 0 and seq % Q_TILE == 0
    assert _band_window(NSA_WINDOW) <= seq and SEL_CHUNK % Q_TILE == 0
    cuts = [int(c) for c in np.cumsum((0,) + ODD_SPLITS)]
    ws = [w_in[:, a:b] for a, b in zip(cuts[:-1], cuts[1:])]
    ws[0] = ws[0] * (HEAD_DIM ** -0.5 * LOG2_E)
    ws[8] = ws[8] * (HEAD_DIM ** -0.5 * LOG2_E)
    ws = [w.astype(BF16) for w in ws]
    hd = HEAD_DIM
    head_dims = (hd, hd, hd, hd, hd, hd, hd, 3 * GROUP_HEADS, hd, hd, hd)
    dtypes = (BF16, F32, F32, BF16, BF16, BF16, BF16, F32, BF16, BF16, BF16)
    q_n, kc, vc, ks, vs, kw, vw, gates, q_s, k_s, v_s = _norm_proj(xt, nw, ws, head_dims, dtypes)

    def heads(a):
        return a.reshape(a.shape[0], bsz, seq, a.shape[-1])

    q_n, kc, vc, ks, vs, kw, vw, gates, q_s, k_s, v_s = map(heads, (q_n, kc, vc, ks, vs, kw, vw, gates, q_s, k_s, v_s))
    k_cmp, v_cmp = _cmp_kv(kc, vc, cmp_pe, cmp_w1.astype(BF16), cmp_w2.astype(BF16))
    o_cw, sel, o_swa = _nsa_dense(q_n, k_cmp, v_cmp, kw, vw, gates, _overlap_t(seq), q_s, k_s, v_s,
                                  sinks.reshape(SWA_KV_HEADS, 1, GROUP_HEADS))
    o_sel = _nsa_sel(q_n, ks, vs, sel, gates)
    n_nsa = NSA_HEADS * hd
    return [[o_cw.reshape(t, n_nsa), o_sel.reshape(t, n_nsa)], [o_swa.reshape(t, SWA_HEADS * hd)]], [w_out[:n_nsa], w_out[n_nsa:]]


def kernel(x, mem, norm_w, final_norm_w, mem_norm_w, ffn_w_gate, ffn_w_up, ffn_w_down, xattn_wq, xattn_wkv, xattn_wo, even_w_in, even_w_out, ssm_conv_w, ssm_conv_b, ssm_dt_bias, ssm_a_log, ssm_d, ssm_norm_w, odd_w_in, odd_w_out, nsa_cmp_pe, nsa_cmp_w1, nsa_cmp_w2, swa_sinks):
    bsz, seq, d = x.shape
    t = bsz * seq
    m = mem.shape[1]
    wg, wu, wd = ffn_w_gate, ffn_w_up, ffn_w_down
    wkv = xattn_wkv.astype(BF16)
    even_out = even_w_out.astype(BF16)
    odd_out = odd_w_out.astype(BF16)

    xt = x.reshape(t, d)
    mem2 = mem.reshape(bsz * m, d)
    mem_nw = mem_norm_w.reshape(1, d)
    for layer in range(DEPTH):
        i = layer // 2
        xt = _ffn(xt, norm_w[layer, 0].reshape(1, d), wg, wu, wd, layer, 0)
        nw1 = norm_w[layer, 1].reshape(1, d)
        if layer % 2 == 0:
            mix, mix_w = _even_mixer(xt, nw1, even_w_in[i], even_out[i], ssm_conv_w[i], ssm_conv_b[i], ssm_dt_bias[i], ssm_a_log[i], ssm_d[i], ssm_norm_w[i], bsz)
        else:
            mix, mix_w = _odd_mixer(xt, nw1, odd_w_in[i], odd_out[i], nsa_cmp_pe[i], nsa_cmp_w1[i], nsa_cmp_w2[i], swa_sinks[i], bsz)
        (kv,) = _norm_proj(mem2, mem_nw, [wkv[layer]])
        xt = _mix_xattn(xt, mix, mix_w, norm_w[layer, 2].reshape(1, d), xattn_wq, kv.reshape(bsz, m, 2 * d), xattn_wo, layer, bsz)
        out_nw = final_norm_w.reshape(1, d) if layer == DEPTH - 1 else None
        xt = _ffn(xt, norm_w[layer, 3].reshape(1, d), wg, wu, wd, layer, 1, out_nw)
    return xt.reshape(bsz, seq, d)
```

```python
import functools

import numpy as np
import jax
import jax.numpy as jnp
from jax import lax
from jax.experimental import pallas as pl
from jax.experimental.pallas import tpu as pltpu

F32 = jnp.float32
BF16 = jnp.bfloat16

D_MODEL = 1024
DEPTH = 4
D_FF = 2816
NORM_EPS = 1e-6
ATTN_BLOCK = 128

RET_HEADS = 4
RET_DK = 128
RET_DV = 128
RET_CHUNK = 128
RET_ROPE_BASE = 10000.0

SSM_HEADS = 8
SSM_HEAD_DIM = 64
SSM_D_INNER = SSM_HEADS * SSM_HEAD_DIM
SSM_D_STATE = 64
SSM_GROUPS = 2
SSM_CONV = 4
SSM_CHUNK = 128
SSM_CONV_DIM = SSM_D_INNER + 2 * SSM_GROUPS * SSM_D_STATE

EVEN_SPLITS = (RET_HEADS * RET_DK, RET_HEADS * RET_DK, RET_HEADS * RET_DV, RET_HEADS * RET_DV, SSM_D_INNER, SSM_CONV_DIM, SSM_HEADS)

NSA_HEADS = 8
NSA_KV_HEADS = 2
NSA_HEAD_DIM = 64
NSA_CMP_BLOCK = 32
NSA_CMP_STRIDE = 16
NSA_CMP_HIDDEN = 256
NSA_SEL_BLOCK = 64
NSA_TOPN = 16
NSA_WINDOW = 512
NSA_FORCE_SCORE = 1e6

SWA_HEADS = 8
SWA_KV_HEADS = 2
SWA_HEAD_DIM = 64
SWA_WINDOW = 128

ODD_SPLITS = (NSA_HEADS * NSA_HEAD_DIM,) + (NSA_KV_HEADS * NSA_HEAD_DIM,) * 6 + (3 * NSA_HEADS, SWA_HEADS * SWA_HEAD_DIM, SWA_KV_HEADS * SWA_HEAD_DIM, SWA_KV_HEADS * SWA_HEAD_DIM)

XATTN_HEADS = 4
XATTN_HEAD_DIM = D_MODEL // XATTN_HEADS

VMEM_LIMIT_BYTES = 52 * 1024 * 1024
TOKEN_TILE = 512
FFN_TILE = 256
FFN_DOWN_CHUNK = 1024
LANES = 128
SUBLANES = 8
MASKED = -1e30
LOG2_E = float(np.log2(np.e))

_NT = (((1,), (1,)), ((), ()))


def _compiler_params(semantics):
    return pltpu.CompilerParams(dimension_semantics=semantics, vmem_limit_bytes=VMEM_LIMIT_BYTES)


def _rms(x, w):
    return x * lax.rsqrt(jnp.mean(x * x, axis=-1, keepdims=True) + NORM_EPS) * w


def _ffn_kernel(has_out_norm, x_ref, nw_ref, wg_ref, wu_ref, wd_ref, *refs):
    out_nw_ref = refs[0] if has_out_norm else None
    o_ref, a_ref = refs[-2:]
    x = x_ref[...]
    h = _rms(x, nw_ref[...]).astype(BF16)
    for c in range(D_FF // FFN_TILE):
        cols = slice(c * FFN_TILE, (c + 1) * FFN_TILE)
        g = jnp.dot(h, wg_ref[:, cols].astype(BF16), preferred_element_type=F32)
        u = jnp.dot(h, wu_ref[:, cols].astype(BF16), preferred_element_type=F32)
        a_ref[:, cols] = (g * jax.nn.sigmoid(g) * u).astype(BF16)
    acc = None
    for lo in range(0, D_FF, FFN_DOWN_CHUNK):
        hi = min(lo + FFN_DOWN_CHUNK, D_FF)
        part = jnp.dot(a_ref[:, lo:hi], wd_ref[lo:hi, :].astype(BF16), preferred_element_type=F32)
        acc = part if acc is None else acc + part
    y = x + 0.5 * acc
    o_ref[...] = _rms(y, out_nw_ref[...]) if has_out_norm else y


def _ffn(x, nw, wg, wu, wd, layer, k, out_nw=None):
    t, d = x.shape
    tm = TOKEN_TILE
    resident = pl.Buffered(1)
    vec = pl.BlockSpec((1, d), lambda i: (0, 0))
    return pl.pallas_call(
        functools.partial(_ffn_kernel, out_nw is not None),
        grid=(t // tm,),
        in_specs=[
            pl.BlockSpec((tm, d), lambda i: (i, 0)),
            vec,
            pl.BlockSpec((None, None, d, D_FF), lambda i: (layer, k, 0, 0), pipeline_mode=resident),
            pl.BlockSpec((None, None, d, D_FF), lambda i: (layer, k, 0, 0), pipeline_mode=resident),
            pl.BlockSpec((None, None, D_FF, d), lambda i: (layer, k, 0, 0), pipeline_mode=resident),
        ] + ([vec] if out_nw is not None else []),
        out_specs=pl.BlockSpec((tm, d), lambda i: (i, 0)),
        out_shape=jax.ShapeDtypeStruct((t, d), F32),
        scratch_shapes=[pltpu.VMEM((tm, D_FF), BF16)],
        compiler_params=_compiler_params(("parallel",)),
        name="ffn",
    )(x, nw, wg, wu, wd, *([out_nw] if out_nw is not None else []))


def _norm_proj_kernel(head_dims, offsets, x_ref, nw_ref, w_ref, *o_refs):
    h = _rms(x_ref[...], nw_ref[...]).astype(BF16)
    y_all = jnp.dot(h, w_ref[...], preferred_element_type=F32)
    for hd, off, o_ref in zip(head_dims, offsets, o_refs):
        if hd is None:
            o_ref[...] = y_all[:, off:off + o_ref.shape[-1]].astype(o_ref.dtype)
        else:
            for j in range(o_ref.shape[0]):
                o_ref[j] = y_all[:, off + j * hd:off + (j + 1) * hd].astype(o_ref.dtype)


def _norm_proj(x, nw, weights, head_dims=None, dtypes=None, tm=TOKEN_TILE):
    t, d = x.shape
    n_out = len(weights)
    head_dims = tuple(head_dims) if head_dims is not None else (None,) * n_out
    dtypes = tuple(dtypes) if dtypes is not None else (F32,) * n_out
    out_specs, out_shape, offsets, padded = [], [], [], []
    total = 0
    for w, hd, dt in zip(weights, head_dims, dtypes):
        n = w.shape[1]
        if hd is None:
            out_specs.append(pl.BlockSpec((tm, n), lambda i: (i, 0)))
            out_shape.append(jax.ShapeDtypeStruct((t, n), dt))
        else:
            out_specs.append(pl.BlockSpec((n // hd, tm, hd), lambda i: (0, i, 0)))
            out_shape.append(jax.ShapeDtypeStruct((n // hd, t, hd), dt))
        offsets.append(total)
        width = -(-n // LANES) * LANES
        padded.append(jnp.pad(w, ((0, 0), (0, width - n))))
        total += width
    w_all = jnp.concatenate(padded, axis=1) if n_out > 1 else padded[0]
    return pl.pallas_call(
        functools.partial(_norm_proj_kernel, head_dims, tuple(offsets)),
        grid=(t // tm,),
        in_specs=[pl.BlockSpec((tm, d), lambda i: (i, 0)), pl.BlockSpec((1, d), lambda i: (0, 0)),
                  pl.BlockSpec(w_all.shape, lambda i: (0, 0))],
        out_specs=out_specs,
        out_shape=out_shape,
        compiler_params=_compiler_params(("parallel",)),
        name="norm_proj",
    )(x, nw, w_all)


def _mix_xattn_kernel(group_sizes, x_ref, *refs):
    n_a = sum(group_sizes)
    n_g = len(group_sizes)
    a_refs, w_refs = refs[:n_a], refs[n_a:n_a + n_g]
    nw_ref, wq_ref, kv_ref, wo_ref, o_ref = refs[n_a + n_g:]
    x = x_ref[...]
    pos = 0
    for size, w_ref in zip(group_sizes, w_refs):
        a = a_refs[pos][...]
        for r in a_refs[pos + 1:pos + size]:
            a = a.astype(F32) + r[...].astype(F32)
        pos += size
        x = x + jnp.dot(a.astype(BF16), w_ref[...], preferred_element_type=F32)
    h = _rms(x, nw_ref[...]).astype(BF16)
    q = jnp.dot(h, wq_ref[...].astype(BF16), preferred_element_type=F32).astype(BF16)
    hd = XATTN_HEAD_DIM
    outs = []
    for hh in range(XATTN_HEADS):
        k = kv_ref[:, hh * hd:(hh + 1) * hd].astype(BF16)
        v = kv_ref[:, D_MODEL + hh * hd:D_MODEL + (hh + 1) * hd].astype(BF16)
        s = lax.dot_general(q[:, hh * hd:(hh + 1) * hd], k, _NT, preferred_element_type=F32)
        s = s * (hd ** -0.5)
        p = jnp.exp(s - jnp.max(s, axis=-1, keepdims=True))
        p = p / jnp.sum(p, axis=-1, keepdims=True)
        outs.append(jnp.dot(p.astype(BF16), v, preferred_element_type=F32).astype(BF16))
    o = jnp.concatenate(outs, axis=-1)
    o_ref[...] = x + jnp.dot(o, wo_ref[...].astype(BF16), preferred_element_type=F32)


def _mix_xattn(x, groups, weights, nw, wq, kv, wo, layer, bsz, tm=2 * TOKEN_TILE):
    t, d = x.shape
    seq = t // bsz
    nt = seq // tm
    m = kv.shape[1]
    arrays = [a for grp in groups for a in grp]

    def tok(width):
        return pl.BlockSpec((tm, width), lambda b, i: (b * nt + i, 0))

    def const(shape):
        return pl.BlockSpec(shape, lambda b, i: (0, 0))

    layer_w = pl.BlockSpec((None, d, d), lambda b, i: (layer, 0, 0))
    return pl.pallas_call(
        functools.partial(_mix_xattn_kernel, tuple(len(grp) for grp in groups)),
        grid=(bsz, nt),
        in_specs=[tok(d)] + [tok(a.shape[1]) for a in arrays] + [const(w.shape) for w in weights]
        + [const((1, d)), layer_w, pl.BlockSpec((None, m, 2 * d), lambda b, i: (b, 0, 0)), layer_w],
        out_specs=tok(d),
        out_shape=jax.ShapeDtypeStruct((t, d), F32),
        compiler_params=_compiler_params(("parallel", "parallel")),
        name="mix_xattn",
    )(x, *arrays, *weights, nw, wq, kv, wo)


def _retention_kernel(q_ref, k_ref, v_ref, g_ref, cos_ref, sin_ref, dmat_ref, zeta_ref, xi_ref, decay_ref, o_ref, state_ref):
    @pl.when(pl.program_id(1) == 0)
    def _():
        state_ref[...] = jnp.zeros_like(state_ref)

    cos = cos_ref[...]
    sin = sin_ref[...]
    half = RET_DK // 2
    outs = []
    for h in range(RET_HEADS):
        qh = q_ref[:, h * RET_DK:(h + 1) * RET_DK]
        kh = k_ref[:, h * RET_DK:(h + 1) * RET_DK]
        qr = qh * cos + pltpu.roll(qh, half, 1) * sin
        kr = (kh * cos + pltpu.roll(kh, half, 1) * sin) * (RET_DK ** -0.5)
        vh = v_ref[:, h * RET_DV:(h + 1) * RET_DV].astype(BF16)
        scores = lax.dot_general(qr.astype(BF16), kr.astype(BF16), _NT, preferred_element_type=F32) * dmat_ref[h]
        inner = jnp.dot(scores.astype(BF16), vh, preferred_element_type=F32)
        state = state_ref[h]
        cross = jnp.dot((qr * xi_ref[h]).astype(BF16), state.astype(BF16), preferred_element_type=F32)
        kz = (kr * zeta_ref[h]).T.astype(BF16)
        state_ref[h] = decay_ref[h] * state + jnp.dot(kz, vh, preferred_element_type=F32)
        o = inner + cross
        o = o * lax.rsqrt(jnp.mean(o * o, axis=-1, keepdims=True) + NORM_EPS)
        gh = g_ref[:, h * RET_DV:(h + 1) * RET_DV]
        outs.append(gh * jax.nn.sigmoid(gh) * o)
    o_ref[...] = jnp.concatenate(outs, axis=-1).astype(o_ref.dtype)


def _retention_tables(seq):
    L = RET_CHUNK
    pos = jnp.arange(seq, dtype=F32)
    inv_freq = 1.0 / (RET_ROPE_BASE ** jnp.linspace(0.0, 1.0, RET_DK // 2, dtype=F32))
    ang = pos[:, None] * inv_freq[None, :]
    cos, sin = jnp.cos(ang), jnp.sin(ang)
    cos2 = jnp.concatenate([cos, cos], axis=-1)
    sin2 = jnp.concatenate([-sin, sin], axis=-1)
    log_g = jnp.log1p(-jnp.exp2(-5.0 - jnp.arange(RET_HEADS, dtype=F32)))
    idx = jnp.arange(L, dtype=F32)
    diff = idx[:, None] - idx[None, :]
    dmat = jnp.where(diff >= 0, jnp.exp(jnp.maximum(diff, 0.0)[None] * log_g[:, None, None]), 0.0)
    ones = jnp.ones((RET_HEADS, L, RET_DK), F32)
    zeta = jnp.exp((L - 1 - idx)[None, :] * log_g[:, None])[:, :, None] * ones
    xi = jnp.exp((idx + 1)[None, :] * log_g[:, None])[:, :, None] * ones
    decay = jnp.exp(L * log_g)[:, None, None] * ones
    return cos2, sin2, dmat, zeta, xi, decay


CONV_PAD = SUBLANES


def _ssd_kernel(xbc_ref, dt_ref, z_ref, convw_ref, convb_ref, dtb_ref, alog_ref, dskip_ref, nw_ref, tri_ref, o_ref, ext_ref, state_ref):
    L = SSM_CHUNK
    hp = SSM_HEAD_DIM
    ns = SSM_D_STATE

    @pl.when(pl.program_id(1) == 0)
    def _():
        ext_ref[0:CONV_PAD, :] = jnp.zeros((CONV_PAD, SSM_CONV_DIM), F32)
        state_ref[...] = jnp.zeros_like(state_ref)

    ext_ref[CONV_PAD:CONV_PAD + L, :] = xbc_ref[...]
    conv = convb_ref[...]
    for w in range(SSM_CONV):
        off = CONV_PAD - (SSM_CONV - 1) + w
        conv = conv + ext_ref[off:off + L, :] * convw_ref[w:w + 1, :]
    ext_ref[0:CONV_PAD, :] = ext_ref[L:L + CONV_PAD, :]
    xa = conv * jax.nn.sigmoid(conv)
    xs = xa[:, :SSM_D_INNER]
    bmat = xa[:, SSM_D_INNER:SSM_D_INNER + SSM_GROUPS * ns]
    cmat = xa[:, SSM_D_INNER + SSM_GROUPS * ns:]

    x = dt_ref[...] + dtb_ref[...]
    dt = jnp.maximum(x, 0.0) + jnp.log1p(jnp.exp(-jnp.abs(x)))
    lane = lax.broadcasted_iota(jnp.int32, (L, LANES), 1)
    dta = jnp.where(lane < SSM_HEADS, dt * -jnp.exp(alog_ref[...]), 0.0)
    cums = jnp.dot(tri_ref[...], dta, preferred_element_type=F32, precision=lax.Precision.HIGHEST)
    cums_t = cums.T
    last = cums[L - 1:L, :]
    decay_to_end = jnp.exp(last - cums)
    decay_from_start = jnp.exp(cums)
    chunk_decay = jnp.exp(last)
    causal = lax.broadcasted_iota(jnp.int32, (L, L), 0) >= lax.broadcasted_iota(jnp.int32, (L, L), 1)
    heads_per_group = SSM_HEADS // SSM_GROUPS
    ys = []
    for g in range(SSM_GROUPS):
        bg = bmat[:, g * ns:(g + 1) * ns].astype(BF16)
        cg = cmat[:, g * ns:(g + 1) * ns].astype(BF16)
        cb = lax.dot_general(cg, bg, _NT, preferred_element_type=F32)
        for e in range(heads_per_group):
            h = g * heads_per_group + e
            xh = xs[:, h * hp:(h + 1) * hp]
            xdt = xh * dt[:, h:h + 1]
            seg = cums[:, h:h + 1] - cums_t[h:h + 1, :]
            lmat = jnp.where(causal, jnp.exp(seg), 0.0)
            y_diag = jnp.dot((cb * lmat).astype(BF16), xdt.astype(BF16), preferred_element_type=F32)
            state = state_ref[h]
            y_off = lax.dot_general(cg, state.astype(BF16), _NT, preferred_element_type=F32) * decay_from_start[:, h:h + 1]
            xw = (xdt * decay_to_end[:, h:h + 1]).astype(BF16)
            new = lax.dot_general(xw, bg, (((0,), (0,)), ((), ())), preferred_element_type=F32)
            state_ref[h] = chunk_decay[:, h:h + 1] * state + new
            ys.append(y_diag + y_off + dskip_ref[:, h * hp:(h + 1) * hp] * xh)
    y = jnp.concatenate(ys, axis=-1)
    z = z_ref[...]
    o_ref[...] = _rms(y * (z * jax.nn.sigmoid(z)), nw_ref[...]).astype(o_ref.dtype)


N_RET_IN = 10
N_SSD_IN = 10


def _even_core_kernel(*refs):
    ret_in, ssd_in = refs[:N_RET_IN], refs[N_RET_IN:N_RET_IN + N_SSD_IN]
    o_ret_ref, o_ssd_ref, ret_state_ref, ext_ref, ssd_state_ref = refs[N_RET_IN + N_SSD_IN:]
    _retention_kernel(*ret_in, o_ret_ref, ret_state_ref)
    _ssd_kernel(*ssd_in, o_ssd_ref, ext_ref, ssd_state_ref)


def _even_core(q, k, v, g, xbc, dt_raw, z, conv_w, conv_b, dt_bias, a_log, d_skip, norm_w):
    bsz, seq, width = q.shape
    L = RET_CHUNK
    assert SSM_CHUNK == L
    tables = _retention_tables(seq)
    pad = LANES - SSM_HEADS
    params = [conv_w, conv_b.reshape(1, -1), jnp.pad(dt_bias, (0, pad)).reshape(1, LANES),
              jnp.pad(a_log, (0, pad)).reshape(1, LANES), jnp.repeat(d_skip, SSM_HEAD_DIM).reshape(1, -1),
              norm_w.reshape(1, -1), jnp.asarray(np.tril(np.ones((L, L), np.float32)))]

    def tok(w):
        return pl.BlockSpec((None, L, w), lambda b, c: (b, c, 0))

    rope = pl.BlockSpec((L, RET_DK), lambda b, c: (c, 0))
    const = pl.BlockSpec((RET_HEADS, L, RET_DK), lambda b, c: (0, 0, 0))
    return pl.pallas_call(
        _even_core_kernel,
        grid=(bsz, seq // L),
        in_specs=[tok(width)] * 4 + [rope, rope, const, const, const, const]
        + [tok(SSM_CONV_DIM), tok(LANES), tok(SSM_D_INNER)] + [pl.BlockSpec(p.shape, lambda b, c: (0, 0)) for p in params],
        out_specs=[tok(width), tok(SSM_D_INNER)],
        out_shape=[jax.ShapeDtypeStruct((bsz, seq, width), BF16), jax.ShapeDtypeStruct((bsz, seq, SSM_D_INNER), BF16)],
        scratch_shapes=[pltpu.VMEM((RET_HEADS, RET_DK, RET_DV), F32),
                        pltpu.VMEM((L + CONV_PAD, SSM_CONV_DIM), F32), pltpu.VMEM((SSM_HEADS, SSM_HEAD_DIM, SSM_D_STATE), F32)],
        compiler_params=_compiler_params(("parallel", "arbitrary")),
        name="even_core",
    )(q, k, v, g, *tables, xbc, dt_raw, z, *params)


def _even_mixer(xt, nw, w_in, w_out, conv_w, conv_b, dt_bias, a_log, d_skip, ssm_norm_w, bsz):
    t, d = xt.shape
    seq = t // bsz
    cuts = [int(c) for c in np.cumsum((0,) + EVEN_SPLITS)]
    ws = [w_in[:, a:b].astype(BF16) for a, b in zip(cuts[:-1], cuts[1:])]
    perm = np.concatenate([h * RET_DK + np.concatenate([np.arange(0, RET_DK, 2), np.arange(1, RET_DK, 2)]) for h in range(RET_HEADS)])
    ws[0] = ws[0][:, perm]
    ws[1] = ws[1][:, perm]
    ws[6] = jnp.pad(ws[6], ((0, 0), (0, LANES - SSM_HEADS)))
    dtypes = (F32, F32, BF16, F32, F32, F32, F32)
    q, k, v, g, z, xbc, dt_raw = [a.reshape(bsz, seq, -1) for a in _norm_proj(xt, nw, ws, dtypes=dtypes)]
    o_ret, y = _even_core(q, k, v, g, xbc, dt_raw, z, conv_w, conv_b, dt_bias, a_log, d_skip, ssm_norm_w)
    n_ret = RET_HEADS * RET_DV
    return [[o_ret.reshape(t, n_ret)], [y.reshape(t, SSM_D_INNER)]], [w_out[:n_ret], w_out[n_ret:]]


Q_TILE = 2 * ATTN_BLOCK
GROUP_HEADS = NSA_HEADS // NSA_KV_HEADS
HEAD_DIM = NSA_HEAD_DIM
SEL_CHUNK = 256


def _cmp_kv_kernel(x_ref, pe_ref, w1_ref, w2_ref, o_ref):
    n = x_ref.shape[0] // NSA_CMP_STRIDE
    top = bot = None
    for r in range(NSA_CMP_STRIDE):
        rows = x_ref[pl.ds(r, n, stride=NSA_CMP_STRIDE), :]
        lo = r * HEAD_DIM
        hi = (NSA_CMP_STRIDE + r) * HEAD_DIM
        a = jnp.dot((rows + pe_ref[r:r + 1, :]).astype(BF16), w1_ref[lo:lo + HEAD_DIM, :], preferred_element_type=F32)
        b = jnp.dot((rows + pe_ref[NSA_CMP_STRIDE + r:NSA_CMP_STRIDE + r + 1, :]).astype(BF16), w1_ref[hi:hi + HEAD_DIM, :], preferred_element_type=F32)
        top = a if top is None else top + a
        bot = b if bot is None else bot + b
    hidden = top + pltpu.roll(bot, n - 1, 0)
    act = hidden * jax.nn.sigmoid(hidden)
    o_ref[...] = jnp.dot(act.astype(BF16), w2_ref[...], preferred_element_type=F32)


def _cmp_kv(x, pe, w1, w2):
    ng, bsz, seq, hd = x.shape
    n = seq // NSA_CMP_STRIDE
    return pl.pallas_call(
        _cmp_kv_kernel,
        grid=(ng, bsz),
        in_specs=[pl.BlockSpec((None, None, seq, hd), lambda g, b: (g, b, 0, 0)),
                  pl.BlockSpec(pe.shape, lambda g, b: (0, 0)),
                  pl.BlockSpec(w1.shape, lambda g, b: (0, 0)),
                  pl.BlockSpec(w2.shape, lambda g, b: (0, 0))],
        out_specs=pl.BlockSpec((None, None, n, hd), lambda g, b: (g, b, 0, 0)),
        out_shape=jax.ShapeDtypeStruct((ng, bsz, n, hd), F32),
        compiler_params=_compiler_params(("parallel", "parallel")),
        name="nsa_compress",
    )(x, pe, w1, w2)


def _heads_from_t(o_t):
    pairs = []
    for e in range(0, GROUP_HEADS, 2):
        pair = jnp.concatenate([o_t[:, e * Q_TILE:(e + 1) * Q_TILE], o_t[:, (e + 1) * Q_TILE:(e + 2) * Q_TILE]], axis=0)
        pairs.append(pair.T)
    return jnp.concatenate(pairs, axis=-1)


def _gate_rows(gate_ref, g):
    gate = jax.nn.sigmoid(gate_ref[g])
    gate = jnp.concatenate([gate, jnp.zeros((Q_TILE, LANES - gate.shape[1]), F32)], axis=1)
    return gate.T


def _gate_cols(gate_t, branch):
    return jnp.concatenate([gate_t[3 * e + branch:3 * e + branch + 1, :] for e in range(GROUP_HEADS)], axis=1)


def _tile_heads(x):
    return jnp.concatenate([x] * GROUP_HEADS, axis=1)


def _pv_t(v, p):
    return lax.dot_general(v, p.astype(BF16), (((0,), (0,)), ((), ())), preferred_element_type=F32)


ACC_ROWS = HEAD_DIM + SUBLANES


def _with_ones(v):
    pad = jnp.zeros(v.shape[:-1] + (LANES - HEAD_DIM - 1,), v.dtype)
    return jnp.concatenate([v, jnp.ones(v.shape[:-1] + (1,), v.dtype), pad], axis=-1)


def _band_window(window):
    return (-(-(window - 1) // Q_TILE) + 1) * Q_TILE


def _band_branch(i, q, k_ref, v_ref, g, window, sink=None):
    n_keys = _band_window(window)
    start = pl.multiple_of(jnp.maximum(i * Q_TILE + Q_TILE - n_keys, 0), Q_TILE)
    rel = (i * Q_TILE + lax.broadcasted_iota(jnp.int32, (n_keys, Q_TILE), 1)) - (start + lax.broadcasted_iota(jnp.int32, (n_keys, Q_TILE), 0))
    bias = _tile_heads(jnp.where(rel >= 0, jnp.where(rel < window, 0.0, MASKED), MASKED))
    s = lax.dot_general(k_ref[g, pl.ds(start, n_keys), :], q, _NT, preferred_element_type=F32) + bias
    m = jnp.max(s, axis=0, keepdims=True)
    if sink is not None:
        m = jnp.maximum(m, sink)
    p = jnp.exp2(s - m)
    denom = jnp.sum(p, axis=0, keepdims=True)
    denom = denom + jnp.exp2(sink - m) if sink is not None else jnp.maximum(denom, 1e-30)
    return _pv_t(v_ref[g, pl.ds(start, n_keys), :], p) * (1.0 / denom)


def _nsa_dense_kernel(q_ref, kc_ref, vc_ref, kw_ref, vw_ref, gate_ref, ov_ref, qs_ref, ks_ref, vs_ref, sink_ref,
                      o_ref, sel_ref, swa_ref):
    i = pl.program_id(1)
    rows = GROUP_HEADS * Q_TILE
    width = GROUP_HEADS * HEAD_DIM
    n_cmp = kc_ref.shape[1]
    n = lax.broadcasted_iota(jnp.int32, (n_cmp, Q_TILE), 0)
    t = i * Q_TILE + lax.broadcasted_iota(jnp.int32, (n_cmp, Q_TILE), 1)
    valid = _tile_heads(jnp.where(n * NSA_CMP_STRIDE + (NSA_CMP_BLOCK - 1) <= t, 1.0, 0.0))
    bias = (valid - 1.0) * (-MASKED)
    n_sel = ov_ref.shape[0]
    jb = lax.broadcasted_iota(jnp.int32, (n_sel, Q_TILE), 0)
    cur = (i * Q_TILE + lax.broadcasted_iota(jnp.int32, (n_sel, Q_TILE), 1)) // NSA_SEL_BLOCK
    valid_b = jb <= cur
    forced = jnp.where(valid_b, jnp.where(jb == 0, 1.0, 0.0) + jnp.where(jb == cur, 1.0, 0.0) + jnp.where(jb == cur - 1, 1.0, 0.0), 0.0)
    for g in range(NSA_KV_HEADS):
        q = q_ref[g * GROUP_HEADS:(g + 1) * GROUP_HEADS].reshape(rows, HEAD_DIM)
        s = lax.dot_general(kc_ref[g].astype(BF16), q, _NT, preferred_element_type=F32) + bias
        m = jnp.max(s, axis=0, keepdims=True)
        p = jnp.exp2(s - m) * valid
        p = p * (1.0 / jnp.maximum(jnp.sum(p, axis=0, keepdims=True), 1e-30))
        o_cmp = _pv_t(vc_ref[g].astype(BF16), p)
        o_win = _band_branch(i, q, kw_ref, vw_ref, g, NSA_WINDOW)
        gate_t = _gate_rows(gate_ref, g)
        o_ref[:, g * width:(g + 1) * width] = _heads_from_t(o_cmp * _gate_cols(gate_t, 0) + o_win * _gate_cols(gate_t, 2)).astype(o_ref.dtype)

        p_sum = p[:, 0:Q_TILE]
        for e in range(1, GROUP_HEADS):
            p_sum = p_sum + p[:, e * Q_TILE:(e + 1) * Q_TILE]
        imp = jnp.dot(ov_ref[...], p_sum, preferred_element_type=F32, precision=lax.Precision.HIGHEST)
        score = jnp.where(forced > 0.5, NSA_FORCE_SCORE, jnp.where(valid_b, imp, -jnp.inf))
        sub = SUBLANES
        ranks = []
        for v in range(n_sel // sub):
            blk_scores = score[v * sub:(v + 1) * sub, :]
            jb_v = v * sub + lax.broadcasted_iota(jnp.int32, (sub, Q_TILE), 0)
            rank = jnp.zeros((sub, Q_TILE), F32)
            for ii in range(n_sel):
                row = score[ii:ii + 1, :]
                if ii < v * sub:
                    ahead = jnp.where(row >= blk_scores, 1.0, 0.0)
                elif ii >= (v + 1) * sub:
                    ahead = jnp.where(row > blk_scores, 1.0, 0.0)
                else:
                    tie = jnp.where(jb_v > ii, 1.0, 0.0)
                    ahead = jnp.where(row > blk_scores, 1.0, jnp.where(row == blk_scores, tie, 0.0))
                rank = rank + ahead
            ranks.append(rank)
        rank = jnp.concatenate(ranks, axis=0)
        sel_t = jnp.where(valid_b, jnp.where(rank < NSA_TOPN, 1.0, 0.0), 0.0)
        sel_t = jnp.concatenate([sel_t, jnp.zeros((LANES - n_sel, Q_TILE), F32)], axis=0)
        sel_ref[g] = sel_t.T.astype(BF16)

    for g in range(SWA_KV_HEADS):
        q = qs_ref[g * GROUP_HEADS:(g + 1) * GROUP_HEADS].reshape(rows, HEAD_DIM)
        sink = LOG2_E * jnp.concatenate([jnp.broadcast_to(sink_ref[g][:, e:e + 1], (1, Q_TILE)) for e in range(GROUP_HEADS)], axis=1)
        o_swa = _band_branch(i, q, ks_ref, vs_ref, g, SWA_WINDOW, sink)
        swa_ref[:, g * width:(g + 1) * width] = _heads_from_t(o_swa).astype(swa_ref.dtype)


def _nsa_dense(q, k_cmp, v_cmp, kw, vw, gates, overlap_t, q_s, k_s, v_s, sinks):
    nh, bsz, seq, hd = q.shape
    ng, _, n_cmp, _ = k_cmp.shape
    heads = pl.BlockSpec((nh, None, Q_TILE, hd), lambda b, i: (0, b, i, 0))
    cmp_kv = pl.BlockSpec((ng, None, n_cmp, hd), lambda b, i: (0, b, 0, 0))
    seq_kv = pl.BlockSpec((ng, None, seq, hd), lambda b, i: (0, b, 0, 0))
    out = pl.BlockSpec((None, Q_TILE, nh * hd), lambda b, i: (b, i, 0))
    return pl.pallas_call(
        _nsa_dense_kernel,
        grid=(bsz, seq // Q_TILE),
        in_specs=[heads, cmp_kv, cmp_kv, seq_kv, seq_kv,
                  pl.BlockSpec((ng, None, Q_TILE, gates.shape[-1]), lambda b, i: (0, b, i, 0)),
                  pl.BlockSpec(overlap_t.shape, lambda b, i: (0, 0)),
                  heads, seq_kv, seq_kv,
                  pl.BlockSpec(sinks.shape, lambda b, i: (0, 0, 0))],
        out_specs=[out, pl.BlockSpec((ng, None, Q_TILE, LANES), lambda b, i: (0, b, i, 0)), out],
        out_shape=[jax.ShapeDtypeStruct((bsz, seq, nh * hd), BF16),
                   jax.ShapeDtypeStruct((ng, bsz, seq, LANES), BF16),
                   jax.ShapeDtypeStruct((bsz, seq, nh * hd), BF16)],
        compiler_params=_compiler_params(("parallel", "parallel")),
        name="nsa_dense",
    )(q, k_cmp, v_cmp, kw, vw, gates, overlap_t, q_s, k_s, v_s, sinks)


def _nsa_sel_kernel(q_ref, k_ref, v_ref, sel_ref, gate_ref, o_ref, s_ref, m_ref, acc_ref):
    i = pl.program_id(1)
    rows = GROUP_HEADS * Q_TILE
    width = GROUP_HEADS * HEAD_DIM
    n_blk = LANES - HEAD_DIM
    kc = SEL_CHUNK
    seq = k_ref.shape[1]
    groups = range(NSA_KV_HEADS)

    def scores(g, start, lhs):
        return lax.dot_general(k_ref[g, pl.ds(start, kc), :], lhs, _NT, preferred_element_type=F32)

    start_d = pl.multiple_of(jnp.maximum(i * Q_TILE + Q_TILE - kc, 0), Q_TILE)
    n_past = (start_d + kc - 1) // kc
    kpos = start_d + lax.broadcasted_iota(jnp.int32, (kc, rows), 0)
    qpos = i * Q_TILE + (lax.broadcasted_iota(jnp.int32, (kc, rows), 1) & (Q_TILE - 1))
    blk = lax.broadcasted_iota(jnp.int32, (Q_TILE, n_blk), 1)
    lhs_past, lhs_none = [], []
    for g in groups:
        q = q_ref[g * GROUP_HEADS:(g + 1) * GROUP_HEADS].reshape(rows, HEAD_DIM)
        sel = sel_ref[g][:, :n_blk].astype(F32)
        neg_diag = (sel - 1.0) * (-MASKED)
        neg_past = jnp.where(blk < start_d // NSA_SEL_BLOCK, neg_diag, MASKED)
        lhs_diag = jnp.concatenate([q, jnp.concatenate([neg_diag.astype(BF16)] * GROUP_HEADS, axis=0)], axis=-1)
        lhs_past.append(jnp.concatenate([q, jnp.concatenate([neg_past.astype(BF16)] * GROUP_HEADS, axis=0)], axis=-1))
        lhs_none.append(jnp.concatenate([q, jnp.full((rows, n_blk), MASKED, BF16)], axis=-1))
        s_ref[g, 0] = jnp.where(kpos <= qpos, scores(g, start_d, lhs_diag), MASKED)

        m_ref[g] = jnp.full((1, rows), MASKED, F32)
        acc_ref[g] = jnp.zeros((ACC_ROWS, rows), F32)

    def step(c, cur, nxt, lookahead=True):
        next_start = pl.multiple_of(jnp.minimum(c * kc, seq - kc), kc)
        v_start = pl.multiple_of(jnp.where(c == 0, start_d, (c - 1) * kc), Q_TILE)
        for g in groups:
            if lookahead:
                s_ref[g, nxt] = scores(g, next_start, jnp.where(c < n_past, lhs_past[g], lhs_none[g]))
            s = s_ref[g, cur]
            m = m_ref[g]
            m_new = jnp.maximum(m, jnp.max(s, axis=0, keepdims=True))
            alpha = jnp.exp2(m - m_new)
            p = jnp.exp2(s - m_new)
            m_ref[g] = m_new
            acc_ref[g] = alpha * acc_ref[g] + _pv_t(v_ref[g, pl.ds(v_start, kc), :], p)[:ACC_ROWS]

    def body(j, carry):
        step(2 * j, 0, 1)
        step(2 * j + 1, 1, 0)
        return carry

    lax.fori_loop(0, (n_past + 1) // 2, body, 0)

    @pl.when(n_past % 2 == 0)
    def _():
        step(n_past, 0, 1, lookahead=False)

    for g in groups:
        gate = _gate_cols(_gate_rows(gate_ref, g), 1)
        acc = acc_ref[g]
        o_ref[:, g * width:(g + 1) * width] = _heads_from_t(acc[:HEAD_DIM] * (gate / acc[HEAD_DIM:HEAD_DIM + 1])).astype(o_ref.dtype)


def _nsa_sel(q, k, v, sel, gates):
    nh, bsz, seq, hd = q.shape
    ng = k.shape[0]
    onehot = (np.arange(seq)[:, None] // NSA_SEL_BLOCK == np.arange(LANES - hd)[None, :]).astype(np.float32)
    k_ext = jnp.concatenate([k, jnp.broadcast_to(jnp.asarray(onehot, BF16), (ng, bsz, seq, LANES - hd))], axis=-1)
    return pl.pallas_call(
        _nsa_sel_kernel,
        grid=(bsz, seq // Q_TILE),
        in_specs=[pl.BlockSpec((nh, None, Q_TILE, hd), lambda b, i: (0, b, i, 0)),
                  pl.BlockSpec((ng, None, seq, LANES), lambda b, i: (0, b, 0, 0)),
                  pl.BlockSpec((ng, None, seq, LANES), lambda b, i: (0, b, 0, 0)),
                  pl.BlockSpec((ng, None, Q_TILE, LANES), lambda b, i: (0, b, i, 0)),
                  pl.BlockSpec((ng, None, Q_TILE, gates.shape[-1]), lambda b, i: (0, b, i, 0))],
        out_specs=pl.BlockSpec((None, Q_TILE, nh * hd), lambda b, i: (b, i, 0)),
        out_shape=jax.ShapeDtypeStruct((bsz, seq, nh * hd), BF16),
        scratch_shapes=[pltpu.VMEM((ng, 2, SEL_CHUNK, GROUP_HEADS * Q_TILE), F32),
                        pltpu.VMEM((ng, 1, GROUP_HEADS * Q_TILE), F32),
                        pltpu.VMEM((ng, ACC_ROWS, GROUP_HEADS * Q_TILE), F32)],
        compiler_params=_compiler_params(("parallel", "parallel")),
        name="nsa_sel",
    )(q, k_ext, _with_ones(v), sel, gates)


def _overlap_t(seq):
    n_cmp = (seq - NSA_CMP_BLOCK) // NSA_CMP_STRIDE + 1
    n_pad = seq // NSA_CMP_STRIDE
    starts = np.arange(n_pad) * NSA_CMP_STRIDE
    sel_start = np.arange(seq // NSA_SEL_BLOCK) * NSA_SEL_BLOCK
    ov = (starts[None, :] < sel_start[:, None] + NSA_SEL_BLOCK) & (starts[None, :] + NSA_CMP_BLOCK > sel_start[:, None])
    ov = ov & (np.arange(n_pad)[None, :] < n_cmp)
    return jnp.asarray(ov.astype(np.float32))


def _odd_mixer(xt, nw, w_in, w_out, cmp_pe, cmp_w1, cmp_w2, sinks, bsz):
    t, d = xt.shape
    seq = t // bsz
    assert seq // NSA_SEL_BLOCK <= LANES - HEAD_DIM and seq % SEL_CHUNK == 0 and seq % Q_TILE == 0
    assert _band_window(NSA_WINDOW) <= seq and SEL_CHUNK % Q_TILE == 0
    cuts = [int(c) for c in np.cumsum((0,) + ODD_SPLITS)]
    ws = [w_in[:, a:b] for a, b in zip(cuts[:-1], cuts[1:])]
    ws[0] = ws[0] * (HEAD_DIM ** -0.5 * LOG2_E)
    ws[8] = ws[8] * (HEAD_DIM ** -0.5 * LOG2_E)
    ws = [w.astype(BF16) for w in ws]
    hd = HEAD_DIM
    head_dims = (hd, hd, hd, hd, hd, hd, hd, 3 * GROUP_HEADS, hd, hd, hd)
    dtypes = (BF16, F32, F32, BF16, BF16, BF16, BF16, F32, BF16, BF16, BF16)
    q_n, kc, vc, ks, vs, kw, vw, gates, q_s, k_s, v_s = _norm_proj(xt, nw, ws, head_dims, dtypes)

    def heads(a):
        return a.reshape(a.shape[0], bsz, seq, a.shape[-1])

    q_n, kc, vc, ks, vs, kw, vw, gates, q_s, k_s, v_s = map(heads, (q_n, kc, vc, ks, vs, kw, vw, gates, q_s, k_s, v_s))
    w1 = cmp_w1.astype(BF16)
    w2 = cmp_w2.astype(BF16)
    k_cmp = _cmp_kv(kc, cmp_pe[0], w1[0], w2[0])
    v_cmp = _cmp_kv(vc, cmp_pe[1], w1[1], w2[1])
    o_cw, sel, o_swa = _nsa_dense(q_n, k_cmp, v_cmp, kw, vw, gates, _overlap_t(seq), q_s, k_s, v_s,
                                  sinks.reshape(SWA_KV_HEADS, 1, GROUP_HEADS))
    o_sel = _nsa_sel(q_n, ks, vs, sel, gates)
    n_nsa = NSA_HEADS * hd
    return [[o_cw.reshape(t, n_nsa), o_sel.reshape(t, n_nsa)], [o_swa.reshape(t, SWA_HEADS * hd)]], [w_out[:n_nsa], w_out[n_nsa:]]


def kernel(x, mem, norm_w, final_norm_w, mem_norm_w, ffn_w_gate, ffn_w_up, ffn_w_down, xattn_wq, xattn_wkv, xattn_wo, even_w_in, even_w_out, ssm_conv_w, ssm_conv_b, ssm_dt_bias, ssm_a_log, ssm_d, ssm_norm_w, odd_w_in, odd_w_out, nsa_cmp_pe, nsa_cmp_w1, nsa_cmp_w2, swa_sinks):
    bsz, seq, d = x.shape
    t = bsz * seq
    m = mem.shape[1]
    wg, wu, wd = ffn_w_gate, ffn_w_up, ffn_w_down
    wkv = xattn_wkv.astype(BF16)
    even_out = even_w_out.astype(BF16)
    odd_out = odd_w_out.astype(BF16)

    xt = x.reshape(t, d)
    mem2 = mem.reshape(bsz * m, d)
    mem_nw = mem_norm_w.reshape(1, d)
    for layer in range(DEPTH):
        i = layer // 2
        xt = _ffn(xt, norm_w[layer, 0].reshape(1, d), wg, wu, wd, layer, 0)
        nw1 = norm_w[layer, 1].reshape(1, d)
        if layer % 2 == 0:
            mix, mix_w = _even_mixer(xt, nw1, even_w_in[i], even_out[i], ssm_conv_w[i], ssm_conv_b[i], ssm_dt_bias[i], ssm_a_log[i], ssm_d[i], ssm_norm_w[i], bsz)
        else:
            mix, mix_w = _odd_mixer(xt, nw1, odd_w_in[i], odd_out[i], nsa_cmp_pe[i], nsa_cmp_w1[i], nsa_cmp_w2[i], swa_sinks[i], bsz)
        (kv,) = _norm_proj(mem2, mem_nw, [wkv[layer]])
        xt = _mix_xattn(xt, mix, mix_w, norm_w[layer, 2].reshape(1, d), xattn_wq, kv.reshape(bsz, m, 2 * d), xattn_wo, layer, bsz)
        out_nw = final_norm_w.reshape(1, d) if layer == DEPTH - 1 else None
        xt = _ffn(xt, norm_w[layer, 3].reshape(1, d), wg, wu, wd, layer, 1, out_nw)
    return xt.reshape(bsz, seq, d)
```

```python
import functools

import numpy as np
import jax
import jax.numpy as jnp
from jax import lax
from jax.experimental import pallas as pl
from jax.experimental.pallas import tpu as pltpu

F32 = jnp.float32
BF16 = jnp.bfloat16

D_MODEL = 1024
DEPTH = 4
D_FF = 2816
NORM_EPS = 1e-6
ATTN_BLOCK = 128

RET_HEADS = 4
RET_DK = 128
RET_DV = 128
RET_CHUNK = 128
RET_ROPE_BASE = 10000.0

SSM_HEADS = 8
SSM_HEAD_DIM = 64
SSM_D_INNER = SSM_HEADS * SSM_HEAD_DIM
SSM_D_STATE = 64
SSM_GROUPS = 2
SSM_CONV = 4
SSM_CHUNK = 128
SSM_CONV_DIM = SSM_D_INNER + 2 * SSM_GROUPS * SSM_D_STATE

EVEN_SPLITS = (RET_HEADS * RET_DK, RET_HEADS * RET_DK, RET_HEADS * RET_DV, RET_HEADS * RET_DV, SSM_D_INNER, SSM_CONV_DIM, SSM_HEADS)

NSA_HEADS = 8
NSA_KV_HEADS = 2
NSA_HEAD_DIM = 64
NSA_CMP_BLOCK = 32
NSA_CMP_STRIDE = 16
NSA_CMP_HIDDEN = 256
NSA_SEL_BLOCK = 64
NSA_TOPN = 16
NSA_WINDOW = 512
NSA_FORCE_SCORE = 1e6

SWA_HEADS = 8
SWA_KV_HEADS = 2
SWA_HEAD_DIM = 64
SWA_WINDOW = 128

ODD_SPLITS = (NSA_HEADS * NSA_HEAD_DIM,) + (NSA_KV_HEADS * NSA_HEAD_DIM,) * 6 + (3 * NSA_HEADS, SWA_HEADS * SWA_HEAD_DIM, SWA_KV_HEADS * SWA_HEAD_DIM, SWA_KV_HEADS * SWA_HEAD_DIM)

XATTN_HEADS = 4
XATTN_HEAD_DIM = D_MODEL // XATTN_HEADS

VMEM_LIMIT_BYTES = 52 * 1024 * 1024
TOKEN_TILE = 512
FFN_TILE = 256
FFN_DOWN_CHUNK = 1024
LANES = 128
SUBLANES = 8
MASKED = -1e30
LOG2_E = float(np.log2(np.e))

_NT = (((1,), (1,)), ((), ()))


def _compiler_params(semantics):
    return pltpu.CompilerParams(dimension_semantics=semantics, vmem_limit_bytes=VMEM_LIMIT_BYTES)


def _rms(x, w):
    return x * lax.rsqrt(jnp.mean(x * x, axis=-1, keepdims=True) + NORM_EPS) * w


def _ffn_kernel(has_out_norm, x_ref, nw_ref, wg_ref, wu_ref, wd_ref, *refs):
    out_nw_ref = refs[0] if has_out_norm else None
    o_ref, a_ref = refs[-2:]
    x = x_ref[...]
    h = _rms(x, nw_ref[...]).astype(BF16)
    for c in range(D_FF // FFN_TILE):
        cols = slice(c * FFN_TILE, (c + 1) * FFN_TILE)
        g = jnp.dot(h, wg_ref[:, cols].astype(BF16), preferred_element_type=F32)
        u = jnp.dot(h, wu_ref[:, cols].astype(BF16), preferred_element_type=F32)
        a_ref[:, cols] = (g * jax.nn.sigmoid(g) * u).astype(BF16)
    acc = None
    for lo in range(0, D_FF, FFN_DOWN_CHUNK):
        hi = min(lo + FFN_DOWN_CHUNK, D_FF)
        part = jnp.dot(a_ref[:, lo:hi], wd_ref[lo:hi, :].astype(BF16), preferred_element_type=F32)
        acc = part if acc is None else acc + part
    y = x + 0.5 * acc
    o_ref[...] = _rms(y, out_nw_ref[...]) if has_out_norm else y


def _ffn(x, nw, wg, wu, wd, layer, k, out_nw=None):
    t, d = x.shape
    tm = TOKEN_TILE
    resident = pl.Buffered(1)
    vec = pl.BlockSpec((1, d), lambda i: (0, 0))
    return pl.pallas_call(
        functools.partial(_ffn_kernel, out_nw is not None),
        grid=(t // tm,),
        in_specs=[
            pl.BlockSpec((tm, d), lambda i: (i, 0)),
            vec,
            pl.BlockSpec((None, None, d, D_FF), lambda i: (layer, k, 0, 0), pipeline_mode=resident),
            pl.BlockSpec((None, None, d, D_FF), lambda i: (layer, k, 0, 0), pipeline_mode=resident),
            pl.BlockSpec((None, None, D_FF, d), lambda i: (layer, k, 0, 0), pipeline_mode=resident),
        ] + ([vec] if out_nw is not None else []),
        out_specs=pl.BlockSpec((tm, d), lambda i: (i, 0)),
        out_shape=jax.ShapeDtypeStruct((t, d), F32),
        scratch_shapes=[pltpu.VMEM((tm, D_FF), BF16)],
        compiler_params=_compiler_params(("parallel",)),
        name="ffn",
    )(x, nw, wg, wu, wd, *([out_nw] if out_nw is not None else []))


def _norm_proj_kernel(head_dims, offsets, x_ref, nw_ref, w_ref, *o_refs):
    h = _rms(x_ref[...], nw_ref[...]).astype(BF16)
    y_all = jnp.dot(h, w_ref[...], preferred_element_type=F32)
    for hd, off, o_ref in zip(head_dims, offsets, o_refs):
        if hd is None:
            o_ref[...] = y_all[:, off:off + o_ref.shape[-1]].astype(o_ref.dtype)
        else:
            for j in range(o_ref.shape[0]):
                o_ref[j] = y_all[:, off + j * hd:off + (j + 1) * hd].astype(o_ref.dtype)


def _norm_proj(x, nw, weights, head_dims=None, dtypes=None, tm=TOKEN_TILE):
    t, d = x.shape
    n_out = len(weights)
    head_dims = tuple(head_dims) if head_dims is not None else (None,) * n_out
    dtypes = tuple(dtypes) if dtypes is not None else (F32,) * n_out
    out_specs, out_shape, offsets, padded = [], [], [], []
    total = 0
    for w, hd, dt in zip(weights, head_dims, dtypes):
        n = w.shape[1]
        if hd is None:
            out_specs.append(pl.BlockSpec((tm, n), lambda i: (i, 0)))
            out_shape.append(jax.ShapeDtypeStruct((t, n), dt))
        else:
            out_specs.append(pl.BlockSpec((n // hd, tm, hd), lambda i: (0, i, 0)))
            out_shape.append(jax.ShapeDtypeStruct((n // hd, t, hd), dt))
        offsets.append(total)
        width = -(-n // LANES) * LANES
        padded.append(jnp.pad(w, ((0, 0), (0, width - n))))
        total += width
    w_all = jnp.concatenate(padded, axis=1) if n_out > 1 else padded[0]
    return pl.pallas_call(
        functools.partial(_norm_proj_kernel, head_dims, tuple(offsets)),
        grid=(t // tm,),
        in_specs=[pl.BlockSpec((tm, d), lambda i: (i, 0)), pl.BlockSpec((1, d), lambda i: (0, 0)),
                  pl.BlockSpec(w_all.shape, lambda i: (0, 0))],
        out_specs=out_specs,
        out_shape=out_shape,
        compiler_params=_compiler_params(("parallel",)),
        name="norm_proj",
    )(x, nw, w_all)


def _mix_xattn_kernel(group_sizes, x_ref, *refs):
    n_a = sum(group_sizes)
    n_g = len(group_sizes)
    a_refs, w_refs = refs[:n_a], refs[n_a:n_a + n_g]
    nw_ref, wq_ref, kv_ref, wo_ref, o_ref = refs[n_a + n_g:]
    x = x_ref[...]
    pos = 0
    for size, w_ref in zip(group_sizes, w_refs):
        a = a_refs[pos][...]
        for r in a_refs[pos + 1:pos + size]:
            a = a.astype(F32) + r[...].astype(F32)
        pos += size
        x = x + jnp.dot(a.astype(BF16), w_ref[...], preferred_element_type=F32)
    h = _rms(x, nw_ref[...]).astype(BF16)
    q = jnp.dot(h, wq_ref[...].astype(BF16), preferred_element_type=F32).astype(BF16)
    hd = XATTN_HEAD_DIM
    outs = []
    for hh in range(XATTN_HEADS):
        k = kv_ref[:, hh * hd:(hh + 1) * hd].astype(BF16)
        v = kv_ref[:, D_MODEL + hh * hd:D_MODEL + (hh + 1) * hd].astype(BF16)
        s = lax.dot_general(q[:, hh * hd:(hh + 1) * hd], k, _NT, preferred_element_type=F32)
        s = s * (hd ** -0.5)
        p = jnp.exp(s - jnp.max(s, axis=-1, keepdims=True))
        p = p / jnp.sum(p, axis=-1, keepdims=True)
        outs.append(jnp.dot(p.astype(BF16), v, preferred_element_type=F32).astype(BF16))
    o = jnp.concatenate(outs, axis=-1)
    o_ref[...] = x + jnp.dot(o, wo_ref[...].astype(BF16), preferred_element_type=F32)


def _mix_xattn(x, groups, weights, nw, wq, kv, wo, layer, bsz, tm=2 * TOKEN_TILE):
    t, d = x.shape
    seq = t // bsz
    nt = seq // tm
    m = kv.shape[1]
    arrays = [a for grp in groups for a in grp]

    def tok(width):
        return pl.BlockSpec((tm, width), lambda b, i: (b * nt + i, 0))

    def const(shape):
        return pl.BlockSpec(shape, lambda b, i: (0, 0))

    layer_w = pl.BlockSpec((None, d, d), lambda b, i: (layer, 0, 0))
    return pl.pallas_call(
        functools.partial(_mix_xattn_kernel, tuple(len(grp) for grp in groups)),
        grid=(bsz, nt),
        in_specs=[tok(d)] + [tok(a.shape[1]) for a in arrays] + [const(w.shape) for w in weights]
        + [const((1, d)), layer_w, pl.BlockSpec((None, m, 2 * d), lambda b, i: (b, 0, 0)), layer_w],
        out_specs=tok(d),
        out_shape=jax.ShapeDtypeStruct((t, d), F32),
        compiler_params=_compiler_params(("parallel", "parallel")),
        name="mix_xattn",
    )(x, *arrays, *weights, nw, wq, kv, wo)


def _retention_kernel(q_ref, k_ref, v_ref, g_ref, cos_ref, sin_ref, dmat_ref, zeta_ref, xi_ref, decay_ref, o_ref, state_ref):
    @pl.when(pl.program_id(1) == 0)
    def _():
        state_ref[...] = jnp.zeros_like(state_ref)

    cos = cos_ref[...]
    sin = sin_ref[...]
    half = RET_DK // 2
    outs = []
    for h in range(RET_HEADS):
        qh = q_ref[:, h * RET_DK:(h + 1) * RET_DK]
        kh = k_ref[:, h * RET_DK:(h + 1) * RET_DK]
        qr = qh * cos + pltpu.roll(qh, half, 1) * sin
        kr = (kh * cos + pltpu.roll(kh, half, 1) * sin) * (RET_DK ** -0.5)
        vh = v_ref[:, h * RET_DV:(h + 1) * RET_DV].astype(BF16)
        scores = lax.dot_general(qr.astype(BF16), kr.astype(BF16), _NT, preferred_element_type=F32) * dmat_ref[h]
        inner = jnp.dot(scores.astype(BF16), vh, preferred_element_type=F32)
        state = state_ref[h]
        cross = jnp.dot((qr * xi_ref[h]).astype(BF16), state.astype(BF16), preferred_element_type=F32)
        kz = (kr * zeta_ref[h]).T.astype(BF16)
        state_ref[h] = decay_ref[h] * state + jnp.dot(kz, vh, preferred_element_type=F32)
        o = inner + cross
        o = o * lax.rsqrt(jnp.mean(o * o, axis=-1, keepdims=True) + NORM_EPS)
        gh = g_ref[:, h * RET_DV:(h + 1) * RET_DV]
        outs.append(gh * jax.nn.sigmoid(gh) * o)
    o_ref[...] = jnp.concatenate(outs, axis=-1).astype(o_ref.dtype)


def _retention_tables(seq):
    L = RET_CHUNK
    pos = jnp.arange(seq, dtype=F32)
    inv_freq = 1.0 / (RET_ROPE_BASE ** jnp.linspace(0.0, 1.0, RET_DK // 2, dtype=F32))
    ang = pos[:, None] * inv_freq[None, :]
    cos, sin = jnp.cos(ang), jnp.sin(ang)
    cos2 = jnp.concatenate([cos, cos], axis=-1)
    sin2 = jnp.concatenate([-sin, sin], axis=-1)
    log_g = jnp.log1p(-jnp.exp2(-5.0 - jnp.arange(RET_HEADS, dtype=F32)))
    idx = jnp.arange(L, dtype=F32)
    diff = idx[:, None] - idx[None, :]
    dmat = jnp.where(diff >= 0, jnp.exp(jnp.maximum(diff, 0.0)[None] * log_g[:, None, None]), 0.0)
    ones = jnp.ones((RET_HEADS, L, RET_DK), F32)
    zeta = jnp.exp((L - 1 - idx)[None, :] * log_g[:, None])[:, :, None] * ones
    xi = jnp.exp((idx + 1)[None, :] * log_g[:, None])[:, :, None] * ones
    decay = jnp.exp(L * log_g)[:, None, None] * ones
    return cos2, sin2, dmat, zeta, xi, decay


CONV_PAD = SUBLANES


def _ssd_kernel(xbc_ref, dt_ref, z_ref, convw_ref, convb_ref, dtb_ref, alog_ref, dskip_ref, nw_ref, tri_ref, o_ref, ext_ref, state_ref):
    L = SSM_CHUNK
    hp = SSM_HEAD_DIM
    ns = SSM_D_STATE

    @pl.when(pl.program_id(1) == 0)
    def _():
        ext_ref[0:CONV_PAD, :] = jnp.zeros((CONV_PAD, SSM_CONV_DIM), F32)
        state_ref[...] = jnp.zeros_like(state_ref)

    ext_ref[CONV_PAD:CONV_PAD + L, :] = xbc_ref[...]
    conv = convb_ref[...]
    for w in range(SSM_CONV):
        off = CONV_PAD - (SSM_CONV - 1) + w
        conv = conv + ext_ref[off:off + L, :] * convw_ref[w:w + 1, :]
    ext_ref[0:CONV_PAD, :] = ext_ref[L:L + CONV_PAD, :]
    xa = conv * jax.nn.sigmoid(conv)
    xs = xa[:, :SSM_D_INNER]
    bmat = xa[:, SSM_D_INNER:SSM_D_INNER + SSM_GROUPS * ns]
    cmat = xa[:, SSM_D_INNER + SSM_GROUPS * ns:]

    x = dt_ref[...] + dtb_ref[...]
    dt = jnp.maximum(x, 0.0) + jnp.log1p(jnp.exp(-jnp.abs(x)))
    lane = lax.broadcasted_iota(jnp.int32, (L, LANES), 1)
    dta = jnp.where(lane < SSM_HEADS, dt * -jnp.exp(alog_ref[...]), 0.0)
    cums = jnp.dot(tri_ref[...], dta, preferred_element_type=F32, precision=lax.Precision.HIGHEST)
    cums_t = cums.T
    last = cums[L - 1:L, :]
    decay_to_end = jnp.exp(last - cums)
    decay_from_start = jnp.exp(cums)
    chunk_decay = jnp.exp(last)
    causal = lax.broadcasted_iota(jnp.int32, (L, L), 0) >= lax.broadcasted_iota(jnp.int32, (L, L), 1)
    heads_per_group = SSM_HEADS // SSM_GROUPS
    ys = []
    for g in range(SSM_GROUPS):
        bg = bmat[:, g * ns:(g + 1) * ns].astype(BF16)
        cg = cmat[:, g * ns:(g + 1) * ns].astype(BF16)
        cb = lax.dot_general(cg, bg, _NT, preferred_element_type=F32)
        for e in range(heads_per_group):
            h = g * heads_per_group + e
            xh = xs[:, h * hp:(h + 1) * hp]
            xdt = xh * dt[:, h:h + 1]
            seg = cums[:, h:h + 1] - cums_t[h:h + 1, :]
            lmat = jnp.where(causal, jnp.exp(seg), 0.0)
            y_diag = jnp.dot((cb * lmat).astype(BF16), xdt.astype(BF16), preferred_element_type=F32)
            state = state_ref[h]
            y_off = lax.dot_general(cg, state.astype(BF16), _NT, preferred_element_type=F32) * decay_from_start[:, h:h + 1]
            xw = (xdt * decay_to_end[:, h:h + 1]).astype(BF16)
            new = lax.dot_general(xw, bg, (((0,), (0,)), ((), ())), preferred_element_type=F32)
            state_ref[h] = chunk_decay[:, h:h + 1] * state + new
            ys.append(y_diag + y_off + dskip_ref[:, h * hp:(h + 1) * hp] * xh)
    y = jnp.concatenate(ys, axis=-1)
    z = z_ref[...]
    o_ref[...] = _rms(y * (z * jax.nn.sigmoid(z)), nw_ref[...]).astype(o_ref.dtype)


N_RET_IN = 10
N_SSD_IN = 10


def _even_core_kernel(*refs):
    ret_in, ssd_in = refs[:N_RET_IN], refs[N_RET_IN:N_RET_IN + N_SSD_IN]
    o_ret_ref, o_ssd_ref, ret_state_ref, ext_ref, ssd_state_ref = refs[N_RET_IN + N_SSD_IN:]
    _retention_kernel(*ret_in, o_ret_ref, ret_state_ref)
    _ssd_kernel(*ssd_in, o_ssd_ref, ext_ref, ssd_state_ref)


def _even_core(q, k, v, g, xbc, dt_raw, z, conv_w, conv_b, dt_bias, a_log, d_skip, norm_w):
    bsz, seq, width = q.shape
    L = RET_CHUNK
    assert SSM_CHUNK == L
    tables = _retention_tables(seq)
    pad = LANES - SSM_HEADS
    params = [conv_w, conv_b.reshape(1, -1), jnp.pad(dt_bias, (0, pad)).reshape(1, LANES),
              jnp.pad(a_log, (0, pad)).reshape(1, LANES), jnp.repeat(d_skip, SSM_HEAD_DIM).reshape(1, -1),
              norm_w.reshape(1, -1), jnp.asarray(np.tril(np.ones((L, L), np.float32)))]

    def tok(w):
        return pl.BlockSpec((None, L, w), lambda b, c: (b, c, 0))

    rope = pl.BlockSpec((L, RET_DK), lambda b, c: (c, 0))
    const = pl.BlockSpec((RET_HEADS, L, RET_DK), lambda b, c: (0, 0, 0))
    return pl.pallas_call(
        _even_core_kernel,
        grid=(bsz, seq // L),
        in_specs=[tok(width)] * 4 + [rope, rope, const, const, const, const]
        + [tok(SSM_CONV_DIM), tok(LANES), tok(SSM_D_INNER)] + [pl.BlockSpec(p.shape, lambda b, c: (0, 0)) for p in params],
        out_specs=[tok(width), tok(SSM_D_INNER)],
        out_shape=[jax.ShapeDtypeStruct((bsz, seq, width), BF16), jax.ShapeDtypeStruct((bsz, seq, SSM_D_INNER), BF16)],
        scratch_shapes=[pltpu.VMEM((RET_HEADS, RET_DK, RET_DV), F32),
                        pltpu.VMEM((L + CONV_PAD, SSM_CONV_DIM), F32), pltpu.VMEM((SSM_HEADS, SSM_HEAD_DIM, SSM_D_STATE), F32)],
        compiler_params=_compiler_params(("parallel", "arbitrary")),
        name="even_core",
    )(q, k, v, g, *tables, xbc, dt_raw, z, *params)


def _even_mixer(xt, nw, w_in, w_out, conv_w, conv_b, dt_bias, a_log, d_skip, ssm_norm_w, bsz):
    t, d = xt.shape
    seq = t // bsz
    cuts = [int(c) for c in np.cumsum((0,) + EVEN_SPLITS)]
    ws = [w_in[:, a:b].astype(BF16) for a, b in zip(cuts[:-1], cuts[1:])]
    perm = np.concatenate([h * RET_DK + np.concatenate([np.arange(0, RET_DK, 2), np.arange(1, RET_DK, 2)]) for h in range(RET_HEADS)])
    ws[0] = ws[0][:, perm]
    ws[1] = ws[1][:, perm]
    ws[6] = jnp.pad(ws[6], ((0, 0), (0, LANES - SSM_HEADS)))
    dtypes = (F32, F32, BF16, F32, F32, F32, F32)
    q, k, v, g, z, xbc, dt_raw = [a.reshape(bsz, seq, -1) for a in _norm_proj(xt, nw, ws, dtypes=dtypes)]
    o_ret, y = _even_core(q, k, v, g, xbc, dt_raw, z, conv_w, conv_b, dt_bias, a_log, d_skip, ssm_norm_w)
    n_ret = RET_HEADS * RET_DV
    return [[o_ret.reshape(t, n_ret)], [y.reshape(t, SSM_D_INNER)]], [w_out[:n_ret], w_out[n_ret:]]


Q_TILE = 2 * ATTN_BLOCK
GROUP_HEADS = NSA_HEADS // NSA_KV_HEADS
HEAD_DIM = NSA_HEAD_DIM
SEL_CHUNK = 256


def _cmp_kv_kernel(xk_ref, xv_ref, pe_ref, w1_ref, w2_ref, ok_ref, ov_ref):
    for j, (x_ref, o_ref) in enumerate(((xk_ref, ok_ref), (xv_ref, ov_ref))):
        n = x_ref.shape[0] // NSA_CMP_STRIDE
        top = bot = None
        for r in range(NSA_CMP_STRIDE):
            rows = x_ref[pl.ds(r, n, stride=NSA_CMP_STRIDE), :]
            lo = r * HEAD_DIM
            hi = (NSA_CMP_STRIDE + r) * HEAD_DIM
            a = jnp.dot((rows + pe_ref[j, r:r + 1, :]).astype(BF16), w1_ref[j, lo:lo + HEAD_DIM, :], preferred_element_type=F32)
            b = jnp.dot((rows + pe_ref[j, NSA_CMP_STRIDE + r:NSA_CMP_STRIDE + r + 1, :]).astype(BF16), w1_ref[j, hi:hi + HEAD_DIM, :], preferred_element_type=F32)
            top = a if top is None else top + a
            bot = b if bot is None else bot + b
        hidden = top + pltpu.roll(bot, n - 1, 0)
        act = hidden * jax.nn.sigmoid(hidden)
        o_ref[...] = jnp.dot(act.astype(BF16), w2_ref[j], preferred_element_type=F32)


def _cmp_kv(xk, xv, pe, w1, w2):
    ng, bsz, seq, hd = xk.shape
    n = seq // NSA_CMP_STRIDE
    tok = pl.BlockSpec((None, None, seq, hd), lambda g, b: (g, b, 0, 0))
    out = pl.BlockSpec((None, None, n, hd), lambda g, b: (g, b, 0, 0))
    return pl.pallas_call(
        _cmp_kv_kernel,
        grid=(ng, bsz),
        in_specs=[tok, tok] + [pl.BlockSpec(a.shape, lambda g, b: (0, 0, 0)) for a in (pe, w1, w2)],
        out_specs=[out, out],
        out_shape=[jax.ShapeDtypeStruct((ng, bsz, n, hd), F32)] * 2,
        compiler_params=_compiler_params(("parallel", "parallel")),
        name="nsa_compress",
    )(xk, xv, pe, w1, w2)


def _heads_from_t(o_t):
    pairs = []
    for e in range(0, GROUP_HEADS, 2):
        pair = jnp.concatenate([o_t[:, e * Q_TILE:(e + 1) * Q_TILE], o_t[:, (e + 1) * Q_TILE:(e + 2) * Q_TILE]], axis=0)
        pairs.append(pair.T)
    return jnp.concatenate(pairs, axis=-1)


def _gate_rows(gate_ref, g):
    gate = jax.nn.sigmoid(gate_ref[g])
    gate = jnp.concatenate([gate, jnp.zeros((Q_TILE, LANES - gate.shape[1]), F32)], axis=1)
    return gate.T


def _gate_cols(gate_t, branch):
    return jnp.concatenate([gate_t[3 * e + branch:3 * e + branch + 1, :] for e in range(GROUP_HEADS)], axis=1)


def _tile_heads(x):
    return jnp.concatenate([x] * GROUP_HEADS, axis=1)


def _pv_t(v, p):
    return lax.dot_general(v, p.astype(BF16), (((0,), (0,)), ((), ())), preferred_element_type=F32)


ACC_ROWS = HEAD_DIM + SUBLANES


def _with_ones(v):
    pad = jnp.zeros(v.shape[:-1] + (LANES - HEAD_DIM - 1,), v.dtype)
    return jnp.concatenate([v, jnp.ones(v.shape[:-1] + (1,), v.dtype), pad], axis=-1)


def _band_window(window):
    return (-(-(window - 1) // Q_TILE) + 1) * Q_TILE


def _band_branch(i, q, k_ref, v_ref, g, window, sink=None):
    n_keys = _band_window(window)
    start = pl.multiple_of(jnp.maximum(i * Q_TILE + Q_TILE - n_keys, 0), Q_TILE)
    rel = (i * Q_TILE + lax.broadcasted_iota(jnp.int32, (n_keys, Q_TILE), 1)) - (start + lax.broadcasted_iota(jnp.int32, (n_keys, Q_TILE), 0))
    bias = _tile_heads(jnp.where(rel >= 0, jnp.where(rel < window, 0.0, MASKED), MASKED))
    s = lax.dot_general(k_ref[g, pl.ds(start, n_keys), :], q, _NT, preferred_element_type=F32) + bias
    m = jnp.max(s, axis=0, keepdims=True)
    if sink is not None:
        m = jnp.maximum(m, sink)
    p = jnp.exp2(s - m)
    denom = jnp.sum(p, axis=0, keepdims=True)
    denom = denom + jnp.exp2(sink - m) if sink is not None else jnp.maximum(denom, 1e-30)
    return _pv_t(v_ref[g, pl.ds(start, n_keys), :], p) * (1.0 / denom)


def _nsa_dense_kernel(q_ref, kc_ref, vc_ref, kw_ref, vw_ref, gate_ref, ov_ref, qs_ref, ks_ref, vs_ref, sink_ref,
                      o_ref, sel_ref, swa_ref):
    i = pl.program_id(1)
    rows = GROUP_HEADS * Q_TILE
    width = GROUP_HEADS * HEAD_DIM
    n_cmp = kc_ref.shape[1]
    n = lax.broadcasted_iota(jnp.int32, (n_cmp, Q_TILE), 0)
    t = i * Q_TILE + lax.broadcasted_iota(jnp.int32, (n_cmp, Q_TILE), 1)
    valid = _tile_heads(jnp.where(n * NSA_CMP_STRIDE + (NSA_CMP_BLOCK - 1) <= t, 1.0, 0.0))
    bias = (valid - 1.0) * (-MASKED)
    n_sel = ov_ref.shape[0]
    jb = lax.broadcasted_iota(jnp.int32, (n_sel, Q_TILE), 0)
    cur = (i * Q_TILE + lax.broadcasted_iota(jnp.int32, (n_sel, Q_TILE), 1)) // NSA_SEL_BLOCK
    valid_b = jb <= cur
    forced = jnp.where(valid_b, jnp.where(jb == 0, 1.0, 0.0) + jnp.where(jb == cur, 1.0, 0.0) + jnp.where(jb == cur - 1, 1.0, 0.0), 0.0)
    for g in range(NSA_KV_HEADS):
        q = q_ref[g * GROUP_HEADS:(g + 1) * GROUP_HEADS].reshape(rows, HEAD_DIM)
        s = lax.dot_general(kc_ref[g].astype(BF16), q, _NT, preferred_element_type=F32) + bias
        m = jnp.max(s, axis=0, keepdims=True)
        p = jnp.exp2(s - m) * valid
        p = p * (1.0 / jnp.maximum(jnp.sum(p, axis=0, keepdims=True), 1e-30))
        o_cmp = _pv_t(vc_ref[g].astype(BF16), p)
        o_win = _band_branch(i, q, kw_ref, vw_ref, g, NSA_WINDOW)
        gate_t = _gate_rows(gate_ref, g)
        o_ref[:, g * width:(g + 1) * width] = _heads_from_t(o_cmp * _gate_cols(gate_t, 0) + o_win * _gate_cols(gate_t, 2)).astype(o_ref.dtype)

        p_sum = p[:, 0:Q_TILE]
        for e in range(1, GROUP_HEADS):
            p_sum = p_sum + p[:, e * Q_TILE:(e + 1) * Q_TILE]
        imp = jnp.dot(ov_ref[...], p_sum, preferred_element_type=F32, precision=lax.Precision.HIGHEST)
        score = jnp.where(forced > 0.5, NSA_FORCE_SCORE, jnp.where(valid_b, imp, -jnp.inf))
        sub = SUBLANES
        ranks = []
        for v in range(n_sel // sub):
            blk_scores = score[v * sub:(v + 1) * sub, :]
            jb_v = v * sub + lax.broadcasted_iota(jnp.int32, (sub, Q_TILE), 0)
            rank = jnp.zeros((sub, Q_TILE), F32)
            for ii in range(n_sel):
                row = score[ii:ii + 1, :]
                if ii < v * sub:
                    ahead = jnp.where(row >= blk_scores, 1.0, 0.0)
                elif ii >= (v + 1) * sub:
                    ahead = jnp.where(row > blk_scores, 1.0, 0.0)
                else:
                    tie = jnp.where(jb_v > ii, 1.0, 0.0)
                    ahead = jnp.where(row > blk_scores, 1.0, jnp.where(row == blk_scores, tie, 0.0))
                rank = rank + ahead
            ranks.append(rank)
        rank = jnp.concatenate(ranks, axis=0)
        sel_t = jnp.where(valid_b, jnp.where(rank < NSA_TOPN, 1.0, 0.0), 0.0)
        sel_t = jnp.concatenate([sel_t, jnp.zeros((LANES - n_sel, Q_TILE), F32)], axis=0)
        sel_ref[g] = sel_t.T.astype(BF16)

    for g in range(SWA_KV_HEADS):
        q = qs_ref[g * GROUP_HEADS:(g + 1) * GROUP_HEADS].reshape(rows, HEAD_DIM)
        sink = LOG2_E * jnp.concatenate([jnp.broadcast_to(sink_ref[g][:, e:e + 1], (1, Q_TILE)) for e in range(GROUP_HEADS)], axis=1)
        o_swa = _band_branch(i, q, ks_ref, vs_ref, g, SWA_WINDOW, sink)
        swa_ref[:, g * width:(g + 1) * width] = _heads_from_t(o_swa).astype(swa_ref.dtype)


def _nsa_dense(q, k_cmp, v_cmp, kw, vw, gates, overlap_t, q_s, k_s, v_s, sinks):
    nh, bsz, seq, hd = q.shape
    ng, _, n_cmp, _ = k_cmp.shape
    heads = pl.BlockSpec((nh, None, Q_TILE, hd), lambda b, i: (0, b, i, 0))
    cmp_kv = pl.BlockSpec((ng, None, n_cmp, hd), lambda b, i: (0, b, 0, 0))
    seq_kv = pl.BlockSpec((ng, None, seq, hd), lambda b, i: (0, b, 0, 0))
    out = pl.BlockSpec((None, Q_TILE, nh * hd), lambda b, i: (b, i, 0))
    return pl.pallas_call(
        _nsa_dense_kernel,
        grid=(bsz, seq // Q_TILE),
        in_specs=[heads, cmp_kv, cmp_kv, seq_kv, seq_kv,
                  pl.BlockSpec((ng, None, Q_TILE, gates.shape[-1]), lambda b, i: (0, b, i, 0)),
                  pl.BlockSpec(overlap_t.shape, lambda b, i: (0, 0)),
                  heads, seq_kv, seq_kv,
                  pl.BlockSpec(sinks.shape, lambda b, i: (0, 0, 0))],
        out_specs=[out, pl.BlockSpec((ng, None, Q_TILE, LANES), lambda b, i: (0, b, i, 0)), out],
        out_shape=[jax.ShapeDtypeStruct((bsz, seq, nh * hd), BF16),
                   jax.ShapeDtypeStruct((ng, bsz, seq, LANES), BF16),
                   jax.ShapeDtypeStruct((bsz, seq, nh * hd), BF16)],
        compiler_params=_compiler_params(("parallel", "parallel")),
        name="nsa_dense",
    )(q, k_cmp, v_cmp, kw, vw, gates, overlap_t, q_s, k_s, v_s, sinks)


def _nsa_sel_kernel(q_ref, k_ref, v_ref, sel_ref, gate_ref, o_ref, s_ref, m_ref, acc_ref):
    i = pl.program_id(1)
    rows = GROUP_HEADS * Q_TILE
    width = GROUP_HEADS * HEAD_DIM
    n_blk = LANES - HEAD_DIM
    kc = SEL_CHUNK
    seq = k_ref.shape[1]
    groups = range(NSA_KV_HEADS)

    def scores(g, start, lhs):
        return lax.dot_general(k_ref[g, pl.ds(start, kc), :], lhs, _NT, preferred_element_type=F32)

    start_d = pl.multiple_of(jnp.maximum(i * Q_TILE + Q_TILE - kc, 0), Q_TILE)
    n_past = (start_d + kc - 1) // kc
    kpos = start_d + lax.broadcasted_iota(jnp.int32, (kc, rows), 0)
    qpos = i * Q_TILE + (lax.broadcasted_iota(jnp.int32, (kc, rows), 1) & (Q_TILE - 1))
    blk = lax.broadcasted_iota(jnp.int32, (Q_TILE, n_blk), 1)
    lhs_past, lhs_none = [], []
    for g in groups:
        q = q_ref[g * GROUP_HEADS:(g + 1) * GROUP_HEADS].reshape(rows, HEAD_DIM)
        sel = sel_ref[g][:, :n_blk].astype(F32)
        neg_diag = (sel - 1.0) * (-MASKED)
        neg_past = jnp.where(blk < start_d // NSA_SEL_BLOCK, neg_diag, MASKED)
        lhs_diag = jnp.concatenate([q, jnp.concatenate([neg_diag.astype(BF16)] * GROUP_HEADS, axis=0)], axis=-1)
        lhs_past.append(jnp.concatenate([q, jnp.concatenate([neg_past.astype(BF16)] * GROUP_HEADS, axis=0)], axis=-1))
        lhs_none.append(jnp.concatenate([q, jnp.full((rows, n_blk), MASKED, BF16)], axis=-1))
        s_ref[g, 0] = jnp.where(kpos <= qpos, scores(g, start_d, lhs_diag), MASKED)

        m_ref[g] = jnp.full((1, rows), MASKED, F32)
        acc_ref[g] = jnp.zeros((ACC_ROWS, rows), F32)

    def step(c, cur, nxt, lookahead=True):
        next_start = pl.multiple_of(jnp.minimum(c * kc, seq - kc), kc)
        v_start = pl.multiple_of(jnp.where(c == 0, start_d, (c - 1) * kc), Q_TILE)
        for g in groups:
            if lookahead:
                s_ref[g, nxt] = scores(g, next_start, jnp.where(c < n_past, lhs_past[g], lhs_none[g]))
            s = s_ref[g, cur]
            m = m_ref[g]
            m_new = jnp.maximum(m, jnp.max(s, axis=0, keepdims=True))
            alpha = jnp.exp2(m - m_new)
            p = jnp.exp2(s - m_new)
            m_ref[g] = m_new
            acc_ref[g] = alpha * acc_ref[g] + _pv_t(v_ref[g, pl.ds(v_start, kc), :], p)[:ACC_ROWS]

    def body(j, carry):
        step(2 * j, 0, 1)
        step(2 * j + 1, 1, 0)
        return carry

    lax.fori_loop(0, (n_past + 1) // 2, body, 0)

    @pl.when(n_past % 2 == 0)
    def _():
        step(n_past, 0, 1, lookahead=False)

    for g in groups:
        gate = _gate_cols(_gate_rows(gate_ref, g), 1)
        acc = acc_ref[g]
        o_ref[:, g * width:(g + 1) * width] = _heads_from_t(acc[:HEAD_DIM] * (gate / acc[HEAD_DIM:HEAD_DIM + 1])).astype(o_ref.dtype)


def _nsa_sel(q, k, v, sel, gates):
    nh, bsz, seq, hd = q.shape
    ng = k.shape[0]
    onehot = (np.arange(seq)[:, None] // NSA_SEL_BLOCK == np.arange(LANES - hd)[None, :]).astype(np.float32)
    k_ext = jnp.concatenate([k, jnp.broadcast_to(jnp.asarray(onehot, BF16), (ng, bsz, seq, LANES - hd))], axis=-1)
    return pl.pallas_call(
        _nsa_sel_kernel,
        grid=(bsz, seq // Q_TILE),
        in_specs=[pl.BlockSpec((nh, None, Q_TILE, hd), lambda b, i: (0, b, i, 0)),
                  pl.BlockSpec((ng, None, seq, LANES), lambda b, i: (0, b, 0, 0)),
                  pl.BlockSpec((ng, None, seq, LANES), lambda b, i: (0, b, 0, 0)),
                  pl.BlockSpec((ng, None, Q_TILE, LANES), lambda b, i: (0, b, i, 0)),
                  pl.BlockSpec((ng, None, Q_TILE, gates.shape[-1]), lambda b, i: (0, b, i, 0))],
        out_specs=pl.BlockSpec((None, Q_TILE, nh * hd), lambda b, i: (b, i, 0)),
        out_shape=jax.ShapeDtypeStruct((bsz, seq, nh * hd), BF16),
        scratch_shapes=[pltpu.VMEM((ng, 2, SEL_CHUNK, GROUP_HEADS * Q_TILE), F32),
                        pltpu.VMEM((ng, 1, GROUP_HEADS * Q_TILE), F32),
                        pltpu.VMEM((ng, ACC_ROWS, GROUP_HEADS * Q_TILE), F32)],
        compiler_params=_compiler_params(("parallel", "parallel")),
        name="nsa_sel",
    )(q, k_ext, _with_ones(v), sel, gates)


def _overlap_t(seq):
    n_cmp = (seq - NSA_CMP_BLOCK) // NSA_CMP_STRIDE + 1
    n_pad = seq // NSA_CMP_STRIDE
    starts = np.arange(n_pad) * NSA_CMP_STRIDE
    sel_start = np.arange(seq // NSA_SEL_BLOCK) * NSA_SEL_BLOCK
    ov = (starts[None, :] < sel_start[:, None] + NSA_SEL_BLOCK) & (starts[None, :] + NSA_CMP_BLOCK > sel_start[:, None])
    ov = ov & (np.arange(n_pad)[None, :] < n_cmp)
    return jnp.asarray(ov.astype(np.float32))


def _odd_mixer(xt, nw, w_in, w_out, cmp_pe, cmp_w1, cmp_w2, sinks, bsz):
    t, d = xt.shape
    seq = t // bsz
    assert seq // NSA_SEL_BLOCK <= LANES - HEAD_DIM and seq % SEL_CHUNK == 0 and seq % Q_TILE == 0
    assert _band_window(NSA_WINDOW) <= seq and SEL_CHUNK % Q_TILE == 0
    cuts = [int(c) for c in np.cumsum((0,) + ODD_SPLITS)]
    ws = [w_in[:, a:b] for a, b in zip(cuts[:-1], cuts[1:])]
    ws[0] = ws[0] * (HEAD_DIM ** -0.5 * LOG2_E)
    ws[8] = ws[8] * (HEAD_DIM ** -0.5 * LOG2_E)
    ws = [w.astype(BF16) for w in ws]
    hd = HEAD_DIM
    head_dims = (hd, hd, hd, hd, hd, hd, hd, 3 * GROUP_HEADS, hd, hd, hd)
    dtypes = (BF16, F32, F32, BF16, BF16, BF16, BF16, F32, BF16, BF16, BF16)
    q_n, kc, vc, ks, vs, kw, vw, gates, q_s, k_s, v_s = _norm_proj(xt, nw, ws, head_dims, dtypes)

    def heads(a):
        return a.reshape(a.shape[0], bsz, seq, a.shape[-1])

    q_n, kc, vc, ks, vs, kw, vw, gates, q_s, k_s, v_s = map(heads, (q_n, kc, vc, ks, vs, kw, vw, gates, q_s, k_s, v_s))
    k_cmp, v_cmp = _cmp_kv(kc, vc, cmp_pe, cmp_w1.astype(BF16), cmp_w2.astype(BF16))
    o_cw, sel, o_swa = _nsa_dense(q_n, k_cmp, v_cmp, kw, vw, gates, _overlap_t(seq), q_s, k_s, v_s,
                                  sinks.reshape(SWA_KV_HEADS, 1, GROUP_HEADS))
    o_sel = _nsa_sel(q_n, ks, vs, sel, gates)
    n_nsa = NSA_HEADS * hd
    return [[o_cw.reshape(t, n_nsa), o_sel.reshape(t, n_nsa)], [o_swa.reshape(t, SWA_HEADS * hd)]], [w_out[:n_nsa], w_out[n_nsa:]]


def kernel(x, mem, norm_w, final_norm_w, mem_norm_w, ffn_w_gate, ffn_w_up, ffn_w_down, xattn_wq, xattn_wkv, xattn_wo, even_w_in, even_w_out, ssm_conv_w, ssm_conv_b, ssm_dt_bias, ssm_a_log, ssm_d, ssm_norm_w, odd_w_in, odd_w_out, nsa_cmp_pe, nsa_cmp_w1, nsa_cmp_w2, swa_sinks):
    bsz, seq, d = x.shape
    t = bsz * seq
    m = mem.shape[1]
    wg, wu, wd = ffn_w_gate, ffn_w_up, ffn_w_down
    wkv = xattn_wkv.astype(BF16)
    even_out = even_w_out.astype(BF16)
    odd_out = odd_w_out.astype(BF16)

    xt = x.reshape(t, d)
    mem2 = mem.reshape(bsz * m, d)
    mem_nw = mem_norm_w.reshape(1, d)
    for layer in range(DEPTH):
        i = layer // 2
        xt = _ffn(xt, norm_w[layer, 0].reshape(1, d), wg, wu, wd, layer, 0)
        nw1 = norm_w[layer, 1].reshape(1, d)
        if layer % 2 == 0:
            mix, mix_w = _even_mixer(xt, nw1, even_w_in[i], even_out[i], ssm_conv_w[i], ssm_conv_b[i], ssm_dt_bias[i], ssm_a_log[i], ssm_d[i], ssm_norm_w[i], bsz)
        else:
            mix, mix_w = _odd_mixer(xt, nw1, odd_w_in[i], odd_out[i], nsa_cmp_pe[i], nsa_cmp_w1[i], nsa_cmp_w2[i], swa_sinks[i], bsz)
        (kv,) = _norm_proj(mem2, mem_nw, [wkv[layer]])
        xt = _mix_xattn(xt, mix, mix_w, norm_w[layer, 2].reshape(1, d), xattn_wq, kv.reshape(bsz, m, 2 * d), xattn_wo, layer, bsz)
        out_nw = final_norm_w.reshape(1, d) if layer == DEPTH - 1 else None
        xt = _ffn(xt, norm_w[layer, 3].reshape(1, d), wg, wu, wd, layer, 1, out_nw)
    return xt.reshape(bsz, seq, d)
```

```python
import functools

import numpy as np
import jax
import jax.numpy as jnp
from jax import lax
from jax.experimental import pallas as pl
from jax.experimental.pallas import tpu as pltpu

F32 = jnp.float32
BF16 = jnp.bfloat16

D_MODEL = 1024
DEPTH = 4
D_FF = 2816
NORM_EPS = 1e-6
ATTN_BLOCK = 128

RET_HEADS = 4
RET_DK = 128
RET_DV = 128
RET_CHUNK = 128
RET_ROPE_BASE = 10000.0

SSM_HEADS = 8
SSM_HEAD_DIM = 64
SSM_D_INNER = SSM_HEADS * SSM_HEAD_DIM
SSM_D_STATE = 64
SSM_GROUPS = 2
SSM_CONV = 4
SSM_CHUNK = 128
SSM_CONV_DIM = SSM_D_INNER + 2 * SSM_GROUPS * SSM_D_STATE

EVEN_SPLITS = (RET_HEADS * RET_DK, RET_HEADS * RET_DK, RET_HEADS * RET_DV, RET_HEADS * RET_DV, SSM_D_INNER, SSM_CONV_DIM, SSM_HEADS)

NSA_HEADS = 8
NSA_KV_HEADS = 2
NSA_HEAD_DIM = 64
NSA_CMP_BLOCK = 32
NSA_CMP_STRIDE = 16
NSA_CMP_HIDDEN = 256
NSA_SEL_BLOCK = 64
NSA_TOPN = 16
NSA_WINDOW = 512
NSA_FORCE_SCORE = 1e6

SWA_HEADS = 8
SWA_KV_HEADS = 2
SWA_HEAD_DIM = 64
SWA_WINDOW = 128

ODD_SPLITS = (NSA_HEADS * NSA_HEAD_DIM,) + (NSA_KV_HEADS * NSA_HEAD_DIM,) * 6 + (3 * NSA_HEADS, SWA_HEADS * SWA_HEAD_DIM, SWA_KV_HEADS * SWA_HEAD_DIM, SWA_KV_HEADS * SWA_HEAD_DIM)

XATTN_HEADS = 4
XATTN_HEAD_DIM = D_MODEL // XATTN_HEADS

VMEM_LIMIT_BYTES = 52 * 1024 * 1024
TOKEN_TILE = 512
FFN_TILE = 256
FFN_DOWN_CHUNK = 1024
LANES = 128
SUBLANES = 8
MASKED = -1e30
LOG2_E = float(np.log2(np.e))

_NT = (((1,), (1,)), ((), ()))


def _compiler_params(semantics):
    return pltpu.CompilerParams(dimension_semantics=semantics, vmem_limit_bytes=VMEM_LIMIT_BYTES)


def _rms(x, w):
    return x * lax.rsqrt(jnp.mean(x * x, axis=-1, keepdims=True) + NORM_EPS) * w


def _ffn_kernel(has_out_norm, x_ref, nw_ref, wg_ref, wu_ref, wd_ref, *refs):
    out_nw_ref = refs[0] if has_out_norm else None
    o_ref, a_ref = refs[-2:]
    x = x_ref[...]
    h = _rms(x, nw_ref[...]).astype(BF16)
    for c in range(D_FF // FFN_TILE):
        cols = slice(c * FFN_TILE, (c + 1) * FFN_TILE)
        g = jnp.dot(h, wg_ref[:, cols].astype(BF16), preferred_element_type=F32)
        u = jnp.dot(h, wu_ref[:, cols].astype(BF16), preferred_element_type=F32)
        a_ref[:, cols] = (g * jax.nn.sigmoid(g) * u).astype(BF16)
    acc = None
    for lo in range(0, D_FF, FFN_DOWN_CHUNK):
        hi = min(lo + FFN_DOWN_CHUNK, D_FF)
        part = jnp.dot(a_ref[:, lo:hi], wd_ref[lo:hi, :].astype(BF16), preferred_element_type=F32)
        acc = part if acc is None else acc + part
    y = x + 0.5 * acc
    o_ref[...] = _rms(y, out_nw_ref[...]) if has_out_norm else y


def _ffn(x, nw, wg, wu, wd, layer, k, out_nw=None):
    t, d = x.shape
    tm = TOKEN_TILE
    resident = pl.Buffered(1)
    vec = pl.BlockSpec((1, d), lambda i: (0, 0))
    return pl.pallas_call(
        functools.partial(_ffn_kernel, out_nw is not None),
        grid=(t // tm,),
        in_specs=[
            pl.BlockSpec((tm, d), lambda i: (i, 0)),
            vec,
            pl.BlockSpec((None, None, d, D_FF), lambda i: (layer, k, 0, 0), pipeline_mode=resident),
            pl.BlockSpec((None, None, d, D_FF), lambda i: (layer, k, 0, 0), pipeline_mode=resident),
            pl.BlockSpec((None, None, D_FF, d), lambda i: (layer, k, 0, 0), pipeline_mode=resident),
        ] + ([vec] if out_nw is not None else []),
        out_specs=pl.BlockSpec((tm, d), lambda i: (i, 0)),
        out_shape=jax.ShapeDtypeStruct((t, d), F32),
        scratch_shapes=[pltpu.VMEM((tm, D_FF), BF16)],
        compiler_params=_compiler_params(("parallel",)),
        name="ffn",
    )(x, nw, wg, wu, wd, *([out_nw] if out_nw is not None else []))


def _norm_proj_kernel(head_dims, offsets, x_ref, nw_ref, w_ref, *o_refs):
    h = _rms(x_ref[...], nw_ref[...]).astype(BF16)
    y_all = jnp.dot(h, w_ref[...], preferred_element_type=F32)
    for hd, off, o_ref in zip(head_dims, offsets, o_refs):
        if hd is None:
            o_ref[...] = y_all[:, off:off + o_ref.shape[-1]].astype(o_ref.dtype)
        else:
            for j in range(o_ref.shape[0]):
                o_ref[j] = y_all[:, off + j * hd:off + (j + 1) * hd].astype(o_ref.dtype)


def _norm_proj(x, nw, weights, head_dims=None, dtypes=None, tm=TOKEN_TILE):
    t, d = x.shape
    n_out = len(weights)
    head_dims = tuple(head_dims) if head_dims is not None else (None,) * n_out
    dtypes = tuple(dtypes) if dtypes is not None else (F32,) * n_out
    out_specs, out_shape, offsets, padded = [], [], [], []
    total = 0
    for w, hd, dt in zip(weights, head_dims, dtypes):
        n = w.shape[1]
        if hd is None:
            out_specs.append(pl.BlockSpec((tm, n), lambda i: (i, 0)))
            out_shape.append(jax.ShapeDtypeStruct((t, n), dt))
        else:
            out_specs.append(pl.BlockSpec((n // hd, tm, hd), lambda i: (0, i, 0)))
            out_shape.append(jax.ShapeDtypeStruct((n // hd, t, hd), dt))
        offsets.append(total)
        width = -(-n // LANES) * LANES
        padded.append(jnp.pad(w, ((0, 0), (0, width - n))))
        total += width
    w_all = jnp.concatenate(padded, axis=1) if n_out > 1 else padded[0]
    return pl.pallas_call(
        functools.partial(_norm_proj_kernel, head_dims, tuple(offsets)),
        grid=(t // tm,),
        in_specs=[pl.BlockSpec((tm, d), lambda i: (i, 0)), pl.BlockSpec((1, d), lambda i: (0, 0)),
                  pl.BlockSpec(w_all.shape, lambda i: (0, 0))],
        out_specs=out_specs,
        out_shape=out_shape,
        compiler_params=_compiler_params(("parallel",)),
        name="norm_proj",
    )(x, nw, w_all)


def _mix_xattn_kernel(group_sizes, x_ref, *refs):
    n_a = sum(group_sizes)
    n_g = len(group_sizes)
    a_refs, w_refs = refs[:n_a], refs[n_a:n_a + n_g]
    nw_ref, wq_ref, kv_ref, wo_ref, o_ref = refs[n_a + n_g:]
    x = x_ref[...]
    pos = 0
    for size, w_ref in zip(group_sizes, w_refs):
        a = a_refs[pos][...]
        for r in a_refs[pos + 1:pos + size]:
            a = a.astype(F32) + r[...].astype(F32)
        pos += size
        x = x + jnp.dot(a.astype(BF16), w_ref[...], preferred_element_type=F32)
    h = _rms(x, nw_ref[...]).astype(BF16)
    q = jnp.dot(h, wq_ref[...].astype(BF16), preferred_element_type=F32).astype(BF16)
    hd = XATTN_HEAD_DIM
    outs = []
    for hh in range(XATTN_HEADS):
        k = kv_ref[:, hh * hd:(hh + 1) * hd].astype(BF16)
        v = kv_ref[:, D_MODEL + hh * hd:D_MODEL + (hh + 1) * hd].astype(BF16)
        s = lax.dot_general(q[:, hh * hd:(hh + 1) * hd], k, _NT, preferred_element_type=F32)
        s = s * (hd ** -0.5)
        p = jnp.exp(s - jnp.max(s, axis=-1, keepdims=True))
        p = p / jnp.sum(p, axis=-1, keepdims=True)
        outs.append(jnp.dot(p.astype(BF16), v, preferred_element_type=F32).astype(BF16))
    o = jnp.concatenate(outs, axis=-1)
    o_ref[...] = x + jnp.dot(o, wo_ref[...].astype(BF16), preferred_element_type=F32)


def _mix_xattn(x, groups, weights, nw, wq, kv, wo, layer, bsz, tm=2 * TOKEN_TILE):
    t, d = x.shape
    seq = t // bsz
    nt = seq // tm
    m = kv.shape[1]
    arrays = [a for grp in groups for a in grp]

    def tok(width):
        return pl.BlockSpec((tm, width), lambda b, i: (b * nt + i, 0))

    def const(shape):
        return pl.BlockSpec(shape, lambda b, i: (0, 0))

    layer_w = pl.BlockSpec((None, d, d), lambda b, i: (layer, 0, 0))
    return pl.pallas_call(
        functools.partial(_mix_xattn_kernel, tuple(len(grp) for grp in groups)),
        grid=(bsz, nt),
        in_specs=[tok(d)] + [tok(a.shape[1]) for a in arrays] + [const(w.shape) for w in weights]
        + [const((1, d)), layer_w, pl.BlockSpec((None, m, 2 * d), lambda b, i: (b, 0, 0)), layer_w],
        out_specs=tok(d),
        out_shape=jax.ShapeDtypeStruct((t, d), F32),
        compiler_params=_compiler_params(("parallel", "parallel")),
        name="mix_xattn",
    )(x, *arrays, *weights, nw, wq, kv, wo)


def _retention_kernel(q_ref, k_ref, v_ref, g_ref, cos_ref, sin_ref, dmat_ref, zeta_ref, xi_ref, decay_ref, o_ref, state_ref):
    @pl.when(pl.program_id(1) == 0)
    def _():
        state_ref[...] = jnp.zeros_like(state_ref)

    cos = cos_ref[...]
    sin = sin_ref[...]
    half = RET_DK // 2
    outs = []
    for h in range(RET_HEADS):
        qh = q_ref[:, h * RET_DK:(h + 1) * RET_DK]
        kh = k_ref[:, h * RET_DK:(h + 1) * RET_DK]
        qr = qh * cos + pltpu.roll(qh, half, 1) * sin
        kr = (kh * cos + pltpu.roll(kh, half, 1) * sin) * (RET_DK ** -0.5)
        vh = v_ref[:, h * RET_DV:(h + 1) * RET_DV].astype(BF16)
        scores = lax.dot_general(qr.astype(BF16), kr.astype(BF16), _NT, preferred_element_type=F32) * dmat_ref[h]
        inner = jnp.dot(scores.astype(BF16), vh, preferred_element_type=F32)
        state = state_ref[h]
        cross = jnp.dot((qr * xi_ref[h]).astype(BF16), state.astype(BF16), preferred_element_type=F32)
        kz = (kr * zeta_ref[h]).T.astype(BF16)
        state_ref[h] = decay_ref[h] * state + jnp.dot(kz, vh, preferred_element_type=F32)
        o = inner + cross
        o = o * lax.rsqrt(jnp.mean(o * o, axis=-1, keepdims=True) + NORM_EPS)
        gh = g_ref[:, h * RET_DV:(h + 1) * RET_DV]
        outs.append(gh * jax.nn.sigmoid(gh) * o)
    o_ref[...] = jnp.concatenate(outs, axis=-1).astype(o_ref.dtype)


def _retention_tables(seq):
    L = RET_CHUNK
    pos = jnp.arange(seq, dtype=F32)
    inv_freq = 1.0 / (RET_ROPE_BASE ** jnp.linspace(0.0, 1.0, RET_DK // 2, dtype=F32))
    ang = pos[:, None] * inv_freq[None, :]
    cos, sin = jnp.cos(ang), jnp.sin(ang)
    cos2 = jnp.concatenate([cos, cos], axis=-1)
    sin2 = jnp.concatenate([-sin, sin], axis=-1)
    log_g = jnp.log1p(-jnp.exp2(-5.0 - jnp.arange(RET_HEADS, dtype=F32)))
    idx = jnp.arange(L, dtype=F32)
    diff = idx[:, None] - idx[None, :]
    dmat = jnp.where(diff >= 0, jnp.exp(jnp.maximum(diff, 0.0)[None] * log_g[:, None, None]), 0.0)
    ones = jnp.ones((RET_HEADS, L, RET_DK), F32)
    zeta = jnp.exp((L - 1 - idx)[None, :] * log_g[:, None])[:, :, None] * ones
    xi = jnp.exp((idx + 1)[None, :] * log_g[:, None])[:, :, None] * ones
    decay = jnp.exp(L * log_g)[:, None, None] * ones
    return cos2, sin2, dmat, zeta, xi, decay


CONV_PAD = SUBLANES


def _ssd_kernel(xbc_ref, dt_ref, z_ref, convw_ref, convb_ref, dtb_ref, alog_ref, dskip_ref, nw_ref, tri_ref, o_ref, ext_ref, state_ref):
    L = SSM_CHUNK
    hp = SSM_HEAD_DIM
    ns = SSM_D_STATE

    @pl.when(pl.program_id(1) == 0)
    def _():
        ext_ref[0:CONV_PAD, :] = jnp.zeros((CONV_PAD, SSM_CONV_DIM), F32)
        state_ref[...] = jnp.zeros_like(state_ref)

    ext_ref[CONV_PAD:CONV_PAD + L, :] = xbc_ref[...]
    conv = convb_ref[...]
    for w in range(SSM_CONV):
        off = CONV_PAD - (SSM_CONV - 1) + w
        conv = conv + ext_ref[off:off + L, :] * convw_ref[w:w + 1, :]
    ext_ref[0:CONV_PAD, :] = ext_ref[L:L + CONV_PAD, :]
    xa = conv * jax.nn.sigmoid(conv)
    xs = xa[:, :SSM_D_INNER]
    bmat = xa[:, SSM_D_INNER:SSM_D_INNER + SSM_GROUPS * ns]
    cmat = xa[:, SSM_D_INNER + SSM_GROUPS * ns:]

    x = dt_ref[...] + dtb_ref[...]
    dt = jnp.maximum(x, 0.0) + jnp.log1p(jnp.exp(-jnp.abs(x)))
    lane = lax.broadcasted_iota(jnp.int32, (L, LANES), 1)
    dta = jnp.where(lane < SSM_HEADS, dt * -jnp.exp(alog_ref[...]), 0.0)
    cums = jnp.dot(tri_ref[...], dta, preferred_element_type=F32, precision=lax.Precision.HIGHEST)
    cums_t = cums.T
    last = cums[L - 1:L, :]
    decay_to_end = jnp.exp(last - cums)
    decay_from_start = jnp.exp(cums)
    chunk_decay = jnp.exp(last)
    causal = lax.broadcasted_iota(jnp.int32, (L, L), 0) >= lax.broadcasted_iota(jnp.int32, (L, L), 1)
    heads_per_group = SSM_HEADS // SSM_GROUPS
    ys = []
    for g in range(SSM_GROUPS):
        bg = bmat[:, g * ns:(g + 1) * ns].astype(BF16)
        cg = cmat[:, g * ns:(g + 1) * ns].astype(BF16)
        cb = lax.dot_general(cg, bg, _NT, preferred_element_type=F32)
        for e in range(heads_per_group):
            h = g * heads_per_group + e
            xh = xs[:, h * hp:(h + 1) * hp]
            xdt = xh * dt[:, h:h + 1]
            seg = cums[:, h:h + 1] - cums_t[h:h + 1, :]
            lmat = jnp.where(causal, jnp.exp(seg), 0.0)
            y_diag = jnp.dot((cb * lmat).astype(BF16), xdt.astype(BF16), preferred_element_type=F32)
            state = state_ref[h]
            y_off = lax.dot_general(cg, state.astype(BF16), _NT, preferred_element_type=F32) * decay_from_start[:, h:h + 1]
            xw = (xdt * decay_to_end[:, h:h + 1]).astype(BF16)
            new = lax.dot_general(xw, bg, (((0,), (0,)), ((), ())), preferred_element_type=F32)
            state_ref[h] = chunk_decay[:, h:h + 1] * state + new
            ys.append(y_diag + y_off + dskip_ref[:, h * hp:(h + 1) * hp] * xh)
    y = jnp.concatenate(ys, axis=-1)
    z = z_ref[...]
    o_ref[...] = _rms(y * (z * jax.nn.sigmoid(z)), nw_ref[...]).astype(o_ref.dtype)


N_RET_IN = 10
N_SSD_IN = 10


def _even_core_kernel(*refs):
    ret_in, ssd_in = refs[:N_RET_IN], refs[N_RET_IN:N_RET_IN + N_SSD_IN]
    o_ret_ref, o_ssd_ref, ret_state_ref, ext_ref, ssd_state_ref = refs[N_RET_IN + N_SSD_IN:]
    _retention_kernel(*ret_in, o_ret_ref, ret_state_ref)
    _ssd_kernel(*ssd_in, o_ssd_ref, ext_ref, ssd_state_ref)


def _even_core(q, k, v, g, xbc, dt_raw, z, conv_w, conv_b, dt_bias, a_log, d_skip, norm_w):
    bsz, seq, width = q.shape
    L = RET_CHUNK
    assert SSM_CHUNK == L
    tables = _retention_tables(seq)
    pad = LANES - SSM_HEADS
    params = [conv_w, conv_b.reshape(1, -1), jnp.pad(dt_bias, (0, pad)).reshape(1, LANES),
              jnp.pad(a_log, (0, pad)).reshape(1, LANES), jnp.repeat(d_skip, SSM_HEAD_DIM).reshape(1, -1),
              norm_w.reshape(1, -1), jnp.asarray(np.tril(np.ones((L, L), np.float32)))]

    def tok(w):
        return pl.BlockSpec((None, L, w), lambda b, c: (b, c, 0))

    rope = pl.BlockSpec((L, RET_DK), lambda b, c: (c, 0))
    const = pl.BlockSpec((RET_HEADS, L, RET_DK), lambda b, c: (0, 0, 0))
    return pl.pallas_call(
        _even_core_kernel,
        grid=(bsz, seq // L),
        in_specs=[tok(width)] * 4 + [rope, rope, const, const, const, const]
        + [tok(SSM_CONV_DIM), tok(LANES), tok(SSM_D_INNER)] + [pl.BlockSpec(p.shape, lambda b, c: (0, 0)) for p in params],
        out_specs=[tok(width), tok(SSM_D_INNER)],
        out_shape=[jax.ShapeDtypeStruct((bsz, seq, width), BF16), jax.ShapeDtypeStruct((bsz, seq, SSM_D_INNER), BF16)],
        scratch_shapes=[pltpu.VMEM((RET_HEADS, RET_DK, RET_DV), F32),
                        pltpu.VMEM((L + CONV_PAD, SSM_CONV_DIM), F32), pltpu.VMEM((SSM_HEADS, SSM_HEAD_DIM, SSM_D_STATE), F32)],
        compiler_params=_compiler_params(("parallel", "arbitrary")),
        name="even_core",
    )(q, k, v, g, *tables, xbc, dt_raw, z, *params)


def _even_mixer(xt, nw, w_in, w_out, conv_w, conv_b, dt_bias, a_log, d_skip, ssm_norm_w, bsz):
    t, d = xt.shape
    seq = t // bsz
    cuts = [int(c) for c in np.cumsum((0,) + EVEN_SPLITS)]
    ws = [w_in[:, a:b].astype(BF16) for a, b in zip(cuts[:-1], cuts[1:])]
    perm = np.concatenate([h * RET_DK + np.concatenate([np.arange(0, RET_DK, 2), np.arange(1, RET_DK, 2)]) for h in range(RET_HEADS)])
    ws[0] = ws[0][:, perm]
    ws[1] = ws[1][:, perm]
    ws[6] = jnp.pad(ws[6], ((0, 0), (0, LANES - SSM_HEADS)))
    dtypes = (F32, F32, BF16, F32, F32, F32, F32)
    q, k, v, g, z, xbc, dt_raw = [a.reshape(bsz, seq, -1) for a in _norm_proj(xt, nw, ws, dtypes=dtypes)]
    o_ret, y = _even_core(q, k, v, g, xbc, dt_raw, z, conv_w, conv_b, dt_bias, a_log, d_skip, ssm_norm_w)
    n_ret = RET_HEADS * RET_DV
    return [[o_ret.reshape(t, n_ret)], [y.reshape(t, SSM_D_INNER)]], [w_out[:n_ret], w_out[n_ret:]]


Q_TILE = 2 * ATTN_BLOCK
GROUP_HEADS = NSA_HEADS // NSA_KV_HEADS
HEAD_DIM = NSA_HEAD_DIM
SEL_CHUNK = 256


def _cmp_kv_kernel(xk_ref, xv_ref, pe_ref, w1_ref, w2_ref, ok_ref, ov_ref):
    for j, (x_ref, o_ref) in enumerate(((xk_ref, ok_ref), (xv_ref, ov_ref))):
        n = x_ref.shape[0] // NSA_CMP_STRIDE
        top = bot = None
        for r in range(NSA_CMP_STRIDE):
            rows = x_ref[pl.ds(r, n, stride=NSA_CMP_STRIDE), :]
            lo = r * HEAD_DIM
            hi = (NSA_CMP_STRIDE + r) * HEAD_DIM
            a = jnp.dot((rows + pe_ref[j, r:r + 1, :]).astype(BF16), w1_ref[j, lo:lo + HEAD_DIM, :], preferred_element_type=F32)
            b = jnp.dot((rows + pe_ref[j, NSA_CMP_STRIDE + r:NSA_CMP_STRIDE + r + 1, :]).astype(BF16), w1_ref[j, hi:hi + HEAD_DIM, :], preferred_element_type=F32)
            top = a if top is None else top + a
            bot = b if bot is None else bot + b
        hidden = top + pltpu.roll(bot, n - 1, 0)
        act = hidden * jax.nn.sigmoid(hidden)
        o_ref[...] = jnp.dot(act.astype(BF16), w2_ref[j], preferred_element_type=F32)


def _cmp_kv(xk, xv, pe, w1, w2):
    ng, bsz, seq, hd = xk.shape
    n = seq // NSA_CMP_STRIDE
    tok = pl.BlockSpec((None, None, seq, hd), lambda g, b: (g, b, 0, 0))
    out = pl.BlockSpec((None, None, n, hd), lambda g, b: (g, b, 0, 0))
    return pl.pallas_call(
        _cmp_kv_kernel,
        grid=(ng, bsz),
        in_specs=[tok, tok] + [pl.BlockSpec(a.shape, lambda g, b: (0, 0, 0)) for a in (pe, w1, w2)],
        out_specs=[out, out],
        out_shape=[jax.ShapeDtypeStruct((ng, bsz, n, hd), F32)] * 2,
        compiler_params=_compiler_params(("parallel", "parallel")),
        name="nsa_compress",
    )(xk, xv, pe, w1, w2)


def _heads_from_t(o_t):
    pairs = []
    for e in range(0, GROUP_HEADS, 2):
        pair = jnp.concatenate([o_t[:, e * Q_TILE:(e + 1) * Q_TILE], o_t[:, (e + 1) * Q_TILE:(e + 2) * Q_TILE]], axis=0)
        pairs.append(pair.T)
    return jnp.concatenate(pairs, axis=-1)


def _gate_rows(gate_ref, g):
    gate = jax.nn.sigmoid(gate_ref[g])
    gate = jnp.concatenate([gate, jnp.zeros((Q_TILE, LANES - gate.shape[1]), F32)], axis=1)
    return gate.T


def _gate_cols(gate_t, branch):
    return jnp.concatenate([gate_t[3 * e + branch:3 * e + branch + 1, :] for e in range(GROUP_HEADS)], axis=1)


def _tile_heads(x):
    return jnp.concatenate([x] * GROUP_HEADS, axis=1)


def _pv_t(v, p):
    return lax.dot_general(v, p.astype(BF16), (((0,), (0,)), ((), ())), preferred_element_type=F32)


ACC_ROWS = HEAD_DIM + SUBLANES


def _with_ones(v):
    pad = jnp.zeros(v.shape[:-1] + (LANES - HEAD_DIM - 1,), v.dtype)
    return jnp.concatenate([v, jnp.ones(v.shape[:-1] + (1,), v.dtype), pad], axis=-1)


def _band_window(window):
    return (-(-(window - 1) // Q_TILE) + 1) * Q_TILE


def _band_branch(i, q, k_ref, v_ref, g, window, sink=None):
    n_keys = _band_window(window)
    start = pl.multiple_of(jnp.maximum(i * Q_TILE + Q_TILE - n_keys, 0), Q_TILE)
    rel = (i * Q_TILE + lax.broadcasted_iota(jnp.int32, (n_keys, Q_TILE), 1)) - (start + lax.broadcasted_iota(jnp.int32, (n_keys, Q_TILE), 0))
    bias = _tile_heads(jnp.where(rel >= 0, jnp.where(rel < window, 0.0, MASKED), MASKED))
    s = lax.dot_general(k_ref[g, pl.ds(start, n_keys), :], q, _NT, preferred_element_type=F32) + bias
    m = jnp.max(s, axis=0, keepdims=True)
    if sink is not None:
        m = jnp.maximum(m, sink)
    p = jnp.exp2(s - m)
    denom = jnp.sum(p, axis=0, keepdims=True)
    denom = denom + jnp.exp2(sink - m) if sink is not None else jnp.maximum(denom, 1e-30)
    return _pv_t(v_ref[g, pl.ds(start, n_keys), :], p) * (1.0 / denom)


def _nsa_dense_kernel(q_ref, kc_ref, vc_ref, kw_ref, vw_ref, gate_ref, ov_ref, qs_ref, ks_ref, vs_ref, sink_ref,
                      o_ref, sel_ref, swa_ref):
    i = pl.program_id(1)
    rows = GROUP_HEADS * Q_TILE
    width = GROUP_HEADS * HEAD_DIM
    n_cmp = kc_ref.shape[1]
    n = lax.broadcasted_iota(jnp.int32, (n_cmp, Q_TILE), 0)
    t = i * Q_TILE + lax.broadcasted_iota(jnp.int32, (n_cmp, Q_TILE), 1)
    valid = _tile_heads(jnp.where(n * NSA_CMP_STRIDE + (NSA_CMP_BLOCK - 1) <= t, 1.0, 0.0))
    bias = (valid - 1.0) * (-MASKED)
    n_sel = ov_ref.shape[0]
    jb = lax.broadcasted_iota(jnp.int32, (n_sel, Q_TILE), 0)
    cur = (i * Q_TILE + lax.broadcasted_iota(jnp.int32, (n_sel, Q_TILE), 1)) // NSA_SEL_BLOCK
    valid_b = jb <= cur
    forced = jnp.where(valid_b, jnp.where(jb == 0, 1.0, 0.0) + jnp.where(jb == cur, 1.0, 0.0) + jnp.where(jb == cur - 1, 1.0, 0.0), 0.0)
    for g in range(NSA_KV_HEADS):
        q = q_ref[g * GROUP_HEADS:(g + 1) * GROUP_HEADS].reshape(rows, HEAD_DIM)
        s = lax.dot_general(kc_ref[g].astype(BF16), q, _NT, preferred_element_type=F32) + bias
        m = jnp.max(s, axis=0, keepdims=True)
        p = jnp.exp2(s - m) * valid
        p = p * (1.0 / jnp.maximum(jnp.sum(p, axis=0, keepdims=True), 1e-30))
        o_cmp = _pv_t(vc_ref[g].astype(BF16), p)
        o_win = _band_branch(i, q, kw_ref, vw_ref, g, NSA_WINDOW)
        gate_t = _gate_rows(gate_ref, g)
        o_ref[:, g * width:(g + 1) * width] = _heads_from_t(o_cmp * _gate_cols(gate_t, 0) + o_win * _gate_cols(gate_t, 2)).astype(o_ref.dtype)

        p_sum = p[:, 0:Q_TILE]
        for e in range(1, GROUP_HEADS):
            p_sum = p_sum + p[:, e * Q_TILE:(e + 1) * Q_TILE]
        imp = jnp.dot(ov_ref[...], p_sum, preferred_element_type=F32, precision=lax.Precision.HIGHEST)
        score = jnp.where(forced > 0.5, NSA_FORCE_SCORE, jnp.where(valid_b, imp, -jnp.inf))
        sub = SUBLANES
        ranks = []
        for v in range(n_sel // sub):
            blk_scores = score[v * sub:(v + 1) * sub, :]
            jb_v = v * sub + lax.broadcasted_iota(jnp.int32, (sub, Q_TILE), 0)
            rank = jnp.zeros((sub, Q_TILE), F32)
            for ii in range(n_sel):
                row = score[ii:ii + 1, :]
                if ii < v * sub:
                    ahead = jnp.where(row >= blk_scores, 1.0, 0.0)
                elif ii >= (v + 1) * sub:
                    ahead = jnp.where(row > blk_scores, 1.0, 0.0)
                else:
                    tie = jnp.where(jb_v > ii, 1.0, 0.0)
                    ahead = jnp.where(row > blk_scores, 1.0, jnp.where(row == blk_scores, tie, 0.0))
                rank = rank + ahead
            ranks.append(rank)
        rank = jnp.concatenate(ranks, axis=0)
        sel_t = jnp.where(valid_b, jnp.where(rank < NSA_TOPN, 1.0, 0.0), 0.0)
        sel_t = jnp.concatenate([sel_t, jnp.zeros((LANES - n_sel, Q_TILE), F32)], axis=0)
        sel_ref[g] = sel_t.T.astype(BF16)

    for g in range(SWA_KV_HEADS):
        q = qs_ref[g * GROUP_HEADS:(g + 1) * GROUP_HEADS].reshape(rows, HEAD_DIM)
        sink = LOG2_E * jnp.concatenate([jnp.broadcast_to(sink_ref[g][:, e:e + 1], (1, Q_TILE)) for e in range(GROUP_HEADS)], axis=1)
        o_swa = _band_branch(i, q, ks_ref, vs_ref, g, SWA_WINDOW, sink)
        swa_ref[:, g * width:(g + 1) * width] = _heads_from_t(o_swa).astype(swa_ref.dtype)


def _nsa_dense(q, k_cmp, v_cmp, kw, vw, gates, overlap_t, q_s, k_s, v_s, sinks):
    nh, bsz, seq, hd = q.shape
    ng, _, n_cmp, _ = k_cmp.shape
    heads = pl.BlockSpec((nh, None, Q_TILE, hd), lambda b, i: (0, b, i, 0))
    cmp_kv = pl.BlockSpec((ng, None, n_cmp, hd), lambda b, i: (0, b, 0, 0))
    seq_kv = pl.BlockSpec((ng, None, seq, hd), lambda b, i: (0, b, 0, 0))
    out = pl.BlockSpec((None, Q_TILE, nh * hd), lambda b, i: (b, i, 0))
    return pl.pallas_call(
        _nsa_dense_kernel,
        grid=(bsz, seq // Q_TILE),
        in_specs=[heads, cmp_kv, cmp_kv, seq_kv, seq_kv,
                  pl.BlockSpec((ng, None, Q_TILE, gates.shape[-1]), lambda b, i: (0, b, i, 0)),
                  pl.BlockSpec(overlap_t.shape, lambda b, i: (0, 0)),
                  heads, seq_kv, seq_kv,
                  pl.BlockSpec(sinks.shape, lambda b, i: (0, 0, 0))],
        out_specs=[out, pl.BlockSpec((ng, None, Q_TILE, LANES), lambda b, i: (0, b, i, 0)), out],
        out_shape=[jax.ShapeDtypeStruct((bsz, seq, nh * hd), BF16),
                   jax.ShapeDtypeStruct((ng, bsz, seq, LANES), BF16),
                   jax.ShapeDtypeStruct((bsz, seq, nh * hd), BF16)],
        compiler_params=_compiler_params(("parallel", "parallel")),
        name="nsa_dense",
    )(q, k_cmp, v_cmp, kw, vw, gates, overlap_t, q_s, k_s, v_s, sinks)


def _nsa_sel_kernel(q_ref, k_ref, v_ref, sel_ref, gate_ref, o_ref, s_ref, m_ref, acc_ref):
    i = pl.program_id(1)
    rows = GROUP_HEADS * Q_TILE
    width = GROUP_HEADS * HEAD_DIM
    n_blk = LANES - HEAD_DIM
    kc = SEL_CHUNK
    seq = k_ref.shape[1]
    groups = range(NSA_KV_HEADS)

    def scores(g, start, lhs):
        return lax.dot_general(k_ref[g, pl.ds(start, kc), :], lhs, _NT, preferred_element_type=F32)

    start_d = pl.multiple_of(jnp.maximum(i * Q_TILE + Q_TILE - kc, 0), Q_TILE)
    n_past = (start_d + kc - 1) // kc
    kpos = start_d + lax.broadcasted_iota(jnp.int32, (kc, rows), 0)
    qpos = i * Q_TILE + (lax.broadcasted_iota(jnp.int32, (kc, rows), 1) & (Q_TILE - 1))
    blk = lax.broadcasted_iota(jnp.int32, (Q_TILE, n_blk), 1)
    lhs_past, lhs_none = [], []
    for g in groups:
        q = q_ref[g * GROUP_HEADS:(g + 1) * GROUP_HEADS].reshape(rows, HEAD_DIM)
        sel = sel_ref[g][:, :n_blk].astype(F32)
        neg_diag = (sel - 1.0) * (-MASKED)
        neg_past = jnp.where(blk < start_d // NSA_SEL_BLOCK, neg_diag, MASKED)
        lhs_diag = jnp.concatenate([q, jnp.concatenate([neg_diag.astype(BF16)] * GROUP_HEADS, axis=0)], axis=-1)
        lhs_past.append(jnp.concatenate([q, jnp.concatenate([neg_past.astype(BF16)] * GROUP_HEADS, axis=0)], axis=-1))
        lhs_none.append(jnp.concatenate([q, jnp.full((rows, n_blk), MASKED, BF16)], axis=-1))
        s_ref[g, 0] = jnp.where(kpos <= qpos, scores(g, start_d, lhs_diag), MASKED)

        m_ref[g] = jnp.full((1, rows), MASKED, F32)
        acc_ref[g] = jnp.zeros((ACC_ROWS, rows), F32)

    def step(c, cur, nxt, lookahead=True):
        next_start = pl.multiple_of(jnp.minimum(c * kc, seq - kc), kc)
        v_start = pl.multiple_of(jnp.where(c == 0, start_d, (c - 1) * kc), Q_TILE)
        for g in groups:
            if lookahead:
                s_ref[g, nxt] = scores(g, next_start, jnp.where(c < n_past, lhs_past[g], lhs_none[g]))
            s = s_ref[g, cur]
            m = m_ref[g]
            m_new = jnp.maximum(m, jnp.max(s, axis=0, keepdims=True))
            alpha = jnp.exp2(m - m_new)
            p = jnp.exp2(s - m_new)
            m_ref[g] = m_new
            acc_ref[g] = alpha * acc_ref[g] + _pv_t(v_ref[g, pl.ds(v_start, kc), :], p)[:ACC_ROWS]

    def body(j, carry):
        step(2 * j, 0, 1)
        step(2 * j + 1, 1, 0)
        return carry

    lax.fori_loop(0, (n_past + 1) // 2, body, 0)

    @pl.when(n_past % 2 == 0)
    def _():
        step(n_past, 0, 1, lookahead=False)

    for g in groups:
        gate = _gate_cols(_gate_rows(gate_ref, g), 1)
        acc = acc_ref[g]
        o_ref[:, g * width:(g + 1) * width] = _heads_from_t(acc[:HEAD_DIM] * (gate / acc[HEAD_DIM:HEAD_DIM + 1])).astype(o_ref.dtype)


def _nsa_sel(q, k, v, sel, gates):
    nh, bsz, seq, hd = q.shape
    ng = k.shape[0]
    onehot = (np.arange(seq)[:, None] // NSA_SEL_BLOCK == np.arange(LANES - hd)[None, :]).astype(np.float32)
    k_ext = jnp.concatenate([k, jnp.broadcast_to(jnp.asarray(onehot, BF16), (ng, bsz, seq, LANES - hd))], axis=-1)
    return pl.pallas_call(
        _nsa_sel_kernel,
        grid=(bsz, seq // Q_TILE),
        in_specs=[pl.BlockSpec((nh, None, Q_TILE, hd), lambda b, i: (0, b, i, 0)),
                  pl.BlockSpec((ng, None, seq, LANES), lambda b, i: (0, b, 0, 0)),
                  pl.BlockSpec((ng, None, seq, LANES), lambda b, i: (0, b, 0, 0)),
                  pl.BlockSpec((ng, None, Q_TILE, LANES), lambda b, i: (0, b, i, 0)),
                  pl.BlockSpec((ng, None, Q_TILE, gates.shape[-1]), lambda b, i: (0, b, i, 0))],
        out_specs=pl.BlockSpec((None, Q_TILE, nh * hd), lambda b, i: (b, i, 0)),
        out_shape=jax.ShapeDtypeStruct((bsz, seq, nh * hd), BF16),
        scratch_shapes=[pltpu.VMEM((ng, 2, SEL_CHUNK, GROUP_HEADS * Q_TILE), F32),
                        pltpu.VMEM((ng, 1, GROUP_HEADS * Q_TILE), F32),
                        pltpu.VMEM((ng, ACC_ROWS, GROUP_HEADS * Q_TILE), F32)],
        compiler_params=_compiler_params(("parallel", "parallel")),
        name="nsa_sel",
    )(q, k_ext, _with_ones(v), sel, gates)


def _overlap_t(seq):
    n_cmp = (seq - NSA_CMP_BLOCK) // NSA_CMP_STRIDE + 1
    n_pad = seq // NSA_CMP_STRIDE
    starts = np.arange(n_pad) * NSA_CMP_STRIDE
    sel_start = np.arange(seq // NSA_SEL_BLOCK) * NSA_SEL_BLOCK
    ov = (starts[None, :] < sel_start[:, None] + NSA_SEL_BLOCK) & (starts[None, :] + NSA_CMP_BLOCK > sel_start[:, None])
    ov = ov & (np.arange(n_pad)[None, :] < n_cmp)
    return jnp.asarray(ov.astype(np.float32))


def _odd_mixer(xt, nw, w_in, w_out, cmp_pe, cmp_w1, cmp_w2, sinks, bsz):
    t, d = xt.shape
    seq = t // bsz
    assert seq // NSA_SEL_BLOCK <= LANES - HEAD_DIM and seq % SEL_CHUNK == 0 and seq % Q_TILE == 0
    assert _band_window(NSA_WINDOW) <= seq and SEL_CHUNK % Q_TILE == 0
    cuts = [int(c) for c in np.cumsum((0,) + ODD_SPLITS)]
    ws = [w_in[:, a:b] for a, b in zip(cuts[:-1], cuts[1:])]
    ws[0] = ws[0] * (HEAD_DIM ** -0.5 * LOG2_E)
    ws[8] = ws[8] * (HEAD_DIM ** -0.5 * LOG2_E)
    ws = [w.astype(BF16) for w in ws]
    hd = HEAD_DIM
    head_dims = (hd, hd, hd, hd, hd, hd, hd, 3 * GROUP_HEADS, hd, hd, hd)
    dtypes = (BF16, F32, F32, BF16, BF16, BF16, BF16, F32, BF16, BF16, BF16)
    q_n, kc, vc, ks, vs, kw, vw, gates, q_s, k_s, v_s = _norm_proj(xt, nw, ws, head_dims, dtypes, tm=2 * TOKEN_TILE)

    def heads(a):
        return a.reshape(a.shape[0], bsz, seq, a.shape[-1])

    q_n, kc, vc, ks, vs, kw, vw, gates, q_s, k_s, v_s = map(heads, (q_n, kc, vc, ks, vs, kw, vw, gates, q_s, k_s, v_s))
    k_cmp, v_cmp = _cmp_kv(kc, vc, cmp_pe, cmp_w1.astype(BF16), cmp_w2.astype(BF16))
    o_cw, sel, o_swa = _nsa_dense(q_n, k_cmp, v_cmp, kw, vw, gates, _overlap_t(seq), q_s, k_s, v_s,
                                  sinks.reshape(SWA_KV_HEADS, 1, GROUP_HEADS))
    o_sel = _nsa_sel(q_n, ks, vs, sel, gates)
    n_nsa = NSA_HEADS * hd
    return [[o_cw.reshape(t, n_nsa), o_sel.reshape(t, n_nsa)], [o_swa.reshape(t, SWA_HEADS * hd)]], [w_out[:n_nsa], w_out[n_nsa:]]


def kernel(x, mem, norm_w, final_norm_w, mem_norm_w, ffn_w_gate, ffn_w_up, ffn_w_down, xattn_wq, xattn_wkv, xattn_wo, even_w_in, even_w_out, ssm_conv_w, ssm_conv_b, ssm_dt_bias, ssm_a_log, ssm_d, ssm_norm_w, odd_w_in, odd_w_out, nsa_cmp_pe, nsa_cmp_w1, nsa_cmp_w2, swa_sinks):
    bsz, seq, d = x.shape
    t = bsz * seq
    m = mem.shape[1]
    wg, wu, wd = ffn_w_gate, ffn_w_up, ffn_w_down
    wkv = xattn_wkv.astype(BF16)
    even_out = even_w_out.astype(BF16)
    odd_out = odd_w_out.astype(BF16)

    xt = x.reshape(t, d)
    mem2 = mem.reshape(bsz * m, d)
    mem_nw = mem_norm_w.reshape(1, d)
    for layer in range(DEPTH):
        i = layer // 2
        xt = _ffn(xt, norm_w[layer, 0].reshape(1, d), wg, wu, wd, layer, 0)
        nw1 = norm_w[layer, 1].reshape(1, d)
        if layer % 2 == 0:
            mix, mix_w = _even_mixer(xt, nw1, even_w_in[i], even_out[i], ssm_conv_w[i], ssm_conv_b[i], ssm_dt_bias[i], ssm_a_log[i], ssm_d[i], ssm_norm_w[i], bsz)
        else:
            mix, mix_w = _odd_mixer(xt, nw1, odd_w_in[i], odd_out[i], nsa_cmp_pe[i], nsa_cmp_w1[i], nsa_cmp_w2[i], swa_sinks[i], bsz)
        (kv,) = _norm_proj(mem2, mem_nw, [wkv[layer]])
        xt = _mix_xattn(xt, mix, mix_w, norm_w[layer, 2].reshape(1, d), xattn_wq, kv.reshape(bsz, m, 2 * d), xattn_wo, layer, bsz)
        out_nw = final_norm_w.reshape(1, d) if layer == DEPTH - 1 else None
        xt = _ffn(xt, norm_w[layer, 3].reshape(1, d), wg, wu, wd, layer, 1, out_nw)
    return xt.reshape(bsz, seq, d)
```
